```python
import math
import jax, jax.numpy as jnp
from jax import lax
import numpy as np

D_MODEL = 1024
BATCH = 1
SEQ = 16384
DEPTH = 2
DEC_BATCH = 32
DEC_SEQ = 8
PAST_LEN = 16384
PAGE_SIZE = 128

D_SSM = D_MODEL // 2
SSM_GROUP = 16
N_SSM_GROUPS = D_SSM // SSM_GROUP
P_STATE = 64
N_HEADS = 8
HEAD_DIM = 64
N_KV = 2
HPG = N_HEADS // N_KV
D_ATT = N_HEADS * HEAD_DIM
D_KV = N_KV * HEAD_DIM
CMP_BLOCK = 32
CMP_STRIDE = 16
SEL_BLOCK = 64
TOP_N = 16
WINDOW = 512
Q_BLOCK = 128
SEL_BIG = 1e4
D_FF = ((8 * D_MODEL // 3 + 255) // 256) * 256
D_PLE = 256
RMS_EPS = 1e-6

kernel_name = "hybrid_s5_nsa_macaron_decode_step"


def rmsnorm(x, g):
    xf = x.astype(jnp.float32)
    y = xf * lax.rsqrt(jnp.mean(xf * xf, axis=-1, keepdims=True) + RMS_EPS)
    return (y * g.astype(jnp.float32)).astype(x.dtype)


def swiglu_half(x, g, w_in, w_out):
    a, b = jnp.split(rmsnorm(x, g) @ w_in, 2, axis=-1)
    return 0.5 * ((jax.nn.silu(a) * b) @ w_out)


def heads(z, n):
    return z.reshape(z.shape[:-1] + (n, HEAD_DIM))


def alibi_slopes():
    return (2.0 ** (-8.0 * jnp.arange(1, N_HEADS + 1, dtype=jnp.float32) / N_HEADS)).reshape(N_KV, HPG)


def masked_softmax(s, mask):
    s = jnp.where(mask, s.astype(jnp.float32), -1e30)
    m = jnp.max(s, axis=-1, keepdims=True)
    e = jnp.where(mask, jnp.exp(s - m), 0.0)
    return e / jnp.maximum(jnp.sum(e, axis=-1, keepdims=True), 1e-30)


def _complex_affine_combine(e1, e2):
    a1r, a1i, b1r, b1i = e1
    a2r, a2i, b2r, b2i = e2
    return (a2r * a1r - a2i * a1i,
            a2r * a1i + a2i * a1r,
            a2r * b1r - a2i * b1i + b2r,
            a2r * b1i + a2i * b1r + b2i)


def s5_scan(u, h0_re, h0_im, W):
    f32 = jnp.float32
    bsz, t = u.shape[:2]
    uf = u.astype(f32).reshape(bsz, t, N_SSM_GROUPS, SSM_GROUP)
    ar = W["ssm_a_re"].astype(f32)
    ai = W["ssm_a_im"].astype(f32)
    dt = jnp.exp(W["ssm_log_dt"].astype(f32))[:, None]
    mag = jnp.exp(ar * dt)
    lr = mag * jnp.cos(ai * dt)
    li = mag * jnp.sin(ai * dt)
    den = ar * ar + ai * ai
    fr = ((lr - 1.0) * ar + li * ai) / den
    fi = (li * ar - (lr - 1.0) * ai) / den
    br = W["ssm_b_re"].astype(f32)
    bi = W["ssm_b_im"].astype(f32)
    bbr = fr[..., None] * br - fi[..., None] * bi
    bbi = fr[..., None] * bi + fi[..., None] * br
    xr = jnp.einsum("btgc,gpc->btgp", uf, bbr)
    xi = jnp.einsum("btgc,gpc->btgp", uf, bbi)
    h0r = h0_re.astype(f32)
    h0i = h0_im.astype(f32)
    xr = xr.at[:, 0].add(lr * h0r - li * h0i)
    xi = xi.at[:, 0].add(lr * h0i + li * h0r)
    a_r = jnp.broadcast_to(lr, xr.shape)
    a_i = jnp.broadcast_to(li, xi.shape)
    _, _, hr, hi = lax.associative_scan(_complex_affine_combine, (a_r, a_i, xr, xi), axis=1)
    y = (jnp.einsum("btgp,gcp->btgc", hr, W["ssm_c_re"].astype(f32))
         - jnp.einsum("btgp,gcp->btgc", hi, W["ssm_c_im"].astype(f32)))
    y = y.reshape(bsz, t, D_SSM) + W["ssm_d"].astype(f32) * uf.reshape(bsz, t, D_SSM)
    return y, hr[:, -1], hi[:, -1]


def compress(kv, pe, w):
    bsz, t = kv.shape[:2]
    r = CMP_BLOCK // CMP_STRIDE
    nc = (t - CMP_BLOCK) // CMP_STRIDE + 1
    kr = kv[:, : (nc + r - 1) * CMP_STRIDE].reshape(bsz, nc + r - 1, CMP_STRIDE, N_KV, HEAD_DIM)
    pe = pe.reshape(r, CMP_STRIDE, HEAD_DIM)
    w = w.reshape(r, CMP_STRIDE, HEAD_DIM, HEAD_DIM)
    out = jnp.einsum("bnsgd,sde->bnge", kr[:, 0:nc] + pe[0][:, None, :], w[0])
    for j in range(1, r):
        out = out + jnp.einsum("bnsgd,sde->bnge", kr[:, j:j + nc] + pe[j][:, None, :], w[j])
    return out


def compress_kv(kc, vc, W):
    kcc = rmsnorm(compress(kc, W["cmp_pe_k"], W["cmp_w_k"]), W["qk_norm"][1])
    vcc = compress(vc, W["cmp_pe_v"], W["cmp_w_v"])
    kc_end = jnp.arange(kcc.shape[1]) * CMP_STRIDE + (CMP_BLOCK - 1)
    return kcc, vcc, kc_end


def cmp_to_slc(p, ns):
    ratio = SEL_BLOCK // CMP_STRIDE
    lo = CMP_BLOCK // CMP_STRIDE - 1
    nc = p.shape[-1]
    length = ns * ratio + lo
    pp = jnp.pad(p, [(0, 0)] * (p.ndim - 1) + [(lo, length - lo - nc)])
    out = pp[..., 0: ns * ratio: ratio]
    for w in range(1, lo + ratio):
        out = out + pp[..., w: w + ns * ratio: ratio]
    return out


def to_blocks(x, ns):
    bsz, t = x.shape[:2]
    x = jnp.pad(x, ((0, 0), (0, ns * SEL_BLOCK - t), (0, 0), (0, 0)))
    return x.reshape(bsz, ns, SEL_BLOCK, N_KV, HEAD_DIM).transpose(0, 3, 1, 2, 4)


_gather_blocks = jax.vmap(jax.vmap(lambda blocks, ix: blocks[ix]))


def nsa_attend(q, qpos, gates, kc, vc, kc_end, ks_blk, vs_blk, kw, vw, kw_pos):
    f32 = jnp.float32
    bsz, nq = q.shape[:2]
    qg = q.astype(f32).reshape(bsz, nq, N_KV, HPG, HEAD_DIM) * (HEAD_DIM ** -0.5)
    slopes = alibi_slopes()[None, :, :, None, None]
    tq = qpos.astype(f32)[:, None]
    dist_c = tq - kc_end.astype(f32)[None, :]
    s_c = jnp.einsum("bqghd,bngd->bghqn", qg, kc.astype(f32))
    p_c = masked_softmax(s_c - slopes * dist_c, dist_c >= 0)
    o_c = jnp.einsum("bghqn,bngd->bqghd", p_c, vc.astype(f32))
    ns = ks_blk.shape[2]
    imp = cmp_to_slc(jnp.sum(p_c, axis=2), ns)
    blk = jnp.arange(ns)[None, :]
    cur = (qpos // SEL_BLOCK)[:, None]
    forced = (blk == 0) | (blk == cur) | (blk == cur - 1)
    score = jnp.where(forced, imp + SEL_BIG, jnp.where(blk <= cur, imp, -SEL_BIG))
    _, idx = lax.top_k(score, min(TOP_N, ns))
    n = idx.shape[-1]
    ksel = _gather_blocks(ks_blk, idx).reshape(bsz, N_KV, nq, n * SEL_BLOCK, HEAD_DIM)
    vsel = _gather_blocks(vs_blk, idx).reshape(bsz, N_KV, nq, n * SEL_BLOCK, HEAD_DIM)
    pos_s = (idx[..., None] * SEL_BLOCK + jnp.arange(SEL_BLOCK)).reshape(bsz, N_KV, nq, n * SEL_BLOCK)
    dist_s = (qpos[:, None] - pos_s).astype(f32)[:, :, None]
    s_s = jnp.einsum("bqghd,bgqkd->bghqk", qg, ksel.astype(f32))
    p_s = masked_softmax(s_s - slopes * dist_s, dist_s >= 0)
    o_s = jnp.einsum("bghqk,bgqkd->bqghd", p_s, vsel.astype(f32))
    dist_w = tq - kw_pos.astype(f32)[None, :]
    mask_w = (dist_w >= 0) & (dist_w < WINDOW) & (kw_pos >= 0)[None, :]
    s_w = jnp.einsum("bqghd,bkgd->bghqk", qg, kw.astype(f32))
    p_w = masked_softmax(s_w - slopes * dist_w, mask_w)
    o_w = jnp.einsum("bghqk,bkgd->bqghd", p_w, vw.astype(f32))
    g = jax.nn.sigmoid(gates.astype(f32)).reshape(bsz, nq, 3, N_KV, HPG, 1)
    o = g[:, :, 0] * o_c + g[:, :, 1] * o_s + g[:, :, 2] * o_w
    return o.reshape(bsz, nq, D_ATT).astype(q.dtype)


def mixer_inputs(h, W):
    z = rmsnorm(h, W["norm_mix"]) @ W["w_in"]
    sizes = [D_SSM, D_ATT] + [D_KV] * 6 + [3 * N_HEADS, D_MODEL]
    cuts = [int(c) for c in np.cumsum(sizes)]
    u_ssm, q, kc, vc, ks, vs, kw, vw, g_nsa, g_a, g_b = jnp.split(z, cuts, axis=-1)
    qkn = W["qk_norm"]
    q = rmsnorm(heads(q, N_HEADS), qkn[0])
    ks = rmsnorm(heads(ks, N_KV), qkn[2])
    kw = rmsnorm(heads(kw, N_KV), qkn[3])
    mix_in = (u_ssm, q, heads(kc, N_KV), heads(vc, N_KV), ks, heads(vs, N_KV), kw, heads(vw, N_KV), g_nsa)
    return mix_in, g_a, g_b


def prompt_mixer(z, W):
    u_ssm, q, kc, vc, ks, vs, kw, vw, g_nsa = z
    bsz, t = u_ssm.shape[:2]
    h0 = jnp.zeros((bsz, N_SSM_GROUPS, P_STATE), jnp.float32)
    y_ssm, hr, hi = s5_scan(u_ssm, h0, h0, W)
    kcc, vcc, kc_end = compress_kv(kc, vc, W)
    ns = -(-t // SEL_BLOCK)
    ksb = to_blocks(ks, ns)
    vsb = to_blocks(vs, ns)
    lw = WINDOW + Q_BLOCK
    kw_pad = jnp.pad(kw, ((0, 0), (WINDOW, 0), (0, 0), (0, 0)))
    vw_pad = jnp.pad(vw, ((0, 0), (WINDOW, 0), (0, 0), (0, 0)))

    def body(i):
        s0 = i * Q_BLOCK
        qpos = s0 + jnp.arange(Q_BLOCK)
        return nsa_attend(lax.dynamic_slice_in_dim(q, s0, Q_BLOCK, axis=1), qpos,
                          lax.dynamic_slice_in_dim(g_nsa, s0, Q_BLOCK, axis=1),
                          kcc, vcc, kc_end, ksb, vsb,
                          lax.dynamic_slice_in_dim(kw_pad, s0, lw, axis=1),
                          lax.dynamic_slice_in_dim(vw_pad, s0, lw, axis=1),
                          s0 - WINDOW + jnp.arange(lw))

    o = lax.map(body, jnp.arange(t // Q_BLOCK))
    o = jnp.swapaxes(o, 0, 1).reshape(bsz, t, D_ATT)
    kv_win = jnp.stack([kw, vw], axis=2)[:, t - min(WINDOW, t):]
    state = (jnp.stack([kc, vc], axis=2), jnp.stack([ks, vs], axis=2), kv_win, hr, hi)
    return y_ssm, o, state


def sample_mixer(z, W, cache_cmp, cache_sel, cache_win, st_re, st_im, page_table):
    u_ssm, q, kc, vc, ks, vs, kw, vw, g_nsa = z
    bsz, nq = u_ssm.shape[:2]
    past = page_table.shape[1] * PAGE_SIZE
    y_ssm, hr, hi = s5_scan(u_ssm, st_re, st_im, W)
    past_cmp = cache_cmp[page_table].reshape(bsz, past, 2, N_KV, HEAD_DIM)
    past_sel = cache_sel[page_table].reshape(bsz, past, 2, N_KV, HEAD_DIM)
    kcc, vcc, kc_end = compress_kv(jnp.concatenate([past_cmp[:, :, 0], kc], axis=1),
                                   jnp.concatenate([past_cmp[:, :, 1], vc], axis=1), W)
    ns = -(-(past + nq) // SEL_BLOCK)
    ksb = to_blocks(jnp.concatenate([past_sel[:, :, 0], ks], axis=1), ns)
    vsb = to_blocks(jnp.concatenate([past_sel[:, :, 1], vs], axis=1), ns)
    w_buf = cache_win.shape[1]
    kw_all = jnp.concatenate([cache_win[:, :, 0], kw], axis=1)
    vw_all = jnp.concatenate([cache_win[:, :, 1], vw], axis=1)
    kw_pos = past - w_buf + jnp.arange(w_buf + nq)
    qpos = past + jnp.arange(nq)
    o = nsa_attend(q, qpos, g_nsa, kcc, vcc, kc_end, ksb, vsb, kw_all, vw_all, kw_pos)
    new_win = jnp.stack([kw_all, vw_all], axis=2)[:, nq:]
    state = (jnp.stack([kc, vc], axis=2), jnp.stack([ks, vs], axis=2), new_win, hr, hi)
    return y_ssm, o, state


def block(h, p, W, mix, mix_args):
    h = h + swiglu_half(h, W["norm_ffn1"], W["w_ffn1_in"], W["w_ffn1_out"])
    z, g_a, g_b = mixer_inputs(h, W)
    y_ssm, o_att, state = mix(z, W, *mix_args)
    ga, gb = jnp.split(jax.nn.gelu(y_ssm.astype(h.dtype)) @ W["w_glu"], 2, axis=-1)
    br_a = ga * jax.nn.sigmoid(gb)
    br_b = o_att @ W["w_att_out"]
    h = h + (jax.nn.sigmoid(g_a) * br_a + jax.nn.sigmoid(g_b) * br_b) @ W["w_out"]
    h = h + swiglu_half(h, W["norm_ffn2"], W["w_ffn2_in"], W["w_ffn2_out"])
    h = h + jax.nn.sigmoid(rmsnorm(h, W["norm_ple"]) @ W["w_ple_gate"]) * (p @ W["w_ple_proj"])
    return h, state


def setup_inputs(seed: int = 0) -> dict:
    key = jax.random.key(seed)
    keys = iter(jax.random.split(key, 64))
    f32 = jnp.float32

    def nrm(shape, scale):
        return scale * jax.random.normal(next(keys), shape, f32)

    def gain(shape):
        return 1.0 + 0.01 * jax.random.normal(next(keys), shape, f32)

    L = DEPTH
    n_pages = PAST_LEN // PAGE_SIZE
    n_pool = (5 * DEC_BATCH * n_pages + 3) // 4
    w_buf = min(WINDOW, PAST_LEN)
    n_in = D_SSM + D_ATT + 6 * D_KV + 3 * N_HEADS + 2 * D_MODEL
    perm = jax.random.permutation(next(keys), n_pool)
    page_table = perm[: DEC_BATCH * n_pages].reshape(DEC_BATCH, n_pages).astype(jnp.int32)
    x_prompt = nrm((BATCH, SEQ, D_MODEL), 1.0)
    x_sample = nrm((DEC_BATCH, DEC_SEQ, D_MODEL), 1.0)
    cache_kv_cmp = nrm((L, n_pool, PAGE_SIZE, 2, N_KV, HEAD_DIM), 1.0)
    cache_kv_sel = nrm((L, n_pool, PAGE_SIZE, 2, N_KV, HEAD_DIM), 1.0)
    cache_kv_win = nrm((L, DEC_BATCH, w_buf, 2, N_KV, HEAD_DIM), 1.0)
    state_ssm_re = nrm((L, DEC_BATCH, N_SSM_GROUPS, P_STATE), 0.5)
    state_ssm_im = nrm((L, DEC_BATCH, N_SSM_GROUPS, P_STATE), 0.5)
    p_prompt = nrm((L, BATCH, SEQ, D_PLE), 1.0)
    p_sample = nrm((L, DEC_BATCH, DEC_SEQ, D_PLE), 1.0)
    ssm_a_re = -0.5 + nrm((L, N_SSM_GROUPS, P_STATE), 0.01)
    ssm_a_im = jnp.pi * jnp.arange(P_STATE, dtype=f32) + nrm((L, N_SSM_GROUPS, P_STATE), 0.01)
    ssm_log_dt = jax.random.uniform(next(keys), (L, N_SSM_GROUPS), f32, math.log(1e-3), math.log(1e-1))
    b_scale = (2.0 * SSM_GROUP) ** -0.5
    c_scale = (2.0 * P_STATE) ** -0.5
    return {
        "x_prompt": x_prompt, "x_sample": x_sample,
        "cache_kv_cmp": cache_kv_cmp, "cache_kv_sel": cache_kv_sel, "cache_kv_win": cache_kv_win,
        "state_ssm_re": state_ssm_re, "state_ssm_im": state_ssm_im,
        "page_table": page_table, "p_prompt": p_prompt, "p_sample": p_sample,
        "norm_ffn1": gain((L, D_MODEL)),
        "w_ffn1_in": nrm((L, D_MODEL, 2 * D_FF), D_MODEL ** -0.5),
        "w_ffn1_out": nrm((L, D_FF, D_MODEL), D_FF ** -0.5),
        "norm_mix": gain((L, D_MODEL)),
        "w_in": nrm((L, D_MODEL, n_in), D_MODEL ** -0.5),
        "qk_norm": gain((L, 4, HEAD_DIM)),
        "ssm_a_re": ssm_a_re, "ssm_a_im": ssm_a_im, "ssm_log_dt": ssm_log_dt,
        "ssm_b_re": nrm((L, N_SSM_GROUPS, P_STATE, SSM_GROUP), b_scale),
        "ssm_b_im": nrm((L, N_SSM_GROUPS, P_STATE, SSM_GROUP), b_scale),
        "ssm_c_re": nrm((L, N_SSM_GROUPS, SSM_GROUP, P_STATE), c_scale),
        "ssm_c_im": nrm((L, N_SSM_GROUPS, SSM_GROUP, P_STATE), c_scale),
        "ssm_d": nrm((L, D_SSM), 1.0),
        "w_glu": nrm((L, D_SSM, 2 * D_MODEL), D_SSM ** -0.5),
        "cmp_pe_k": nrm((L, CMP_BLOCK, HEAD_DIM), 0.1),
        "cmp_pe_v": nrm((L, CMP_BLOCK, HEAD_DIM), 0.1),
        "cmp_w_k": nrm((L, CMP_BLOCK, HEAD_DIM, HEAD_DIM), (CMP_BLOCK * HEAD_DIM) ** -0.5),
        "cmp_w_v": nrm((L, CMP_BLOCK, HEAD_DIM, HEAD_DIM), (CMP_BLOCK * HEAD_DIM) ** -0.5),
        "w_att_out": nrm((L, D_ATT, D_MODEL), D_ATT ** -0.5),
        "w_out": nrm((L, D_MODEL, D_MODEL), D_MODEL ** -0.5),
        "norm_ffn2": gain((L, D_MODEL)),
        "w_ffn2_in": nrm((L, D_MODEL, 2 * D_FF), D_MODEL ** -0.5),
        "w_ffn2_out": nrm((L, D_FF, D_MODEL), D_FF ** -0.5),
        "norm_ple": gain((L, D_MODEL)),
        "w_ple_gate": nrm((L, D_MODEL, D_MODEL), D_MODEL ** -0.5),
        "w_ple_proj": nrm((L, D_PLE, D_MODEL), D_PLE ** -0.5),
    }


def reference(x_prompt, x_sample, cache_kv_cmp, cache_kv_sel, cache_kv_win, state_ssm_re, state_ssm_im,
              page_table, p_prompt, p_sample, norm_ffn1, w_ffn1_in, w_ffn1_out, norm_mix, w_in, qk_norm,
              ssm_a_re, ssm_a_im, ssm_log_dt, ssm_b_re, ssm_b_im, ssm_c_re, ssm_c_im, ssm_d, w_glu,
              cmp_pe_k, cmp_pe_v, cmp_w_k, cmp_w_v, w_att_out, w_out, norm_ffn2, w_ffn2_in, w_ffn2_out,
              norm_ple, w_ple_gate, w_ple_proj):
    hp, hs = x_prompt, x_sample
    st_p = [[], [], [], [], []]
    st_s = [[], [], [], [], []]
    for i in range(DEPTH):
        W = dict(norm_ffn1=norm_ffn1[i], w_ffn1_in=w_ffn1_in[i], w_ffn1_out=w_ffn1_out[i],
                 norm_mix=norm_mix[i], w_in=w_in[i], qk_norm=qk_norm[i],
                 ssm_a_re=ssm_a_re[i], ssm_a_im=ssm_a_im[i], ssm_log_dt=ssm_log_dt[i],
                 ssm_b_re=ssm_b_re[i], ssm_b_im=ssm_b_im[i], ssm_c_re=ssm_c_re[i], ssm_c_im=ssm_c_im[i],
                 ssm_d=ssm_d[i], w_glu=w_glu[i], cmp_pe_k=cmp_pe_k[i], cmp_pe_v=cmp_pe_v[i],
                 cmp_w_k=cmp_w_k[i], cmp_w_v=cmp_w_v[i], w_att_out=w_att_out[i], w_out=w_out[i],
                 norm_ffn2=norm_ffn2[i], w_ffn2_in=w_ffn2_in[i], w_ffn2_out=w_ffn2_out[i],
                 norm_ple=norm_ple[i], w_ple_gate=w_ple_gate[i], w_ple_proj=w_ple_proj[i])
        hp, sp = block(hp, p_prompt[i], W, prompt_mixer, ())
        hs, ss = block(hs, p_sample[i], W, sample_mixer,
                       (cache_kv_cmp[i], cache_kv_sel[i], cache_kv_win[i], state_ssm_re[i], state_ssm_im[i],
                        page_table))
        for j in range(5):
            st_p[j].append(sp[j])
            st_s[j].append(ss[j])
    kv_cmp_p, kv_sel_p, kv_win_p, re_p, im_p = [jnp.stack(a) for a in st_p]
    kv_cmp_s, kv_sel_s, kv_win_s, re_s, im_s = [jnp.stack(a) for a in st_s]
    return (hp, hs, kv_cmp_p, kv_sel_p, kv_win_p, re_p, im_p, kv_cmp_s, kv_sel_s, kv_win_s, re_s, im_s)
```

```python
import functools

import jax
import jax.numpy as jnp
from jax import lax
from jax.experimental import pallas as pl
from jax.experimental.pallas import tpu as pltpu

F32 = jnp.float32
BF16 = jnp.bfloat16

D_MODEL = 1024
DEPTH = 2
D_SSM = 512
SSM_GROUP = 16
N_SSM_GROUPS = 32
P_STATE = 64
N_STATE = N_SSM_GROUPS * P_STATE
N_HEADS = 8
HEAD_DIM = 64
N_KV = 2
HPG = 4
D_KV = 128
CMP_BLOCK = 32
CMP_STRIDE = 16
SEL_BLOCK = 64
TOP_N = 16
WINDOW = 512
PAGE_SIZE = 128
SEL_BIG = 1e4
D_FF = 2816
D_PLE = 256
RMS_EPS = 1e-6
NEG = -1e30

LANE = 128
Q_TILE = 128
SEL_CHUNK = 256
WIN_KEYS = WINDOW + Q_TILE
CMP_ROW = CMP_STRIDE * 2 * D_KV
Q_PAD = N_HEADS * LANE
VMEM_LIMIT = 56 * 2 ** 20


def _cparams(*sem):
    return pltpu.CompilerParams(dimension_semantics=sem, vmem_limit_bytes=VMEM_LIMIT)


def _dot(a, b):
    return jnp.dot(a, b, preferred_element_type=F32)


def _dot_nt(a, b):
    return lax.dot_general(a, b, (((1,), (1,)), ((), ())), preferred_element_type=F32)


def _pick_tile(n, target):
    for t in range(min(n, target), 15, -1):
        if n % t == 0 and t % 16 == 0:
            return t
    raise ValueError(f"no row tile for {n}")


def _const_spec(shape):
    nd = len(shape)
    return pl.BlockSpec(shape, lambda *_: (0,) * nd, pipeline_mode=pl.Buffered(1))


def _rms(x, g):
    ms = jnp.mean(x * x, axis=-1, keepdims=True)
    return x * lax.rsqrt(ms + RMS_EPS) * g


def _halfnorm(x, gain):
    lo = lax.broadcasted_iota(jnp.int32, (1, LANE), 1) < HEAD_DIM
    x2 = x * x
    s_lo = jnp.sum(jnp.where(lo, x2, 0.0), axis=-1, keepdims=True)
    s_hi = jnp.sum(jnp.where(lo, 0.0, x2), axis=-1, keepdims=True)
    ms = jnp.where(lo, s_lo, s_hi) * (1.0 / HEAD_DIM)
    return x * lax.rsqrt(ms + RMS_EPS) * gain


def _ffn_kernel(*refs, ple):
    if ple:
        h_ref, g_ref, wi_ref, wo_ref, p_ref, gp_ref, wg_ref, wp_ref, o_ref = refs
    else:
        h_ref, g_ref, wi_ref, wo_ref, o_ref = refs
    h = h_ref[...]
    xn = _rms(h, g_ref[...]).astype(BF16)
    a = _dot(xn, wi_ref[:, :D_FF])
    b = _dot(xn, wi_ref[:, D_FF:])
    act = (a * jax.nn.sigmoid(a) * b).astype(BF16)
    h = h + 0.5 * _dot(act, wo_ref[...])
    if ple:
        xg = _rms(h, gp_ref[...]).astype(BF16)
        gate = jax.nn.sigmoid(_dot(xg, wg_ref[...]))
        h = h + gate * _dot(p_ref[...].astype(BF16), wp_ref[...])
    o_ref[...] = h


def _ffn(h, g, wi, wo, ple_args=None):
    n = h.shape[0]
    tm = _pick_tile(n, 256)
    row = lambda w: pl.BlockSpec((tm, w), lambda i: (i, 0))
    in_specs = [row(D_MODEL), _const_spec((1, D_MODEL)), _const_spec(wi.shape), _const_spec(wo.shape)]
    args = [h, g, wi, wo]
    if ple_args is not None:
        p, gp, wg, wp = ple_args
        in_specs += [row(D_PLE), _const_spec((1, D_MODEL)), _const_spec(wg.shape), _const_spec(wp.shape)]
        args += [p, gp, wg, wp]
    return pl.pallas_call(
        functools.partial(_ffn_kernel, ple=ple_args is not None),
        out_shape=jax.ShapeDtypeStruct((n, D_MODEL), F32),
        grid=(n // tm,),
        in_specs=in_specs,
        out_specs=row(D_MODEL),
        compiler_params=_cparams("parallel"),
    )(*args)


_C_U = 0
_C_Q = _C_U + D_SSM
_C_KV = _C_Q + Q_PAD
_C_GN = _C_KV + 6 * D_KV
_C_GA = _C_GN + LANE
_C_GB = _C_GA + D_MODEL
_C_END = _C_GB + D_MODEL


def _mix_in_kernel(h_ref, g_ref, w_ref, gq_ref, gks_ref, gkw_ref,
                   u_ref, qp_ref, kvc_ref, kvs_ref, kvw_ref, kvsb_ref, kvwb_ref, gn_ref, ga_ref, gb_ref):
    xn = _rms(h_ref[...], g_ref[...]).astype(BF16)
    u_ref[...] = _dot(xn, w_ref[:, _C_U:_C_Q])
    zq = _dot(xn, w_ref[:, _C_Q:_C_KV])
    for h in range(N_HEADS):
        qh = zq[:, h * LANE:(h + 1) * LANE]
        ms = jnp.sum(qh * qh, axis=-1, keepdims=True) * (1.0 / HEAD_DIM)
        qn = qh * lax.rsqrt(ms + RMS_EPS) * gq_ref[:, h * LANE:(h + 1) * LANE]
        qp_ref[:, h * LANE:(h + 1) * LANE] = qn
    zkv = _dot(xn, w_ref[:, _C_KV:_C_GN])
    kvc_ref[...] = zkv[:, 0:2 * D_KV]
    ks = _halfnorm(zkv[:, 2 * D_KV:3 * D_KV], gks_ref[...])
    vs = zkv[:, 3 * D_KV:4 * D_KV]
    kw = _halfnorm(zkv[:, 4 * D_KV:5 * D_KV], gkw_ref[...])
    vw = zkv[:, 5 * D_KV:6 * D_KV]
    kvs_ref[:, 0:D_KV] = ks
    kvs_ref[:, D_KV:] = vs
    kvw_ref[:, 0:D_KV] = kw
    kvw_ref[:, D_KV:] = vw
    kvsb_ref[:, 0:D_KV] = ks.astype(BF16)
    kvsb_ref[:, D_KV:] = vs.astype(BF16)
    kvwb_ref[:, 0:D_KV] = kw.astype(BF16)
    kvwb_ref[:, D_KV:] = vw.astype(BF16)
    gn_ref[...] = jax.nn.sigmoid(_dot(xn, w_ref[:, _C_GN:_C_GA]))
    ga_ref[...] = jax.nn.sigmoid(_dot(xn, w_ref[:, _C_GA:_C_GB]))
    gb_ref[...] = jax.nn.sigmoid(_dot(xn, w_ref[:, _C_GB:_C_END]))


def _mix_in(h, g, w, gq, gks, gkw):
    n = h.shape[0]
    tm = _pick_tile(n, 256)
    row = lambda width: pl.BlockSpec((tm, width), lambda i: (i, 0))
    widths = [(D_SSM, F32), (Q_PAD, F32), (2 * D_KV, F32), (2 * D_KV, F32), (2 * D_KV, F32),
              (2 * D_KV, BF16), (2 * D_KV, BF16), (LANE, F32), (D_MODEL, F32), (D_MODEL, F32)]
    return pl.pallas_call(
        _mix_in_kernel,
        out_shape=[jax.ShapeDtypeStruct((n, wd), dt) for wd, dt in widths],
        grid=(n // tm,),
        in_specs=[row(D_MODEL), _const_spec((1, D_MODEL)), _const_spec(w.shape),
                  _const_spec((1, Q_PAD)), _const_spec((1, LANE)), _const_spec((1, LANE))],
        out_specs=[row(wd) for wd, _ in widths],
        compiler_params=_cparams("parallel"),
    )(h, g, w, gq, gks, gkw)


def _gelu_tanh(x):
    return 0.5 * x * (1.0 + jnp.tanh(0.7978845608028654 * (x + 0.044715 * (x * x * x))))


def _s5_kernel(*refs, seg, carry):
    if carry:
        (u_ref, bm_ref, cr_ref, ci_ref, d_ref, lam_ref, pr_ref, pi_ref,
         y_ref, hr_out, hi_out, car_ref, cai_ref) = refs
    else:
        (u_ref, bm_ref, cr_ref, ci_ref, d_ref, lam_ref, pr_ref, pi_ref, h0r_ref, h0i_ref,
         y_ref, hr_out, hi_out) = refs
    u = u_ref[...]
    rows = u.shape[0]
    x = _dot(u.astype(BF16), bm_ref[...])
    xr = x[:, :N_STATE]
    xi = x[:, N_STATE:]
    pos = lax.broadcasted_iota(jnp.int32, (rows, 1), 0) % seg
    d, k = 1, 0
    while d < seg:
        lr = lam_ref[2 * k:2 * k + 1, :]
        li = lam_ref[2 * k + 1:2 * k + 2, :]
        keep = pos >= d
        sr = jnp.where(keep, pltpu.roll(xr, d, 0), 0.0)
        si = jnp.where(keep, pltpu.roll(xi, d, 0), 0.0)
        xr, xi = xr + lr * sr - li * si, xi + lr * si + li * sr
        d *= 2
        k += 1
    if carry:
        @pl.when(pl.program_id(0) == 0)
        def _():
            car_ref[...] = jnp.zeros_like(car_ref)
            cai_ref[...] = jnp.zeros_like(cai_ref)
        c_r = car_ref[...]
        c_i = cai_ref[...]
    else:
        c_r = h0r_ref[...]
        c_i = h0i_ref[...]
    p_r = pr_ref[...]
    p_i = pi_ref[...]
    hr = xr + p_r * c_r - p_i * c_i
    hi = xi + p_r * c_i + p_i * c_r
    y = _dot(hr.astype(BF16), cr_ref[...]) + _dot(hi.astype(BF16), ci_ref[...]) + d_ref[...] * u
    y_ref[...] = _gelu_tanh(y).astype(BF16)
    if carry:
        car_ref[...] = hr[rows - 1:rows, :]
        cai_ref[...] = hi[rows - 1:rows, :]
        hr_out[...] = hr[rows - 1:rows, :]
        hi_out[...] = hi[rows - 1:rows, :]
    else:
        hr_out[...] = hr
        hi_out[...] = hi


def _s5_prompt(u, t, sp):
    rows = Q_TILE
    consts = [sp["bmat"], sp["cr"], sp["ci"], sp["d"], sp["lam_p"], sp["pr_p"], sp["pi_p"]]
    return pl.pallas_call(
        functools.partial(_s5_kernel, seg=rows, carry=True),
        out_shape=[jax.ShapeDtypeStruct((t, D_SSM), BF16),
                   jax.ShapeDtypeStruct((1, N_STATE), F32), jax.ShapeDtypeStruct((1, N_STATE), F32)],
        grid=(t // rows,),
        in_specs=[pl.BlockSpec((rows, D_SSM), lambda i: (i, 0))] + [_const_spec(c.shape) for c in consts],
        out_specs=[pl.BlockSpec((rows, D_SSM), lambda i: (i, 0)),
                   _const_spec((1, N_STATE)), _const_spec((1, N_STATE))],
        scratch_shapes=[pltpu.VMEM((1, N_STATE), F32), pltpu.VMEM((1, N_STATE), F32)],
        compiler_params=_cparams("arbitrary"),
    )(u, *consts)


def _s5_sample(u, t, ns_rows, nq, h0r, h0i, sp):
    consts = [sp["bmat"], sp["cr"], sp["ci"], sp["d"], sp["lam_s"], sp["pr_s"], sp["pi_s"], h0r, h0i]
    blk = t // ns_rows
    return pl.pallas_call(
        functools.partial(_s5_kernel, seg=nq, carry=False),
        out_shape=[jax.ShapeDtypeStruct((ns_rows, D_SSM), BF16),
                   jax.ShapeDtypeStruct((ns_rows, N_STATE), F32), jax.ShapeDtypeStruct((ns_rows, N_STATE), F32)],
        grid=(1,),
        in_specs=[pl.BlockSpec((ns_rows, D_SSM), lambda i: (blk, 0))] + [_const_spec(c.shape) for c in consts],
        out_specs=[_const_spec((ns_rows, D_SSM)), _const_spec((ns_rows, N_STATE)), _const_spec((ns_rows, N_STATE))],
        compiler_params=_cparams("arbitrary"),
    )(u, *consts)


def _compress_kernel(pt_ref, pages_ref, w0_ref, w1_ref, pe0_ref, pe1_ref, gk_ref,
                     kcc_ref, vcc_ref, buf, sem, a0, a1, *, pps, nsplit, nb):
    s = pl.program_id(0)
    nsteps = pl.num_programs(0)
    rows = pps * 8

    def page_copy(step, p, slot):
        b = step // nsplit
        part = step % nsplit
        pg = pt_ref[b, part * pps + p]
        return pltpu.make_async_copy(pages_ref.at[pg], buf.at[slot, pl.ds(p * 8, 8), :], sem.at[slot])

    def start(step, slot):
        def body(p, c):
            page_copy(step, p, slot).start()
            return c
        lax.fori_loop(0, pps, body, 0)

    def wait(step, slot):
        def body(p, c):
            page_copy(step, p, slot).wait()
            return c
        lax.fori_loop(0, pps, body, 0)

    @pl.when(s == 0)
    def _():
        start(s, 0)

    @pl.when(s + 1 < nsteps)
    def _():
        start(s + 1, (s + 1) % 2)

    slot = s % 2
    wait(s, slot)
    part = s % nsplit
    sub = min(rows, 256)
    for c in range(rows // sub):
        x = buf[slot, c * sub:(c + 1) * sub, :]
        base = pl.multiple_of(part * rows + c * sub, sub)
        a0[pl.ds(base, sub), :] = _dot((x + pe0_ref[...]).astype(BF16), w0_ref[...])
        a1[pl.ds(base, sub), :] = _dot((x + pe1_ref[...]).astype(BF16), w1_ref[...])

    @pl.when(part == nsplit - 1)
    def _():
        out = a0[...] + pltpu.roll(a1[...], nb - 1, 0)
        valid = lax.broadcasted_iota(jnp.int32, (nb, 1), 0) < nb - 1
        out = jnp.where(valid, out, 0.0)
        kcc_ref[0] = _halfnorm(out[:, 0:D_KV], gk_ref[...]).astype(BF16)
        vcc_ref[0] = out[:, D_KV:].astype(BF16)


def _compress(page_table, pages, cw):
    bsz, n_pages = page_table.shape
    nb = n_pages * 8
    nsplit = 2 if n_pages % 2 == 0 else 1
    pps = n_pages // nsplit
    grid_spec = pltpu.PrefetchScalarGridSpec(
        num_scalar_prefetch=1,
        grid=(bsz * nsplit,),
        in_specs=[pl.BlockSpec(memory_space=pl.ANY),
                  pl.BlockSpec((CMP_ROW, 2 * D_KV), lambda s, pt: (0, 0)),
                  pl.BlockSpec((CMP_ROW, 2 * D_KV), lambda s, pt: (0, 0)),
                  pl.BlockSpec((1, CMP_ROW), lambda s, pt: (0, 0)),
                  pl.BlockSpec((1, CMP_ROW), lambda s, pt: (0, 0)),
                  pl.BlockSpec((1, LANE), lambda s, pt: (0, 0))],
        out_specs=[pl.BlockSpec((1, nb, D_KV), lambda s, pt: (s // nsplit, 0, 0)),
                   pl.BlockSpec((1, nb, D_KV), lambda s, pt: (s // nsplit, 0, 0))],
        scratch_shapes=[pltpu.VMEM((2, pps * 8, CMP_ROW), F32), pltpu.SemaphoreType.DMA((2,)),
                        pltpu.VMEM((nb, 2 * D_KV), F32), pltpu.VMEM((nb, 2 * D_KV), F32)],
    )
    return pl.pallas_call(
        functools.partial(_compress_kernel, pps=pps, nsplit=nsplit, nb=nb),
        out_shape=[jax.ShapeDtypeStruct((bsz, nb, D_KV), BF16), jax.ShapeDtypeStruct((bsz, nb, D_KV), BF16)],
        grid_spec=grid_spec,
        compiler_params=_cparams("arbitrary"),
    )(page_table, pages, cw["w0"], cw["w1"], cw["pe0"], cw["pe1"], cw["gk"])


def _heads_rows(qp):
    return jnp.concatenate([qp[:, h * LANE:(h + 1) * LANE] for h in range(N_HEADS)], axis=0).astype(BF16)


def _slope(h):
    return 2.0 ** (-8.0 * (h + 1) / N_HEADS)


def _branch(q_all, tq, pieces):
    s_all = [_dot_nt(q_all, k) for k, _, _, _ in pieces]
    probs = []
    for h in range(N_HEADS):
        rows = slice(h * tq, (h + 1) * tq)
        sm = [jnp.where(mask, s[rows] - _slope(h) * dist, NEG) for s, (_, _, dist, mask) in zip(s_all, pieces)]
        m = functools.reduce(jnp.maximum, [jnp.max(x, axis=-1, keepdims=True) for x in sm])
        e = [jnp.where(mask, jnp.exp(x - m), 0.0) for x, (_, _, _, mask) in zip(sm, pieces)]
        l = functools.reduce(lambda a, b: a + b, [jnp.sum(x, axis=-1, keepdims=True) for x in e])
        inv = 1.0 / jnp.maximum(l, 1e-30)
        probs.append([x * inv for x in e])
    o = None
    for i, (_, v, _, _) in enumerate(pieces):
        p = jnp.concatenate([probs[h][i] for h in range(N_HEADS)], axis=0).astype(BF16)
        t = _dot(p, v)
        o = t if o is None else o + t
    return o, probs


def _cmp_win_topk(q_all, tq, qpos, gn, kcc, vcc, mband, win_pieces, nb, ns, topn):
    n_io = lax.broadcasted_iota(jnp.int32, (1, nb), 1)
    kc_end = n_io * CMP_STRIDE + (CMP_BLOCK - 1)
    dist_c = (qpos - kc_end).astype(F32)
    mask_c = (dist_c >= 0) & (n_io < nb - 1)
    o_c, probs = _branch(q_all, tq, [(kcc, vcc, dist_c, mask_c)])
    o_w, _ = _branch(q_all, tq, win_pieces)
    imps = []
    for g in range(N_KV):
        psum = probs[HPG * g][0]
        for hh in range(1, HPG):
            psum = psum + probs[HPG * g + hh][0]
        hi = psum.astype(BF16)
        r = psum - hi.astype(F32)
        mid = r.astype(BF16)
        lo = (r - mid.astype(F32)).astype(BF16)
        imps.append(_dot(hi, mband) + _dot(mid, mband) + _dot(lo, mband))
    imp = jnp.concatenate(imps, axis=0)
    blk = lax.broadcasted_iota(jnp.int32, (1, ns), 1)
    cur = jnp.concatenate([jnp.right_shift(qpos, 6)] * N_KV, axis=0)
    forced = (blk == 0) | (blk == cur) | (blk == cur - 1)
    score = jnp.where(forced, imp + SEL_BIG, jnp.where(blk <= cur, imp, -SEL_BIG))
    blkf = blk.astype(F32)
    sel = jnp.zeros_like(score)
    for _ in range(topn):
        m = jnp.max(score, axis=-1, keepdims=True)
        first = jnp.min(jnp.where(score == m, blkf, float(ns)), axis=-1, keepdims=True)
        hit = blkf == first
        sel = jnp.where(hit, 1.0, sel)
        score = jnp.where(hit, -jnp.inf, score)
    sel = jnp.where(blk <= cur, sel, 0.0)
    outs = []
    for h in range(N_HEADS):
        rows = slice(h * tq, (h + 1) * tq)
        outs.append(gn[:, h:h + 1] * o_c[rows] + gn[:, 2 * N_HEADS + h:2 * N_HEADS + h + 1] * o_w[rows])
    return jnp.concatenate(outs, axis=1), sel


def _e1_prompt_kernel(qp_ref, gn_ref, kcc_ref, vcc_ref, mband_ref, kvw_ref, ocw_ref, sel_ref, flag_ref,
                      *, nb, ns):
    tq = Q_TILE
    s0 = pl.program_id(0) * tq
    qpos = s0 + lax.broadcasted_iota(jnp.int32, (tq, 1), 0)
    q_all = _heads_rows(qp_ref[...])
    start = pl.multiple_of(jnp.maximum(s0 - WINDOW, 0), Q_TILE)
    kw = kvw_ref[pl.ds(start, WIN_KEYS), 0:D_KV]
    vw = kvw_ref[pl.ds(start, WIN_KEYS), D_KV:]
    kpos = start + lax.broadcasted_iota(jnp.int32, (1, WIN_KEYS), 1)
    dist_w = (qpos - kpos).astype(F32)
    mask_w = (dist_w >= 0) & (dist_w < WINDOW)
    ocw, sel = _cmp_win_topk(q_all, tq, qpos, gn_ref[...], kcc_ref[0], vcc_ref[0], mband_ref[...],
                             [(kw, vw, dist_w, mask_w)], nb, ns, TOP_N)
    ocw_ref[...] = ocw
    sel_ref[0] = sel[:tq]
    sel_ref[1] = sel[tq:]
    colany = jnp.max(sel, axis=0, keepdims=True)
    j_io = lax.broadcasted_iota(jnp.int32, (ns, LANE), 0)
    c_io = lax.broadcasted_iota(jnp.int32, (ns, LANE), 1)
    grp = jnp.where(jnp.right_shift(j_io, 2) == c_io, 1.0, 0.0).astype(BF16)
    cnt = _dot(jnp.broadcast_to(colany, (8, ns)).astype(BF16), grp)
    flag_ref[0] = (cnt > 0.5).astype(jnp.int32)


def _e1_prompt(qp, gn, kcc, vcc, mband, kvw_bf, t):
    nb, ns = mband.shape
    nt = t // Q_TILE
    row = lambda w: pl.BlockSpec((Q_TILE, w), lambda i: (i, 0))
    return pl.pallas_call(
        functools.partial(_e1_prompt_kernel, nb=nb, ns=ns),
        out_shape=[jax.ShapeDtypeStruct((t, Q_PAD), F32), jax.ShapeDtypeStruct((N_KV, t, ns), F32),
                   jax.ShapeDtypeStruct((nt, 8, LANE), jnp.int32)],
        grid=(nt,),
        in_specs=[row(Q_PAD), row(LANE), _const_spec((1, nb, D_KV)), _const_spec((1, nb, D_KV)),
                  _const_spec(mband.shape), _const_spec(kvw_bf.shape)],
        out_specs=[row(Q_PAD), pl.BlockSpec((N_KV, Q_TILE, ns), lambda i: (0, i, 0)),
                   pl.BlockSpec((1, 8, LANE), lambda i: (i, 0, 0))],
        compiler_params=_cparams("parallel"),
    )(qp, gn, kcc, vcc, mband, kvw_bf)


def _pad_rows(x, rows):
    return jnp.concatenate([x, jnp.zeros((rows - x.shape[0], x.shape[1]), x.dtype)], axis=0)


def _e1_sample_kernel(qp_ref, gn_ref, kcc_ref, vcc_ref, mband_ref, cwin_ref, nwin_ref, ocw_ref, sel_ref,
                      *, nb, ns, nq, past):
    qi = lax.broadcasted_iota(jnp.int32, (nq, 1), 0)
    qpos = past + qi
    q_all = _heads_rows(qp_ref[...])
    cw = cwin_ref[0]
    w_buf = cw.shape[0]
    j_c = lax.broadcasted_iota(jnp.int32, (1, w_buf), 1)
    dist_cw = (qi + (w_buf - j_c)).astype(F32)
    mask_cw = (dist_cw >= 0) & (dist_cw < WINDOW)
    nw = _pad_rows(nwin_ref[...], LANE)
    j_n = lax.broadcasted_iota(jnp.int32, (1, LANE), 1)
    dist_nw = (qi - j_n).astype(F32)
    mask_nw = (dist_nw >= 0) & (j_n < nq)
    pieces = [(cw[:, 0:D_KV].astype(BF16), cw[:, D_KV:].astype(BF16), dist_cw, mask_cw),
              (nw[:, 0:D_KV].astype(BF16), nw[:, D_KV:].astype(BF16), dist_nw, mask_nw)]
    ocw, sel = _cmp_win_topk(q_all, nq, qpos, gn_ref[...], kcc_ref[0], vcc_ref[0], mband_ref[...],
                             pieces, nb, ns, TOP_N - 1)
    ocw_ref[...] = ocw
    sel_ref[0, 0] = sel[:nq]
    sel_ref[0, 1] = sel[nq:]


def _e1_sample(qp, gn, kcc, vcc, mband, cache_win, kv_win, t, bsz, nq, past):
    nb, ns = mband.shape
    w_buf = cache_win.shape[1]
    off = t // nq
    row = lambda w: pl.BlockSpec((nq, w), lambda b: (off + b, 0))
    per_b = lambda shape: pl.BlockSpec((1,) + shape, lambda b: (b,) + (0,) * len(shape))
    return pl.pallas_call(
        functools.partial(_e1_sample_kernel, nb=nb, ns=ns, nq=nq, past=past),
        out_shape=[jax.ShapeDtypeStruct((bsz * nq, Q_PAD), F32), jax.ShapeDtypeStruct((bsz, N_KV, nq, ns), F32)],
        grid=(bsz,),
        in_specs=[row(Q_PAD), row(LANE), per_b((nb, D_KV)), per_b((nb, D_KV)), _const_spec(mband.shape),
                  per_b((w_buf, 2 * D_KV)), row(2 * D_KV)],
        out_specs=[pl.BlockSpec((nq, Q_PAD), lambda b: (b, 0)), per_b((N_KV, nq, ns))],
        compiler_params=_cparams("parallel"),
    )(qp, gn, kcc, vcc, mband, cache_win, kv_win)


def _online_step(s, dist, masks, v, m_ref, l_ref, acc_ref, tq):
    ps, alphas = [], []
    for h in range(N_HEADS):
        rows = slice(h * tq, (h + 1) * tq)
        mask = masks[h // HPG]
        sm = jnp.where(mask, s[rows] - _slope(h) * dist, NEG)
        m_old = m_ref[rows, :]
        m_new = jnp.maximum(m_old, jnp.max(sm, axis=-1, keepdims=True))
        alpha = jnp.exp(m_old - m_new)
        p = jnp.where(mask, jnp.exp(sm - m_new), 0.0)
        l_ref[rows, :] = alpha * l_ref[rows, :] + jnp.sum(p, axis=-1, keepdims=True)
        m_ref[rows, :] = m_new
        ps.append(p)
        alphas.append(alpha)
    p_all = jnp.concatenate(ps, axis=0).astype(BF16)
    a_all = jnp.concatenate(alphas, axis=0)
    acc_ref[...] = a_all * acc_ref[...] + _dot(p_all, v)


def _expand_sel(sel, first_blk, n_keys):
    ns = sel.shape[1]
    j_io = lax.broadcasted_iota(jnp.int32, (ns, n_keys), 0)
    kb = first_blk + jnp.right_shift(lax.broadcasted_iota(jnp.int32, (ns, n_keys), 1), 6)
    e = jnp.where(j_io == kb, 1.0, 0.0).astype(BF16)
    return _dot(sel.astype(BF16), e) > 0.5


def _init_online(m_ref, l_ref, acc_ref):
    m_ref[...] = jnp.full(m_ref.shape, NEG, F32)
    l_ref[...] = jnp.zeros(l_ref.shape, F32)
    acc_ref[...] = jnp.zeros(acc_ref.shape, F32)


def _finish_online(ocw, gn, l_ref, acc_ref, tq):
    o_s = acc_ref[...] * (1.0 / jnp.maximum(l_ref[...], 1e-30))
    outs = []
    for h in range(N_HEADS):
        g1 = gn[:, N_HEADS + h:N_HEADS + h + 1]
        outs.append(ocw[:, h * LANE:(h + 1) * LANE] + g1 * o_s[h * tq:(h + 1) * tq])
    return jnp.concatenate(outs, axis=1)


def _e2_prompt_kernel(flags_ref, qp_ref, sel_ref, gn_ref, ocw_ref, kvs_ref, out_ref, m_ref, l_ref, acc_ref,
                      *, nchunks):
    tq = Q_TILE
    i = pl.program_id(0)
    s0 = i * tq
    qpos = s0 + lax.broadcasted_iota(jnp.int32, (tq, 1), 0)
    q_all = _heads_rows(qp_ref[...])
    _init_online(m_ref, l_ref, acc_ref)

    def body(c, carry):
        @pl.when(flags_ref[i * nchunks + c] > 0)
        def _():
            base = pl.multiple_of(c * SEL_CHUNK, SEL_CHUNK)
            k = kvs_ref[pl.ds(base, SEL_CHUNK), 0:D_KV]
            v = kvs_ref[pl.ds(base, SEL_CHUNK), D_KV:]
            s = _dot_nt(q_all, k)
            kpos = base + lax.broadcasted_iota(jnp.int32, (1, SEL_CHUNK), 1)
            dist = (qpos - kpos).astype(F32)
            first_blk = c * (SEL_CHUNK // SEL_BLOCK)
            masks = [_expand_sel(sel_ref[g], first_blk, SEL_CHUNK) & (dist >= 0) for g in range(N_KV)]
            _online_step(s, dist, masks, v, m_ref, l_ref, acc_ref, tq)
        return carry

    lax.fori_loop(0, s0 // SEL_CHUNK + 1, body, 0)
    out_ref[...] = _finish_online(ocw_ref[...], gn_ref[...], l_ref, acc_ref, tq)


def _e2_prompt(flags, qp, sel, gn, ocw, kvs_bf, t):
    ns = sel.shape[2]
    nt = t // Q_TILE
    nchunks = flags.shape[0] // nt
    row = lambda w: pl.BlockSpec((Q_TILE, w), lambda i, f: (i, 0))
    grid_spec = pltpu.PrefetchScalarGridSpec(
        num_scalar_prefetch=1,
        grid=(nt,),
        in_specs=[row(Q_PAD), pl.BlockSpec((N_KV, Q_TILE, ns), lambda i, f: (0, i, 0)), row(LANE), row(Q_PAD),
                  pl.BlockSpec(kvs_bf.shape, lambda i, f: (0, 0))],
        out_specs=row(Q_PAD),
        scratch_shapes=[pltpu.VMEM((N_HEADS * Q_TILE, 1), F32), pltpu.VMEM((N_HEADS * Q_TILE, 1), F32),
                        pltpu.VMEM((N_HEADS * Q_TILE, LANE), F32)],
    )
    return pl.pallas_call(
        functools.partial(_e2_prompt_kernel, nchunks=nchunks),
        out_shape=jax.ShapeDtypeStruct((t, Q_PAD), F32),
        grid_spec=grid_spec,
        compiler_params=_cparams("arbitrary"),
    )(flags, qp, sel, gn, ocw, kvs_bf)


def _e2_sample_kernel(pt_ref, pages_ref, qp_ref, sel_ref, gn_ref, ocw_ref, nsel_ref, out_ref,
                      buf, sem, m_ref, l_ref, acc_ref, *, pps, nsplit, nq):
    s = pl.program_id(0)
    nsteps = pl.num_programs(0)
    n_keys = pps * PAGE_SIZE

    def page_copy(step, p, slot):
        b = step // nsplit
        part = step % nsplit
        pg = pt_ref[b, part * pps + p]
        return pltpu.make_async_copy(pages_ref.at[pg], buf.at[slot, pl.ds(p * PAGE_SIZE, PAGE_SIZE), :],
                                     sem.at[slot])

    def start(step, slot):
        def body(p, c):
            page_copy(step, p, slot).start()
            return c
        lax.fori_loop(0, pps, body, 0)

    def wait(step, slot):
        def body(p, c):
            page_copy(step, p, slot).wait()
            return c
        lax.fori_loop(0, pps, body, 0)

    @pl.when(s == 0)
    def _():
        start(s, 0)

    @pl.when(s + 1 < nsteps)
    def _():
        start(s + 1, (s + 1) % 2)

    slot = s % 2
    part = s % nsplit
    qi = lax.broadcasted_iota(jnp.int32, (nq, 1), 0)
    q_all = _heads_rows(qp_ref[...])

    @pl.when(part == 0)
    def _():
        _init_online(m_ref, l_ref, acc_ref)

    wait(s, slot)
    kv = buf[slot]
    k = kv[:, 0:D_KV].astype(BF16)
    v = kv[:, D_KV:].astype(BF16)
    sc = _dot_nt(q_all, k)
    back = (nsplit - part) * n_keys - lax.broadcasted_iota(jnp.int32, (1, n_keys), 1)
    dist = (qi + back).astype(F32)
    first_blk = part * (n_keys // SEL_BLOCK)
    masks = [_expand_sel(sel_ref[0, g], first_blk, n_keys) for g in range(N_KV)]
    _online_step(sc, dist, masks, v, m_ref, l_ref, acc_ref, nq)

    @pl.when(part == nsplit - 1)
    def _():
        nw = _pad_rows(nsel_ref[...], LANE)
        j_n = lax.broadcasted_iota(jnp.int32, (1, LANE), 1)
        dist_n = (qi - j_n).astype(F32)
        mask_n = (dist_n >= 0) & (j_n < nq)
        s_n = _dot_nt(q_all, nw[:, 0:D_KV].astype(BF16))
        _online_step(s_n, dist_n, [mask_n, mask_n], nw[:, D_KV:].astype(BF16), m_ref, l_ref, acc_ref, nq)
        out_ref[...] = _finish_online(ocw_ref[...], gn_ref[...], l_ref, acc_ref, nq)


def _e2_sample(page_table, pages, qp, sel, gn, ocw, kv_sel, t, nq):
    bsz, n_pages = page_table.shape
    ns = sel.shape[3]
    nsplit = 4 if n_pages % 4 == 0 else 1
    pps = n_pages // nsplit
    off = t // nq
    row = lambda w: pl.BlockSpec((nq, w), lambda s, pt: (off + s // nsplit, 0))
    grid_spec = pltpu.PrefetchScalarGridSpec(
        num_scalar_prefetch=1,
        grid=(bsz * nsplit,),
        in_specs=[pl.BlockSpec(memory_space=pl.ANY), row(Q_PAD),
                  pl.BlockSpec((1, N_KV, nq, ns), lambda s, pt: (s // nsplit, 0, 0, 0)), row(LANE),
                  pl.BlockSpec((nq, Q_PAD), lambda s, pt: (s // nsplit, 0)), row(2 * D_KV)],
        out_specs=pl.BlockSpec((nq, Q_PAD), lambda s, pt: (s // nsplit, 0)),
        scratch_shapes=[pltpu.VMEM((2, pps * PAGE_SIZE, 2 * D_KV), F32), pltpu.SemaphoreType.DMA((2,)),
                        pltpu.VMEM((N_HEADS * nq, 1), F32), pltpu.VMEM((N_HEADS * nq, 1), F32),
                        pltpu.VMEM((N_HEADS * nq, LANE), F32)],
    )
    return pl.pallas_call(
        functools.partial(_e2_sample_kernel, pps=pps, nsplit=nsplit, nq=nq),
        out_shape=jax.ShapeDtypeStruct((bsz * nq, Q_PAD), F32),
        grid_spec=grid_spec,
        compiler_params=_cparams("arbitrary"),
    )(page_table, pages, qp, sel, gn, ocw, kv_sel)


def _post_kernel(h_ref, yg_ref, oatt_ref, ga_ref, gb_ref, wglu_ref, watt_ref, wout_ref, o_ref):
    gl = _dot(yg_ref[...], wglu_ref[...])
    br_a = gl[:, :D_MODEL] * jax.nn.sigmoid(gl[:, D_MODEL:])
    br_b = _dot(oatt_ref[...].astype(BF16), watt_ref[...])
    merged = (ga_ref[...] * br_a + gb_ref[...] * br_b).astype(BF16)
    o_ref[...] = h_ref[...] + _dot(merged, wout_ref[...])


def _post(h, yg, oatt, ga, gb, wglu, watt, wout):
    n = h.shape[0]
    tm = _pick_tile(n, 256)
    row = lambda w: pl.BlockSpec((tm, w), lambda i: (i, 0))
    return pl.pallas_call(
        _post_kernel,
        out_shape=jax.ShapeDtypeStruct((n, D_MODEL), F32),
        grid=(n // tm,),
        in_specs=[row(D_MODEL), row(D_SSM), row(Q_PAD), row(D_MODEL), row(D_MODEL),
                  _const_spec(wglu.shape), _const_spec(watt.shape), _const_spec(wout.shape)],
        out_specs=row(D_MODEL),
        compiler_params=_cparams("parallel"),
    )(h, yg, oatt, ga, gb, wglu, watt, wout)


def _head_pad_index():
    h = jnp.arange(N_HEADS)[:, None]
    d = jnp.arange(HEAD_DIM)[None, :]
    return (LANE * h + HEAD_DIM * (h // HPG) + d).reshape(-1)


def _prep_mix_weights(w_in, qk_norm):
    idx = _head_pad_index()
    wq = jnp.zeros((D_MODEL, Q_PAD), F32).at[:, idx].set(w_in[:, D_SSM:D_SSM + N_HEADS * HEAD_DIM])
    c0 = D_SSM + N_HEADS * HEAD_DIM
    c1 = c0 + 6 * D_KV
    c2 = c1 + 3 * N_HEADS
    wgn = jnp.zeros((D_MODEL, LANE), F32).at[:, :3 * N_HEADS].set(w_in[:, c1:c2])
    w = jnp.concatenate([w_in[:, :D_SSM], wq, w_in[:, c0:c1], wgn, w_in[:, c2:]], axis=1).astype(BF16)
    gq = jnp.zeros((Q_PAD,), F32).at[idx].set(jnp.tile(qk_norm[0] * (HEAD_DIM ** -0.5), N_HEADS))[None]
    gks = jnp.tile(qk_norm[2], N_KV)[None]
    gkw = jnp.tile(qk_norm[3], N_KV)[None]
    return w, gq, gks, gkw


def _prep_att_out(w_att_out):
    return jnp.zeros((Q_PAD, D_MODEL), F32).at[_head_pad_index()].set(w_att_out).astype(BF16)


def _prep_compress(pe_k, pe_v, w_k, w_v, gain_k):
    r = CMP_BLOCK // CMP_STRIDE
    eye = jnp.eye(N_KV, dtype=F32)
    out = {}
    for j in range(r):
        wk = w_k[j * CMP_STRIDE:(j + 1) * CMP_STRIDE]
        wv = w_v[j * CMP_STRIDE:(j + 1) * CMP_STRIDE]
        z = jnp.zeros((CMP_STRIDE, N_KV, HEAD_DIM, N_KV, HEAD_DIM), F32)
        bk = jnp.einsum("sde,gh->sgdhe", wk, eye)
        bv = jnp.einsum("sde,gh->sgdhe", wv, eye)
        top = jnp.concatenate([bk, z], axis=3)
        bot = jnp.concatenate([z, bv], axis=3)
        w = jnp.stack([top, bot], axis=1)
        out[f"w{j}"] = w.reshape(CMP_ROW, 2 * D_KV).astype(BF16)
        pk = pe_k[j * CMP_STRIDE:(j + 1) * CMP_STRIDE]
        pv = pe_v[j * CMP_STRIDE:(j + 1) * CMP_STRIDE]
        pe = jnp.stack([jnp.broadcast_to(pk[:, None, :], (CMP_STRIDE, N_KV, HEAD_DIM)),
                        jnp.broadcast_to(pv[:, None, :], (CMP_STRIDE, N_KV, HEAD_DIM))], axis=1)
        out[f"pe{j}"] = pe.reshape(1, CMP_ROW)
    out["gk"] = jnp.tile(gain_k, N_KV)[None]
    return out


def _prep_s5(a_re, a_im, log_dt, b_re, b_im, c_re, c_im, d, nq):
    dt = jnp.exp(log_dt)[:, None]
    mag = jnp.exp(a_re * dt)
    lr = mag * jnp.cos(a_im * dt)
    li = mag * jnp.sin(a_im * dt)
    den = a_re * a_re + a_im * a_im
    fr = ((lr - 1.0) * a_re + li * a_im) / den
    fi = (li * a_re - (lr - 1.0) * a_im) / den
    bbr = fr[..., None] * b_re - fi[..., None] * b_im
    bbi = fr[..., None] * b_im + fi[..., None] * b_re
    eye = jnp.eye(N_SSM_GROUPS, dtype=F32)
    blk_b = lambda m: jnp.einsum("gpc,gh->gchp", m, eye).reshape(D_SSM, N_STATE)
    blk_c = lambda m: jnp.einsum("gcp,gh->gphc", m, eye).reshape(N_STATE, D_SSM)

    def lam_pow(k):
        kk = k.astype(F32)[:, None, None]
        m = jnp.exp(a_re * dt * kk)
        th = a_im * dt * kk
        return (m * jnp.cos(th)).reshape(-1, N_STATE), (m * jnp.sin(th)).reshape(-1, N_STATE)

    def step_table(seg):
        ks = []
        dd = 1
        while dd < seg:
            ks.append(dd)
            dd *= 2
        re, im = lam_pow(jnp.array(ks))
        return jnp.stack([re, im], axis=1).reshape(-1, N_STATE)

    sp = {"bmat": jnp.concatenate([blk_b(bbr), blk_b(bbi)], axis=1).astype(BF16),
          "cr": blk_c(c_re).astype(BF16), "ci": (-blk_c(c_im)).astype(BF16), "d": d[None],
          "lam_p": step_table(Q_TILE), "lam_s": step_table(nq)}
    sp["pr_p"], sp["pi_p"] = lam_pow(jnp.arange(Q_TILE) + 1)
    return sp, lam_pow


def _band_matrix(nb, ns):
    ratio = SEL_BLOCK // CMP_STRIDE
    lo = CMP_BLOCK // CMP_STRIDE - 1
    c = jnp.arange(nb)[:, None]
    j = jnp.arange(ns)[None, :]
    return ((c >= ratio * j - lo) & (c <= ratio * j + ratio - 1)).astype(BF16)


def kernel(x_prompt, x_sample, cache_kv_cmp, cache_kv_sel, cache_kv_win, state_ssm_re, state_ssm_im, page_table, p_prompt, p_sample, norm_ffn1, w_ffn1_in, w_ffn1_out, norm_mix, w_in, qk_norm, ssm_a_re, ssm_a_im, ssm_log_dt, ssm_b_re, ssm_b_im, ssm_c_re, ssm_c_im, ssm_d, w_glu, cmp_pe_k, cmp_pe_v, cmp_w_k, cmp_w_v, w_att_out, w_out, norm_ffn2, w_ffn2_in, w_ffn2_out, norm_ple, w_ple_gate, w_ple_proj):
    bp, t = x_prompt.shape[:2]
    bsz, nq = x_sample.shape[:2]
    n_pages = page_table.shape[1]
    past = n_pages * PAGE_SIZE
    n_pool = cache_kv_cmp.shape[1]
    w_buf = cache_kv_win.shape[2]
    ns_rows = bsz * nq
    assert bp == 1 and t % SEL_CHUNK == 0 and t >= WIN_KEYS and t % ns_rows == 0 and nq < CMP_STRIDE
    assert past == t and w_buf == WINDOW
    nb = t // CMP_STRIDE
    ns = t // SEL_BLOCK
    nt = t // Q_TILE
    nchunks = t // SEL_CHUNK
    mband = _band_matrix(nb, ns)
    prompt_table = jnp.arange(t // PAGE_SIZE, dtype=jnp.int32)[None]
    page_table = page_table.astype(jnp.int32)

    h = jnp.concatenate([x_prompt[0], x_sample.reshape(ns_rows, D_MODEL)], axis=0)
    st_p = [[] for _ in range(5)]
    st_s = [[] for _ in range(5)]
    for i in range(DEPTH):
        row1 = lambda a: a[i][None]
        h = _ffn(h, row1(norm_ffn1), w_ffn1_in[i].astype(BF16), w_ffn1_out[i].astype(BF16))
        w_mix, gq, gks, gkw = _prep_mix_weights(w_in[i], qk_norm[i])
        u, qp, kv_cmp, kv_sel, kv_win, kvs_bf, kvw_bf, gn, ga, gb = _mix_in(h, row1(norm_mix), w_mix, gq, gks, gkw)

        sp, lam_pow = _prep_s5(ssm_a_re[i], ssm_a_im[i], ssm_log_dt[i], ssm_b_re[i], ssm_b_im[i],
                               ssm_c_re[i], ssm_c_im[i], ssm_d[i], nq)
        sp["pr_s"], sp["pi_s"] = lam_pow(jnp.arange(ns_rows) % nq + 1)
        yg_p, hr_p, hi_p = _s5_prompt(u, t, sp)
        h0r = jnp.repeat(state_ssm_re[i].reshape(bsz, N_STATE), nq, axis=0)
        h0i = jnp.repeat(state_ssm_im[i].reshape(bsz, N_STATE), nq, axis=0)
        yg_s, hr_s, hi_s = _s5_sample(u, t, ns_rows, nq, h0r, h0i, sp)

        cw = _prep_compress(cmp_pe_k[i], cmp_pe_v[i], cmp_w_k[i], cmp_w_v[i], qk_norm[i, 1])
        kcc_p, vcc_p = _compress(prompt_table, kv_cmp[:t].reshape(t // PAGE_SIZE, 8, CMP_ROW), cw)
        kcc_s, vcc_s = _compress(page_table, cache_kv_cmp[i].reshape(n_pool, 8, CMP_ROW), cw)
        ocw_p, sel_p, flags = _e1_prompt(qp, gn, kcc_p, vcc_p, mband, kvw_bf, t)
        win_cache = cache_kv_win[i].reshape(bsz, w_buf, 2 * D_KV)
        ocw_s, sel_s = _e1_sample(qp, gn, kcc_s, vcc_s, mband, win_cache, kv_win, t, bsz, nq, past)
        flags = flags[:, 0, :nchunks].reshape(-1)
        oatt_p = _e2_prompt(flags, qp, sel_p, gn, ocw_p, kvs_bf, t)
        oatt_s = _e2_sample(page_table, cache_kv_sel[i].reshape(n_pool, PAGE_SIZE, 2 * D_KV),
                            qp, sel_s, gn, ocw_s, kv_sel, t, nq)

        h = _post(h, jnp.concatenate([yg_p, yg_s], axis=0), jnp.concatenate([oatt_p, oatt_s], axis=0), ga, gb,
                  w_glu[i].astype(BF16), _prep_att_out(w_att_out[i]), w_out[i].astype(BF16))
        p_all = jnp.concatenate([p_prompt[i, 0], p_sample[i].reshape(ns_rows, D_PLE)], axis=0)
        h = _ffn(h, row1(norm_ffn2), w_ffn2_in[i].astype(BF16), w_ffn2_out[i].astype(BF16),
                 (p_all, row1(norm_ple), w_ple_gate[i].astype(BF16), w_ple_proj[i].astype(BF16)))

        kv5 = lambda a, lead: a.reshape(lead + (2, N_KV, HEAD_DIM))
        st_p[0].append(kv5(kv_cmp[:t], (1, t)))
        st_p[1].append(kv5(kv_sel[:t], (1, t)))
        st_p[2].append(kv5(kv_win[t - min(WINDOW, t):t], (1, min(WINDOW, t))))
        st_p[3].append(hr_p.reshape(1, N_SSM_GROUPS, P_STATE))
        st_p[4].append(hi_p.reshape(1, N_SSM_GROUPS, P_STATE))
        st_s[0].append(kv5(kv_cmp[t:], (bsz, nq)))
        st_s[1].append(kv5(kv_sel[t:], (bsz, nq)))
        new_win = jnp.concatenate([win_cache, kv_win[t:].reshape(bsz, nq, 2 * D_KV)], axis=1)[:, nq:]
        st_s[2].append(kv5(new_win, (bsz, w_buf)))
        st_s[3].append(hr_s[nq - 1::nq].reshape(bsz, N_SSM_GROUPS, P_STATE))
        st_s[4].append(hi_s[nq - 1::nq].reshape(bsz, N_SSM_GROUPS, P_STATE))

    outs_p = [jnp.stack(a) for a in st_p]
    outs_s = [jnp.stack(a) for a in st_s]
    y_prompt = h[:t][None]
    y_sample = h[t:].reshape(bsz, nq, D_MODEL)
    return (y_prompt, y_sample, *outs_p, *outs_s)
```

```python
import functools

import jax
import jax.numpy as jnp
from jax import lax
from jax.experimental import pallas as pl
from jax.experimental.pallas import tpu as pltpu

F32 = jnp.float32
BF16 = jnp.bfloat16

D_MODEL = 1024
DEPTH = 2
D_SSM = 512
SSM_GROUP = 16
N_SSM_GROUPS = 32
P_STATE = 64
N_STATE = N_SSM_GROUPS * P_STATE
N_HEADS = 8
HEAD_DIM = 64
N_KV = 2
HPG = 4
D_KV = 128
CMP_BLOCK = 32
CMP_STRIDE = 16
SEL_BLOCK = 64
TOP_N = 16
WINDOW = 512
PAGE_SIZE = 128
SEL_BIG = 1e4
D_FF = 2816
D_PLE = 256
RMS_EPS = 1e-6
NEG = -1e30

LANE = 128
Q_TILE = 128
SEL_CHUNK = 256
WIN_KEYS = WINDOW + Q_TILE
Q_PAD = N_HEADS * LANE
VMEM_LIMIT = 56 * 2 ** 20


def _cparams(*sem):
    return pltpu.CompilerParams(dimension_semantics=sem, vmem_limit_bytes=VMEM_LIMIT)


def _dot(a, b):
    return jnp.dot(a, b, preferred_element_type=F32)


def _dot_nt(a, b):
    return lax.dot_general(a, b, (((1,), (1,)), ((), ())), preferred_element_type=F32)


def _pick_tile(n, target):
    for t in range(min(n, target), 15, -1):
        if n % t == 0 and t % 16 == 0:
            return t
    raise ValueError(f"no row tile for {n}")


def _const_spec(shape):
    nd = len(shape)
    return pl.BlockSpec(shape, lambda *_: (0,) * nd, pipeline_mode=pl.Buffered(1))


def _rms(x, g):
    ms = jnp.mean(x * x, axis=-1, keepdims=True)
    return x * lax.rsqrt(ms + RMS_EPS) * g


def _halfnorm(x, gain):
    lo = lax.broadcasted_iota(jnp.int32, (1, LANE), 1) < HEAD_DIM
    x2 = x * x
    s_lo = jnp.sum(jnp.where(lo, x2, 0.0), axis=-1, keepdims=True)
    s_hi = jnp.sum(jnp.where(lo, 0.0, x2), axis=-1, keepdims=True)
    ms = jnp.where(lo, s_lo, s_hi) * (1.0 / HEAD_DIM)
    return x * lax.rsqrt(ms + RMS_EPS) * gain


def _ffn_kernel(*refs, ple):
    if ple:
        h_ref, g_ref, wi_ref, wo_ref, p_ref, gp_ref, wg_ref, wp_ref, o_ref = refs
    else:
        h_ref, g_ref, wi_ref, wo_ref, o_ref = refs
    h = h_ref[...]
    xn = _rms(h, g_ref[...]).astype(BF16)
    a = _dot(xn, wi_ref[:, :D_FF])
    b = _dot(xn, wi_ref[:, D_FF:])
    act = (a * jax.nn.sigmoid(a) * b).astype(BF16)
    h = h + 0.5 * _dot(act, wo_ref[...])
    if ple:
        xg = _rms(h, gp_ref[...]).astype(BF16)
        gate = jax.nn.sigmoid(_dot(xg, wg_ref[...]))
        h = h + gate * _dot(p_ref[...].astype(BF16), wp_ref[...])
    o_ref[...] = h


def _ffn(h, g, wi, wo, ple_args=None):
    n = h.shape[0]
    tm = _pick_tile(n, 256)
    row = lambda w: pl.BlockSpec((tm, w), lambda i: (i, 0))
    in_specs = [row(D_MODEL), _const_spec((1, D_MODEL)), _const_spec(wi.shape), _const_spec(wo.shape)]
    args = [h, g, wi, wo]
    if ple_args is not None:
        p, gp, wg, wp = ple_args
        in_specs += [row(D_PLE), _const_spec((1, D_MODEL)), _const_spec(wg.shape), _const_spec(wp.shape)]
        args += [p, gp, wg, wp]
    return pl.pallas_call(
        functools.partial(_ffn_kernel, ple=ple_args is not None),
        out_shape=jax.ShapeDtypeStruct((n, D_MODEL), F32),
        grid=(n // tm,),
        in_specs=in_specs,
        out_specs=row(D_MODEL),
        compiler_params=_cparams("parallel"),
    )(*args)


_C_U = 0
_C_Q = _C_U + D_SSM
_C_KV = _C_Q + Q_PAD
_C_GN = _C_KV + 6 * D_KV
_C_GA = _C_GN + LANE
_C_GB = _C_GA + D_MODEL
_C_END = _C_GB + D_MODEL


def _mix_in_kernel(h_ref, g_ref, w_ref, gq_ref, gks_ref, gkw_ref,
                   u_ref, qp_ref, kvc_ref, kvs_ref, kvw_ref, kvsb_ref, kvwb_ref, gn_ref, ga_ref, gb_ref):
    xn = _rms(h_ref[...], g_ref[...]).astype(BF16)
    u_ref[...] = _dot(xn, w_ref[:, _C_U:_C_Q])
    zq = _dot(xn, w_ref[:, _C_Q:_C_KV])
    for h in range(N_HEADS):
        qh = zq[:, h * LANE:(h + 1) * LANE]
        ms = jnp.sum(qh * qh, axis=-1, keepdims=True) * (1.0 / HEAD_DIM)
        qn = qh * lax.rsqrt(ms + RMS_EPS) * gq_ref[:, h * LANE:(h + 1) * LANE]
        qp_ref[:, h * LANE:(h + 1) * LANE] = qn
    zkv = _dot(xn, w_ref[:, _C_KV:_C_GN])
    kvc_ref[...] = zkv[:, 0:2 * D_KV]
    ks = _halfnorm(zkv[:, 2 * D_KV:3 * D_KV], gks_ref[...])
    vs = zkv[:, 3 * D_KV:4 * D_KV]
    kw = _halfnorm(zkv[:, 4 * D_KV:5 * D_KV], gkw_ref[...])
    vw = zkv[:, 5 * D_KV:6 * D_KV]
    kvs_ref[:, 0:D_KV] = ks
    kvs_ref[:, D_KV:] = vs
    kvw_ref[:, 0:D_KV] = kw
    kvw_ref[:, D_KV:] = vw
    kvsb_ref[:, 0:D_KV] = ks.astype(BF16)
    kvsb_ref[:, D_KV:] = vs.astype(BF16)
    kvwb_ref[:, 0:D_KV] = kw.astype(BF16)
    kvwb_ref[:, D_KV:] = vw.astype(BF16)
    gn_ref[...] = jax.nn.sigmoid(_dot(xn, w_ref[:, _C_GN:_C_GA]))
    ga_ref[...] = jax.nn.sigmoid(_dot(xn, w_ref[:, _C_GA:_C_GB]))
    gb_ref[...] = jax.nn.sigmoid(_dot(xn, w_ref[:, _C_GB:_C_END]))


def _mix_in(h, g, w, gq, gks, gkw):
    n = h.shape[0]
    tm = _pick_tile(n, 256)
    row = lambda width: pl.BlockSpec((tm, width), lambda i: (i, 0))
    widths = [(D_SSM, F32), (Q_PAD, F32), (2 * D_KV, F32), (2 * D_KV, F32), (2 * D_KV, F32),
              (2 * D_KV, BF16), (2 * D_KV, BF16), (LANE, F32), (D_MODEL, F32), (D_MODEL, F32)]
    return pl.pallas_call(
        _mix_in_kernel,
        out_shape=[jax.ShapeDtypeStruct((n, wd), dt) for wd, dt in widths],
        grid=(n // tm,),
        in_specs=[row(D_MODEL), _const_spec((1, D_MODEL)), _const_spec(w.shape),
                  _const_spec((1, Q_PAD)), _const_spec((1, LANE)), _const_spec((1, LANE))],
        out_specs=[row(wd) for wd, _ in widths],
        compiler_params=_cparams("parallel"),
    )(h, g, w, gq, gks, gkw)


def _gelu_tanh(x):
    return 0.5 * x * (1.0 + jnp.tanh(0.7978845608028654 * (x + 0.044715 * (x * x * x))))


def _s5_kernel(*refs, seg, carry):
    if carry:
        (u_ref, bm_ref, cr_ref, ci_ref, d_ref, lam_ref, pr_ref, pi_ref,
         y_ref, hr_out, hi_out, car_ref, cai_ref) = refs
    else:
        (u_ref, bm_ref, cr_ref, ci_ref, d_ref, lam_ref, pr_ref, pi_ref, h0r_ref, h0i_ref,
         y_ref, hr_out, hi_out) = refs
    u = u_ref[...]
    rows = u.shape[0]
    x = _dot(u.astype(BF16), bm_ref[...])
    xr = x[:, :N_STATE]
    xi = x[:, N_STATE:]
    pos = lax.broadcasted_iota(jnp.int32, (rows, 1), 0) % seg
    d, k = 1, 0
    while d < seg:
        lr = lam_ref[2 * k:2 * k + 1, :]
        li = lam_ref[2 * k + 1:2 * k + 2, :]
        keep = pos >= d
        sr = jnp.where(keep, pltpu.roll(xr, d, 0), 0.0)
        si = jnp.where(keep, pltpu.roll(xi, d, 0), 0.0)
        xr, xi = xr + lr * sr - li * si, xi + lr * si + li * sr
        d *= 2
        k += 1
    if carry:
        @pl.when(pl.program_id(0) == 0)
        def _():
            car_ref[...] = jnp.zeros_like(car_ref)
            cai_ref[...] = jnp.zeros_like(cai_ref)
        c_r = car_ref[...]
        c_i = cai_ref[...]
    else:
        c_r = h0r_ref[...]
        c_i = h0i_ref[...]
    p_r = pr_ref[...]
    p_i = pi_ref[...]
    hr = xr + p_r * c_r - p_i * c_i
    hi = xi + p_r * c_i + p_i * c_r
    y = _dot(hr.astype(BF16), cr_ref[...]) + _dot(hi.astype(BF16), ci_ref[...]) + d_ref[...] * u
    y_ref[...] = _gelu_tanh(y).astype(BF16)
    if carry:
        car_ref[...] = hr[rows - 1:rows, :]
        cai_ref[...] = hi[rows - 1:rows, :]
        hr_out[...] = hr[rows - 1:rows, :]
        hi_out[...] = hi[rows - 1:rows, :]
    else:
        hr_out[...] = hr
        hi_out[...] = hi


def _s5_prompt(u, t, sp):
    rows = Q_TILE
    consts = [sp["bmat"], sp["cr"], sp["ci"], sp["d"], sp["lam_p"], sp["pr_p"], sp["pi_p"]]
    return pl.pallas_call(
        functools.partial(_s5_kernel, seg=rows, carry=True),
        out_shape=[jax.ShapeDtypeStruct((t, D_SSM), BF16),
                   jax.ShapeDtypeStruct((1, N_STATE), F32), jax.ShapeDtypeStruct((1, N_STATE), F32)],
        grid=(t // rows,),
        in_specs=[pl.BlockSpec((rows, D_SSM), lambda i: (i, 0))] + [_const_spec(c.shape) for c in consts],
        out_specs=[pl.BlockSpec((rows, D_SSM), lambda i: (i, 0)),
                   _const_spec((1, N_STATE)), _const_spec((1, N_STATE))],
        scratch_shapes=[pltpu.VMEM((1, N_STATE), F32), pltpu.VMEM((1, N_STATE), F32)],
        compiler_params=_cparams("arbitrary"),
    )(u, *consts)


def _s5_sample(u, t, ns_rows, nq, h0r, h0i, sp):
    consts = [sp["bmat"], sp["cr"], sp["ci"], sp["d"], sp["lam_s"], sp["pr_s"], sp["pi_s"], h0r, h0i]
    blk = t // ns_rows
    return pl.pallas_call(
        functools.partial(_s5_kernel, seg=nq, carry=False),
        out_shape=[jax.ShapeDtypeStruct((ns_rows, D_SSM), BF16),
                   jax.ShapeDtypeStruct((ns_rows, N_STATE), F32), jax.ShapeDtypeStruct((ns_rows, N_STATE), F32)],
        grid=(1,),
        in_specs=[pl.BlockSpec((ns_rows, D_SSM), lambda i: (blk, 0))] + [_const_spec(c.shape) for c in consts],
        out_specs=[_const_spec((ns_rows, D_SSM)), _const_spec((ns_rows, N_STATE)), _const_spec((ns_rows, N_STATE))],
        compiler_params=_cparams("arbitrary"),
    )(u, *consts)


def _compress_part(xk_ref, xv_ref, w_ref, pe_ref, a0, a1, part, rows):
    acc0 = acc1 = None
    for s in range(CMP_STRIDE):
        tok = pl.ds(s, rows, stride=CMP_STRIDE)
        xs = jnp.concatenate([xk_ref[tok, :], xv_ref[tok, :]], axis=1)
        t0 = _dot((xs + pe_ref[s:s + 1, :]).astype(BF16), w_ref[s])
        t1 = _dot((xs + pe_ref[CMP_STRIDE + s:CMP_STRIDE + s + 1, :]).astype(BF16), w_ref[CMP_STRIDE + s])
        acc0 = t0 if acc0 is None else acc0 + t0
        acc1 = t1 if acc1 is None else acc1 + t1
    base = pl.multiple_of(part * rows, rows)
    a0[pl.ds(base, rows), :] = acc0
    a1[pl.ds(base, rows), :] = acc1


def _compress_finish(a0, a1, gk_ref, kcc_ref, vcc_ref, nb):
    out = a0[...] + pltpu.roll(a1[...], nb - 1, 0)
    valid = lax.broadcasted_iota(jnp.int32, (nb, 1), 0) < nb - 1
    out = jnp.where(valid, out, 0.0)
    kcc_ref[0] = _halfnorm(out[:, 0:D_KV], gk_ref[...]).astype(BF16)
    vcc_ref[0] = out[:, D_KV:].astype(BF16)


def _compress_rows_kernel(xk_ref, xv_ref, w_ref, pe_ref, gk_ref, kcc_ref, vcc_ref, a0, a1, *, nsplit, nb):
    part = pl.program_id(0)
    _compress_part(xk_ref, xv_ref, w_ref, pe_ref, a0, a1, part, nb // nsplit)

    @pl.when(part == nsplit - 1)
    def _():
        _compress_finish(a0, a1, gk_ref, kcc_ref, vcc_ref, nb)


def _compress_rows(kv_cmp, t, cw):
    nb = t // CMP_STRIDE
    nsplit = 2
    return pl.pallas_call(
        functools.partial(_compress_rows_kernel, nsplit=nsplit, nb=nb),
        out_shape=[jax.ShapeDtypeStruct((1, nb, D_KV), BF16), jax.ShapeDtypeStruct((1, nb, D_KV), BF16)],
        grid=(nsplit,),
        in_specs=[pl.BlockSpec((t // nsplit, D_KV), lambda i: (i, 0)),
                  pl.BlockSpec((t // nsplit, D_KV), lambda i: (i, 1)),
                  _const_spec(cw["w"].shape), _const_spec(cw["pe"].shape), _const_spec((1, LANE))],
        out_specs=[_const_spec((1, nb, D_KV)), _const_spec((1, nb, D_KV))],
        scratch_shapes=[pltpu.VMEM((nb, 2 * D_KV), F32), pltpu.VMEM((nb, 2 * D_KV), F32)],
        compiler_params=_cparams("arbitrary"),
    )(kv_cmp, kv_cmp, cw["w"], cw["pe"], cw["gk"])


def _compress_kernel(pt_ref, pages_ref, w_ref, pe_ref, gk_ref,
                     kcc_ref, vcc_ref, buf, sem, x_scr, a0, a1, *, pps, nsplit, nb):
    s = pl.program_id(0)
    nsteps = pl.num_programs(0)

    def page_copy(step, p, slot):
        b = step // nsplit
        part = step % nsplit
        pg = pt_ref[b, part * pps + p]
        return pltpu.make_async_copy(pages_ref.at[pg], buf.at[slot, p], sem.at[slot])

    def start(step, slot):
        def body(p, c):
            page_copy(step, p, slot).start()
            return c
        lax.fori_loop(0, pps, body, 0)

    def wait(step, slot):
        def body(p, c):
            page_copy(step, p, slot).wait()
            return c
        lax.fori_loop(0, pps, body, 0)

    @pl.when(s == 0)
    def _():
        start(s, 0)

    @pl.when(s + 1 < nsteps)
    def _():
        start(s + 1, (s + 1) % 2)

    slot = s % 2
    wait(s, slot)
    part = s % nsplit

    def to_rows(p, c):
        tok = pl.ds(pl.multiple_of(p * PAGE_SIZE, PAGE_SIZE), PAGE_SIZE)
        x_scr[0, tok, :] = buf[slot, p, 0:D_KV, :].T
        x_scr[1, tok, :] = buf[slot, p, D_KV:, :].T
        return c
    lax.fori_loop(0, pps, to_rows, 0)
    _compress_part(x_scr.at[0], x_scr.at[1], w_ref, pe_ref, a0, a1, part, nb // nsplit)

    @pl.when(part == nsplit - 1)
    def _():
        _compress_finish(a0, a1, gk_ref, kcc_ref, vcc_ref, nb)


def _compress_paged(page_table, pages, cw):
    bsz, n_pages = page_table.shape
    nb = n_pages * (PAGE_SIZE // CMP_STRIDE)
    nsplit = 2
    pps = n_pages // nsplit
    const = lambda shape: pl.BlockSpec(shape, lambda s, pt: (0,) * len(shape), pipeline_mode=pl.Buffered(1))
    grid_spec = pltpu.PrefetchScalarGridSpec(
        num_scalar_prefetch=1,
        grid=(bsz * nsplit,),
        in_specs=[pl.BlockSpec(memory_space=pl.ANY), const(cw["w"].shape), const(cw["pe"].shape),
                  const((1, LANE))],
        out_specs=[pl.BlockSpec((1, nb, D_KV), lambda s, pt: (s // nsplit, 0, 0)),
                   pl.BlockSpec((1, nb, D_KV), lambda s, pt: (s // nsplit, 0, 0))],
        scratch_shapes=[pltpu.VMEM((2, pps, 2 * D_KV, PAGE_SIZE), F32), pltpu.SemaphoreType.DMA((2,)),
                        pltpu.VMEM((2, pps * PAGE_SIZE, D_KV), F32),
                        pltpu.VMEM((nb, 2 * D_KV), F32), pltpu.VMEM((nb, 2 * D_KV), F32)],
    )
    return pl.pallas_call(
        functools.partial(_compress_kernel, pps=pps, nsplit=nsplit, nb=nb),
        out_shape=[jax.ShapeDtypeStruct((bsz, nb, D_KV), BF16), jax.ShapeDtypeStruct((bsz, nb, D_KV), BF16)],
        grid_spec=grid_spec,
        compiler_params=_cparams("arbitrary"),
    )(page_table, pages, cw["w"], cw["pe"], cw["gk"])


def _heads_rows(qp):
    return jnp.concatenate([qp[:, h * LANE:(h + 1) * LANE] for h in range(N_HEADS)], axis=0).astype(BF16)


def _slope(h):
    return 2.0 ** (-8.0 * (h + 1) / N_HEADS)


def _branch(q_all, tq, pieces):
    s_all = [_dot(q_all, k) if fm else _dot_nt(q_all, k) for k, _, _, _, fm in pieces]
    probs = []
    for h in range(N_HEADS):
        rows = slice(h * tq, (h + 1) * tq)
        sm = [jnp.where(pc[3], s[rows] - _slope(h) * pc[2], NEG) for s, pc in zip(s_all, pieces)]
        m = functools.reduce(jnp.maximum, [jnp.max(x, axis=-1, keepdims=True) for x in sm])
        e = [jnp.where(pc[3], jnp.exp(x - m), 0.0) for x, pc in zip(sm, pieces)]
        l = functools.reduce(lambda a, b: a + b, [jnp.sum(x, axis=-1, keepdims=True) for x in e])
        inv = 1.0 / jnp.maximum(l, 1e-30)
        probs.append([x * inv for x in e])
    o = None
    for i, (_, v, _, _, fm) in enumerate(pieces):
        p = jnp.concatenate([probs[h][i] for h in range(N_HEADS)], axis=0).astype(BF16)
        t = _dot_nt(p, v) if fm else _dot(p, v)
        o = t if o is None else o + t
    return o, probs


def _cmp_win_topk(q_all, tq, qpos, gn, kcc, vcc, mband, win_pieces, nb, ns, topn):
    n_io = lax.broadcasted_iota(jnp.int32, (1, nb), 1)
    kc_end = n_io * CMP_STRIDE + (CMP_BLOCK - 1)
    dist_c = (qpos - kc_end).astype(F32)
    mask_c = (dist_c >= 0) & (n_io < nb - 1)
    o_c, probs = _branch(q_all, tq, [(kcc, vcc, dist_c, mask_c, False)])
    o_w, _ = _branch(q_all, tq, win_pieces)
    imps = []
    for g in range(N_KV):
        psum = probs[HPG * g][0]
        for hh in range(1, HPG):
            psum = psum + probs[HPG * g + hh][0]
        hi = psum.astype(BF16)
        r = psum - hi.astype(F32)
        mid = r.astype(BF16)
        lo = (r - mid.astype(F32)).astype(BF16)
        imps.append(_dot(hi, mband) + _dot(mid, mband) + _dot(lo, mband))
    imp = jnp.concatenate(imps, axis=0)
    blk = lax.broadcasted_iota(jnp.int32, (1, ns), 1)
    cur = jnp.concatenate([jnp.right_shift(qpos, 6)] * N_KV, axis=0)
    forced = (blk == 0) | (blk == cur) | (blk == cur - 1)
    score = jnp.where(forced, imp + SEL_BIG, jnp.where(blk <= cur, imp, -SEL_BIG))
    blkf = blk.astype(F32)
    sel = jnp.zeros_like(score)
    for _ in range(topn):
        m = jnp.max(score, axis=-1, keepdims=True)
        first = jnp.min(jnp.where(score == m, blkf, float(ns)), axis=-1, keepdims=True)
        hit = blkf == first
        sel = jnp.where(hit, 1.0, sel)
        score = jnp.where(hit, -jnp.inf, score)
    sel = jnp.where(blk <= cur, sel, 0.0)
    outs = []
    for h in range(N_HEADS):
        rows = slice(h * tq, (h + 1) * tq)
        outs.append(gn[:, h:h + 1] * o_c[rows] + gn[:, 2 * N_HEADS + h:2 * N_HEADS + h + 1] * o_w[rows])
    return jnp.concatenate(outs, axis=1), sel


def _e1_prompt_kernel(qp_ref, gn_ref, kcc_ref, vcc_ref, mband_ref, kvw_ref, ocw_ref, sel_ref, flag_ref,
                      *, nb, ns):
    tq = Q_TILE
    s0 = pl.program_id(0) * tq
    qpos = s0 + lax.broadcasted_iota(jnp.int32, (tq, 1), 0)
    q_all = _heads_rows(qp_ref[...])
    start = pl.multiple_of(jnp.maximum(s0 - WINDOW, 0), Q_TILE)
    kw = kvw_ref[pl.ds(start, WIN_KEYS), 0:D_KV]
    vw = kvw_ref[pl.ds(start, WIN_KEYS), D_KV:]
    kpos = start + lax.broadcasted_iota(jnp.int32, (1, WIN_KEYS), 1)
    dist_w = (qpos - kpos).astype(F32)
    mask_w = (dist_w >= 0) & (dist_w < WINDOW)
    ocw, sel = _cmp_win_topk(q_all, tq, qpos, gn_ref[...], kcc_ref[0], vcc_ref[0], mband_ref[...],
                             [(kw, vw, dist_w, mask_w, False)], nb, ns, TOP_N)
    ocw_ref[...] = ocw
    sel_ref[0] = sel[:tq]
    sel_ref[1] = sel[tq:]
    colany = jnp.max(sel, axis=0, keepdims=True)
    j_io = lax.broadcasted_iota(jnp.int32, (ns, LANE), 0)
    c_io = lax.broadcasted_iota(jnp.int32, (ns, LANE), 1)
    grp = jnp.where(jnp.right_shift(j_io, 2) == c_io, 1.0, 0.0).astype(BF16)
    cnt = _dot(jnp.broadcast_to(colany, (8, ns)).astype(BF16), grp)
    flag_ref[0] = (cnt > 0.5).astype(jnp.int32)


def _e1_prompt(qp, gn, kcc, vcc, mband, kvw_bf, t):
    nb, ns = mband.shape
    nt = t // Q_TILE
    row = lambda w: pl.BlockSpec((Q_TILE, w), lambda i: (i, 0))
    return pl.pallas_call(
        functools.partial(_e1_prompt_kernel, nb=nb, ns=ns),
        out_shape=[jax.ShapeDtypeStruct((t, Q_PAD), F32), jax.ShapeDtypeStruct((N_KV, t, ns), F32),
                   jax.ShapeDtypeStruct((nt, 8, LANE), jnp.int32)],
        grid=(nt,),
        in_specs=[row(Q_PAD), row(LANE), _const_spec((1, nb, D_KV)), _const_spec((1, nb, D_KV)),
                  _const_spec(mband.shape), _const_spec(kvw_bf.shape)],
        out_specs=[row(Q_PAD), pl.BlockSpec((N_KV, Q_TILE, ns), lambda i: (0, i, 0)),
                   pl.BlockSpec((1, 8, LANE), lambda i: (i, 0, 0))],
        compiler_params=_cparams("parallel"),
    )(qp, gn, kcc, vcc, mband, kvw_bf)


def _pad_rows(x, rows):
    return jnp.concatenate([x, jnp.zeros((rows - x.shape[0], x.shape[1]), x.dtype)], axis=0)


def _e1_sample_kernel(qp_ref, gn_ref, kcc_ref, vcc_ref, mband_ref, cwin_ref, nwin_ref, ocw_ref, sel_ref,
                      *, nb, ns, nq, past):
    qi = lax.broadcasted_iota(jnp.int32, (nq, 1), 0)
    qpos = past + qi
    q_all = _heads_rows(qp_ref[...])
    cw = cwin_ref[0]
    w_buf = cw.shape[1]
    j_c = lax.broadcasted_iota(jnp.int32, (1, w_buf), 1)
    dist_cw = (qi + (w_buf - j_c)).astype(F32)
    mask_cw = (dist_cw >= 0) & (dist_cw < WINDOW)
    nw = _pad_rows(nwin_ref[...], LANE)
    j_n = lax.broadcasted_iota(jnp.int32, (1, LANE), 1)
    dist_nw = (qi - j_n).astype(F32)
    mask_nw = (dist_nw >= 0) & (j_n < nq)
    pieces = [(cw[0:D_KV, :].astype(BF16), cw[D_KV:, :].astype(BF16), dist_cw, mask_cw, True),
              (nw[:, 0:D_KV].astype(BF16), nw[:, D_KV:].astype(BF16), dist_nw, mask_nw, False)]
    ocw, sel = _cmp_win_topk(q_all, nq, qpos, gn_ref[...], kcc_ref[0], vcc_ref[0], mband_ref[...],
                             pieces, nb, ns, TOP_N - 1)
    ocw_ref[...] = ocw
    sel_ref[0, 0] = sel[:nq]
    sel_ref[0, 1] = sel[nq:]


def _e1_sample(qp, gn, kcc, vcc, mband, cache_win, win_off, kv_win, t, bsz, nq, past):
    nb, ns = mband.shape
    w_buf = cache_win.shape[2]
    off = t // nq
    row = lambda w: pl.BlockSpec((nq, w), lambda b: (off + b, 0))
    per_b = lambda shape: pl.BlockSpec((1,) + shape, lambda b: (b,) + (0,) * len(shape))
    return pl.pallas_call(
        functools.partial(_e1_sample_kernel, nb=nb, ns=ns, nq=nq, past=past),
        out_shape=[jax.ShapeDtypeStruct((bsz * nq, Q_PAD), F32), jax.ShapeDtypeStruct((bsz, N_KV, nq, ns), F32)],
        grid=(bsz,),
        in_specs=[row(Q_PAD), row(LANE), per_b((nb, D_KV)), per_b((nb, D_KV)), _const_spec(mband.shape),
                  pl.BlockSpec((1, 2 * D_KV, w_buf), lambda b: (win_off + b, 0, 0)), row(2 * D_KV)],
        out_specs=[pl.BlockSpec((nq, Q_PAD), lambda b: (b, 0)), per_b((N_KV, nq, ns))],
        compiler_params=_cparams("parallel"),
    )(qp, gn, kcc, vcc, mband, cache_win, kv_win)


def _online_step(s, dist, masks, v, m_ref, l_ref, acc_ref, tq, feature_major=False):
    ps, alphas = [], []
    for h in range(N_HEADS):
        rows = slice(h * tq, (h + 1) * tq)
        mask = masks[h // HPG]
        sm = jnp.where(mask, s[rows] - _slope(h) * dist, NEG)
        m_old = m_ref[rows, :]
        m_new = jnp.maximum(m_old, jnp.max(sm, axis=-1, keepdims=True))
        alpha = jnp.exp(m_old - m_new)
        p = jnp.where(mask, jnp.exp(sm - m_new), 0.0)
        l_ref[rows, :] = alpha * l_ref[rows, :] + jnp.sum(p, axis=-1, keepdims=True)
        m_ref[rows, :] = m_new
        ps.append(p)
        alphas.append(alpha)
    p_all = jnp.concatenate(ps, axis=0).astype(BF16)
    a_all = jnp.concatenate(alphas, axis=0)
    pv = _dot_nt(p_all, v) if feature_major else _dot(p_all, v)
    acc_ref[...] = a_all * acc_ref[...] + pv


def _expand_sel(sel, first_blk, n_keys):
    ns = sel.shape[1]
    j_io = lax.broadcasted_iota(jnp.int32, (ns, n_keys), 0)
    kb = first_blk + jnp.right_shift(lax.broadcasted_iota(jnp.int32, (ns, n_keys), 1), 6)
    e = jnp.where(j_io == kb, 1.0, 0.0).astype(BF16)
    return _dot(sel.astype(BF16), e) > 0.5


def _init_online(m_ref, l_ref, acc_ref):
    m_ref[...] = jnp.full(m_ref.shape, NEG, F32)
    l_ref[...] = jnp.zeros(l_ref.shape, F32)
    acc_ref[...] = jnp.zeros(acc_ref.shape, F32)


def _finish_online(ocw, gn, l_ref, acc_ref, tq):
    o_s = acc_ref[...] * (1.0 / jnp.maximum(l_ref[...], 1e-30))
    outs = []
    for h in range(N_HEADS):
        g1 = gn[:, N_HEADS + h:N_HEADS + h + 1]
        outs.append(ocw[:, h * LANE:(h + 1) * LANE] + g1 * o_s[h * tq:(h + 1) * tq])
    return jnp.concatenate(outs, axis=1)


def _e2_prompt_kernel(flags_ref, qp_ref, sel_ref, gn_ref, ocw_ref, kvs_ref, out_ref, m_ref, l_ref, acc_ref,
                      *, nchunks):
    tq = Q_TILE
    i = pl.program_id(0)
    s0 = i * tq
    qpos = s0 + lax.broadcasted_iota(jnp.int32, (tq, 1), 0)
    q_all = _heads_rows(qp_ref[...])
    _init_online(m_ref, l_ref, acc_ref)

    def body(c, carry):
        @pl.when(flags_ref[i * nchunks + c] > 0)
        def _():
            base = pl.multiple_of(c * SEL_CHUNK, SEL_CHUNK)
            k = kvs_ref[pl.ds(base, SEL_CHUNK), 0:D_KV]
            v = kvs_ref[pl.ds(base, SEL_CHUNK), D_KV:]
            s = _dot_nt(q_all, k)
            kpos = base + lax.broadcasted_iota(jnp.int32, (1, SEL_CHUNK), 1)
            dist = (qpos - kpos).astype(F32)
            first_blk = c * (SEL_CHUNK // SEL_BLOCK)
            masks = [_expand_sel(sel_ref[g], first_blk, SEL_CHUNK) & (dist >= 0) for g in range(N_KV)]
            _online_step(s, dist, masks, v, m_ref, l_ref, acc_ref, tq)
        return carry

    lax.fori_loop(0, s0 // SEL_CHUNK + 1, body, 0)
    out_ref[...] = _finish_online(ocw_ref[...], gn_ref[...], l_ref, acc_ref, tq)


def _e2_prompt(flags, qp, sel, gn, ocw, kvs_bf, t):
    ns = sel.shape[2]
    nt = t // Q_TILE
    nchunks = flags.shape[0] // nt
    row = lambda w: pl.BlockSpec((Q_TILE, w), lambda i, f: (i, 0))
    grid_spec = pltpu.PrefetchScalarGridSpec(
        num_scalar_prefetch=1,
        grid=(nt,),
        in_specs=[row(Q_PAD), pl.BlockSpec((N_KV, Q_TILE, ns), lambda i, f: (0, i, 0)), row(LANE), row(Q_PAD),
                  pl.BlockSpec(kvs_bf.shape, lambda i, f: (0, 0))],
        out_specs=row(Q_PAD),
        scratch_shapes=[pltpu.VMEM((N_HEADS * Q_TILE, 1), F32), pltpu.VMEM((N_HEADS * Q_TILE, 1), F32),
                        pltpu.VMEM((N_HEADS * Q_TILE, LANE), F32)],
    )
    return pl.pallas_call(
        functools.partial(_e2_prompt_kernel, nchunks=nchunks),
        out_shape=jax.ShapeDtypeStruct((t, Q_PAD), F32),
        grid_spec=grid_spec,
        compiler_params=_cparams("arbitrary"),
    )(flags, qp, sel, gn, ocw, kvs_bf)


def _e2_sample_kernel(pt_ref, pages_ref, qp_ref, sel_ref, gn_ref, ocw_ref, nsel_ref, out_ref,
                      buf, sem, m_ref, l_ref, acc_ref, *, pps, nsplit, nq):
    s = pl.program_id(0)
    nsteps = pl.num_programs(0)
    n_keys = pps * PAGE_SIZE

    def page_copy(step, p, slot):
        b = step // nsplit
        part = step % nsplit
        pg = pt_ref[b, part * pps + p]
        col = pl.multiple_of(p * PAGE_SIZE, PAGE_SIZE)
        return pltpu.make_async_copy(pages_ref.at[pg], buf.at[slot, :, pl.ds(col, PAGE_SIZE)], sem.at[slot])

    def start(step, slot):
        def body(p, c):
            page_copy(step, p, slot).start()
            return c
        lax.fori_loop(0, pps, body, 0)

    def wait(step, slot):
        def body(p, c):
            page_copy(step, p, slot).wait()
            return c
        lax.fori_loop(0, pps, body, 0)

    @pl.when(s == 0)
    def _():
        start(s, 0)

    @pl.when(s + 1 < nsteps)
    def _():
        start(s + 1, (s + 1) % 2)

    slot = s % 2
    part = s % nsplit
    qi = lax.broadcasted_iota(jnp.int32, (nq, 1), 0)
    q_all = _heads_rows(qp_ref[...])

    @pl.when(part == 0)
    def _():
        _init_online(m_ref, l_ref, acc_ref)

    wait(s, slot)
    k = buf[slot, 0:D_KV, :].astype(BF16)
    v = buf[slot, D_KV:, :].astype(BF16)
    sc = _dot(q_all, k)
    back = (nsplit - part) * n_keys - lax.broadcasted_iota(jnp.int32, (1, n_keys), 1)
    dist = (qi + back).astype(F32)
    first_blk = part * (n_keys // SEL_BLOCK)
    masks = [_expand_sel(sel_ref[0, g], first_blk, n_keys) for g in range(N_KV)]
    _online_step(sc, dist, masks, v, m_ref, l_ref, acc_ref, nq, feature_major=True)

    @pl.when(part == nsplit - 1)
    def _():
        nw = _pad_rows(nsel_ref[...], LANE)
        j_n = lax.broadcasted_iota(jnp.int32, (1, LANE), 1)
        dist_n = (qi - j_n).astype(F32)
        mask_n = (dist_n >= 0) & (j_n < nq)
        s_n = _dot_nt(q_all, nw[:, 0:D_KV].astype(BF16))
        _online_step(s_n, dist_n, [mask_n, mask_n], nw[:, D_KV:].astype(BF16), m_ref, l_ref, acc_ref, nq)
        out_ref[...] = _finish_online(ocw_ref[...], gn_ref[...], l_ref, acc_ref, nq)


def _e2_sample(page_table, pages, qp, sel, gn, ocw, kv_sel, t, nq):
    bsz, n_pages = page_table.shape
    ns = sel.shape[3]
    nsplit = 4 if n_pages % 4 == 0 else 1
    pps = n_pages // nsplit
    off = t // nq
    row = lambda w: pl.BlockSpec((nq, w), lambda s, pt: (off + s // nsplit, 0))
    grid_spec = pltpu.PrefetchScalarGridSpec(
        num_scalar_prefetch=1,
        grid=(bsz * nsplit,),
        in_specs=[pl.BlockSpec(memory_space=pl.ANY), row(Q_PAD),
                  pl.BlockSpec((1, N_KV, nq, ns), lambda s, pt: (s // nsplit, 0, 0, 0)), row(LANE),
                  pl.BlockSpec((nq, Q_PAD), lambda s, pt: (s // nsplit, 0)), row(2 * D_KV)],
        out_specs=pl.BlockSpec((nq, Q_PAD), lambda s, pt: (s // nsplit, 0)),
        scratch_shapes=[pltpu.VMEM((2, 2 * D_KV, pps * PAGE_SIZE), F32), pltpu.SemaphoreType.DMA((2,)),
                        pltpu.VMEM((N_HEADS * nq, 1), F32), pltpu.VMEM((N_HEADS * nq, 1), F32),
                        pltpu.VMEM((N_HEADS * nq, LANE), F32)],
    )
    return pl.pallas_call(
        functools.partial(_e2_sample_kernel, pps=pps, nsplit=nsplit, nq=nq),
        out_shape=jax.ShapeDtypeStruct((bsz * nq, Q_PAD), F32),
        grid_spec=grid_spec,
        compiler_params=_cparams("arbitrary"),
    )(page_table, pages, qp, sel, gn, ocw, kv_sel)


def _post_kernel(h_ref, yg_ref, oatt_ref, ga_ref, gb_ref, wglu_ref, watt_ref, wout_ref, o_ref):
    gl = _dot(yg_ref[...], wglu_ref[...])
    br_a = gl[:, :D_MODEL] * jax.nn.sigmoid(gl[:, D_MODEL:])
    br_b = _dot(oatt_ref[...].astype(BF16), watt_ref[...])
    merged = (ga_ref[...] * br_a + gb_ref[...] * br_b).astype(BF16)
    o_ref[...] = h_ref[...] + _dot(merged, wout_ref[...])


def _post(h, yg, oatt, ga, gb, wglu, watt, wout):
    n = h.shape[0]
    tm = _pick_tile(n, 256)
    row = lambda w: pl.BlockSpec((tm, w), lambda i: (i, 0))
    return pl.pallas_call(
        _post_kernel,
        out_shape=jax.ShapeDtypeStruct((n, D_MODEL), F32),
        grid=(n // tm,),
        in_specs=[row(D_MODEL), row(D_SSM), row(Q_PAD), row(D_MODEL), row(D_MODEL),
                  _const_spec(wglu.shape), _const_spec(watt.shape), _const_spec(wout.shape)],
        out_specs=row(D_MODEL),
        compiler_params=_cparams("parallel"),
    )(h, yg, oatt, ga, gb, wglu, watt, wout)


def _head_pad_index():
    h = jnp.arange(N_HEADS)[:, None]
    d = jnp.arange(HEAD_DIM)[None, :]
    return (LANE * h + HEAD_DIM * (h // HPG) + d).reshape(-1)


def _prep_mix_weights(w_in, qk_norm):
    idx = _head_pad_index()
    wq = jnp.zeros((D_MODEL, Q_PAD), F32).at[:, idx].set(w_in[:, D_SSM:D_SSM + N_HEADS * HEAD_DIM])
    c0 = D_SSM + N_HEADS * HEAD_DIM
    c1 = c0 + 6 * D_KV
    c2 = c1 + 3 * N_HEADS
    wgn = jnp.zeros((D_MODEL, LANE), F32).at[:, :3 * N_HEADS].set(w_in[:, c1:c2])
    w = jnp.concatenate([w_in[:, :D_SSM], wq, w_in[:, c0:c1], wgn, w_in[:, c2:]], axis=1).astype(BF16)
    gq = jnp.zeros((Q_PAD,), F32).at[idx].set(jnp.tile(qk_norm[0] * (HEAD_DIM ** -0.5), N_HEADS))[None]
    gks = jnp.tile(qk_norm[2], N_KV)[None]
    gkw = jnp.tile(qk_norm[3], N_KV)[None]
    return w, gq, gks, gkw


def _prep_att_out(w_att_out):
    return jnp.zeros((Q_PAD, D_MODEL), F32).at[_head_pad_index()].set(w_att_out).astype(BF16)


def _prep_compress(pe_k, pe_v, w_k, w_v, gain_k):
    blocks = jnp.stack([w_k, w_k, w_v, w_v], axis=1)
    w = jnp.einsum("scde,ch->scdhe", blocks, jnp.eye(2 * N_KV, dtype=F32))
    pe = jnp.concatenate([pe_k, pe_k, pe_v, pe_v], axis=1)
    return {"w": w.reshape(CMP_BLOCK, 2 * D_KV, 2 * D_KV).astype(BF16), "pe": pe,
            "gk": jnp.tile(gain_k, N_KV)[None]}


def _prep_s5(a_re, a_im, log_dt, b_re, b_im, c_re, c_im, d, nq):
    dt = jnp.exp(log_dt)[:, None]
    mag = jnp.exp(a_re * dt)
    lr = mag * jnp.cos(a_im * dt)
    li = mag * jnp.sin(a_im * dt)
    den = a_re * a_re + a_im * a_im
    fr = ((lr - 1.0) * a_re + li * a_im) / den
    fi = (li * a_re - (lr - 1.0) * a_im) / den
    bbr = fr[..., None] * b_re - fi[..., None] * b_im
    bbi = fr[..., None] * b_im + fi[..., None] * b_re
    eye = jnp.eye(N_SSM_GROUPS, dtype=F32)
    blk_b = lambda m: jnp.einsum("gpc,gh->gchp", m, eye).reshape(D_SSM, N_STATE)
    blk_c = lambda m: jnp.einsum("gcp,gh->gphc", m, eye).reshape(N_STATE, D_SSM)

    def lam_pow(k):
        kk = k.astype(F32)[:, None, None]
        m = jnp.exp(a_re * dt * kk)
        th = a_im * dt * kk
        return (m * jnp.cos(th)).reshape(-1, N_STATE), (m * jnp.sin(th)).reshape(-1, N_STATE)

    def step_table(seg):
        ks = []
        dd = 1
        while dd < seg:
            ks.append(dd)
            dd *= 2
        re, im = lam_pow(jnp.array(ks))
        return jnp.stack([re, im], axis=1).reshape(-1, N_STATE)

    sp = {"bmat": jnp.concatenate([blk_b(bbr), blk_b(bbi)], axis=1).astype(BF16),
          "cr": blk_c(c_re).astype(BF16), "ci": (-blk_c(c_im)).astype(BF16), "d": d[None],
          "lam_p": step_table(Q_TILE), "lam_s": step_table(nq)}
    sp["pr_p"], sp["pi_p"] = lam_pow(jnp.arange(Q_TILE) + 1)
    return sp, lam_pow


def _band_matrix(nb, ns):
    ratio = SEL_BLOCK // CMP_STRIDE
    lo = CMP_BLOCK // CMP_STRIDE - 1
    c = jnp.arange(nb)[:, None]
    j = jnp.arange(ns)[None, :]
    return ((c >= ratio * j - lo) & (c <= ratio * j + ratio - 1)).astype(BF16)


def kernel(x_prompt, x_sample, cache_kv_cmp, cache_kv_sel, cache_kv_win, state_ssm_re, state_ssm_im, page_table, p_prompt, p_sample, norm_ffn1, w_ffn1_in, w_ffn1_out, norm_mix, w_in, qk_norm, ssm_a_re, ssm_a_im, ssm_log_dt, ssm_b_re, ssm_b_im, ssm_c_re, ssm_c_im, ssm_d, w_glu, cmp_pe_k, cmp_pe_v, cmp_w_k, cmp_w_v, w_att_out, w_out, norm_ffn2, w_ffn2_in, w_ffn2_out, norm_ple, w_ple_gate, w_ple_proj):
    bp, t = x_prompt.shape[:2]
    bsz, nq = x_sample.shape[:2]
    n_pages = page_table.shape[1]
    past = n_pages * PAGE_SIZE
    n_pool = cache_kv_cmp.shape[1]
    w_buf = cache_kv_win.shape[2]
    ns_rows = bsz * nq
    assert bp == 1 and t % SEL_CHUNK == 0 and t >= WIN_KEYS and t % ns_rows == 0 and nq < CMP_STRIDE
    assert past == t and w_buf == WINDOW
    nb = t // CMP_STRIDE
    ns = t // SEL_BLOCK
    nt = t // Q_TILE
    nchunks = t // SEL_CHUNK
    mband = _band_matrix(nb, ns)
    page_table = page_table.astype(jnp.int32)
    feat_major = lambda c: jnp.transpose(c, (0, 1, 3, 4, 5, 2)).reshape(DEPTH * c.shape[1], 2 * D_KV, c.shape[2])
    pages_cmp = feat_major(cache_kv_cmp)
    pages_sel = feat_major(cache_kv_sel)
    pages_win = feat_major(cache_kv_win)

    h = jnp.concatenate([x_prompt[0], x_sample.reshape(ns_rows, D_MODEL)], axis=0)
    st_p = [[] for _ in range(5)]
    st_s = [[] for _ in range(5)]
    for i in range(DEPTH):
        row1 = lambda a: a[i][None]
        h = _ffn(h, row1(norm_ffn1), w_ffn1_in[i].astype(BF16), w_ffn1_out[i].astype(BF16))
        w_mix, gq, gks, gkw = _prep_mix_weights(w_in[i], qk_norm[i])
        u, qp, kv_cmp, kv_sel, kv_win, kvs_bf, kvw_bf, gn, ga, gb = _mix_in(h, row1(norm_mix), w_mix, gq, gks, gkw)

        sp, lam_pow = _prep_s5(ssm_a_re[i], ssm_a_im[i], ssm_log_dt[i], ssm_b_re[i], ssm_b_im[i],
                               ssm_c_re[i], ssm_c_im[i], ssm_d[i], nq)
        sp["pr_s"], sp["pi_s"] = lam_pow(jnp.arange(ns_rows) % nq + 1)
        yg_p, hr_p, hi_p = _s5_prompt(u, t, sp)
        h0r = jnp.repeat(state_ssm_re[i].reshape(bsz, N_STATE), nq, axis=0)
        h0i = jnp.repeat(state_ssm_im[i].reshape(bsz, N_STATE), nq, axis=0)
        yg_s, hr_s, hi_s = _s5_sample(u, t, ns_rows, nq, h0r, h0i, sp)

        cw = _prep_compress(cmp_pe_k[i], cmp_pe_v[i], cmp_w_k[i], cmp_w_v[i], qk_norm[i, 1])
        layer_pages = page_table + i * n_pool
        kcc_p, vcc_p = _compress_rows(kv_cmp, t, cw)
        kcc_s, vcc_s = _compress_paged(layer_pages, pages_cmp, cw)
        ocw_p, sel_p, flags = _e1_prompt(qp, gn, kcc_p, vcc_p, mband, kvw_bf, t)
        ocw_s, sel_s = _e1_sample(qp, gn, kcc_s, vcc_s, mband, pages_win, i * bsz, kv_win, t, bsz, nq, past)
        flags = flags[:, 0, :nchunks].reshape(-1)
        oatt_p = _e2_prompt(flags, qp, sel_p, gn, ocw_p, kvs_bf, t)
        oatt_s = _e2_sample(layer_pages, pages_sel, qp, sel_s, gn, ocw_s, kv_sel, t, nq)

        h = _post(h, jnp.concatenate([yg_p, yg_s], axis=0), jnp.concatenate([oatt_p, oatt_s], axis=0), ga, gb,
                  w_glu[i].astype(BF16), _prep_att_out(w_att_out[i]), w_out[i].astype(BF16))
        p_all = jnp.concatenate([p_prompt[i, 0], p_sample[i].reshape(ns_rows, D_PLE)], axis=0)
        h = _ffn(h, row1(norm_ffn2), w_ffn2_in[i].astype(BF16), w_ffn2_out[i].astype(BF16),
                 (p_all, row1(norm_ple), w_ple_gate[i].astype(BF16), w_ple_proj[i].astype(BF16)))

        kv5 = lambda a, lead: a.reshape(lead + (2, N_KV, HEAD_DIM))
        st_p[0].append(kv5(kv_cmp[:t], (1, t)))
        st_p[1].append(kv5(kv_sel[:t], (1, t)))
        st_p[2].append(kv5(kv_win[t - min(WINDOW, t):t], (1, min(WINDOW, t))))
        st_p[3].append(hr_p.reshape(1, N_SSM_GROUPS, P_STATE))
        st_p[4].append(hi_p.reshape(1, N_SSM_GROUPS, P_STATE))
        st_s[0].append(kv5(kv_cmp[t:], (bsz, nq)))
        st_s[1].append(kv5(kv_sel[t:], (bsz, nq)))
        st_s[2].append(jnp.concatenate([cache_kv_win[i, :, nq:], kv5(kv_win[t:], (bsz, nq))], axis=1))
        st_s[3].append(hr_s[nq - 1::nq].reshape(bsz, N_SSM_GROUPS, P_STATE))
        st_s[4].append(hi_s[nq - 1::nq].reshape(bsz, N_SSM_GROUPS, P_STATE))

    outs_p = [jnp.stack(a) for a in st_p]
    outs_s = [jnp.stack(a) for a in st_s]
    y_prompt = h[:t][None]
    y_sample = h[t:].reshape(bsz, nq, D_MODEL)
    return (y_prompt, y_sample, *outs_p, *outs_s)
```

```python
import functools

import jax
import jax.numpy as jnp
from jax import lax
from jax.experimental import pallas as pl
from jax.experimental.pallas import tpu as pltpu

F32 = jnp.float32
BF16 = jnp.bfloat16

D_MODEL = 1024
DEPTH = 2
D_SSM = 512
SSM_GROUP = 16
N_SSM_GROUPS = 32
P_STATE = 64
N_STATE = N_SSM_GROUPS * P_STATE
N_HEADS = 8
HEAD_DIM = 64
N_KV = 2
HPG = 4
D_KV = 128
CMP_BLOCK = 32
CMP_STRIDE = 16
SEL_BLOCK = 64
TOP_N = 16
WINDOW = 512
PAGE_SIZE = 128
SEL_BIG = 1e4
D_FF = 2816
D_PLE = 256
RMS_EPS = 1e-6
NEG = -1e30
HALF_NEG = -0.5e30

LANE = 128
Q_TILE = 128
SCAN_SEG = 8
SEL_CHUNK = 256
WIN_KEYS = WINDOW + Q_TILE
Q_PAD = N_HEADS * LANE
VMEM_LIMIT = 56 * 2 ** 20


def _cparams(*sem):
    return pltpu.CompilerParams(dimension_semantics=sem, vmem_limit_bytes=VMEM_LIMIT)


def _dot(a, b):
    return jnp.dot(a, b, preferred_element_type=F32)


def _dot_nt(a, b):
    return lax.dot_general(a, b, (((1,), (1,)), ((), ())), preferred_element_type=F32)


def _pick_tile(n, target):
    for t in range(min(n, target), 15, -1):
        if n % t == 0 and t % 16 == 0:
            return t
    raise ValueError(f"no row tile for {n}")


def _const_spec(shape):
    nd = len(shape)
    return pl.BlockSpec(shape, lambda *_: (0,) * nd, pipeline_mode=pl.Buffered(1))


def _rms(x, g):
    ms = jnp.mean(x * x, axis=-1, keepdims=True)
    return x * lax.rsqrt(ms + RMS_EPS) * g


def _halfnorm(x, gain):
    lo = lax.broadcasted_iota(jnp.int32, (1, LANE), 1) < HEAD_DIM
    x2 = x * x
    s_lo = jnp.sum(jnp.where(lo, x2, 0.0), axis=-1, keepdims=True)
    s_hi = jnp.sum(jnp.where(lo, 0.0, x2), axis=-1, keepdims=True)
    ms = jnp.where(lo, s_lo, s_hi) * (1.0 / HEAD_DIM)
    return x * lax.rsqrt(ms + RMS_EPS) * gain


def _ffn_kernel(*refs, ple):
    if ple:
        h_ref, g_ref, wi_ref, wo_ref, p_ref, gp_ref, wg_ref, wp_ref, o_ref = refs
    else:
        h_ref, g_ref, wi_ref, wo_ref, o_ref = refs
    h = h_ref[...]
    xn = _rms(h, g_ref[...]).astype(BF16)
    a = _dot(xn, wi_ref[:, :D_FF])
    b = _dot(xn, wi_ref[:, D_FF:])
    act = (a * jax.nn.sigmoid(a) * b).astype(BF16)
    h = h + 0.5 * _dot(act, wo_ref[...])
    if ple:
        xg = _rms(h, gp_ref[...]).astype(BF16)
        gate = jax.nn.sigmoid(_dot(xg, wg_ref[...]))
        h = h + gate * _dot(p_ref[...].astype(BF16), wp_ref[...])
    o_ref[...] = h


def _ffn(h, g, wi, wo, ple_args=None):
    n = h.shape[0]
    tm = _pick_tile(n, 256)
    row = lambda w: pl.BlockSpec((tm, w), lambda i: (i, 0))
    in_specs = [row(D_MODEL), _const_spec((1, D_MODEL)), _const_spec(wi.shape), _const_spec(wo.shape)]
    args = [h, g, wi, wo]
    if ple_args is not None:
        p, gp, wg, wp = ple_args
        in_specs += [row(D_PLE), _const_spec((1, D_MODEL)), _const_spec(wg.shape), _const_spec(wp.shape)]
        args += [p, gp, wg, wp]
    return pl.pallas_call(
        functools.partial(_ffn_kernel, ple=ple_args is not None),
        out_shape=jax.ShapeDtypeStruct((n, D_MODEL), F32),
        grid=(n // tm,),
        in_specs=in_specs,
        out_specs=row(D_MODEL),
        compiler_params=_cparams("parallel"),
    )(*args)


_C_U = 0
_C_Q = _C_U + D_SSM
_C_KV = _C_Q + Q_PAD
_C_GN = _C_KV + 6 * D_KV
_C_GA = _C_GN + LANE
_C_GB = _C_GA + D_MODEL
_C_END = _C_GB + D_MODEL


def _mix_in_kernel(h_ref, g_ref, w_ref, gq_ref, gks_ref, gkw_ref,
                   u_ref, qp_ref, kvc_ref, kvs_ref, kvw_ref, kvsb_ref, kvwb_ref, gn_ref, ga_ref, gb_ref):
    xn = _rms(h_ref[...], g_ref[...]).astype(BF16)
    u_ref[...] = _dot(xn, w_ref[:, _C_U:_C_Q])
    zq = _dot(xn, w_ref[:, _C_Q:_C_KV])
    for h in range(N_HEADS):
        qh = zq[:, h * LANE:(h + 1) * LANE]
        ms = jnp.sum(qh * qh, axis=-1, keepdims=True) * (1.0 / HEAD_DIM)
        qn = qh * lax.rsqrt(ms + RMS_EPS) * gq_ref[:, h * LANE:(h + 1) * LANE]
        qp_ref[:, h * LANE:(h + 1) * LANE] = qn
    zkv = _dot(xn, w_ref[:, _C_KV:_C_GN])
    kvc_ref[...] = zkv[:, 0:2 * D_KV]
    ks = _halfnorm(zkv[:, 2 * D_KV:3 * D_KV], gks_ref[...])
    vs = zkv[:, 3 * D_KV:4 * D_KV]
    kw = _halfnorm(zkv[:, 4 * D_KV:5 * D_KV], gkw_ref[...])
    vw = zkv[:, 5 * D_KV:6 * D_KV]
    kvs_ref[:, 0:D_KV] = ks
    kvs_ref[:, D_KV:] = vs
    kvw_ref[:, 0:D_KV] = kw
    kvw_ref[:, D_KV:] = vw
    kvsb_ref[:, 0:D_KV] = ks.astype(BF16)
    kvsb_ref[:, D_KV:2 * D_KV] = vs.astype(BF16)
    kvsb_ref[:, 2 * D_KV:] = jnp.ones((ks.shape[0], D_KV), BF16)
    kvwb_ref[:, 0:D_KV] = kw.astype(BF16)
    kvwb_ref[:, D_KV:] = vw.astype(BF16)
    gn_ref[...] = jax.nn.sigmoid(_dot(xn, w_ref[:, _C_GN:_C_GA]))
    ga_ref[...] = jax.nn.sigmoid(_dot(xn, w_ref[:, _C_GA:_C_GB]))
    gb_ref[...] = jax.nn.sigmoid(_dot(xn, w_ref[:, _C_GB:_C_END]))


def _mix_in(h, g, w, gq, gks, gkw):
    n = h.shape[0]
    tm = _pick_tile(n, 256)
    row = lambda width: pl.BlockSpec((tm, width), lambda i: (i, 0))
    widths = [(D_SSM, F32), (Q_PAD, F32), (2 * D_KV, F32), (2 * D_KV, F32), (2 * D_KV, F32),
              (3 * D_KV, BF16), (2 * D_KV, BF16), (LANE, F32), (D_MODEL, F32), (D_MODEL, F32)]
    return pl.pallas_call(
        _mix_in_kernel,
        out_shape=[jax.ShapeDtypeStruct((n, wd), dt) for wd, dt in widths],
        grid=(n // tm,),
        in_specs=[row(D_MODEL), _const_spec((1, D_MODEL)), _const_spec(w.shape),
                  _const_spec((1, Q_PAD)), _const_spec((1, LANE)), _const_spec((1, LANE))],
        out_specs=[row(wd) for wd, _ in widths],
        compiler_params=_cparams("parallel"),
    )(h, g, w, gq, gks, gkw)


def _gelu_tanh(x):
    return 0.5 * x * (1.0 + jnp.tanh(0.7978845608028654 * (x + 0.044715 * (x * x * x))))


def _s5_kernel(*refs, seg, carry):
    if carry:
        (u_ref, bm_ref, cr_ref, ci_ref, d_ref, lam_ref, pr_ref, pi_ref,
         y_ref, hr_out, hi_out, car_ref, cai_ref) = refs
    else:
        (u_ref, bm_ref, cr_ref, ci_ref, d_ref, lam_ref, pr_ref, pi_ref, h0r_ref, h0i_ref,
         y_ref, hr_out, hi_out) = refs
    u = u_ref[...]
    rows = u.shape[0]
    x = _dot(u.astype(BF16), bm_ref[...])
    xr = x[:, :N_STATE]
    xi = x[:, N_STATE:]
    pos = lax.broadcasted_iota(jnp.int32, (rows, 1), 0) % seg
    d, k = 1, 0
    while d < seg:
        lr = lam_ref[2 * k:2 * k + 1, :]
        li = lam_ref[2 * k + 1:2 * k + 2, :]
        keep = pos >= d
        sr = jnp.where(keep, pltpu.roll(xr, d, 0), 0.0)
        si = jnp.where(keep, pltpu.roll(xi, d, 0), 0.0)
        xr, xi = xr + lr * sr - li * si, xi + lr * si + li * sr
        d *= 2
        k += 1
    p_r = pr_ref[...]
    p_i = pi_ref[...]
    if carry:
        @pl.when(pl.program_id(0) == 0)
        def _():
            car_ref[...] = jnp.zeros_like(car_ref)
            cai_ref[...] = jnp.zeros_like(cai_ref)
        c_r = car_ref[...]
        c_i = cai_ref[...]
        hr_parts, hi_parts = [], []
        for j in range(rows // seg):
            tr = xr[j * seg:(j + 1) * seg]
            ti = xi[j * seg:(j + 1) * seg]
            hr_j = tr + p_r * c_r - p_i * c_i
            hi_j = ti + p_r * c_i + p_i * c_r
            c_r = hr_j[seg - 1:seg, :]
            c_i = hi_j[seg - 1:seg, :]
            hr_parts.append(hr_j)
            hi_parts.append(hi_j)
        hr = jnp.concatenate(hr_parts, axis=0)
        hi = jnp.concatenate(hi_parts, axis=0)
    else:
        c_r = h0r_ref[...]
        c_i = h0i_ref[...]
        hr = xr + p_r * c_r - p_i * c_i
        hi = xi + p_r * c_i + p_i * c_r
    y = _dot(hr.astype(BF16), cr_ref[...]) + _dot(hi.astype(BF16), ci_ref[...]) + d_ref[...] * u
    y_ref[...] = _gelu_tanh(y).astype(BF16)
    if carry:
        car_ref[...] = hr[rows - 1:rows, :]
        cai_ref[...] = hi[rows - 1:rows, :]
        hr_out[...] = hr[rows - 1:rows, :]
        hi_out[...] = hi[rows - 1:rows, :]
    else:
        hr_out[...] = hr
        hi_out[...] = hi


def _s5_prompt(u, t, sp):
    rows = Q_TILE
    consts = [sp["bmat"], sp["cr"], sp["ci"], sp["d"], sp["lam_p"], sp["pr_p"], sp["pi_p"]]
    return pl.pallas_call(
        functools.partial(_s5_kernel, seg=SCAN_SEG, carry=True),
        out_shape=[jax.ShapeDtypeStruct((t, D_SSM), BF16),
                   jax.ShapeDtypeStruct((1, N_STATE), F32), jax.ShapeDtypeStruct((1, N_STATE), F32)],
        grid=(t // rows,),
        in_specs=[pl.BlockSpec((rows, D_SSM), lambda i: (i, 0))] + [_const_spec(c.shape) for c in consts],
        out_specs=[pl.BlockSpec((rows, D_SSM), lambda i: (i, 0)),
                   _const_spec((1, N_STATE)), _const_spec((1, N_STATE))],
        scratch_shapes=[pltpu.VMEM((1, N_STATE), F32), pltpu.VMEM((1, N_STATE), F32)],
        compiler_params=_cparams("arbitrary"),
    )(u, *consts)


def _s5_sample(u, t, ns_rows, nq, h0r, h0i, sp):
    consts = [sp["bmat"], sp["cr"], sp["ci"], sp["d"], sp["lam_s"], sp["pr_s"], sp["pi_s"], h0r, h0i]
    blk = t // ns_rows
    return pl.pallas_call(
        functools.partial(_s5_kernel, seg=nq, carry=False),
        out_shape=[jax.ShapeDtypeStruct((ns_rows, D_SSM), BF16),
                   jax.ShapeDtypeStruct((ns_rows, N_STATE), F32), jax.ShapeDtypeStruct((ns_rows, N_STATE), F32)],
        grid=(1,),
        in_specs=[pl.BlockSpec((ns_rows, D_SSM), lambda i: (blk, 0))] + [_const_spec(c.shape) for c in consts],
        out_specs=[_const_spec((ns_rows, D_SSM)), _const_spec((ns_rows, N_STATE)), _const_spec((ns_rows, N_STATE))],
        compiler_params=_cparams("arbitrary"),
    )(u, *consts)


def _compress_part(get_xs, w_ref, a0, a1, part, rows):
    acc0 = acc1 = None
    for s in range(CMP_STRIDE):
        xs = get_xs(s)
        t0 = _dot(xs, w_ref[s])
        t1 = _dot(xs, w_ref[CMP_STRIDE + s])
        acc0 = t0 if acc0 is None else acc0 + t0
        acc1 = t1 if acc1 is None else acc1 + t1
    base = pl.multiple_of(part * rows, rows)
    a0[pl.ds(base, rows), :] = acc0
    a1[pl.ds(base, rows), :] = acc1


def _compress_finish(a0, a1, w_ref, pe_ref, gk_ref, kcc_ref, vcc_ref, nb):
    bias = []
    for j in range(CMP_BLOCK // CMP_STRIDE):
        b = jnp.zeros((8, 2 * D_KV), F32)
        for s in range(CMP_STRIDE):
            r = j * CMP_STRIDE + s
            b = b + _dot(jnp.broadcast_to(pe_ref[r:r + 1, :], (8, 2 * D_KV)).astype(BF16), w_ref[r])
        bias.append(b[0:1, :])
    out = (a0[...] + bias[0]) + pltpu.roll(a1[...] + bias[1], nb - 1, 0)
    valid = lax.broadcasted_iota(jnp.int32, (nb, 1), 0) < nb - 1
    out = jnp.where(valid, out, 0.0)
    kcc_ref[0] = _halfnorm(out[:, 0:D_KV], gk_ref[...]).astype(BF16)
    vcc_ref[0] = out[:, D_KV:].astype(BF16)


def _compress_rows_kernel(xk_ref, xv_ref, w_ref, pe_ref, gk_ref, kcc_ref, vcc_ref, a0, a1, *, nsplit, nb):
    part = pl.program_id(0)
    rows = nb // nsplit

    def get_xs(s):
        tok = pl.ds(s, rows, stride=CMP_STRIDE)
        return jnp.concatenate([xk_ref[tok, :], xv_ref[tok, :]], axis=1).astype(BF16)
    _compress_part(get_xs, w_ref, a0, a1, part, rows)

    @pl.when(part == nsplit - 1)
    def _():
        _compress_finish(a0, a1, w_ref, pe_ref, gk_ref, kcc_ref, vcc_ref, nb)


def _compress_rows(kv_cmp, t, cw):
    nb = t // CMP_STRIDE
    nsplit = 2
    return pl.pallas_call(
        functools.partial(_compress_rows_kernel, nsplit=nsplit, nb=nb),
        out_shape=[jax.ShapeDtypeStruct((1, nb, D_KV), BF16), jax.ShapeDtypeStruct((1, nb, D_KV), BF16)],
        grid=(nsplit,),
        in_specs=[pl.BlockSpec((t // nsplit, D_KV), lambda i: (i, 0)),
                  pl.BlockSpec((t // nsplit, D_KV), lambda i: (i, 1)),
                  _const_spec(cw["w"].shape), _const_spec(cw["pe"].shape), _const_spec((1, LANE))],
        out_specs=[_const_spec((1, nb, D_KV)), _const_spec((1, nb, D_KV))],
        scratch_shapes=[pltpu.VMEM((nb, 2 * D_KV), F32), pltpu.VMEM((nb, 2 * D_KV), F32)],
        compiler_params=_cparams("arbitrary"),
    )(kv_cmp, kv_cmp, cw["w"], cw["pe"], cw["gk"])


def _compress_kernel(pt_ref, pages_ref, w_ref, pe_ref, gk_ref, perm_ref,
                     kcc_ref, vcc_ref, buf, sem, x_scr, a0, a1, *, pps, nsplit, nb):
    s = pl.program_id(0)
    nsteps = pl.num_programs(0)

    def page_copy(step, p, slot):
        b = step // nsplit
        part = step % nsplit
        pg = pt_ref[b, part * pps + p]
        col = pl.multiple_of((p % 2) * PAGE_SIZE, PAGE_SIZE)
        return pltpu.make_async_copy(pages_ref.at[pg], buf.at[slot, p // 2, :, pl.ds(col, PAGE_SIZE)],
                                     sem.at[slot])

    def start(step, slot):
        def body(p, c):
            page_copy(step, p, slot).start()
            return c
        lax.fori_loop(0, pps, body, 0)

    def wait(step, slot):
        def body(p, c):
            page_copy(step, p, slot).wait()
            return c
        lax.fori_loop(0, pps, body, 0)

    @pl.when(s == 0)
    def _():
        start(s, 0)

    @pl.when(s + 1 < nsteps)
    def _():
        start(s + 1, (s + 1) % 2)

    slot = s % 2
    wait(s, slot)
    part = s % nsplit

    def to_rows(q, c):
        xt = buf[slot, q].astype(BF16)
        xp = _dot_nt(perm_ref[...], xt).astype(BF16)
        base = pl.multiple_of(q * CMP_STRIDE, CMP_STRIDE)
        for s2 in range(CMP_STRIDE):
            x_scr[s2, pl.ds(base, CMP_STRIDE), :] = xp[s2 * CMP_STRIDE:(s2 + 1) * CMP_STRIDE, :]
        return c
    lax.fori_loop(0, pps // 2, to_rows, 0)
    _compress_part(lambda s2: x_scr[s2], w_ref, a0, a1, part, nb // nsplit)

    @pl.when(part == nsplit - 1)
    def _():
        _compress_finish(a0, a1, w_ref, pe_ref, gk_ref, kcc_ref, vcc_ref, nb)


def _compress_paged(page_table, pages, cw):
    bsz, n_pages = page_table.shape
    nb = n_pages * (PAGE_SIZE // CMP_STRIDE)
    nsplit = 2
    pps = n_pages // nsplit
    const = lambda shape: pl.BlockSpec(shape, lambda s, pt: (0,) * len(shape), pipeline_mode=pl.Buffered(1))
    grid_spec = pltpu.PrefetchScalarGridSpec(
        num_scalar_prefetch=1,
        grid=(bsz * nsplit,),
        in_specs=[pl.BlockSpec(memory_space=pl.ANY), const(cw["w"].shape), const(cw["pe"].shape),
                  const((1, LANE)), const(cw["perm"].shape)],
        out_specs=[pl.BlockSpec((1, nb, D_KV), lambda s, pt: (s // nsplit, 0, 0)),
                   pl.BlockSpec((1, nb, D_KV), lambda s, pt: (s // nsplit, 0, 0))],
        scratch_shapes=[pltpu.VMEM((2, pps // 2, 2 * D_KV, 2 * PAGE_SIZE), F32), pltpu.SemaphoreType.DMA((2,)),
                        pltpu.VMEM((CMP_STRIDE, nb // nsplit, 2 * D_KV), BF16),
                        pltpu.VMEM((nb, 2 * D_KV), F32), pltpu.VMEM((nb, 2 * D_KV), F32)],
    )
    return pl.pallas_call(
        functools.partial(_compress_kernel, pps=pps, nsplit=nsplit, nb=nb),
        out_shape=[jax.ShapeDtypeStruct((bsz, nb, D_KV), BF16), jax.ShapeDtypeStruct((bsz, nb, D_KV), BF16)],
        grid_spec=grid_spec,
        compiler_params=_cparams("arbitrary"),
    )(page_table, pages, cw["w"], cw["pe"], cw["gk"], cw["perm"])


def _heads_rows(qp):
    return jnp.concatenate([qp[:, h * LANE:(h + 1) * LANE] for h in range(N_HEADS)], axis=0).astype(BF16)


def _slope(h):
    return 2.0 ** (-8.0 * (h + 1) / N_HEADS)


def _branch(q_all, tq, pieces):
    s_all = [_dot(q_all, k) if fm else _dot_nt(q_all, k) for k, _, _, _, fm in pieces]
    es, invs = [], []
    for h in range(N_HEADS):
        rows = slice(h * tq, (h + 1) * tq)
        sm = [jnp.where(pc[3], s[rows] - _slope(h) * pc[2], NEG) for s, pc in zip(s_all, pieces)]
        m = functools.reduce(jnp.maximum, [jnp.max(x, axis=-1, keepdims=True) for x in sm])
        m = jnp.maximum(m, HALF_NEG)
        e = [jnp.exp(x - m) for x in sm]
        l = functools.reduce(lambda a, b: a + b, [jnp.sum(x, axis=-1, keepdims=True) for x in e])
        es.append(e)
        invs.append(1.0 / jnp.maximum(l, 1e-30))
    o = None
    for i, (_, v, _, _, fm) in enumerate(pieces):
        p = jnp.concatenate([es[h][i] for h in range(N_HEADS)], axis=0).astype(BF16)
        t = _dot_nt(p, v) if fm else _dot(p, v)
        o = t if o is None else o + t
    return o * jnp.concatenate(invs, axis=0), es, invs


def _cmp_win_topk(q_all, tq, qpos, gn, kcc, vcc, mband, win_pieces, nb, ns, topn):
    n_io = lax.broadcasted_iota(jnp.int32, (1, nb), 1)
    kc_end = n_io * CMP_STRIDE + (CMP_BLOCK - 1)
    dist_c = (qpos - kc_end).astype(F32)
    mask_c = (dist_c >= 0) & (n_io < nb - 1)
    o_c, es, invs = _branch(q_all, tq, [(kcc, vcc, dist_c, mask_c, False)])
    o_w, _, _ = _branch(q_all, tq, win_pieces)
    imps = []
    for g in range(N_KV):
        psum = es[HPG * g][0] * invs[HPG * g]
        for hh in range(1, HPG):
            psum = psum + es[HPG * g + hh][0] * invs[HPG * g + hh]
        hi = psum.astype(BF16)
        r = psum - hi.astype(F32)
        mid = r.astype(BF16)
        lo = (r - mid.astype(F32)).astype(BF16)
        imps.append(_dot(hi, mband) + _dot(mid, mband) + _dot(lo, mband))
    imp = jnp.concatenate(imps, axis=0)
    blk = lax.broadcasted_iota(jnp.int32, (1, ns), 1)
    cur = jnp.concatenate([jnp.right_shift(qpos, 6)] * N_KV, axis=0)
    forced = (blk == 0) | (blk == cur) | (blk == cur - 1)
    score = jnp.where(forced, imp + SEL_BIG, jnp.where(blk <= cur, imp, -SEL_BIG))
    blkf = blk.astype(F32)
    sel = jnp.zeros_like(score)
    for _ in range(topn):
        m = jnp.max(score, axis=-1, keepdims=True)
        first = jnp.min(jnp.where(score == m, blkf, float(ns)), axis=-1, keepdims=True)
        hit = blkf == first
        sel = jnp.where(hit, 1.0, sel)
        score = jnp.where(hit, -jnp.inf, score)
    sel = jnp.where(blk <= cur, sel, 0.0)
    outs = []
    for h in range(N_HEADS):
        rows = slice(h * tq, (h + 1) * tq)
        outs.append(gn[:, h:h + 1] * o_c[rows] + gn[:, 2 * N_HEADS + h:2 * N_HEADS + h + 1] * o_w[rows])
    return jnp.concatenate(outs, axis=1), sel


def _e1_prompt_kernel(qp_ref, gn_ref, kcc_ref, vcc_ref, mband_ref, kvw_ref, ocw_ref, sel_ref, flag_ref,
                      *, nb, ns):
    tq = Q_TILE
    s0 = pl.program_id(0) * tq
    qpos = s0 + lax.broadcasted_iota(jnp.int32, (tq, 1), 0)
    q_all = _heads_rows(qp_ref[...])
    start = pl.multiple_of(jnp.maximum(s0 - WINDOW, 0), Q_TILE)
    kw = kvw_ref[pl.ds(start, WIN_KEYS), 0:D_KV]
    vw = kvw_ref[pl.ds(start, WIN_KEYS), D_KV:]
    kpos = start + lax.broadcasted_iota(jnp.int32, (1, WIN_KEYS), 1)
    dist_w = (qpos - kpos).astype(F32)
    mask_w = (dist_w >= 0) & (dist_w < WINDOW)
    ocw, sel = _cmp_win_topk(q_all, tq, qpos, gn_ref[...], kcc_ref[0], vcc_ref[0], mband_ref[...],
                             [(kw, vw, dist_w, mask_w, False)], nb, ns, TOP_N)
    ocw_ref[...] = ocw
    sel_ref[0] = sel[:tq]
    sel_ref[1] = sel[tq:]
    colany = jnp.max(sel, axis=0, keepdims=True)
    j_io = lax.broadcasted_iota(jnp.int32, (ns, LANE), 0)
    c_io = lax.broadcasted_iota(jnp.int32, (ns, LANE), 1)
    grp = jnp.where(jnp.right_shift(j_io, 2) == c_io, 1.0, 0.0).astype(BF16)
    cnt = _dot(jnp.broadcast_to(colany, (8, ns)).astype(BF16), grp)
    flag_ref[0] = (cnt > 0.5).astype(jnp.int32)


def _e1_prompt(qp, gn, kcc, vcc, mband, kvw_bf, t):
    nb, ns = mband.shape
    nt = t // Q_TILE
    row = lambda w: pl.BlockSpec((Q_TILE, w), lambda i: (i, 0))
    return pl.pallas_call(
        functools.partial(_e1_prompt_kernel, nb=nb, ns=ns),
        out_shape=[jax.ShapeDtypeStruct((t, Q_PAD), F32), jax.ShapeDtypeStruct((N_KV, t, ns), F32),
                   jax.ShapeDtypeStruct((nt, 8, LANE), jnp.int32)],
        grid=(nt,),
        in_specs=[row(Q_PAD), row(LANE), _const_spec((1, nb, D_KV)), _const_spec((1, nb, D_KV)),
                  _const_spec(mband.shape), _const_spec(kvw_bf.shape)],
        out_specs=[row(Q_PAD), pl.BlockSpec((N_KV, Q_TILE, ns), lambda i: (0, i, 0)),
                   pl.BlockSpec((1, 8, LANE), lambda i: (i, 0, 0))],
        compiler_params=_cparams("parallel"),
    )(qp, gn, kcc, vcc, mband, kvw_bf)


def _pad_rows(x, rows):
    return jnp.concatenate([x, jnp.zeros((rows - x.shape[0], x.shape[1]), x.dtype)], axis=0)


def _e1_sample_kernel(qp_ref, gn_ref, kcc_ref, vcc_ref, mband_ref, cwin_ref, nwin_ref, ocw_ref, sel_ref,
                      *, nb, ns, nq, past):
    qi = lax.broadcasted_iota(jnp.int32, (nq, 1), 0)
    qpos = past + qi
    q_all = _heads_rows(qp_ref[...])
    cw = cwin_ref[0]
    w_buf = cw.shape[1]
    j_c = lax.broadcasted_iota(jnp.int32, (1, w_buf), 1)
    dist_cw = (qi + (w_buf - j_c)).astype(F32)
    mask_cw = (dist_cw >= 0) & (dist_cw < WINDOW)
    nw = _pad_rows(nwin_ref[...], LANE)
    j_n = lax.broadcasted_iota(jnp.int32, (1, LANE), 1)
    dist_nw = (qi - j_n).astype(F32)
    mask_nw = (dist_nw >= 0) & (j_n < nq)
    pieces = [(cw[0:D_KV, :].astype(BF16), cw[D_KV:, :].astype(BF16), dist_cw, mask_cw, True),
              (nw[:, 0:D_KV].astype(BF16), nw[:, D_KV:].astype(BF16), dist_nw, mask_nw, False)]
    ocw, sel = _cmp_win_topk(q_all, nq, qpos, gn_ref[...], kcc_ref[0], vcc_ref[0], mband_ref[...],
                             pieces, nb, ns, TOP_N - 1)
    ocw_ref[...] = ocw
    sel_ref[0, 0] = sel[:nq]
    sel_ref[0, 1] = sel[nq:]


def _e1_sample(qp, gn, kcc, vcc, mband, cache_win, win_off, kv_win, t, bsz, nq, past):
    nb, ns = mband.shape
    w_buf = cache_win.shape[2]
    off = t // nq
    row = lambda w: pl.BlockSpec((nq, w), lambda b: (off + b, 0))
    per_b = lambda shape: pl.BlockSpec((1,) + shape, lambda b: (b,) + (0,) * len(shape))
    return pl.pallas_call(
        functools.partial(_e1_sample_kernel, nb=nb, ns=ns, nq=nq, past=past),
        out_shape=[jax.ShapeDtypeStruct((bsz * nq, Q_PAD), F32), jax.ShapeDtypeStruct((bsz, N_KV, nq, ns), F32)],
        grid=(bsz,),
        in_specs=[row(Q_PAD), row(LANE), per_b((nb, D_KV)), per_b((nb, D_KV)), _const_spec(mband.shape),
                  pl.BlockSpec((1, 2 * D_KV, w_buf), lambda b: (win_off + b, 0, 0)), row(2 * D_KV)],
        out_specs=[pl.BlockSpec((nq, Q_PAD), lambda b: (b, 0)), per_b((N_KV, nq, ns))],
        compiler_params=_cparams("parallel"),
    )(qp, gn, kcc, vcc, mband, cache_win, kv_win)


def _online_step(s, dist, masks, v_ones, m_ref, acc_ref, tq, feature_major=False):
    reps = s.shape[1] // LANE
    ps, alphas = [], []
    for h in range(N_HEADS):
        rows = slice(h * tq, (h + 1) * tq)
        sm = jnp.where(masks[h // HPG], s[rows] - _slope(h) * dist, NEG)
        m_old = m_ref[rows, :]
        m_new = jnp.maximum(m_old, jnp.max(sm, axis=-1, keepdims=True))
        alphas.append(jnp.exp(m_old - m_new))
        ps.append(jnp.exp(sm - jnp.tile(m_new, (1, reps))))
        m_ref[rows, :] = m_new
    p_all = jnp.concatenate(ps, axis=0).astype(BF16)
    a_all = jnp.concatenate(alphas, axis=0)
    pv = _dot_nt(p_all, v_ones) if feature_major else _dot(p_all, v_ones)
    acc_ref[...] = jnp.tile(a_all, (1, 2)) * acc_ref[...] + pv


def _expand_sel(sel, first_blk, n_keys):
    ns = sel.shape[1]
    j_io = lax.broadcasted_iota(jnp.int32, (ns, n_keys), 0)
    kb = first_blk + jnp.right_shift(lax.broadcasted_iota(jnp.int32, (ns, n_keys), 1), 6)
    e = jnp.where(j_io == kb, 1.0, 0.0).astype(BF16)
    return _dot(sel.astype(BF16), e) > 0.5


def _init_online(m_ref, acc_ref):
    m_ref[...] = jnp.full(m_ref.shape, HALF_NEG, F32)
    acc_ref[...] = jnp.zeros(acc_ref.shape, F32)


def _finish_online(ocw, gn, acc_ref, tq):
    o_s = acc_ref[:, 0:LANE] * (1.0 / jnp.maximum(acc_ref[:, LANE:], 1e-30))
    outs = []
    for h in range(N_HEADS):
        g1 = gn[:, N_HEADS + h:N_HEADS + h + 1]
        outs.append(ocw[:, h * LANE:(h + 1) * LANE] + g1 * o_s[h * tq:(h + 1) * tq])
    return jnp.concatenate(outs, axis=1)


def _e2_prompt_kernel(flags_ref, qp_ref, sel_ref, gn_ref, ocw_ref, kvs_ref, out_ref, m_ref, acc_ref,
                      *, nchunks):
    tq = Q_TILE
    i = pl.program_id(0)
    s0 = i * tq
    qpos = s0 + lax.broadcasted_iota(jnp.int32, (tq, 1), 0)
    q_all = _heads_rows(qp_ref[...])
    _init_online(m_ref, acc_ref)

    def body(c, carry):
        @pl.when(flags_ref[i * nchunks + c] > 0)
        def _():
            base = pl.multiple_of(c * SEL_CHUNK, SEL_CHUNK)
            k = kvs_ref[pl.ds(base, SEL_CHUNK), 0:D_KV]
            v_ones = kvs_ref[pl.ds(base, SEL_CHUNK), D_KV:]
            s = _dot_nt(q_all, k)
            kpos = base + lax.broadcasted_iota(jnp.int32, (1, SEL_CHUNK), 1)
            dist = (qpos - kpos).astype(F32)
            first_blk = c * (SEL_CHUNK // SEL_BLOCK)
            masks = [_expand_sel(sel_ref[g], first_blk, SEL_CHUNK) & (dist >= 0) for g in range(N_KV)]
            _online_step(s, dist, masks, v_ones, m_ref, acc_ref, tq)
        return carry

    lax.fori_loop(0, s0 // SEL_CHUNK + 1, body, 0)
    out_ref[...] = _finish_online(ocw_ref[...], gn_ref[...], acc_ref, tq)


def _e2_prompt(flags, qp, sel, gn, ocw, kvs_bf, t):
    ns = sel.shape[2]
    nt = t // Q_TILE
    nchunks = flags.shape[0] // nt
    row = lambda w: pl.BlockSpec((Q_TILE, w), lambda i, f: (i, 0))
    grid_spec = pltpu.PrefetchScalarGridSpec(
        num_scalar_prefetch=1,
        grid=(nt,),
        in_specs=[row(Q_PAD), pl.BlockSpec((N_KV, Q_TILE, ns), lambda i, f: (0, i, 0)), row(LANE), row(Q_PAD),
                  pl.BlockSpec(kvs_bf.shape, lambda i, f: (0, 0))],
        out_specs=row(Q_PAD),
        scratch_shapes=[pltpu.VMEM((N_HEADS * Q_TILE, LANE), F32), pltpu.VMEM((N_HEADS * Q_TILE, 2 * LANE), F32)],
    )
    return pl.pallas_call(
        functools.partial(_e2_prompt_kernel, nchunks=nchunks),
        out_shape=jax.ShapeDtypeStruct((t, Q_PAD), F32),
        grid_spec=grid_spec,
        compiler_params=_cparams("arbitrary"),
    )(flags, qp, sel, gn, ocw, kvs_bf)


def _e2_sample_kernel(pt_ref, pages_ref, qp_ref, sel_ref, gn_ref, ocw_ref, nsel_ref, expand_ref, out_ref,
                      buf, sem, m_ref, acc_ref, *, pps, nsplit, nq):
    s = pl.program_id(0)
    nsteps = pl.num_programs(0)
    n_keys = pps * PAGE_SIZE

    def page_copy(step, p, slot):
        b = step // nsplit
        part = step % nsplit
        pg = pt_ref[b, part * pps + p]
        col = pl.multiple_of(p * PAGE_SIZE, PAGE_SIZE)
        return pltpu.make_async_copy(pages_ref.at[pg], buf.at[slot, :, pl.ds(col, PAGE_SIZE)], sem.at[slot])

    def start(step, slot):
        def body(p, c):
            page_copy(step, p, slot).start()
            return c
        lax.fori_loop(0, pps, body, 0)

    def wait(step, slot):
        def body(p, c):
            page_copy(step, p, slot).wait()
            return c
        lax.fori_loop(0, pps, body, 0)

    @pl.when(s == 0)
    def _():
        start(s, 0)

    @pl.when(s + 1 < nsteps)
    def _():
        start(s + 1, (s + 1) % 2)

    slot = s % 2
    part = s % nsplit
    qi = lax.broadcasted_iota(jnp.int32, (nq, 1), 0)
    q_all = _heads_rows(qp_ref[...])

    @pl.when(part == 0)
    def _():
        _init_online(m_ref, acc_ref)

    wait(s, slot)
    k = buf[slot, 0:D_KV, :].astype(BF16)
    v_ones = jnp.concatenate([buf[slot, D_KV:, :].astype(BF16), jnp.ones((D_KV, n_keys), BF16)], axis=0)
    sc = _dot(q_all, k)
    back = (nsplit - part) * n_keys - lax.broadcasted_iota(jnp.int32, (1, n_keys), 1)
    dist = (qi + back).astype(F32)
    masks = [_dot(sel_ref[0, g, 0].astype(BF16), expand_ref[...]) > 0.5 for g in range(N_KV)]
    _online_step(sc, dist, masks, v_ones, m_ref, acc_ref, nq, feature_major=True)

    @pl.when(part == nsplit - 1)
    def _():
        nw = _pad_rows(nsel_ref[...], LANE)
        j_n = lax.broadcasted_iota(jnp.int32, (1, LANE), 1)
        dist_n = (qi - j_n).astype(F32)
        mask_n = (dist_n >= 0) & (j_n < nq)
        s_n = _dot_nt(q_all, nw[:, 0:D_KV].astype(BF16))
        vn_ones = jnp.concatenate([nw[:, D_KV:].astype(BF16), jnp.ones((LANE, D_KV), BF16)], axis=1)
        _online_step(s_n, dist_n, [mask_n, mask_n], vn_ones, m_ref, acc_ref, nq)
        out_ref[...] = _finish_online(ocw_ref[...], gn_ref[...], acc_ref, nq)


def _e2_sample(page_table, pages, qp, sel, gn, ocw, kv_sel, t, nq):
    bsz, n_pages = page_table.shape
    ns = sel.shape[3]
    nsplit = 4
    pps = n_pages // nsplit
    n_keys = pps * PAGE_SIZE
    nblk = ns // nsplit
    off = t // nq
    sel_parts = sel.reshape(bsz, N_KV, nq, nsplit, nblk).transpose(0, 1, 3, 2, 4)
    expand = (jnp.arange(nblk)[:, None] == jnp.arange(n_keys)[None, :] // SEL_BLOCK).astype(BF16)
    row = lambda w: pl.BlockSpec((nq, w), lambda s, pt: (off + s // nsplit, 0))
    grid_spec = pltpu.PrefetchScalarGridSpec(
        num_scalar_prefetch=1,
        grid=(bsz * nsplit,),
        in_specs=[pl.BlockSpec(memory_space=pl.ANY), row(Q_PAD),
                  pl.BlockSpec((1, N_KV, 1, nq, nblk), lambda s, pt: (s // nsplit, 0, s % nsplit, 0, 0)), row(LANE),
                  pl.BlockSpec((nq, Q_PAD), lambda s, pt: (s // nsplit, 0)), row(2 * D_KV),
                  pl.BlockSpec((nblk, n_keys), lambda s, pt: (0, 0), pipeline_mode=pl.Buffered(1))],
        out_specs=pl.BlockSpec((nq, Q_PAD), lambda s, pt: (s // nsplit, 0)),
        scratch_shapes=[pltpu.VMEM((2, 2 * D_KV, n_keys), F32), pltpu.SemaphoreType.DMA((2,)),
                        pltpu.VMEM((N_HEADS * nq, LANE), F32), pltpu.VMEM((N_HEADS * nq, 2 * LANE), F32)],
    )
    return pl.pallas_call(
        functools.partial(_e2_sample_kernel, pps=pps, nsplit=nsplit, nq=nq),
        out_shape=jax.ShapeDtypeStruct((bsz * nq, Q_PAD), F32),
        grid_spec=grid_spec,
        compiler_params=_cparams("arbitrary"),
    )(page_table, pages, qp, sel_parts, gn, ocw, kv_sel, expand)


def _post_kernel(h_ref, yg_ref, oatt_ref, ga_ref, gb_ref, wglu_ref, watt_ref, wout_ref, o_ref):
    gl = _dot(yg_ref[...], wglu_ref[...])
    br_a = gl[:, :D_MODEL] * jax.nn.sigmoid(gl[:, D_MODEL:])
    br_b = _dot(oatt_ref[...].astype(BF16), watt_ref[...])
    merged = (ga_ref[...] * br_a + gb_ref[...] * br_b).astype(BF16)
    o_ref[...] = h_ref[...] + _dot(merged, wout_ref[...])


def _post(h, yg, oatt, ga, gb, wglu, watt, wout):
    n = h.shape[0]
    tm = _pick_tile(n, 256)
    row = lambda w: pl.BlockSpec((tm, w), lambda i: (i, 0))
    return pl.pallas_call(
        _post_kernel,
        out_shape=jax.ShapeDtypeStruct((n, D_MODEL), F32),
        grid=(n // tm,),
        in_specs=[row(D_MODEL), row(D_SSM), row(Q_PAD), row(D_MODEL), row(D_MODEL),
                  _const_spec(wglu.shape), _const_spec(watt.shape), _const_spec(wout.shape)],
        out_specs=row(D_MODEL),
        compiler_params=_cparams("parallel"),
    )(h, yg, oatt, ga, gb, wglu, watt, wout)


def _head_pad_index():
    h = jnp.arange(N_HEADS)[:, None]
    d = jnp.arange(HEAD_DIM)[None, :]
    return (LANE * h + HEAD_DIM * (h // HPG) + d).reshape(-1)


def _prep_mix_weights(w_in, qk_norm):
    idx = _head_pad_index()
    wq = jnp.zeros((D_MODEL, Q_PAD), F32).at[:, idx].set(w_in[:, D_SSM:D_SSM + N_HEADS * HEAD_DIM])
    c0 = D_SSM + N_HEADS * HEAD_DIM
    c1 = c0 + 6 * D_KV
    c2 = c1 + 3 * N_HEADS
    wgn = jnp.zeros((D_MODEL, LANE), F32).at[:, :3 * N_HEADS].set(w_in[:, c1:c2])
    w = jnp.concatenate([w_in[:, :D_SSM], wq, w_in[:, c0:c1], wgn, w_in[:, c2:]], axis=1).astype(BF16)
    gq = jnp.zeros((Q_PAD,), F32).at[idx].set(jnp.tile(qk_norm[0] * (HEAD_DIM ** -0.5), N_HEADS))[None]
    gks = jnp.tile(qk_norm[2], N_KV)[None]
    gkw = jnp.tile(qk_norm[3], N_KV)[None]
    return w, gq, gks, gkw


def _prep_att_out(w_att_out):
    return jnp.zeros((Q_PAD, D_MODEL), F32).at[_head_pad_index()].set(w_att_out).astype(BF16)


def _prep_compress(pe_k, pe_v, w_k, w_v, gain_k):
    blocks = jnp.stack([w_k, w_k, w_v, w_v], axis=1)
    w = jnp.einsum("scde,ch->scdhe", blocks, jnp.eye(2 * N_KV, dtype=F32))
    pe = jnp.concatenate([pe_k, pe_k, pe_v, pe_v], axis=1)
    r = jnp.arange(2 * PAGE_SIZE)
    perm = (r[None, :] == (CMP_STRIDE * (r % CMP_STRIDE) + r // CMP_STRIDE)[:, None]).astype(BF16)
    return {"w": w.reshape(CMP_BLOCK, 2 * D_KV, 2 * D_KV).astype(BF16), "pe": pe,
            "gk": jnp.tile(gain_k, N_KV)[None], "perm": perm}


def _prep_s5(a_re, a_im, log_dt, b_re, b_im, c_re, c_im, d, nq):
    dt = jnp.exp(log_dt)[:, None]
    mag = jnp.exp(a_re * dt)
    lr = mag * jnp.cos(a_im * dt)
    li = mag * jnp.sin(a_im * dt)
    den = a_re * a_re + a_im * a_im
    fr = ((lr - 1.0) * a_re + li * a_im) / den
    fi = (li * a_re - (lr - 1.0) * a_im) / den
    bbr = fr[..., None] * b_re - fi[..., None] * b_im
    bbi = fr[..., None] * b_im + fi[..., None] * b_re
    eye = jnp.eye(N_SSM_GROUPS, dtype=F32)
    blk_b = lambda m: jnp.einsum("gpc,gh->gchp", m, eye).reshape(D_SSM, N_STATE)
    blk_c = lambda m: jnp.einsum("gcp,gh->gphc", m, eye).reshape(N_STATE, D_SSM)

    def lam_pow(k):
        kk = k.astype(F32)[:, None, None]
        m = jnp.exp(a_re * dt * kk)
        th = a_im * dt * kk
        return (m * jnp.cos(th)).reshape(-1, N_STATE), (m * jnp.sin(th)).reshape(-1, N_STATE)

    def step_table(seg):
        ks = []
        dd = 1
        while dd < seg:
            ks.append(dd)
            dd *= 2
        re, im = lam_pow(jnp.array(ks))
        return jnp.stack([re, im], axis=1).reshape(-1, N_STATE)

    sp = {"bmat": jnp.concatenate([blk_b(bbr), blk_b(bbi)], axis=1).astype(BF16),
          "cr": blk_c(c_re).astype(BF16), "ci": (-blk_c(c_im)).astype(BF16), "d": d[None],
          "lam_p": step_table(SCAN_SEG), "lam_s": step_table(nq)}
    sp["pr_p"], sp["pi_p"] = lam_pow(jnp.arange(SCAN_SEG) + 1)
    return sp, lam_pow


def _band_matrix(nb, ns):
    ratio = SEL_BLOCK // CMP_STRIDE
    lo = CMP_BLOCK // CMP_STRIDE - 1
    c = jnp.arange(nb)[:, None]
    j = jnp.arange(ns)[None, :]
    return ((c >= ratio * j - lo) & (c <= ratio * j + ratio - 1)).astype(BF16)


def kernel(x_prompt, x_sample, cache_kv_cmp, cache_kv_sel, cache_kv_win, state_ssm_re, state_ssm_im, page_table, p_prompt, p_sample, norm_ffn1, w_ffn1_in, w_ffn1_out, norm_mix, w_in, qk_norm, ssm_a_re, ssm_a_im, ssm_log_dt, ssm_b_re, ssm_b_im, ssm_c_re, ssm_c_im, ssm_d, w_glu, cmp_pe_k, cmp_pe_v, cmp_w_k, cmp_w_v, w_att_out, w_out, norm_ffn2, w_ffn2_in, w_ffn2_out, norm_ple, w_ple_gate, w_ple_proj):
    bp, t = x_prompt.shape[:2]
    bsz, nq = x_sample.shape[:2]
    n_pages = page_table.shape[1]
    past = n_pages * PAGE_SIZE
    n_pool = cache_kv_cmp.shape[1]
    w_buf = cache_kv_win.shape[2]
    ns_rows = bsz * nq
    assert bp == 1 and t % SEL_CHUNK == 0 and t >= WIN_KEYS and t % ns_rows == 0 and nq < CMP_STRIDE
    assert past == t and w_buf == WINDOW
    nb = t // CMP_STRIDE
    ns = t // SEL_BLOCK
    nt = t // Q_TILE
    nchunks = t // SEL_CHUNK
    mband = _band_matrix(nb, ns)
    page_table = page_table.astype(jnp.int32)
    feat_major = lambda c: jnp.transpose(c, (0, 1, 3, 4, 5, 2)).reshape(DEPTH * c.shape[1], 2 * D_KV, c.shape[2])
    pages_cmp = feat_major(cache_kv_cmp)
    pages_sel = feat_major(cache_kv_sel)
    pages_win = feat_major(cache_kv_win)

    h = jnp.concatenate([x_prompt[0], x_sample.reshape(ns_rows, D_MODEL)], axis=0)
    st_p = [[] for _ in range(5)]
    st_s = [[] for _ in range(5)]
    for i in range(DEPTH):
        row1 = lambda a: a[i][None]
        h = _ffn(h, row1(norm_ffn1), w_ffn1_in[i].astype(BF16), w_ffn1_out[i].astype(BF16))
        w_mix, gq, gks, gkw = _prep_mix_weights(w_in[i], qk_norm[i])
        u, qp, kv_cmp, kv_sel, kv_win, kvs_bf, kvw_bf, gn, ga, gb = _mix_in(h, row1(norm_mix), w_mix, gq, gks, gkw)

        sp, lam_pow = _prep_s5(ssm_a_re[i], ssm_a_im[i], ssm_log_dt[i], ssm_b_re[i], ssm_b_im[i],
                               ssm_c_re[i], ssm_c_im[i], ssm_d[i], nq)
        sp["pr_s"], sp["pi_s"] = lam_pow(jnp.arange(ns_rows) % nq + 1)
        yg_p, hr_p, hi_p = _s5_prompt(u, t, sp)
        h0r = jnp.repeat(state_ssm_re[i].reshape(bsz, N_STATE), nq, axis=0)
        h0i = jnp.repeat(state_ssm_im[i].reshape(bsz, N_STATE), nq, axis=0)
        yg_s, hr_s, hi_s = _s5_sample(u, t, ns_rows, nq, h0r, h0i, sp)

        cw = _prep_compress(cmp_pe_k[i], cmp_pe_v[i], cmp_w_k[i], cmp_w_v[i], qk_norm[i, 1])
        layer_pages = page_table + i * n_pool
        kcc_p, vcc_p = _compress_rows(kv_cmp, t, cw)
        kcc_s, vcc_s = _compress_paged(layer_pages, pages_cmp, cw)
        ocw_p, sel_p, flags = _e1_prompt(qp, gn, kcc_p, vcc_p, mband, kvw_bf, t)
        ocw_s, sel_s = _e1_sample(qp, gn, kcc_s, vcc_s, mband, pages_win, i * bsz, kv_win, t, bsz, nq, past)
        flags = flags[:, 0, :nchunks].reshape(-1)
        oatt_p = _e2_prompt(flags, qp, sel_p, gn, ocw_p, kvs_bf, t)
        oatt_s = _e2_sample(layer_pages, pages_sel, qp, sel_s, gn, ocw_s, kv_sel, t, nq)

        h = _post(h, jnp.concatenate([yg_p, yg_s], axis=0), jnp.concatenate([oatt_p, oatt_s], axis=0), ga, gb,
                  w_glu[i].astype(BF16), _prep_att_out(w_att_out[i]), w_out[i].astype(BF16))
        p_all = jnp.concatenate([p_prompt[i, 0], p_sample[i].reshape(ns_rows, D_PLE)], axis=0)
        h = _ffn(h, row1(norm_ffn2), w_ffn2_in[i].astype(BF16), w_ffn2_out[i].astype(BF16),
                 (p_all, row1(norm_ple), w_ple_gate[i].astype(BF16), w_ple_proj[i].astype(BF16)))

        kv5 = lambda a, lead: a.reshape(lead + (2, N_KV, HEAD_DIM))
        st_p[0].append(kv5(kv_cmp[:t], (1, t)))
        st_p[1].append(kv5(kv_sel[:t], (1, t)))
        st_p[2].append(kv5(kv_win[t - min(WINDOW, t):t], (1, min(WINDOW, t))))
        st_p[3].append(hr_p.reshape(1, N_SSM_GROUPS, P_STATE))
        st_p[4].append(hi_p.reshape(1, N_SSM_GROUPS, P_STATE))
        st_s[0].append(kv5(kv_cmp[t:], (bsz, nq)))
        st_s[1].append(kv5(kv_sel[t:], (bsz, nq)))
        st_s[2].append(jnp.concatenate([cache_kv_win[i, :, nq:], kv5(kv_win[t:], (bsz, nq))], axis=1))
        st_s[3].append(hr_s[nq - 1::nq].reshape(bsz, N_SSM_GROUPS, P_STATE))
        st_s[4].append(hi_s[nq - 1::nq].reshape(bsz, N_SSM_GROUPS, P_STATE))

    outs_p = [jnp.stack(a) for a in st_p]
    outs_s = [jnp.stack(a) for a in st_s]
    y_prompt = h[:t][None]
    y_sample = h[t:].reshape(bsz, nq, D_MODEL)
    return (y_prompt, y_sample, *outs_p, *outs_s)
```

```python
import functools

import jax
import jax.numpy as jnp
from jax import lax
from jax.experimental import pallas as pl
from jax.experimental.pallas import tpu as pltpu

F32 = jnp.float32
BF16 = jnp.bfloat16

D_MODEL = 1024
DEPTH = 2
D_SSM = 512
SSM_GROUP = 16
N_SSM_GROUPS = 32
P_STATE = 64
N_STATE = N_SSM_GROUPS * P_STATE
N_HEADS = 8
HEAD_DIM = 64
N_KV = 2
HPG = 4
D_KV = 128
CMP_BLOCK = 32
CMP_STRIDE = 16
SEL_BLOCK = 64
TOP_N = 16
WINDOW = 512
PAGE_SIZE = 128
SEL_BIG = 1e4
D_FF = 2816
D_PLE = 256
RMS_EPS = 1e-6
NEG = -1e30
HALF_NEG = -0.5e30

LANE = 128
Q_TILE = 128
SCAN_SEG = 8
SEL_CHUNK = 256
WIN_KEYS = WINDOW + Q_TILE
Q_PAD = N_HEADS * LANE
VMEM_LIMIT = 56 * 2 ** 20


def _cparams(*sem):
    return pltpu.CompilerParams(dimension_semantics=sem, vmem_limit_bytes=VMEM_LIMIT)


def _dot(a, b):
    return jnp.dot(a, b, preferred_element_type=F32)


def _dot_nt(a, b):
    return lax.dot_general(a, b, (((1,), (1,)), ((), ())), preferred_element_type=F32)


def _pick_tile(n, target):
    for t in range(min(n, target), 15, -1):
        if n % t == 0 and t % 16 == 0:
            return t
    raise ValueError(f"no row tile for {n}")


def _const_spec(shape):
    nd = len(shape)
    return pl.BlockSpec(shape, lambda *_: (0,) * nd, pipeline_mode=pl.Buffered(1))


def _rms(x, g):
    ms = jnp.mean(x * x, axis=-1, keepdims=True)
    return x * lax.rsqrt(ms + RMS_EPS) * g


def _halfnorm(x, gain):
    lo = lax.broadcasted_iota(jnp.int32, (1, LANE), 1) < HEAD_DIM
    x2 = x * x
    s_lo = jnp.sum(jnp.where(lo, x2, 0.0), axis=-1, keepdims=True)
    s_hi = jnp.sum(jnp.where(lo, 0.0, x2), axis=-1, keepdims=True)
    ms = jnp.where(lo, s_lo, s_hi) * (1.0 / HEAD_DIM)
    return x * lax.rsqrt(ms + RMS_EPS) * gain


def _ffn_kernel(*refs, ple):
    if ple:
        h_ref, g_ref, wi_ref, wo_ref, p_ref, gp_ref, wg_ref, wp_ref, o_ref = refs
    else:
        h_ref, g_ref, wi_ref, wo_ref, o_ref = refs
    h = h_ref[...]
    xn = _rms(h, g_ref[...]).astype(BF16)
    a = _dot(xn, wi_ref[:, :D_FF])
    b = _dot(xn, wi_ref[:, D_FF:])
    act = (a * jax.nn.sigmoid(a) * b).astype(BF16)
    h = h + 0.5 * _dot(act, wo_ref[...])
    if ple:
        xg = _rms(h, gp_ref[...]).astype(BF16)
        gate = jax.nn.sigmoid(_dot(xg, wg_ref[...]))
        h = h + gate * _dot(p_ref[...].astype(BF16), wp_ref[...])
    o_ref[...] = h


def _ffn(h, g, wi, wo, ple_args=None):
    n = h.shape[0]
    tm = _pick_tile(n, 256)
    row = lambda w: pl.BlockSpec((tm, w), lambda i: (i, 0))
    in_specs = [row(D_MODEL), _const_spec((1, D_MODEL)), _const_spec(wi.shape), _const_spec(wo.shape)]
    args = [h, g, wi, wo]
    if ple_args is not None:
        p, gp, wg, wp = ple_args
        in_specs += [row(D_PLE), _const_spec((1, D_MODEL)), _const_spec(wg.shape), _const_spec(wp.shape)]
        args += [p, gp, wg, wp]
    return pl.pallas_call(
        functools.partial(_ffn_kernel, ple=ple_args is not None),
        out_shape=jax.ShapeDtypeStruct((n, D_MODEL), F32),
        grid=(n // tm,),
        in_specs=in_specs,
        out_specs=row(D_MODEL),
        compiler_params=_cparams("parallel"),
    )(*args)


_C_U = 0
_C_Q = _C_U + D_SSM
_C_KV = _C_Q + Q_PAD
_C_GN = _C_KV + 6 * D_KV
_C_GA = _C_GN + LANE
_C_GB = _C_GA + D_MODEL
_C_END = _C_GB + D_MODEL


def _mix_in_kernel(h_ref, g_ref, w_ref, gq_ref, gks_ref, gkw_ref,
                   u_ref, qp_ref, kvc_ref, kvs_ref, kvw_ref, kvsb_ref, kvwb_ref, gn_ref, ga_ref, gb_ref):
    xn = _rms(h_ref[...], g_ref[...]).astype(BF16)
    u_ref[...] = _dot(xn, w_ref[:, _C_U:_C_Q])
    zq = _dot(xn, w_ref[:, _C_Q:_C_KV])
    for h in range(N_HEADS):
        qh = zq[:, h * LANE:(h + 1) * LANE]
        ms = jnp.sum(qh * qh, axis=-1, keepdims=True) * (1.0 / HEAD_DIM)
        qn = qh * lax.rsqrt(ms + RMS_EPS) * gq_ref[:, h * LANE:(h + 1) * LANE]
        qp_ref[:, h * LANE:(h + 1) * LANE] = qn
    zkv = _dot(xn, w_ref[:, _C_KV:_C_GN])
    kvc_ref[...] = zkv[:, 0:2 * D_KV]
    ks = _halfnorm(zkv[:, 2 * D_KV:3 * D_KV], gks_ref[...])
    vs = zkv[:, 3 * D_KV:4 * D_KV]
    kw = _halfnorm(zkv[:, 4 * D_KV:5 * D_KV], gkw_ref[...])
    vw = zkv[:, 5 * D_KV:6 * D_KV]
    kvs_ref[:, 0:D_KV] = ks
    kvs_ref[:, D_KV:] = vs
    kvw_ref[:, 0:D_KV] = kw
    kvw_ref[:, D_KV:] = vw
    kvsb_ref[:, 0:D_KV] = ks.astype(BF16)
    kvsb_ref[:, D_KV:2 * D_KV] = vs.astype(BF16)
    kvsb_ref[:, 2 * D_KV:] = jnp.ones((ks.shape[0], D_KV), BF16)
    kvwb_ref[:, 0:D_KV] = kw.astype(BF16)
    kvwb_ref[:, D_KV:] = vw.astype(BF16)
    gn_ref[...] = jax.nn.sigmoid(_dot(xn, w_ref[:, _C_GN:_C_GA]))
    ga_ref[...] = jax.nn.sigmoid(_dot(xn, w_ref[:, _C_GA:_C_GB]))
    gb_ref[...] = jax.nn.sigmoid(_dot(xn, w_ref[:, _C_GB:_C_END]))


def _mix_in(h, g, w, gq, gks, gkw):
    n = h.shape[0]
    tm = _pick_tile(n, 256)
    row = lambda width: pl.BlockSpec((tm, width), lambda i: (i, 0))
    widths = [(D_SSM, F32), (Q_PAD, F32), (2 * D_KV, F32), (2 * D_KV, F32), (2 * D_KV, F32),
              (3 * D_KV, BF16), (2 * D_KV, BF16), (LANE, F32), (D_MODEL, F32), (D_MODEL, F32)]
    return pl.pallas_call(
        _mix_in_kernel,
        out_shape=[jax.ShapeDtypeStruct((n, wd), dt) for wd, dt in widths],
        grid=(n // tm,),
        in_specs=[row(D_MODEL), _const_spec((1, D_MODEL)), _const_spec(w.shape),
                  _const_spec((1, Q_PAD)), _const_spec((1, LANE)), _const_spec((1, LANE))],
        out_specs=[row(wd) for wd, _ in widths],
        compiler_params=_cparams("parallel"),
    )(h, g, w, gq, gks, gkw)


def _gelu_tanh(x):
    return 0.5 * x * (1.0 + jnp.tanh(0.7978845608028654 * (x + 0.044715 * (x * x * x))))


def _s5_kernel(*refs, seg, carry):
    if carry:
        (u_ref, bm_ref, cr_ref, ci_ref, d_ref, lam_ref, pr_ref, pi_ref,
         y_ref, hr_out, hi_out, car_ref, cai_ref) = refs
    else:
        (u_ref, bm_ref, cr_ref, ci_ref, d_ref, lam_ref, pr_ref, pi_ref, h0r_ref, h0i_ref,
         y_ref, hr_out, hi_out) = refs
    u = u_ref[...]
    rows = u.shape[0]
    x = _dot(u.astype(BF16), bm_ref[...])
    xr = x[:, :N_STATE]
    xi = x[:, N_STATE:]
    d, k = 1, 0
    while d < seg:
        lr = jnp.tile(lam_ref[2 * k * seg:(2 * k + 1) * seg, :], (rows // seg, 1))
        li = jnp.tile(lam_ref[(2 * k + 1) * seg:(2 * k + 2) * seg, :], (rows // seg, 1))
        sr = pltpu.roll(xr, d, 0)
        si = pltpu.roll(xi, d, 0)
        xr, xi = xr + lr * sr - li * si, xi + lr * si + li * sr
        d *= 2
        k += 1
    p_r = pr_ref[...]
    p_i = pi_ref[...]
    if carry:
        @pl.when(pl.program_id(0) == 0)
        def _():
            car_ref[...] = jnp.zeros_like(car_ref)
            cai_ref[...] = jnp.zeros_like(cai_ref)
        c_r = car_ref[...]
        c_i = cai_ref[...]
        hr_parts, hi_parts = [], []
        for j in range(rows // seg):
            tr = xr[j * seg:(j + 1) * seg]
            ti = xi[j * seg:(j + 1) * seg]
            hr_j = tr + p_r * c_r - p_i * c_i
            hi_j = ti + p_r * c_i + p_i * c_r
            c_r = hr_j[seg - 1:seg, :]
            c_i = hi_j[seg - 1:seg, :]
            hr_parts.append(hr_j)
            hi_parts.append(hi_j)
        hr = jnp.concatenate(hr_parts, axis=0)
        hi = jnp.concatenate(hi_parts, axis=0)
    else:
        c_r = h0r_ref[...]
        c_i = h0i_ref[...]
        hr = xr + p_r * c_r - p_i * c_i
        hi = xi + p_r * c_i + p_i * c_r
    y = _dot(hr.astype(BF16), cr_ref[...]) + _dot(hi.astype(BF16), ci_ref[...]) + d_ref[...] * u
    y_ref[...] = _gelu_tanh(y).astype(BF16)
    if carry:
        car_ref[...] = hr[rows - 1:rows, :]
        cai_ref[...] = hi[rows - 1:rows, :]
        hr_out[...] = hr[rows - 1:rows, :]
        hi_out[...] = hi[rows - 1:rows, :]
    else:
        hr_out[...] = hr
        hi_out[...] = hi


def _s5_prompt(u, t, sp):
    rows = Q_TILE
    consts = [sp["bmat"], sp["cr"], sp["ci"], sp["d"], sp["lam_p"], sp["pr_p"], sp["pi_p"]]
    return pl.pallas_call(
        functools.partial(_s5_kernel, seg=SCAN_SEG, carry=True),
        out_shape=[jax.ShapeDtypeStruct((t, D_SSM), BF16),
                   jax.ShapeDtypeStruct((1, N_STATE), F32), jax.ShapeDtypeStruct((1, N_STATE), F32)],
        grid=(t // rows,),
        in_specs=[pl.BlockSpec((rows, D_SSM), lambda i: (i, 0))] + [_const_spec(c.shape) for c in consts],
        out_specs=[pl.BlockSpec((rows, D_SSM), lambda i: (i, 0)),
                   _const_spec((1, N_STATE)), _const_spec((1, N_STATE))],
        scratch_shapes=[pltpu.VMEM((1, N_STATE), F32), pltpu.VMEM((1, N_STATE), F32)],
        compiler_params=_cparams("arbitrary"),
    )(u, *consts)


def _s5_sample(u, t, ns_rows, nq, h0r, h0i, sp):
    consts = [sp["bmat"], sp["cr"], sp["ci"], sp["d"], sp["lam_s"], sp["pr_s"], sp["pi_s"], h0r, h0i]
    blk = t // ns_rows
    return pl.pallas_call(
        functools.partial(_s5_kernel, seg=nq, carry=False),
        out_shape=[jax.ShapeDtypeStruct((ns_rows, D_SSM), BF16),
                   jax.ShapeDtypeStruct((ns_rows, N_STATE), F32), jax.ShapeDtypeStruct((ns_rows, N_STATE), F32)],
        grid=(1,),
        in_specs=[pl.BlockSpec((ns_rows, D_SSM), lambda i: (blk, 0))] + [_const_spec(c.shape) for c in consts],
        out_specs=[_const_spec((ns_rows, D_SSM)), _const_spec((ns_rows, N_STATE)), _const_spec((ns_rows, N_STATE))],
        compiler_params=_cparams("arbitrary"),
    )(u, *consts)


def _compress_part(get_xs, w_ref, a0, a1, part, rows):
    acc0 = acc1 = None
    for s in range(CMP_STRIDE):
        xs = get_xs(s)
        t0 = _dot(xs, w_ref[s])
        t1 = _dot(xs, w_ref[CMP_STRIDE + s])
        acc0 = t0 if acc0 is None else acc0 + t0
        acc1 = t1 if acc1 is None else acc1 + t1
    base = pl.multiple_of(part * rows, rows)
    a0[pl.ds(base, rows), :] = acc0
    a1[pl.ds(base, rows), :] = acc1


def _compress_finish(a0, a1, w_ref, pe_ref, gk_ref, kcc_ref, vcc_ref, nb):
    bias = []
    for j in range(CMP_BLOCK // CMP_STRIDE):
        b = jnp.zeros((8, 2 * D_KV), F32)
        for s in range(CMP_STRIDE):
            r = j * CMP_STRIDE + s
            b = b + _dot(jnp.broadcast_to(pe_ref[r:r + 1, :], (8, 2 * D_KV)).astype(BF16), w_ref[r])
        bias.append(b[0:1, :])
    out = (a0[...] + bias[0]) + pltpu.roll(a1[...] + bias[1], nb - 1, 0)
    valid = lax.broadcasted_iota(jnp.int32, (nb, 1), 0) < nb - 1
    out = jnp.where(valid, out, 0.0)
    kcc_ref[0] = _halfnorm(out[:, 0:D_KV], gk_ref[...]).astype(BF16)
    vcc_ref[0] = out[:, D_KV:].astype(BF16)


def _compress_rows_kernel(xk_ref, xv_ref, w_ref, pe_ref, gk_ref, kcc_ref, vcc_ref, a0, a1, *, nsplit, nb):
    part = pl.program_id(0)
    rows = nb // nsplit

    def get_xs(s):
        tok = pl.ds(s, rows, stride=CMP_STRIDE)
        return jnp.concatenate([xk_ref[tok, :], xv_ref[tok, :]], axis=1).astype(BF16)
    _compress_part(get_xs, w_ref, a0, a1, part, rows)

    @pl.when(part == nsplit - 1)
    def _():
        _compress_finish(a0, a1, w_ref, pe_ref, gk_ref, kcc_ref, vcc_ref, nb)


def _compress_rows(kv_cmp, t, cw):
    nb = t // CMP_STRIDE
    nsplit = 2
    return pl.pallas_call(
        functools.partial(_compress_rows_kernel, nsplit=nsplit, nb=nb),
        out_shape=[jax.ShapeDtypeStruct((1, nb, D_KV), BF16), jax.ShapeDtypeStruct((1, nb, D_KV), BF16)],
        grid=(nsplit,),
        in_specs=[pl.BlockSpec((t // nsplit, D_KV), lambda i: (i, 0)),
                  pl.BlockSpec((t // nsplit, D_KV), lambda i: (i, 1)),
                  _const_spec(cw["w"].shape), _const_spec(cw["pe"].shape), _const_spec((1, LANE))],
        out_specs=[_const_spec((1, nb, D_KV)), _const_spec((1, nb, D_KV))],
        scratch_shapes=[pltpu.VMEM((nb, 2 * D_KV), F32), pltpu.VMEM((nb, 2 * D_KV), F32)],
        compiler_params=_cparams("arbitrary"),
    )(kv_cmp, kv_cmp, cw["w"], cw["pe"], cw["gk"])


def _compress_kernel(pt_ref, pages_ref, w_ref, pe_ref, gk_ref, perm_ref,
                     kcc_ref, vcc_ref, buf, sem, x_scr, a0, a1, *, pps, nsplit, nb):
    s = pl.program_id(0)
    nsteps = pl.num_programs(0)

    def page_copy(step, p, slot):
        b = step // nsplit
        part = step % nsplit
        pg = pt_ref[b, part * pps + p]
        col = pl.multiple_of((p % 2) * PAGE_SIZE, PAGE_SIZE)
        return pltpu.make_async_copy(pages_ref.at[pg], buf.at[slot, p // 2, :, pl.ds(col, PAGE_SIZE)],
                                     sem.at[slot])

    def start(step, slot):
        def body(p, c):
            page_copy(step, p, slot).start()
            return c
        lax.fori_loop(0, pps, body, 0)

    def wait(step, slot):
        def body(p, c):
            page_copy(step, p, slot).wait()
            return c
        lax.fori_loop(0, pps, body, 0)

    @pl.when(s == 0)
    def _():
        start(s, 0)

    @pl.when(s + 1 < nsteps)
    def _():
        start(s + 1, (s + 1) % 2)

    slot = s % 2
    wait(s, slot)
    part = s % nsplit

    def to_rows(q, c):
        xt = buf[slot, q].astype(BF16)
        xp = _dot_nt(perm_ref[...], xt).astype(BF16)
        base = pl.multiple_of(q * CMP_STRIDE, CMP_STRIDE)
        for s2 in range(CMP_STRIDE):
            x_scr[s2, pl.ds(base, CMP_STRIDE), :] = xp[s2 * CMP_STRIDE:(s2 + 1) * CMP_STRIDE, :]
        return c
    lax.fori_loop(0, pps // 2, to_rows, 0, unroll=4)
    _compress_part(lambda s2: x_scr[s2], w_ref, a0, a1, part, nb // nsplit)

    @pl.when(part == nsplit - 1)
    def _():
        _compress_finish(a0, a1, w_ref, pe_ref, gk_ref, kcc_ref, vcc_ref, nb)


def _compress_paged(page_table, pages, cw):
    bsz, n_pages = page_table.shape
    nb = n_pages * (PAGE_SIZE // CMP_STRIDE)
    nsplit = 2
    pps = n_pages // nsplit
    const = lambda shape: pl.BlockSpec(shape, lambda s, pt: (0,) * len(shape), pipeline_mode=pl.Buffered(1))
    grid_spec = pltpu.PrefetchScalarGridSpec(
        num_scalar_prefetch=1,
        grid=(bsz * nsplit,),
        in_specs=[pl.BlockSpec(memory_space=pl.ANY), const(cw["w"].shape), const(cw["pe"].shape),
                  const((1, LANE)), const(cw["perm"].shape)],
        out_specs=[pl.BlockSpec((1, nb, D_KV), lambda s, pt: (s // nsplit, 0, 0)),
                   pl.BlockSpec((1, nb, D_KV), lambda s, pt: (s // nsplit, 0, 0))],
        scratch_shapes=[pltpu.VMEM((2, pps // 2, 2 * D_KV, 2 * PAGE_SIZE), F32), pltpu.SemaphoreType.DMA((2,)),
                        pltpu.VMEM((CMP_STRIDE, nb // nsplit, 2 * D_KV), BF16),
                        pltpu.VMEM((nb, 2 * D_KV), F32), pltpu.VMEM((nb, 2 * D_KV), F32)],
    )
    return pl.pallas_call(
        functools.partial(_compress_kernel, pps=pps, nsplit=nsplit, nb=nb),
        out_shape=[jax.ShapeDtypeStruct((bsz, nb, D_KV), BF16), jax.ShapeDtypeStruct((bsz, nb, D_KV), BF16)],
        grid_spec=grid_spec,
        compiler_params=_cparams("arbitrary"),
    )(page_table, pages, cw["w"], cw["pe"], cw["gk"], cw["perm"])


def _heads_rows(qp):
    return jnp.concatenate([qp[:, h * LANE:(h + 1) * LANE] for h in range(N_HEADS)], axis=0).astype(BF16)


def _slope(h):
    return 2.0 ** (-8.0 * (h + 1) / N_HEADS)


def _branch(q_all, tq, pieces):
    s_all = [_dot(q_all, k) if fm else _dot_nt(q_all, k) for k, _, _, _, fm in pieces]
    es, invs = [], []
    for h in range(N_HEADS):
        rows = slice(h * tq, (h + 1) * tq)
        sm = [jnp.where(pc[3], s[rows] - _slope(h) * pc[2], NEG) for s, pc in zip(s_all, pieces)]
        m = functools.reduce(jnp.maximum, [jnp.max(x, axis=-1, keepdims=True) for x in sm])
        m = jnp.maximum(m, HALF_NEG)
        e = [jnp.exp(x - m) for x in sm]
        l = functools.reduce(lambda a, b: a + b, [jnp.sum(x, axis=-1, keepdims=True) for x in e])
        es.append(e)
        invs.append(1.0 / jnp.maximum(l, 1e-30))
    o = None
    for i, (_, v, _, _, fm) in enumerate(pieces):
        p = jnp.concatenate([es[h][i] for h in range(N_HEADS)], axis=0).astype(BF16)
        t = _dot_nt(p, v) if fm else _dot(p, v)
        o = t if o is None else o + t
    return o * jnp.concatenate(invs, axis=0), es, invs


def _cmp_win_topk(q_all, tq, qpos, gn, kcc, vcc, mband, win_pieces, nb, ns, topn):
    n_io = lax.broadcasted_iota(jnp.int32, (1, nb), 1)
    kc_end = n_io * CMP_STRIDE + (CMP_BLOCK - 1)
    dist_c = (qpos - kc_end).astype(F32)
    mask_c = (dist_c >= 0) & (n_io < nb - 1)
    o_c, es, invs = _branch(q_all, tq, [(kcc, vcc, dist_c, mask_c, False)])
    o_w, _, _ = _branch(q_all, tq, win_pieces)
    imps = []
    for g in range(N_KV):
        psum = es[HPG * g][0] * invs[HPG * g]
        for hh in range(1, HPG):
            psum = psum + es[HPG * g + hh][0] * invs[HPG * g + hh]
        hi = psum.astype(BF16)
        r = psum - hi.astype(F32)
        mid = r.astype(BF16)
        lo = (r - mid.astype(F32)).astype(BF16)
        imps.append(_dot(hi, mband) + _dot(mid, mband) + _dot(lo, mband))
    imp = jnp.concatenate(imps, axis=0)
    blk = lax.broadcasted_iota(jnp.int32, (1, ns), 1)
    cur = jnp.concatenate([jnp.right_shift(qpos, 6)] * N_KV, axis=0)
    forced = (blk == 0) | (blk == cur) | (blk == cur - 1)
    score = jnp.where(forced, imp + SEL_BIG, jnp.where(blk <= cur, imp, -SEL_BIG))
    blkf = blk.astype(F32)
    sel = jnp.zeros_like(score)
    for _ in range(topn):
        m = jnp.max(score, axis=-1, keepdims=True)
        first = jnp.min(jnp.where(score == m, blkf, float(ns)), axis=-1, keepdims=True)
        hit = blkf == first
        sel = jnp.where(hit, 1.0, sel)
        score = jnp.where(hit, -jnp.inf, score)
    sel = jnp.where(blk <= cur, sel, 0.0)
    outs = []
    for h in range(N_HEADS):
        rows = slice(h * tq, (h + 1) * tq)
        outs.append(gn[:, h:h + 1] * o_c[rows] + gn[:, 2 * N_HEADS + h:2 * N_HEADS + h + 1] * o_w[rows])
    return jnp.concatenate(outs, axis=1), sel


def _e1_prompt_kernel(qp_ref, gn_ref, kcc_ref, vcc_ref, mband_ref, kvw_ref, ocw_ref, sel_ref, flag_ref,
                      *, nb, ns):
    tq = Q_TILE
    s0 = pl.program_id(0) * tq
    qpos = s0 + lax.broadcasted_iota(jnp.int32, (tq, 1), 0)
    q_all = _heads_rows(qp_ref[...])
    start = pl.multiple_of(jnp.maximum(s0 - WINDOW, 0), Q_TILE)
    kw = kvw_ref[pl.ds(start, WIN_KEYS), 0:D_KV]
    vw = kvw_ref[pl.ds(start, WIN_KEYS), D_KV:]
    kpos = start + lax.broadcasted_iota(jnp.int32, (1, WIN_KEYS), 1)
    dist_w = (qpos - kpos).astype(F32)
    mask_w = (dist_w >= 0) & (dist_w < WINDOW)
    ocw, sel = _cmp_win_topk(q_all, tq, qpos, gn_ref[...], kcc_ref[0], vcc_ref[0], mband_ref[...],
                             [(kw, vw, dist_w, mask_w, False)], nb, ns, TOP_N)
    ocw_ref[...] = ocw
    sel_ref[0] = sel[:tq]
    sel_ref[1] = sel[tq:]
    colany = jnp.max(sel, axis=0, keepdims=True)
    j_io = lax.broadcasted_iota(jnp.int32, (ns, LANE), 0)
    c_io = lax.broadcasted_iota(jnp.int32, (ns, LANE), 1)
    grp = jnp.where(jnp.right_shift(j_io, 2) == c_io, 1.0, 0.0).astype(BF16)
    cnt = _dot(jnp.broadcast_to(colany, (8, ns)).astype(BF16), grp)
    flag_ref[0] = (cnt > 0.5).astype(jnp.int32)


def _e1_prompt(qp, gn, kcc, vcc, mband, kvw_bf, t):
    nb, ns = mband.shape
    nt = t // Q_TILE
    row = lambda w: pl.BlockSpec((Q_TILE, w), lambda i: (i, 0))
    return pl.pallas_call(
        functools.partial(_e1_prompt_kernel, nb=nb, ns=ns),
        out_shape=[jax.ShapeDtypeStruct((t, Q_PAD), F32), jax.ShapeDtypeStruct((N_KV, t, ns), F32),
                   jax.ShapeDtypeStruct((nt, 8, LANE), jnp.int32)],
        grid=(nt,),
        in_specs=[row(Q_PAD), row(LANE), _const_spec((1, nb, D_KV)), _const_spec((1, nb, D_KV)),
                  _const_spec(mband.shape), _const_spec(kvw_bf.shape)],
        out_specs=[row(Q_PAD), pl.BlockSpec((N_KV, Q_TILE, ns), lambda i: (0, i, 0)),
                   pl.BlockSpec((1, 8, LANE), lambda i: (i, 0, 0))],
        compiler_params=_cparams("parallel"),
    )(qp, gn, kcc, vcc, mband, kvw_bf)


def _pad_rows(x, rows):
    return jnp.concatenate([x, jnp.zeros((rows - x.shape[0], x.shape[1]), x.dtype)], axis=0)


def _e1_sample_kernel(qp_ref, gn_ref, kcc_ref, vcc_ref, mband_ref, cwin_ref, nwin_ref, ocw_ref, sel_ref,
                      *, nb, ns, nq, past):
    qi = lax.broadcasted_iota(jnp.int32, (nq, 1), 0)
    qpos = past + qi
    q_all = _heads_rows(qp_ref[...])
    cw = cwin_ref[0]
    w_buf = cw.shape[1]
    j_c = lax.broadcasted_iota(jnp.int32, (1, w_buf), 1)
    dist_cw = (qi + (w_buf - j_c)).astype(F32)
    mask_cw = (dist_cw >= 0) & (dist_cw < WINDOW)
    nw = _pad_rows(nwin_ref[...], LANE)
    j_n = lax.broadcasted_iota(jnp.int32, (1, LANE), 1)
    dist_nw = (qi - j_n).astype(F32)
    mask_nw = (dist_nw >= 0) & (j_n < nq)
    pieces = [(cw[0:D_KV, :].astype(BF16), cw[D_KV:, :].astype(BF16), dist_cw, mask_cw, True),
              (nw[:, 0:D_KV].astype(BF16), nw[:, D_KV:].astype(BF16), dist_nw, mask_nw, False)]
    ocw, sel = _cmp_win_topk(q_all, nq, qpos, gn_ref[...], kcc_ref[0], vcc_ref[0], mband_ref[...],
                             pieces, nb, ns, TOP_N - 1)
    ocw_ref[...] = ocw
    sel_ref[0, 0] = sel[:nq]
    sel_ref[0, 1] = sel[nq:]


def _e1_sample(qp, gn, kcc, vcc, mband, cache_win, win_off, kv_win, t, bsz, nq, past):
    nb, ns = mband.shape
    w_buf = cache_win.shape[2]
    off = t // nq
    row = lambda w: pl.BlockSpec((nq, w), lambda b: (off + b, 0))
    per_b = lambda shape: pl.BlockSpec((1,) + shape, lambda b: (b,) + (0,) * len(shape))
    return pl.pallas_call(
        functools.partial(_e1_sample_kernel, nb=nb, ns=ns, nq=nq, past=past),
        out_shape=[jax.ShapeDtypeStruct((bsz * nq, Q_PAD), F32), jax.ShapeDtypeStruct((bsz, N_KV, nq, ns), F32)],
        grid=(bsz,),
        in_specs=[row(Q_PAD), row(LANE), per_b((nb, D_KV)), per_b((nb, D_KV)), _const_spec(mband.shape),
                  pl.BlockSpec((1, 2 * D_KV, w_buf), lambda b: (win_off + b, 0, 0)), row(2 * D_KV)],
        out_specs=[pl.BlockSpec((nq, Q_PAD), lambda b: (b, 0)), per_b((N_KV, nq, ns))],
        compiler_params=_cparams("parallel"),
    )(qp, gn, kcc, vcc, mband, cache_win, kv_win)


def _online_step(s, dist, masks, v_ones, m_ref, acc_ref, tq, feature_major=False):
    reps = s.shape[1] // LANE
    ps, alphas = [], []
    for h in range(N_HEADS):
        rows = slice(h * tq, (h + 1) * tq)
        sm = jnp.where(masks[h // HPG], s[rows] - _slope(h) * dist, NEG)
        m_old = m_ref[rows, :]
        m_new = jnp.maximum(m_old, jnp.max(sm, axis=-1, keepdims=True))
        alphas.append(jnp.exp(m_old - m_new))
        ps.append(jnp.exp(sm - jnp.tile(m_new, (1, reps))))
        m_ref[rows, :] = m_new
    p_all = jnp.concatenate(ps, axis=0).astype(BF16)
    a_all = jnp.concatenate(alphas, axis=0)
    pv = _dot_nt(p_all, v_ones) if feature_major else _dot(p_all, v_ones)
    acc_ref[...] = jnp.tile(a_all, (1, 2)) * acc_ref[...] + pv


def _expand_sel(sel, first_blk, n_keys):
    ns = sel.shape[1]
    j_io = lax.broadcasted_iota(jnp.int32, (ns, n_keys), 0)
    kb = first_blk + jnp.right_shift(lax.broadcasted_iota(jnp.int32, (ns, n_keys), 1), 6)
    e = jnp.where(j_io == kb, 1.0, 0.0).astype(BF16)
    return _dot(sel.astype(BF16), e) > 0.5


def _init_online(m_ref, acc_ref):
    m_ref[...] = jnp.full(m_ref.shape, HALF_NEG, F32)
    acc_ref[...] = jnp.zeros(acc_ref.shape, F32)


def _finish_online(ocw, gn, acc_ref, tq):
    o_s = acc_ref[:, 0:LANE] * (1.0 / jnp.maximum(acc_ref[:, LANE:], 1e-30))
    outs = []
    for h in range(N_HEADS):
        g1 = gn[:, N_HEADS + h:N_HEADS + h + 1]
        outs.append(ocw[:, h * LANE:(h + 1) * LANE] + g1 * o_s[h * tq:(h + 1) * tq])
    return jnp.concatenate(outs, axis=1)


def _e2_prompt_kernel(order_ref, cnt_ref, qp_ref, sel_ref, gn_ref, ocw_ref, kvs_ref, out_ref, m_ref, acc_ref,
                      *, nchunks, ns):
    tq = Q_TILE
    i = pl.program_id(0)
    s0 = i * tq
    qpos = s0 + lax.broadcasted_iota(jnp.int32, (tq, 1), 0)
    q_all = _heads_rows(qp_ref[...])
    _init_online(m_ref, acc_ref)
    n_act = cnt_ref[i]
    blocks_per_chunk = SEL_CHUNK // SEL_BLOCK

    def body(j, carry):
        c1 = order_ref[i * nchunks + 2 * j]
        paired = 2 * j + 1 < n_act
        c2 = jnp.where(paired, order_ref[i * nchunks + jnp.minimum(2 * j + 1, nchunks - 1)], c1)
        ks, vs, dists, masks = [], [], [], [[] for _ in range(N_KV)]
        for c, first_blk in ((c1, c1 * blocks_per_chunk), (c2, jnp.where(paired, c2 * blocks_per_chunk, ns))):
            base = pl.multiple_of(c * SEL_CHUNK, SEL_CHUNK)
            ks.append(kvs_ref[pl.ds(base, SEL_CHUNK), 0:D_KV])
            vs.append(kvs_ref[pl.ds(base, SEL_CHUNK), D_KV:])
            kpos = base + lax.broadcasted_iota(jnp.int32, (1, SEL_CHUNK), 1)
            dist = (qpos - kpos).astype(F32)
            dists.append(dist)
            for g in range(N_KV):
                masks[g].append(_expand_sel(sel_ref[g], first_blk, SEL_CHUNK) & (dist >= 0))
        s = _dot_nt(q_all, jnp.concatenate(ks, axis=0))
        _online_step(s, jnp.concatenate(dists, axis=1), [jnp.concatenate(mg, axis=1) for mg in masks],
                     jnp.concatenate(vs, axis=0), m_ref, acc_ref, tq)
        return carry

    lax.fori_loop(0, (n_act + 1) // 2, body, 0)
    out_ref[...] = _finish_online(ocw_ref[...], gn_ref[...], acc_ref, tq)


def _e2_prompt(order, cnt, qp, sel, gn, ocw, kvs_bf, t):
    ns = sel.shape[2]
    nt = t // Q_TILE
    nchunks = order.shape[0] // nt
    row = lambda w: pl.BlockSpec((Q_TILE, w), lambda i, o, c: (i, 0))
    grid_spec = pltpu.PrefetchScalarGridSpec(
        num_scalar_prefetch=2,
        grid=(nt,),
        in_specs=[row(Q_PAD), pl.BlockSpec((N_KV, Q_TILE, ns), lambda i, o, c: (0, i, 0)), row(LANE), row(Q_PAD),
                  pl.BlockSpec(kvs_bf.shape, lambda i, o, c: (0, 0))],
        out_specs=row(Q_PAD),
        scratch_shapes=[pltpu.VMEM((N_HEADS * Q_TILE, LANE), F32), pltpu.VMEM((N_HEADS * Q_TILE, 2 * LANE), F32)],
    )
    return pl.pallas_call(
        functools.partial(_e2_prompt_kernel, nchunks=nchunks, ns=ns),
        out_shape=jax.ShapeDtypeStruct((t, Q_PAD), F32),
        grid_spec=grid_spec,
        compiler_params=_cparams("arbitrary"),
    )(order, cnt, qp, sel, gn, ocw, kvs_bf)


def _e2_sample_kernel(pt_ref, pages_ref, qp_ref, sel_ref, gn_ref, ocw_ref, nsel_ref, expand_ref, out_ref,
                      buf, sem, m_ref, acc_ref, *, pps, nsplit, nq):
    s = pl.program_id(0)
    nsteps = pl.num_programs(0)
    n_keys = pps * PAGE_SIZE

    def page_copy(step, p, slot):
        b = step // nsplit
        part = step % nsplit
        pg = pt_ref[b, part * pps + p]
        col = pl.multiple_of(p * PAGE_SIZE, PAGE_SIZE)
        return pltpu.make_async_copy(pages_ref.at[pg], buf.at[slot, :, pl.ds(col, PAGE_SIZE)], sem.at[slot])

    def start(step, slot):
        def body(p, c):
            page_copy(step, p, slot).start()
            return c
        lax.fori_loop(0, pps, body, 0)

    def wait(step, slot):
        def body(p, c):
            page_copy(step, p, slot).wait()
            return c
        lax.fori_loop(0, pps, body, 0)

    @pl.when(s == 0)
    def _():
        start(s, 0)

    @pl.when(s + 1 < nsteps)
    def _():
        start(s + 1, (s + 1) % 2)

    slot = s % 2
    part = s % nsplit
    qi = lax.broadcasted_iota(jnp.int32, (nq, 1), 0)
    q_all = _heads_rows(qp_ref[...])

    @pl.when(part == 0)
    def _():
        _init_online(m_ref, acc_ref)

    wait(s, slot)
    k = buf[slot, 0:D_KV, :].astype(BF16)
    v_ones = jnp.concatenate([buf[slot, D_KV:, :].astype(BF16), jnp.ones((D_KV, n_keys), BF16)], axis=0)
    sc = _dot(q_all, k)
    back = (nsplit - part) * n_keys - lax.broadcasted_iota(jnp.int32, (1, n_keys), 1)
    dist = (qi + back).astype(F32)
    masks = [_dot(sel_ref[0, g, 0].astype(BF16), expand_ref[...]) > 0.5 for g in range(N_KV)]
    _online_step(sc, dist, masks, v_ones, m_ref, acc_ref, nq, feature_major=True)

    @pl.when(part == nsplit - 1)
    def _():
        nw = _pad_rows(nsel_ref[...], LANE)
        j_n = lax.broadcasted_iota(jnp.int32, (1, LANE), 1)
        dist_n = (qi - j_n).astype(F32)
        mask_n = (dist_n >= 0) & (j_n < nq)
        s_n = _dot_nt(q_all, nw[:, 0:D_KV].astype(BF16))
        vn_ones = jnp.concatenate([nw[:, D_KV:].astype(BF16), jnp.ones((LANE, D_KV), BF16)], axis=1)
        _online_step(s_n, dist_n, [mask_n, mask_n], vn_ones, m_ref, acc_ref, nq)
        out_ref[...] = _finish_online(ocw_ref[...], gn_ref[...], acc_ref, nq)


def _e2_sample(page_table, pages, qp, sel, gn, ocw, kv_sel, t, nq):
    bsz, n_pages = page_table.shape
    ns = sel.shape[3]
    nsplit = 4
    pps = n_pages // nsplit
    n_keys = pps * PAGE_SIZE
    nblk = ns // nsplit
    off = t // nq
    sel_parts = sel.reshape(bsz, N_KV, nq, nsplit, nblk).transpose(0, 1, 3, 2, 4)
    expand = (jnp.arange(nblk)[:, None] == jnp.arange(n_keys)[None, :] // SEL_BLOCK).astype(BF16)
    row = lambda w: pl.BlockSpec((nq, w), lambda s, pt: (off + s // nsplit, 0))
    grid_spec = pltpu.PrefetchScalarGridSpec(
        num_scalar_prefetch=1,
        grid=(bsz * nsplit,),
        in_specs=[pl.BlockSpec(memory_space=pl.ANY), row(Q_PAD),
                  pl.BlockSpec((1, N_KV, 1, nq, nblk), lambda s, pt: (s // nsplit, 0, s % nsplit, 0, 0)), row(LANE),
                  pl.BlockSpec((nq, Q_PAD), lambda s, pt: (s // nsplit, 0)), row(2 * D_KV),
                  pl.BlockSpec((nblk, n_keys), lambda s, pt: (0, 0), pipeline_mode=pl.Buffered(1))],
        out_specs=pl.BlockSpec((nq, Q_PAD), lambda s, pt: (s // nsplit, 0)),
        scratch_shapes=[pltpu.VMEM((2, 2 * D_KV, n_keys), F32), pltpu.SemaphoreType.DMA((2,)),
                        pltpu.VMEM((N_HEADS * nq, LANE), F32), pltpu.VMEM((N_HEADS * nq, 2 * LANE), F32)],
    )
    return pl.pallas_call(
        functools.partial(_e2_sample_kernel, pps=pps, nsplit=nsplit, nq=nq),
        out_shape=jax.ShapeDtypeStruct((bsz * nq, Q_PAD), F32),
        grid_spec=grid_spec,
        compiler_params=_cparams("arbitrary"),
    )(page_table, pages, qp, sel_parts, gn, ocw, kv_sel, expand)


def _post_kernel(h_ref, yg_ref, oatt_ref, ga_ref, gb_ref, wglu_ref, watt_ref, wout_ref, o_ref):
    gl = _dot(yg_ref[...], wglu_ref[...])
    br_a = gl[:, :D_MODEL] * jax.nn.sigmoid(gl[:, D_MODEL:])
    br_b = _dot(oatt_ref[...].astype(BF16), watt_ref[...])
    merged = (ga_ref[...] * br_a + gb_ref[...] * br_b).astype(BF16)
    o_ref[...] = h_ref[...] + _dot(merged, wout_ref[...])


def _post(h, yg, oatt, ga, gb, wglu, watt, wout):
    n = h.shape[0]
    tm = _pick_tile(n, 256)
    row = lambda w: pl.BlockSpec((tm, w), lambda i: (i, 0))
    return pl.pallas_call(
        _post_kernel,
        out_shape=jax.ShapeDtypeStruct((n, D_MODEL), F32),
        grid=(n // tm,),
        in_specs=[row(D_MODEL), row(D_SSM), row(Q_PAD), row(D_MODEL), row(D_MODEL),
                  _const_spec(wglu.shape), _const_spec(watt.shape), _const_spec(wout.shape)],
        out_specs=row(D_MODEL),
        compiler_params=_cparams("parallel"),
    )(h, yg, oatt, ga, gb, wglu, watt, wout)


def _head_pad_index():
    h = jnp.arange(N_HEADS)[:, None]
    d = jnp.arange(HEAD_DIM)[None, :]
    return (LANE * h + HEAD_DIM * (h // HPG) + d).reshape(-1)


def _prep_mix_weights(w_in, qk_norm):
    idx = _head_pad_index()
    wq = jnp.zeros((D_MODEL, Q_PAD), F32).at[:, idx].set(w_in[:, D_SSM:D_SSM + N_HEADS * HEAD_DIM])
    c0 = D_SSM + N_HEADS * HEAD_DIM
    c1 = c0 + 6 * D_KV
    c2 = c1 + 3 * N_HEADS
    wgn = jnp.zeros((D_MODEL, LANE), F32).at[:, :3 * N_HEADS].set(w_in[:, c1:c2])
    w = jnp.concatenate([w_in[:, :D_SSM], wq, w_in[:, c0:c1], wgn, w_in[:, c2:]], axis=1).astype(BF16)
    gq = jnp.zeros((Q_PAD,), F32).at[idx].set(jnp.tile(qk_norm[0] * (HEAD_DIM ** -0.5), N_HEADS))[None]
    gks = jnp.tile(qk_norm[2], N_KV)[None]
    gkw = jnp.tile(qk_norm[3], N_KV)[None]
    return w, gq, gks, gkw


def _prep_att_out(w_att_out):
    return jnp.zeros((Q_PAD, D_MODEL), F32).at[_head_pad_index()].set(w_att_out).astype(BF16)


def _prep_compress(pe_k, pe_v, w_k, w_v, gain_k):
    blocks = jnp.stack([w_k, w_k, w_v, w_v], axis=1)
    w = jnp.einsum("scde,ch->scdhe", blocks, jnp.eye(2 * N_KV, dtype=F32))
    pe = jnp.concatenate([pe_k, pe_k, pe_v, pe_v], axis=1)
    r = jnp.arange(2 * PAGE_SIZE)
    perm = (r[None, :] == (CMP_STRIDE * (r % CMP_STRIDE) + r // CMP_STRIDE)[:, None]).astype(BF16)
    return {"w": w.reshape(CMP_BLOCK, 2 * D_KV, 2 * D_KV).astype(BF16), "pe": pe,
            "gk": jnp.tile(gain_k, N_KV)[None], "perm": perm}


def _prep_s5(a_re, a_im, log_dt, b_re, b_im, c_re, c_im, d, nq):
    dt = jnp.exp(log_dt)[:, None]
    mag = jnp.exp(a_re * dt)
    lr = mag * jnp.cos(a_im * dt)
    li = mag * jnp.sin(a_im * dt)
    den = a_re * a_re + a_im * a_im
    fr = ((lr - 1.0) * a_re + li * a_im) / den
    fi = (li * a_re - (lr - 1.0) * a_im) / den
    bbr = fr[..., None] * b_re - fi[..., None] * b_im
    bbi = fr[..., None] * b_im + fi[..., None] * b_re
    eye = jnp.eye(N_SSM_GROUPS, dtype=F32)
    blk_b = lambda m: jnp.einsum("gpc,gh->gchp", m, eye).reshape(D_SSM, N_STATE)
    blk_c = lambda m: jnp.einsum("gcp,gh->gphc", m, eye).reshape(N_STATE, D_SSM)

    def lam_pow(k):
        kk = k.astype(F32)[:, None, None]
        m = jnp.exp(a_re * dt * kk)
        th = a_im * dt * kk
        return (m * jnp.cos(th)).reshape(-1, N_STATE), (m * jnp.sin(th)).reshape(-1, N_STATE)

    def step_table(seg):
        ks = []
        dd = 1
        while dd < seg:
            ks.append(dd)
            dd *= 2
        re, im = lam_pow(jnp.array(ks))
        live = jnp.arange(seg)[None, :, None] >= jnp.array(ks)[:, None, None]
        tab = jnp.stack([jnp.where(live, re[:, None, :], 0.0), jnp.where(live, im[:, None, :], 0.0)], axis=1)
        return tab.reshape(-1, N_STATE)

    sp = {"bmat": jnp.concatenate([blk_b(bbr), blk_b(bbi)], axis=1).astype(BF16),
          "cr": blk_c(c_re).astype(BF16), "ci": (-blk_c(c_im)).astype(BF16), "d": d[None],
          "lam_p": step_table(SCAN_SEG), "lam_s": step_table(nq)}
    sp["pr_p"], sp["pi_p"] = lam_pow(jnp.arange(SCAN_SEG) + 1)
    return sp, lam_pow


def _band_matrix(nb, ns):
    ratio = SEL_BLOCK // CMP_STRIDE
    lo = CMP_BLOCK // CMP_STRIDE - 1
    c = jnp.arange(nb)[:, None]
    j = jnp.arange(ns)[None, :]
    return ((c >= ratio * j - lo) & (c <= ratio * j + ratio - 1)).astype(BF16)


def kernel(x_prompt, x_sample, cache_kv_cmp, cache_kv_sel, cache_kv_win, state_ssm_re, state_ssm_im, page_table, p_prompt, p_sample, norm_ffn1, w_ffn1_in, w_ffn1_out, norm_mix, w_in, qk_norm, ssm_a_re, ssm_a_im, ssm_log_dt, ssm_b_re, ssm_b_im, ssm_c_re, ssm_c_im, ssm_d, w_glu, cmp_pe_k, cmp_pe_v, cmp_w_k, cmp_w_v, w_att_out, w_out, norm_ffn2, w_ffn2_in, w_ffn2_out, norm_ple, w_ple_gate, w_ple_proj):
    bp, t = x_prompt.shape[:2]
    bsz, nq = x_sample.shape[:2]
    n_pages = page_table.shape[1]
    past = n_pages * PAGE_SIZE
    n_pool = cache_kv_cmp.shape[1]
    w_buf = cache_kv_win.shape[2]
    ns_rows = bsz * nq
    assert bp == 1 and t % SEL_CHUNK == 0 and t >= WIN_KEYS and t % ns_rows == 0 and nq < CMP_STRIDE
    assert past == t and w_buf == WINDOW
    nb = t // CMP_STRIDE
    ns = t // SEL_BLOCK
    nt = t // Q_TILE
    nchunks = t // SEL_CHUNK
    mband = _band_matrix(nb, ns)
    page_table = page_table.astype(jnp.int32)
    feat_major = lambda c: jnp.transpose(c, (0, 1, 3, 4, 5, 2)).reshape(DEPTH * c.shape[1], 2 * D_KV, c.shape[2])
    pages_cmp = feat_major(cache_kv_cmp)
    pages_sel = feat_major(cache_kv_sel)
    pages_win = feat_major(cache_kv_win)

    h = jnp.concatenate([x_prompt[0], x_sample.reshape(ns_rows, D_MODEL)], axis=0)
    st_p = [[] for _ in range(5)]
    st_s = [[] for _ in range(5)]
    for i in range(DEPTH):
        row1 = lambda a: a[i][None]
        h = _ffn(h, row1(norm_ffn1), w_ffn1_in[i].astype(BF16), w_ffn1_out[i].astype(BF16))
        w_mix, gq, gks, gkw = _prep_mix_weights(w_in[i], qk_norm[i])
        u, qp, kv_cmp, kv_sel, kv_win, kvs_bf, kvw_bf, gn, ga, gb = _mix_in(h, row1(norm_mix), w_mix, gq, gks, gkw)

        sp, lam_pow = _prep_s5(ssm_a_re[i], ssm_a_im[i], ssm_log_dt[i], ssm_b_re[i], ssm_b_im[i],
                               ssm_c_re[i], ssm_c_im[i], ssm_d[i], nq)
        sp["pr_s"], sp["pi_s"] = lam_pow(jnp.arange(ns_rows) % nq + 1)
        yg_p, hr_p, hi_p = _s5_prompt(u, t, sp)
        h0r = jnp.repeat(state_ssm_re[i].reshape(bsz, N_STATE), nq, axis=0)
        h0i = jnp.repeat(state_ssm_im[i].reshape(bsz, N_STATE), nq, axis=0)
        yg_s, hr_s, hi_s = _s5_sample(u, t, ns_rows, nq, h0r, h0i, sp)

        cw = _prep_compress(cmp_pe_k[i], cmp_pe_v[i], cmp_w_k[i], cmp_w_v[i], qk_norm[i, 1])
        layer_pages = page_table + i * n_pool
        kcc_p, vcc_p = _compress_rows(kv_cmp, t, cw)
        kcc_s, vcc_s = _compress_paged(layer_pages, pages_cmp, cw)
        ocw_p, sel_p, flags = _e1_prompt(qp, gn, kcc_p, vcc_p, mband, kvw_bf, t)
        ocw_s, sel_s = _e1_sample(qp, gn, kcc_s, vcc_s, mband, pages_win, i * bsz, kv_win, t, bsz, nq, past)
        idle = flags[:, 0, :nchunks] == 0
        order = jnp.argsort(idle, axis=1, stable=True).astype(jnp.int32).reshape(-1)
        cnt = (nchunks - jnp.sum(idle, axis=1)).astype(jnp.int32)
        oatt_p = _e2_prompt(order, cnt, qp, sel_p, gn, ocw_p, kvs_bf, t)
        oatt_s = _e2_sample(layer_pages, pages_sel, qp, sel_s, gn, ocw_s, kv_sel, t, nq)

        h = _post(h, jnp.concatenate([yg_p, yg_s], axis=0), jnp.concatenate([oatt_p, oatt_s], axis=0), ga, gb,
                  w_glu[i].astype(BF16), _prep_att_out(w_att_out[i]), w_out[i].astype(BF16))
        p_all = jnp.concatenate([p_prompt[i, 0], p_sample[i].reshape(ns_rows, D_PLE)], axis=0)
        h = _ffn(h, row1(norm_ffn2), w_ffn2_in[i].astype(BF16), w_ffn2_out[i].astype(BF16),
                 (p_all, row1(norm_ple), w_ple_gate[i].astype(BF16), w_ple_proj[i].astype(BF16)))

        kv5 = lambda a, lead: a.reshape(lead + (2, N_KV, HEAD_DIM))
        st_p[0].append(kv5(kv_cmp[:t], (1, t)))
        st_p[1].append(kv5(kv_sel[:t], (1, t)))
        st_p[2].append(kv5(kv_win[t - min(WINDOW, t):t], (1, min(WINDOW, t))))
        st_p[3].append(hr_p.reshape(1, N_SSM_GROUPS, P_STATE))
        st_p[4].append(hi_p.reshape(1, N_SSM_GROUPS, P_STATE))
        st_s[0].append(kv5(kv_cmp[t:], (bsz, nq)))
        st_s[1].append(kv5(kv_sel[t:], (bsz, nq)))
        st_s[2].append(jnp.concatenate([cache_kv_win[i, :, nq:], kv5(kv_win[t:], (bsz, nq))], axis=1))
        st_s[3].append(hr_s[nq - 1::nq].reshape(bsz, N_SSM_GROUPS, P_STATE))
        st_s[4].append(hi_s[nq - 1::nq].reshape(bsz, N_SSM_GROUPS, P_STATE))

    outs_p = [jnp.stack(a) for a in st_p]
    outs_s = [jnp.stack(a) for a in st_s]
    y_prompt = h[:t][None]
    y_sample = h[t:].reshape(bsz, nq, D_MODEL)
    return (y_prompt, y_sample, *outs_p, *outs_s)
```

```python
import functools

import jax
import jax.numpy as jnp
from jax import lax
from jax.experimental import pallas as pl
from jax.experimental.pallas import tpu as pltpu

F32 = jnp.float32
BF16 = jnp.bfloat16

D_MODEL = 1024
DEPTH = 2
D_SSM = 512
SSM_GROUP = 16
N_SSM_GROUPS = 32
P_STATE = 64
N_STATE = N_SSM_GROUPS * P_STATE
N_HEADS = 8
HEAD_DIM = 64
N_KV = 2
HPG = 4
D_KV = 128
CMP_BLOCK = 32
CMP_STRIDE = 16
SEL_BLOCK = 64
TOP_N = 16
WINDOW = 512
PAGE_SIZE = 128
SEL_BIG = 1e4
D_FF = 2816
D_PLE = 256
RMS_EPS = 1e-6
NEG = -1e30
HALF_NEG = -0.5e30
LOG2E = 1.4426950408889634

LANE = 128
Q_TILE = 128
SCAN_SEG = 8
SEL_CHUNK = 256
WIN_KEYS = WINDOW + Q_TILE
Q_PAD = N_HEADS * LANE
VMEM_LIMIT = 56 * 2 ** 20


def _cparams(*sem):
    return pltpu.CompilerParams(dimension_semantics=sem, vmem_limit_bytes=VMEM_LIMIT)


def _dot(a, b):
    return jnp.dot(a, b, preferred_element_type=F32)


def _dot_nt(a, b):
    return lax.dot_general(a, b, (((1,), (1,)), ((), ())), preferred_element_type=F32)


def _pick_tile(n, target):
    for t in range(min(n, target), 15, -1):
        if n % t == 0 and t % 16 == 0:
            return t
    raise ValueError(f"no row tile for {n}")


def _const_spec(shape):
    nd = len(shape)
    return pl.BlockSpec(shape, lambda *_: (0,) * nd, pipeline_mode=pl.Buffered(1))


def _rms(x, g):
    ms = jnp.mean(x * x, axis=-1, keepdims=True)
    return x * lax.rsqrt(ms + RMS_EPS) * g


def _halfnorm(x, gain):
    lo = lax.broadcasted_iota(jnp.int32, (1, LANE), 1) < HEAD_DIM
    x2 = x * x
    s_lo = jnp.sum(jnp.where(lo, x2, 0.0), axis=-1, keepdims=True)
    s_hi = jnp.sum(jnp.where(lo, 0.0, x2), axis=-1, keepdims=True)
    ms = jnp.where(lo, s_lo, s_hi) * (1.0 / HEAD_DIM)
    return x * lax.rsqrt(ms + RMS_EPS) * gain


def _ffn_kernel(*refs, ple):
    if ple:
        h_ref, g_ref, wi_ref, wo_ref, p_ref, gp_ref, wg_ref, wp_ref, o_ref = refs
    else:
        h_ref, g_ref, wi_ref, wo_ref, o_ref = refs
    h = h_ref[...]
    xn = _rms(h, g_ref[...]).astype(BF16)
    a = _dot(xn, wi_ref[:, :D_FF])
    b = _dot(xn, wi_ref[:, D_FF:])
    act = (a * jax.nn.sigmoid(a) * b).astype(BF16)
    h = h + 0.5 * _dot(act, wo_ref[...])
    if ple:
        xg = _rms(h, gp_ref[...]).astype(BF16)
        gate = jax.nn.sigmoid(_dot(xg, wg_ref[...]))
        h = h + gate * _dot(p_ref[...].astype(BF16), wp_ref[...])
    o_ref[...] = h


def _ffn(h, g, wi, wo, ple_args=None):
    n = h.shape[0]
    tm = _pick_tile(n, 256)
    row = lambda w: pl.BlockSpec((tm, w), lambda i: (i, 0))
    in_specs = [row(D_MODEL), _const_spec((1, D_MODEL)), _const_spec(wi.shape), _const_spec(wo.shape)]
    args = [h, g, wi, wo]
    if ple_args is not None:
        p, gp, wg, wp = ple_args
        in_specs += [row(D_PLE), _const_spec((1, D_MODEL)), _const_spec(wg.shape), _const_spec(wp.shape)]
        args += [p, gp, wg, wp]
    return pl.pallas_call(
        functools.partial(_ffn_kernel, ple=ple_args is not None),
        out_shape=jax.ShapeDtypeStruct((n, D_MODEL), F32),
        grid=(n // tm,),
        in_specs=in_specs,
        out_specs=row(D_MODEL),
        compiler_params=_cparams("parallel"),
    )(*args)


_C_U = 0
_C_Q = _C_U + D_SSM
_C_KV = _C_Q + Q_PAD
_C_GN = _C_KV + 6 * D_KV
_C_GA = _C_GN + LANE
_C_GB = _C_GA + D_MODEL
_C_END = _C_GB + D_MODEL


def _mix_in_kernel(h_ref, g_ref, w_ref, gq_ref, gks_ref, gkw_ref,
                   u_ref, qp_ref, kvc_ref, kvs_ref, kvw_ref, kvsb_ref, kvwb_ref, gn_ref, ga_ref, gb_ref):
    xn = _rms(h_ref[...], g_ref[...]).astype(BF16)
    u_ref[...] = _dot(xn, w_ref[:, _C_U:_C_Q])
    zq = _dot(xn, w_ref[:, _C_Q:_C_KV])
    for h in range(N_HEADS):
        qh = zq[:, h * LANE:(h + 1) * LANE]
        ms = jnp.sum(qh * qh, axis=-1, keepdims=True) * (1.0 / HEAD_DIM)
        qn = qh * lax.rsqrt(ms + RMS_EPS) * gq_ref[:, h * LANE:(h + 1) * LANE]
        qp_ref[:, h * LANE:(h + 1) * LANE] = qn
    zkv = _dot(xn, w_ref[:, _C_KV:_C_GN])
    kvc_ref[...] = zkv[:, 0:2 * D_KV]
    ks = _halfnorm(zkv[:, 2 * D_KV:3 * D_KV], gks_ref[...])
    vs = zkv[:, 3 * D_KV:4 * D_KV]
    kw = _halfnorm(zkv[:, 4 * D_KV:5 * D_KV], gkw_ref[...])
    vw = zkv[:, 5 * D_KV:6 * D_KV]
    kvs_ref[:, 0:D_KV] = ks
    kvs_ref[:, D_KV:] = vs
    kvw_ref[:, 0:D_KV] = kw
    kvw_ref[:, D_KV:] = vw
    kvsb_ref[:, 0:D_KV] = ks.astype(BF16)
    kvsb_ref[:, D_KV:2 * D_KV] = vs.astype(BF16)
    kvsb_ref[:, 2 * D_KV:] = jnp.ones((ks.shape[0], D_KV), BF16)
    kvwb_ref[:, 0:D_KV] = kw.astype(BF16)
    kvwb_ref[:, D_KV:] = vw.astype(BF16)
    gn_ref[...] = jax.nn.sigmoid(_dot(xn, w_ref[:, _C_GN:_C_GA]))
    ga_ref[...] = jax.nn.sigmoid(_dot(xn, w_ref[:, _C_GA:_C_GB]))
    gb_ref[...] = jax.nn.sigmoid(_dot(xn, w_ref[:, _C_GB:_C_END]))


def _mix_in(h, g, w, gq, gks, gkw):
    n = h.shape[0]
    tm = _pick_tile(n, 256)
    row = lambda width: pl.BlockSpec((tm, width), lambda i: (i, 0))
    widths = [(D_SSM, F32), (Q_PAD, F32), (2 * D_KV, F32), (2 * D_KV, F32), (2 * D_KV, F32),
              (3 * D_KV, BF16), (2 * D_KV, BF16), (LANE, F32), (D_MODEL, F32), (D_MODEL, F32)]
    return pl.pallas_call(
        _mix_in_kernel,
        out_shape=[jax.ShapeDtypeStruct((n, wd), dt) for wd, dt in widths],
        grid=(n // tm,),
        in_specs=[row(D_MODEL), _const_spec((1, D_MODEL)), _const_spec(w.shape),
                  _const_spec((1, Q_PAD)), _const_spec((1, LANE)), _const_spec((1, LANE))],
        out_specs=[row(wd) for wd, _ in widths],
        compiler_params=_cparams("parallel"),
    )(h, g, w, gq, gks, gkw)


def _gelu_tanh(x):
    return 0.5 * x * (1.0 + jnp.tanh(0.7978845608028654 * (x + 0.044715 * (x * x * x))))


def _s5_kernel(*refs, seg, carry, seq_steps=None):
    if carry:
        (u_ref, bm_ref, cr_ref, ci_ref, d_ref, lam_ref, pr_ref, pi_ref,
         y_ref, hr_out, hi_out, car_ref, cai_ref) = refs
    else:
        (u_ref, bm_ref, cr_ref, ci_ref, d_ref, lam_ref, pr_ref, pi_ref, h0r_ref, h0i_ref, y_all_ref,
         y_ref, hr_out, hi_out) = refs
    u = u_ref[...]
    rows = u.shape[0]
    x = _dot(u.astype(BF16), bm_ref[...])
    xr = x[:, :N_STATE]
    xi = x[:, N_STATE:]
    d, k = 1, 0
    while d < seg:
        lr = jnp.tile(lam_ref[2 * k * seg:(2 * k + 1) * seg, :], (rows // seg, 1))
        li = jnp.tile(lam_ref[(2 * k + 1) * seg:(2 * k + 2) * seg, :], (rows // seg, 1))
        sr = pltpu.roll(xr, d, 0)
        si = pltpu.roll(xi, d, 0)
        xr, xi = xr + lr * sr - li * si, xi + lr * si + li * sr
        d *= 2
        k += 1
    p_r = pr_ref[...]
    p_i = pi_ref[...]
    if carry:
        @pl.when(pl.program_id(0) == 0)
        def _():
            car_ref[...] = jnp.zeros_like(car_ref)
            cai_ref[...] = jnp.zeros_like(cai_ref)
        c_r = car_ref[...]
        c_i = cai_ref[...]
        hr_parts, hi_parts = [], []
        for j in range(rows // seg):
            tr = xr[j * seg:(j + 1) * seg]
            ti = xi[j * seg:(j + 1) * seg]
            hr_j = tr + p_r * c_r - p_i * c_i
            hi_j = ti + p_r * c_i + p_i * c_r
            c_r = hr_j[seg - 1:seg, :]
            c_i = hi_j[seg - 1:seg, :]
            hr_parts.append(hr_j)
            hi_parts.append(hi_j)
        hr = jnp.concatenate(hr_parts, axis=0)
        hi = jnp.concatenate(hi_parts, axis=0)
    else:
        c_r = h0r_ref[...]
        c_i = h0i_ref[...]
        hr = xr + p_r * c_r - p_i * c_i
        hi = xi + p_r * c_i + p_i * c_r
    y = _dot(hr.astype(BF16), cr_ref[...]) + _dot(hi.astype(BF16), ci_ref[...]) + d_ref[...] * u
    y_ref[...] = _gelu_tanh(y).astype(BF16)
    if carry:
        car_ref[...] = hr[rows - 1:rows, :]
        cai_ref[...] = hi[rows - 1:rows, :]

        @pl.when(pl.program_id(0) < seq_steps)
        def _():
            hr_out[...] = hr[rows - 1:rows, :]
            hi_out[...] = hi[rows - 1:rows, :]
    else:
        hr_out[...] = hr
        hi_out[...] = hi


def _s5_prompt(u, t, sp):
    rows = Q_TILE
    consts = [sp["bmat"], sp["cr"], sp["ci"], sp["d"], sp["lam_p"], sp["pr_p"], sp["pi_p"]]
    return pl.pallas_call(
        functools.partial(_s5_kernel, seg=SCAN_SEG, carry=True, seq_steps=t // rows),
        out_shape=[jax.ShapeDtypeStruct((u.shape[0], D_SSM), BF16),
                   jax.ShapeDtypeStruct((1, N_STATE), F32), jax.ShapeDtypeStruct((1, N_STATE), F32)],
        grid=(u.shape[0] // rows,),
        in_specs=[pl.BlockSpec((rows, D_SSM), lambda i: (i, 0))] + [_const_spec(c.shape) for c in consts],
        out_specs=[pl.BlockSpec((rows, D_SSM), lambda i: (i, 0)),
                   _const_spec((1, N_STATE)), _const_spec((1, N_STATE))],
        scratch_shapes=[pltpu.VMEM((1, N_STATE), F32), pltpu.VMEM((1, N_STATE), F32)],
        compiler_params=_cparams("arbitrary"),
    )(u, *consts)


def _s5_sample(u, yg, t, ns_rows, nq, h0r, h0i, sp):
    consts = [sp["bmat"], sp["cr"], sp["ci"], sp["d"], sp["lam_s"], sp["pr_s"], sp["pi_s"], h0r, h0i]
    blk = t // ns_rows
    return pl.pallas_call(
        functools.partial(_s5_kernel, seg=nq, carry=False),
        out_shape=[jax.ShapeDtypeStruct(yg.shape, BF16),
                   jax.ShapeDtypeStruct((ns_rows, N_STATE), F32), jax.ShapeDtypeStruct((ns_rows, N_STATE), F32)],
        grid=(1,),
        in_specs=([pl.BlockSpec((ns_rows, D_SSM), lambda i: (blk, 0))] + [_const_spec(c.shape) for c in consts]
                  + [pl.BlockSpec(memory_space=pl.ANY)]),
        out_specs=[pl.BlockSpec((ns_rows, D_SSM), lambda i: (blk, 0)),
                   _const_spec((ns_rows, N_STATE)), _const_spec((ns_rows, N_STATE))],
        input_output_aliases={len(consts) + 1: 0},
        compiler_params=_cparams("arbitrary"),
    )(u, *consts, yg)


def _compress_part(get_xs, w_ref, a0, a1, part, rows):
    acc0 = acc1 = None
    for s in range(CMP_STRIDE):
        xs = get_xs(s)
        t0 = _dot(xs, w_ref[s])
        t1 = _dot(xs, w_ref[CMP_STRIDE + s])
        acc0 = t0 if acc0 is None else acc0 + t0
        acc1 = t1 if acc1 is None else acc1 + t1
    base = pl.multiple_of(part * rows, rows)
    a0[pl.ds(base, rows), :] = acc0
    a1[pl.ds(base, rows), :] = acc1


def _compress_finish(a0, a1, w_ref, pe_ref, gk_ref, kcc_ref, vcc_ref, nb):
    bias = []
    for j in range(CMP_BLOCK // CMP_STRIDE):
        b = jnp.zeros((8, 2 * D_KV), F32)
        for s in range(CMP_STRIDE):
            r = j * CMP_STRIDE + s
            b = b + _dot(jnp.broadcast_to(pe_ref[r:r + 1, :], (8, 2 * D_KV)).astype(BF16), w_ref[r])
        bias.append(b[0:1, :])
    out = (a0[...] + bias[0]) + pltpu.roll(a1[...] + bias[1], nb - 1, 0)
    valid = lax.broadcasted_iota(jnp.int32, (nb, 1), 0) < nb - 1
    out = jnp.where(valid, out, 0.0)
    kcc_ref[0] = _halfnorm(out[:, 0:D_KV], gk_ref[...]).astype(BF16)
    vcc_ref[0] = out[:, D_KV:].astype(BF16)


def _compress_rows_kernel(xk_ref, xv_ref, w_ref, pe_ref, gk_ref, kcc_ref, vcc_ref, a0, a1, *, nsplit, nb):
    part = pl.program_id(0)
    rows = nb // nsplit

    def get_xs(s):
        tok = pl.ds(s, rows, stride=CMP_STRIDE)
        return jnp.concatenate([xk_ref[tok, :], xv_ref[tok, :]], axis=1).astype(BF16)
    _compress_part(get_xs, w_ref, a0, a1, part, rows)

    @pl.when(part == nsplit - 1)
    def _():
        _compress_finish(a0, a1, w_ref, pe_ref, gk_ref, kcc_ref, vcc_ref, nb)


def _compress_rows(kv_cmp, t, cw):
    nb = t // CMP_STRIDE
    nsplit = 2
    return pl.pallas_call(
        functools.partial(_compress_rows_kernel, nsplit=nsplit, nb=nb),
        out_shape=[jax.ShapeDtypeStruct((1, nb, D_KV), BF16), jax.ShapeDtypeStruct((1, nb, D_KV), BF16)],
        grid=(nsplit,),
        in_specs=[pl.BlockSpec((t // nsplit, D_KV), lambda i: (i, 0)),
                  pl.BlockSpec((t // nsplit, D_KV), lambda i: (i, 1)),
                  _const_spec(cw["w"].shape), _const_spec(cw["pe"].shape), _const_spec((1, LANE))],
        out_specs=[_const_spec((1, nb, D_KV)), _const_spec((1, nb, D_KV))],
        scratch_shapes=[pltpu.VMEM((nb, 2 * D_KV), F32), pltpu.VMEM((nb, 2 * D_KV), F32)],
        compiler_params=_cparams("arbitrary"),
    )(kv_cmp, kv_cmp, cw["w"], cw["pe"], cw["gk"])


def _compress_kernel(pt_ref, pages_ref, w_ref, pe_ref, gk_ref, perm_ref,
                     kcc_ref, vcc_ref, buf, sem, x_scr, a0, a1, *, pps, nsplit, nb):
    s = pl.program_id(0)
    nsteps = pl.num_programs(0)

    def page_copy(step, p, slot):
        b = step // nsplit
        part = step % nsplit
        pg = pt_ref[b, part * pps + p]
        col = pl.multiple_of((p % 2) * PAGE_SIZE, PAGE_SIZE)
        return pltpu.make_async_copy(pages_ref.at[pg], buf.at[slot, p // 2, :, pl.ds(col, PAGE_SIZE)],
                                     sem.at[slot])

    def start(step, slot):
        def body(p, c):
            page_copy(step, p, slot).start()
            return c
        lax.fori_loop(0, pps, body, 0)

    def wait(step, slot):
        def body(p, c):
            page_copy(step, p, slot).wait()
            return c
        lax.fori_loop(0, pps, body, 0)

    @pl.when(s == 0)
    def _():
        start(s, 0)

    @pl.when(s + 1 < nsteps)
    def _():
        start(s + 1, (s + 1) % 2)

    slot = s % 2
    wait(s, slot)
    part = s % nsplit

    def to_rows(q, c):
        xt = buf[slot, q].astype(BF16)
        xp = _dot_nt(perm_ref[...], xt).astype(BF16)
        base = pl.multiple_of(q * CMP_STRIDE, CMP_STRIDE)
        for s2 in range(CMP_STRIDE):
            x_scr[s2, pl.ds(base, CMP_STRIDE), :] = xp[s2 * CMP_STRIDE:(s2 + 1) * CMP_STRIDE, :]
        return c
    lax.fori_loop(0, pps // 2, to_rows, 0, unroll=4)
    _compress_part(lambda s2: x_scr[s2], w_ref, a0, a1, part, nb // nsplit)

    @pl.when(part == nsplit - 1)
    def _():
        _compress_finish(a0, a1, w_ref, pe_ref, gk_ref, kcc_ref, vcc_ref, nb)


def _compress_paged(page_table, pages, cw):
    bsz, n_pages = page_table.shape
    nb = n_pages * (PAGE_SIZE // CMP_STRIDE)
    nsplit = 2
    pps = n_pages // nsplit
    const = lambda shape: pl.BlockSpec(shape, lambda s, pt: (0,) * len(shape), pipeline_mode=pl.Buffered(1))
    grid_spec = pltpu.PrefetchScalarGridSpec(
        num_scalar_prefetch=1,
        grid=(bsz * nsplit,),
        in_specs=[pl.BlockSpec(memory_space=pl.ANY), const(cw["w"].shape), const(cw["pe"].shape),
                  const((1, LANE)), const(cw["perm"].shape)],
        out_specs=[pl.BlockSpec((1, nb, D_KV), lambda s, pt: (s // nsplit, 0, 0)),
                   pl.BlockSpec((1, nb, D_KV), lambda s, pt: (s // nsplit, 0, 0))],
        scratch_shapes=[pltpu.VMEM((2, pps // 2, 2 * D_KV, 2 * PAGE_SIZE), F32), pltpu.SemaphoreType.DMA((2,)),
                        pltpu.VMEM((CMP_STRIDE, nb // nsplit, 2 * D_KV), BF16),
                        pltpu.VMEM((nb, 2 * D_KV), F32), pltpu.VMEM((nb, 2 * D_KV), F32)],
    )
    return pl.pallas_call(
        functools.partial(_compress_kernel, pps=pps, nsplit=nsplit, nb=nb),
        out_shape=[jax.ShapeDtypeStruct((bsz, nb, D_KV), BF16), jax.ShapeDtypeStruct((bsz, nb, D_KV), BF16)],
        grid_spec=grid_spec,
        compiler_params=_cparams("arbitrary"),
    )(page_table, pages, cw["w"], cw["pe"], cw["gk"], cw["perm"])


def _heads_rows(qp):
    return jnp.concatenate([qp[:, h * LANE:(h + 1) * LANE] for h in range(N_HEADS)], axis=0).astype(BF16)


def _slope2(h):
    return LOG2E * 2.0 ** (-8.0 * (h + 1) / N_HEADS)


def _branch(q_all, tq, pieces):
    s_all = [_dot(q_all, k) if fm else _dot_nt(q_all, k) for k, _, _, _, fm in pieces]
    es, invs = [], []
    for h in range(N_HEADS):
        rows = slice(h * tq, (h + 1) * tq)
        sm = [jnp.where(pc[3], s[rows] - _slope2(h) * pc[2], NEG) for s, pc in zip(s_all, pieces)]
        m = functools.reduce(jnp.maximum, [jnp.max(x, axis=-1, keepdims=True) for x in sm])
        m = jnp.maximum(m, HALF_NEG)
        e = [jnp.exp2(x - m) for x in sm]
        l = functools.reduce(lambda a, b: a + b, [jnp.sum(x, axis=-1, keepdims=True) for x in e])
        es.append(e)
        invs.append(1.0 / jnp.maximum(l, 1e-30))
    o = None
    for i, (_, v, _, _, fm) in enumerate(pieces):
        p = jnp.concatenate([es[h][i] for h in range(N_HEADS)], axis=0).astype(BF16)
        t = _dot_nt(p, v) if fm else _dot(p, v)
        o = t if o is None else o + t
    return o * jnp.concatenate(invs, axis=0), es, invs


def _cmp_win_topk(q_all, tq, qpos, gn, kcc, vcc, mband, win_pieces, nb, ns, topn):
    n_io = lax.broadcasted_iota(jnp.int32, (1, nb), 1)
    kc_end = n_io * CMP_STRIDE + (CMP_BLOCK - 1)
    mask_c = (qpos >= kc_end) & (n_io < nb - 1)
    kdist_c = (qpos[0:1, :] - kc_end).astype(F32)
    o_c, es, invs = _branch(q_all, tq, [(kcc, vcc, kdist_c, mask_c, False)])
    o_w, _, _ = _branch(q_all, tq, win_pieces)
    imps = []
    for g in range(N_KV):
        psum = es[HPG * g][0] * invs[HPG * g]
        for hh in range(1, HPG):
            psum = psum + es[HPG * g + hh][0] * invs[HPG * g + hh]
        hi = psum.astype(BF16)
        r = psum - hi.astype(F32)
        mid = r.astype(BF16)
        lo = (r - mid.astype(F32)).astype(BF16)
        imps.append(_dot(hi, mband) + _dot(mid, mband) + _dot(lo, mband))
    imp = jnp.concatenate(imps, axis=0)
    blk = lax.broadcasted_iota(jnp.int32, (1, ns), 1)
    cur = jnp.concatenate([jnp.right_shift(qpos, 6)] * N_KV, axis=0)
    forced = (blk == 0) | (blk == cur) | (blk == cur - 1)
    score = jnp.where(forced, imp + SEL_BIG, jnp.where(blk <= cur, imp, -SEL_BIG))
    blkf = blk.astype(F32)
    sel = jnp.zeros_like(score)
    for _ in range(topn):
        m = jnp.max(score, axis=-1, keepdims=True)
        first = jnp.min(jnp.where(score == m, blkf, float(ns)), axis=-1, keepdims=True)
        hit = blkf == first
        sel = jnp.where(hit, 1.0, sel)
        score = jnp.where(hit, -jnp.inf, score)
    sel = jnp.where(blk <= cur, sel, 0.0)
    outs = []
    for h in range(N_HEADS):
        rows = slice(h * tq, (h + 1) * tq)
        outs.append(gn[:, h:h + 1] * o_c[rows] + gn[:, 2 * N_HEADS + h:2 * N_HEADS + h + 1] * o_w[rows])
    return jnp.concatenate(outs, axis=1), sel


def _e1_prompt_kernel(qp_ref, gn_ref, kcc_ref, vcc_ref, mband_ref, kvw_ref, ocw_ref, sel_ref, flag_ref,
                      *, nb, ns):
    tq = Q_TILE
    s0 = pl.program_id(0) * tq
    qpos = s0 + lax.broadcasted_iota(jnp.int32, (tq, 1), 0)
    q_all = _heads_rows(qp_ref[...])
    start = pl.multiple_of(jnp.maximum(s0 - WINDOW, 0), Q_TILE)
    kw = kvw_ref[pl.ds(start, WIN_KEYS), 0:D_KV]
    vw = kvw_ref[pl.ds(start, WIN_KEYS), D_KV:]
    kpos = start + lax.broadcasted_iota(jnp.int32, (1, WIN_KEYS), 1)
    dist_w = qpos - kpos
    mask_w = (dist_w >= 0) & (dist_w < WINDOW)
    ocw, sel = _cmp_win_topk(q_all, tq, qpos, gn_ref[...], kcc_ref[0], vcc_ref[0], mband_ref[...],
                             [(kw, vw, (s0 - kpos).astype(F32), mask_w, False)], nb, ns, TOP_N)
    ocw_ref[...] = ocw
    sel_ref[0] = sel[:tq]
    sel_ref[1] = sel[tq:]
    colany = jnp.max(sel, axis=0, keepdims=True)
    j_io = lax.broadcasted_iota(jnp.int32, (ns, LANE), 0)
    c_io = lax.broadcasted_iota(jnp.int32, (ns, LANE), 1)
    grp = jnp.where(jnp.right_shift(j_io, 2) == c_io, 1.0, 0.0).astype(BF16)
    cnt = _dot(jnp.broadcast_to(colany, (8, ns)).astype(BF16), grp)
    flag_ref[0] = (cnt > 0.5).astype(jnp.int32)


def _e1_prompt(qp, gn, kcc, vcc, mband, kvw_bf, t):
    nb, ns = mband.shape
    nt = t // Q_TILE
    row = lambda w: pl.BlockSpec((Q_TILE, w), lambda i: (i, 0))
    return pl.pallas_call(
        functools.partial(_e1_prompt_kernel, nb=nb, ns=ns),
        out_shape=[jax.ShapeDtypeStruct((t, Q_PAD), F32), jax.ShapeDtypeStruct((N_KV, t, ns), F32),
                   jax.ShapeDtypeStruct((nt, 8, LANE), jnp.int32)],
        grid=(nt,),
        in_specs=[row(Q_PAD), row(LANE), _const_spec((1, nb, D_KV)), _const_spec((1, nb, D_KV)),
                  _const_spec(mband.shape), _const_spec(kvw_bf.shape)],
        out_specs=[row(Q_PAD), pl.BlockSpec((N_KV, Q_TILE, ns), lambda i: (0, i, 0)),
                   pl.BlockSpec((1, 8, LANE), lambda i: (i, 0, 0))],
        compiler_params=_cparams("parallel"),
    )(qp, gn, kcc, vcc, mband, kvw_bf)


def _pad_rows(x, rows):
    return jnp.concatenate([x, jnp.zeros((rows - x.shape[0], x.shape[1]), x.dtype)], axis=0)


def _e1_sample_kernel(qp_ref, gn_ref, kcc_ref, vcc_ref, mband_ref, cwin_ref, nwin_ref, ocw_ref, sel_ref,
                      *, nb, ns, nq, past):
    qi = lax.broadcasted_iota(jnp.int32, (nq, 1), 0)
    qpos = past + qi
    q_all = _heads_rows(qp_ref[...])
    cw = cwin_ref[0]
    w_buf = cw.shape[1]
    j_c = lax.broadcasted_iota(jnp.int32, (1, w_buf), 1)
    dist_cw = qi + (w_buf - j_c)
    mask_cw = (dist_cw >= 0) & (dist_cw < WINDOW)
    nw = _pad_rows(nwin_ref[...], LANE)
    j_n = lax.broadcasted_iota(jnp.int32, (1, LANE), 1)
    mask_nw = (qi >= j_n) & (j_n < nq)
    pieces = [(cw[0:D_KV, :].astype(BF16), cw[D_KV:, :].astype(BF16), (w_buf - j_c).astype(F32), mask_cw, True),
              (nw[:, 0:D_KV].astype(BF16), nw[:, D_KV:].astype(BF16), (-j_n).astype(F32), mask_nw, False)]
    ocw, sel = _cmp_win_topk(q_all, nq, qpos, gn_ref[...], kcc_ref[0], vcc_ref[0], mband_ref[...],
                             pieces, nb, ns, TOP_N - 1)
    ocw_ref[...] = ocw
    sel_ref[0, 0] = sel[:nq]
    sel_ref[0, 1] = sel[nq:]


def _e1_sample(qp, gn, kcc, vcc, mband, cache_win, win_off, kv_win, t, bsz, nq, past):
    nb, ns = mband.shape
    w_buf = cache_win.shape[2]
    off = t // nq
    row = lambda w: pl.BlockSpec((nq, w), lambda b: (off + b, 0))
    per_b = lambda shape: pl.BlockSpec((1,) + shape, lambda b: (b,) + (0,) * len(shape))
    return pl.pallas_call(
        functools.partial(_e1_sample_kernel, nb=nb, ns=ns, nq=nq, past=past),
        out_shape=[jax.ShapeDtypeStruct((bsz * nq, Q_PAD), F32), jax.ShapeDtypeStruct((bsz, N_KV, nq, ns), F32)],
        grid=(bsz,),
        in_specs=[row(Q_PAD), row(LANE), per_b((nb, D_KV)), per_b((nb, D_KV)), _const_spec(mband.shape),
                  pl.BlockSpec((1, 2 * D_KV, w_buf), lambda b: (win_off + b, 0, 0)), row(2 * D_KV)],
        out_specs=[pl.BlockSpec((nq, Q_PAD), lambda b: (b, 0)), per_b((N_KV, nq, ns))],
        compiler_params=_cparams("parallel"),
    )(qp, gn, kcc, vcc, mband, cache_win, kv_win)


def _online_step(s, kdist, masks, v_ones, m_ref, acc_ref, tq, feature_major=False):
    reps = s.shape[1] // LANE
    ps, alphas = [], []
    for h in range(N_HEADS):
        rows = slice(h * tq, (h + 1) * tq)
        sm = jnp.where(masks[h // HPG], s[rows] - _slope2(h) * kdist, NEG)
        m_old = m_ref[rows, :]
        m_new = jnp.maximum(m_old, jnp.max(sm, axis=-1, keepdims=True))
        alphas.append(jnp.exp2(m_old - m_new))
        ps.append(jnp.exp2(sm - jnp.tile(m_new, (1, reps))))
        m_ref[rows, :] = m_new
    p_all = jnp.concatenate(ps, axis=0).astype(BF16)
    a_all = jnp.concatenate(alphas, axis=0)
    pv = _dot_nt(p_all, v_ones) if feature_major else _dot(p_all, v_ones)
    acc_ref[...] = jnp.tile(a_all, (1, 2)) * acc_ref[...] + pv


def _expand_sel(sel, first_blk, n_keys):
    ns = sel.shape[1]
    j_io = lax.broadcasted_iota(jnp.int32, (ns, n_keys), 0)
    kb = first_blk + jnp.right_shift(lax.broadcasted_iota(jnp.int32, (ns, n_keys), 1), 6)
    e = jnp.where(j_io == kb, 1.0, 0.0).astype(BF16)
    return _dot(sel.astype(BF16), e) > 0.5


def _init_online(m_ref, acc_ref):
    m_ref[...] = jnp.full(m_ref.shape, HALF_NEG, F32)
    acc_ref[...] = jnp.zeros(acc_ref.shape, F32)


def _finish_online(ocw, gn, acc_ref, tq):
    o_s = acc_ref[:, 0:LANE] * (1.0 / jnp.maximum(acc_ref[:, LANE:], 1e-30))
    outs = []
    for h in range(N_HEADS):
        g1 = gn[:, N_HEADS + h:N_HEADS + h + 1]
        outs.append(ocw[:, h * LANE:(h + 1) * LANE] + g1 * o_s[h * tq:(h + 1) * tq])
    return jnp.concatenate(outs, axis=1)


def _e2_prompt_kernel(order_ref, cnt_ref, qp_ref, sel_ref, gn_ref, ocw_ref, kvs_ref, out_ref, m_ref, acc_ref,
                      *, nchunks, ns, nt):
    tq = Q_TILE
    i = pl.program_id(0)
    s0 = i * tq
    qpos = s0 + lax.broadcasted_iota(jnp.int32, (tq, 1), 0)
    q_all = _heads_rows(qp_ref[...])
    _init_online(m_ref, acc_ref)
    n_act = jnp.where(i < nt, cnt_ref[jnp.minimum(i, nt - 1)], 0)
    blocks_per_chunk = SEL_CHUNK // SEL_BLOCK

    def body(j, carry):
        c1 = order_ref[i * nchunks + 2 * j]
        paired = 2 * j + 1 < n_act
        c2 = jnp.where(paired, order_ref[i * nchunks + jnp.minimum(2 * j + 1, nchunks - 1)], c1)
        ks, vs, dists, masks = [], [], [], [[] for _ in range(N_KV)]
        for c, first_blk in ((c1, c1 * blocks_per_chunk), (c2, jnp.where(paired, c2 * blocks_per_chunk, ns))):
            base = pl.multiple_of(c * SEL_CHUNK, SEL_CHUNK)
            ks.append(kvs_ref[pl.ds(base, SEL_CHUNK), 0:D_KV])
            vs.append(kvs_ref[pl.ds(base, SEL_CHUNK), D_KV:])
            kpos = base + lax.broadcasted_iota(jnp.int32, (1, SEL_CHUNK), 1)
            dists.append((s0 - kpos).astype(F32))
            for g in range(N_KV):
                masks[g].append(_expand_sel(sel_ref[g], first_blk, SEL_CHUNK) & (qpos >= kpos))
        s = _dot_nt(q_all, jnp.concatenate(ks, axis=0))
        _online_step(s, jnp.concatenate(dists, axis=1), [jnp.concatenate(mg, axis=1) for mg in masks],
                     jnp.concatenate(vs, axis=0), m_ref, acc_ref, tq)
        return carry

    lax.fori_loop(0, (n_act + 1) // 2, body, 0)
    out_ref[...] = _finish_online(ocw_ref[...], gn_ref[...], acc_ref, tq)


def _e2_prompt(order, cnt, qp, sel, gn, ocw, kvs_bf, t):
    ns = sel.shape[2]
    nt = t // Q_TILE
    nchunks = order.shape[0] // nt
    row = lambda w: pl.BlockSpec((Q_TILE, w), lambda i, o, c: (i, 0))
    prow = lambda w: pl.BlockSpec((Q_TILE, w), lambda i, o, c: (jnp.minimum(i, nt - 1), 0))
    grid_spec = pltpu.PrefetchScalarGridSpec(
        num_scalar_prefetch=2,
        grid=(qp.shape[0] // Q_TILE,),
        in_specs=[row(Q_PAD), pl.BlockSpec((N_KV, Q_TILE, ns), lambda i, o, c: (0, jnp.minimum(i, nt - 1), 0)),
                  row(LANE), prow(Q_PAD), pl.BlockSpec(kvs_bf.shape, lambda i, o, c: (0, 0))],
        out_specs=row(Q_PAD),
        scratch_shapes=[pltpu.VMEM((N_HEADS * Q_TILE, LANE), F32), pltpu.VMEM((N_HEADS * Q_TILE, 2 * LANE), F32)],
    )
    return pl.pallas_call(
        functools.partial(_e2_prompt_kernel, nchunks=nchunks, ns=ns, nt=nt),
        out_shape=jax.ShapeDtypeStruct((qp.shape[0], Q_PAD), F32),
        grid_spec=grid_spec,
        compiler_params=_cparams("arbitrary"),
    )(order, cnt, qp, sel, gn, ocw, kvs_bf)


def _e2_sample_kernel(pt_ref, pages_ref, qp_ref, sel_ref, gn_ref, ocw_ref, nsel_ref, expand_ref, oatt_all_ref,
                      out_ref, buf, sem, m_ref, acc_ref, *, pps, nsplit, nq):
    s = pl.program_id(0)
    nsteps = pl.num_programs(0)
    n_keys = pps * PAGE_SIZE

    def page_copy(step, p, slot):
        b = step // nsplit
        part = step % nsplit
        pg = pt_ref[b, part * pps + p]
        col = pl.multiple_of(p * PAGE_SIZE, PAGE_SIZE)
        return pltpu.make_async_copy(pages_ref.at[pg], buf.at[slot, :, pl.ds(col, PAGE_SIZE)], sem.at[slot])

    def start(step, slot):
        def body(p, c):
            page_copy(step, p, slot).start()
            return c
        lax.fori_loop(0, pps, body, 0)

    def wait(step, slot):
        def body(p, c):
            page_copy(step, p, slot).wait()
            return c
        lax.fori_loop(0, pps, body, 0)

    @pl.when(s == 0)
    def _():
        start(s, 0)

    @pl.when(s + 1 < nsteps)
    def _():
        start(s + 1, (s + 1) % 2)

    slot = s % 2
    part = s % nsplit
    qi = lax.broadcasted_iota(jnp.int32, (nq, 1), 0)
    q_all = _heads_rows(qp_ref[...])

    @pl.when(part == 0)
    def _():
        _init_online(m_ref, acc_ref)

    wait(s, slot)
    k = buf[slot, 0:D_KV, :].astype(BF16)
    v_ones = jnp.concatenate([buf[slot, D_KV:, :].astype(BF16), jnp.ones((D_KV, n_keys), BF16)], axis=0)
    sc = _dot(q_all, k)
    back = (nsplit - part) * n_keys - lax.broadcasted_iota(jnp.int32, (1, n_keys), 1)
    masks = [_dot(sel_ref[0, g, 0].astype(BF16), expand_ref[...]) > 0.5 for g in range(N_KV)]
    _online_step(sc, back.astype(F32), masks, v_ones, m_ref, acc_ref, nq, feature_major=True)

    @pl.when(part == nsplit - 1)
    def _():
        nw = _pad_rows(nsel_ref[...], LANE)
        j_n = lax.broadcasted_iota(jnp.int32, (1, LANE), 1)
        mask_n = (qi >= j_n) & (j_n < nq)
        s_n = _dot_nt(q_all, nw[:, 0:D_KV].astype(BF16))
        vn_ones = jnp.concatenate([nw[:, D_KV:].astype(BF16), jnp.ones((LANE, D_KV), BF16)], axis=1)
        _online_step(s_n, (-j_n).astype(F32), [mask_n, mask_n], vn_ones, m_ref, acc_ref, nq)
        out_ref[...] = _finish_online(ocw_ref[...], gn_ref[...], acc_ref, nq)


def _e2_sample(page_table, pages, qp, sel, gn, ocw, kv_sel, oatt, t, nq):
    bsz, n_pages = page_table.shape
    ns = sel.shape[3]
    nsplit = 4
    pps = n_pages // nsplit
    n_keys = pps * PAGE_SIZE
    nblk = ns // nsplit
    off = t // nq
    sel_parts = sel.reshape(bsz, N_KV, nq, nsplit, nblk).transpose(0, 1, 3, 2, 4)
    expand = (jnp.arange(nblk)[:, None] == jnp.arange(n_keys)[None, :] // SEL_BLOCK).astype(BF16)
    row = lambda w: pl.BlockSpec((nq, w), lambda s, pt: (off + s // nsplit, 0))
    grid_spec = pltpu.PrefetchScalarGridSpec(
        num_scalar_prefetch=1,
        grid=(bsz * nsplit,),
        in_specs=[pl.BlockSpec(memory_space=pl.ANY), row(Q_PAD),
                  pl.BlockSpec((1, N_KV, 1, nq, nblk), lambda s, pt: (s // nsplit, 0, s % nsplit, 0, 0)), row(LANE),
                  pl.BlockSpec((nq, Q_PAD), lambda s, pt: (s // nsplit, 0)), row(2 * D_KV),
                  pl.BlockSpec((nblk, n_keys), lambda s, pt: (0, 0), pipeline_mode=pl.Buffered(1)),
                  pl.BlockSpec(memory_space=pl.ANY)],
        out_specs=row(Q_PAD),
        scratch_shapes=[pltpu.VMEM((2, 2 * D_KV, n_keys), F32), pltpu.SemaphoreType.DMA((2,)),
                        pltpu.VMEM((N_HEADS * nq, LANE), F32), pltpu.VMEM((N_HEADS * nq, 2 * LANE), F32)],
    )
    return pl.pallas_call(
        functools.partial(_e2_sample_kernel, pps=pps, nsplit=nsplit, nq=nq),
        out_shape=jax.ShapeDtypeStruct(oatt.shape, F32),
        grid_spec=grid_spec,
        input_output_aliases={8: 0},
        compiler_params=_cparams("arbitrary"),
    )(page_table, pages, qp, sel_parts, gn, ocw, kv_sel, expand, oatt)


def _post_kernel(h_ref, yg_ref, oatt_ref, ga_ref, gb_ref, wglu_ref, watt_ref, wout_ref, o_ref):
    gl = _dot(yg_ref[...], wglu_ref[...])
    br_a = gl[:, :D_MODEL] * jax.nn.sigmoid(gl[:, D_MODEL:])
    br_b = _dot(oatt_ref[...].astype(BF16), watt_ref[...])
    merged = (ga_ref[...] * br_a + gb_ref[...] * br_b).astype(BF16)
    o_ref[...] = h_ref[...] + _dot(merged, wout_ref[...])


def _post(h, yg, oatt, ga, gb, wglu, watt, wout):
    n = h.shape[0]
    tm = _pick_tile(n, 256)
    row = lambda w: pl.BlockSpec((tm, w), lambda i: (i, 0))
    return pl.pallas_call(
        _post_kernel,
        out_shape=jax.ShapeDtypeStruct((n, D_MODEL), F32),
        grid=(n // tm,),
        in_specs=[row(D_MODEL), row(D_SSM), row(Q_PAD), row(D_MODEL), row(D_MODEL),
                  _const_spec(wglu.shape), _const_spec(watt.shape), _const_spec(wout.shape)],
        out_specs=row(D_MODEL),
        compiler_params=_cparams("parallel"),
    )(h, yg, oatt, ga, gb, wglu, watt, wout)


def _head_pad_index():
    h = jnp.arange(N_HEADS)[:, None]
    d = jnp.arange(HEAD_DIM)[None, :]
    return (LANE * h + HEAD_DIM * (h // HPG) + d).reshape(-1)


def _prep_mix_weights(w_in, qk_norm):
    idx = _head_pad_index()
    wq = jnp.zeros((D_MODEL, Q_PAD), F32).at[:, idx].set(w_in[:, D_SSM:D_SSM + N_HEADS * HEAD_DIM])
    c0 = D_SSM + N_HEADS * HEAD_DIM
    c1 = c0 + 6 * D_KV
    c2 = c1 + 3 * N_HEADS
    wgn = jnp.zeros((D_MODEL, LANE), F32).at[:, :3 * N_HEADS].set(w_in[:, c1:c2])
    w = jnp.concatenate([w_in[:, :D_SSM], wq, w_in[:, c0:c1], wgn, w_in[:, c2:]], axis=1).astype(BF16)
    gq = jnp.zeros((Q_PAD,), F32).at[idx].set(jnp.tile(qk_norm[0] * (HEAD_DIM ** -0.5 * LOG2E), N_HEADS))[None]
    gks = jnp.tile(qk_norm[2], N_KV)[None]
    gkw = jnp.tile(qk_norm[3], N_KV)[None]
    return w, gq, gks, gkw


def _prep_att_out(w_att_out):
    return jnp.zeros((Q_PAD, D_MODEL), F32).at[_head_pad_index()].set(w_att_out).astype(BF16)


def _prep_compress(pe_k, pe_v, w_k, w_v, gain_k):
    blocks = jnp.stack([w_k, w_k, w_v, w_v], axis=1)
    w = jnp.einsum("scde,ch->scdhe", blocks, jnp.eye(2 * N_KV, dtype=F32))
    pe = jnp.concatenate([pe_k, pe_k, pe_v, pe_v], axis=1)
    r = jnp.arange(2 * PAGE_SIZE)
    perm = (r[None, :] == (CMP_STRIDE * (r % CMP_STRIDE) + r // CMP_STRIDE)[:, None]).astype(BF16)
    return {"w": w.reshape(CMP_BLOCK, 2 * D_KV, 2 * D_KV).astype(BF16), "pe": pe,
            "gk": jnp.tile(gain_k, N_KV)[None], "perm": perm}


def _prep_s5(a_re, a_im, log_dt, b_re, b_im, c_re, c_im, d, nq):
    dt = jnp.exp(log_dt)[:, None]
    mag = jnp.exp(a_re * dt)
    lr = mag * jnp.cos(a_im * dt)
    li = mag * jnp.sin(a_im * dt)
    den = a_re * a_re + a_im * a_im
    fr = ((lr - 1.0) * a_re + li * a_im) / den
    fi = (li * a_re - (lr - 1.0) * a_im) / den
    bbr = fr[..., None] * b_re - fi[..., None] * b_im
    bbi = fr[..., None] * b_im + fi[..., None] * b_re
    eye = jnp.eye(N_SSM_GROUPS, dtype=F32)
    blk_b = lambda m: jnp.einsum("gpc,gh->gchp", m, eye).reshape(D_SSM, N_STATE)
    blk_c = lambda m: jnp.einsum("gcp,gh->gphc", m, eye).reshape(N_STATE, D_SSM)

    def lam_pow(k):
        kk = k.astype(F32)[:, None, None]
        m = jnp.exp(a_re * dt * kk)
        th = a_im * dt * kk
        return (m * jnp.cos(th)).reshape(-1, N_STATE), (m * jnp.sin(th)).reshape(-1, N_STATE)

    def step_table(seg):
        ks = []
        dd = 1
        while dd < seg:
            ks.append(dd)
            dd *= 2
        re, im = lam_pow(jnp.array(ks))
        live = jnp.arange(seg)[None, :, None] >= jnp.array(ks)[:, None, None]
        tab = jnp.stack([jnp.where(live, re[:, None, :], 0.0), jnp.where(live, im[:, None, :], 0.0)], axis=1)
        return tab.reshape(-1, N_STATE)

    sp = {"bmat": jnp.concatenate([blk_b(bbr), blk_b(bbi)], axis=1).astype(BF16),
          "cr": blk_c(c_re).astype(BF16), "ci": (-blk_c(c_im)).astype(BF16), "d": d[None],
          "lam_p": step_table(SCAN_SEG), "lam_s": step_table(nq)}
    sp["pr_p"], sp["pi_p"] = lam_pow(jnp.arange(SCAN_SEG) + 1)
    return sp, lam_pow


def _band_matrix(nb, ns):
    ratio = SEL_BLOCK // CMP_STRIDE
    lo = CMP_BLOCK // CMP_STRIDE - 1
    c = jnp.arange(nb)[:, None]
    j = jnp.arange(ns)[None, :]
    return ((c >= ratio * j - lo) & (c <= ratio * j + ratio - 1)).astype(BF16)


def kernel(x_prompt, x_sample, cache_kv_cmp, cache_kv_sel, cache_kv_win, state_ssm_re, state_ssm_im, page_table, p_prompt, p_sample, norm_ffn1, w_ffn1_in, w_ffn1_out, norm_mix, w_in, qk_norm, ssm_a_re, ssm_a_im, ssm_log_dt, ssm_b_re, ssm_b_im, ssm_c_re, ssm_c_im, ssm_d, w_glu, cmp_pe_k, cmp_pe_v, cmp_w_k, cmp_w_v, w_att_out, w_out, norm_ffn2, w_ffn2_in, w_ffn2_out, norm_ple, w_ple_gate, w_ple_proj):
    bp, t = x_prompt.shape[:2]
    bsz, nq = x_sample.shape[:2]
    n_pages = page_table.shape[1]
    past = n_pages * PAGE_SIZE
    n_pool = cache_kv_cmp.shape[1]
    w_buf = cache_kv_win.shape[2]
    ns_rows = bsz * nq
    assert bp == 1 and t % SEL_CHUNK == 0 and t >= WIN_KEYS and t % ns_rows == 0 and nq < CMP_STRIDE
    assert past == t and w_buf == WINDOW and ns_rows % Q_TILE == 0
    nb = t // CMP_STRIDE
    ns = t // SEL_BLOCK
    nt = t // Q_TILE
    nchunks = t // SEL_CHUNK
    mband = _band_matrix(nb, ns)
    page_table = page_table.astype(jnp.int32)
    feat_major = lambda c: jnp.transpose(c, (0, 1, 3, 4, 5, 2)).reshape(DEPTH * c.shape[1], 2 * D_KV, c.shape[2])
    pages_cmp = feat_major(cache_kv_cmp)
    pages_sel = feat_major(cache_kv_sel)
    pages_win = feat_major(cache_kv_win)

    h = jnp.concatenate([x_prompt[0], x_sample.reshape(ns_rows, D_MODEL)], axis=0)
    st_p = [[] for _ in range(5)]
    st_s = [[] for _ in range(5)]
    for i in range(DEPTH):
        row1 = lambda a: a[i][None]
        h = _ffn(h, row1(norm_ffn1), w_ffn1_in[i].astype(BF16), w_ffn1_out[i].astype(BF16))
        w_mix, gq, gks, gkw = _prep_mix_weights(w_in[i], qk_norm[i])
        u, qp, kv_cmp, kv_sel, kv_win, kvs_bf, kvw_bf, gn, ga, gb = _mix_in(h, row1(norm_mix), w_mix, gq, gks, gkw)

        sp, lam_pow = _prep_s5(ssm_a_re[i], ssm_a_im[i], ssm_log_dt[i], ssm_b_re[i], ssm_b_im[i],
                               ssm_c_re[i], ssm_c_im[i], ssm_d[i], nq)
        sp["pr_s"], sp["pi_s"] = lam_pow(jnp.arange(ns_rows) % nq + 1)
        yg, hr_p, hi_p = _s5_prompt(u, t, sp)
        h0r = jnp.repeat(state_ssm_re[i].reshape(bsz, N_STATE), nq, axis=0)
        h0i = jnp.repeat(state_ssm_im[i].reshape(bsz, N_STATE), nq, axis=0)
        yg, hr_s, hi_s = _s5_sample(u, yg, t, ns_rows, nq, h0r, h0i, sp)

        cw = _prep_compress(cmp_pe_k[i], cmp_pe_v[i], cmp_w_k[i], cmp_w_v[i], qk_norm[i, 1])
        layer_pages = page_table + i * n_pool
        kcc_p, vcc_p = _compress_rows(kv_cmp, t, cw)
        kcc_s, vcc_s = _compress_paged(layer_pages, pages_cmp, cw)
        ocw_p, sel_p, flags = _e1_prompt(qp, gn, kcc_p, vcc_p, mband, kvw_bf, t)
        ocw_s, sel_s = _e1_sample(qp, gn, kcc_s, vcc_s, mband, pages_win, i * bsz, kv_win, t, bsz, nq, past)
        idle = flags[:, 0, :nchunks] == 0
        order = jnp.argsort(idle, axis=1, stable=True).astype(jnp.int32).reshape(-1)
        cnt = (nchunks - jnp.sum(idle, axis=1)).astype(jnp.int32)
        oatt = _e2_prompt(order, cnt, qp, sel_p, gn, ocw_p, kvs_bf, t)
        oatt = _e2_sample(layer_pages, pages_sel, qp, sel_s, gn, ocw_s, kv_sel, oatt, t, nq)

        h = _post(h, yg, oatt, ga, gb,
                  w_glu[i].astype(BF16), _prep_att_out(w_att_out[i]), w_out[i].astype(BF16))
        p_all = jnp.concatenate([p_prompt[i, 0], p_sample[i].reshape(ns_rows, D_PLE)], axis=0)
        h = _ffn(h, row1(norm_ffn2), w_ffn2_in[i].astype(BF16), w_ffn2_out[i].astype(BF16),
                 (p_all, row1(norm_ple), w_ple_gate[i].astype(BF16), w_ple_proj[i].astype(BF16)))

        kv5 = lambda a, lead: a.reshape(lead + (2, N_KV, HEAD_DIM))
        st_p[0].append(kv5(kv_cmp[:t], (1, t)))
        st_p[1].append(kv5(kv_sel[:t], (1, t)))
        st_p[2].append(kv5(kv_win[t - min(WINDOW, t):t], (1, min(WINDOW, t))))
        st_p[3].append(hr_p.reshape(1, N_SSM_GROUPS, P_STATE))
        st_p[4].append(hi_p.reshape(1, N_SSM_GROUPS, P_STATE))
        st_s[0].append(kv5(kv_cmp[t:], (bsz, nq)))
        st_s[1].append(kv5(kv_sel[t:], (bsz, nq)))
        st_s[2].append(jnp.concatenate([cache_kv_win[i, :, nq:], kv5(kv_win[t:], (bsz, nq))], axis=1))
        st_s[3].append(hr_s[nq - 1::nq].reshape(bsz, N_SSM_GROUPS, P_STATE))
        st_s[4].append(hi_s[nq - 1::nq].reshape(bsz, N_SSM_GROUPS, P_STATE))

    outs_p = [jnp.stack(a) for a in st_p]
    outs_s = [jnp.stack(a) for a in st_s]
    y_prompt = h[:t][None]
    y_sample = h[t:].reshape(bsz, nq, D_MODEL)
    return (y_prompt, y_sample, *outs_p, *outs_s)
```

```python
import functools

import jax
import jax.numpy as jnp
from jax import lax
from jax.experimental import pallas as pl
from jax.experimental.pallas import tpu as pltpu

F32 = jnp.float32
BF16 = jnp.bfloat16

D_MODEL = 1024
DEPTH = 2
D_SSM = 512
SSM_GROUP = 16
N_SSM_GROUPS = 32
P_STATE = 64
N_STATE = N_SSM_GROUPS * P_STATE
N_HEADS = 8
HEAD_DIM = 64
N_KV = 2
HPG = 4
D_KV = 128
CMP_BLOCK = 32
CMP_STRIDE = 16
SEL_BLOCK = 64
TOP_N = 16
WINDOW = 512
PAGE_SIZE = 128
SEL_BIG = 1e4
D_FF = 2816
D_PLE = 256
RMS_EPS = 1e-6
NEG = -1e30
HALF_NEG = -0.5e30
LOG2E = 1.4426950408889634

LANE = 128
Q_TILE = 128
E1_TILES = 1
SCAN_SEG = 8
SEL_CHUNK = 256
WIN_KEYS = WINDOW + Q_TILE
Q_PAD = N_HEADS * LANE
VMEM_LIMIT = 56 * 2 ** 20


def _cparams(*sem):
    return pltpu.CompilerParams(dimension_semantics=sem, vmem_limit_bytes=VMEM_LIMIT)


def _dot(a, b):
    return jnp.dot(a, b, preferred_element_type=F32)


def _dot_nt(a, b):
    return lax.dot_general(a, b, (((1,), (1,)), ((), ())), preferred_element_type=F32)


def _pick_tile(n, target):
    for t in range(min(n, target), 15, -1):
        if n % t == 0 and t % 16 == 0:
            return t
    raise ValueError(f"no row tile for {n}")


def _const_spec(shape):
    nd = len(shape)
    return pl.BlockSpec(shape, lambda *_: (0,) * nd, pipeline_mode=pl.Buffered(1))


def _rms(x, g):
    ms = jnp.mean(x * x, axis=-1, keepdims=True)
    return x * lax.rsqrt(ms + RMS_EPS) * g


def _halfnorm(x, gain):
    lo = lax.broadcasted_iota(jnp.int32, (1, LANE), 1) < HEAD_DIM
    x2 = x * x
    s_lo = jnp.sum(jnp.where(lo, x2, 0.0), axis=-1, keepdims=True)
    s_hi = jnp.sum(jnp.where(lo, 0.0, x2), axis=-1, keepdims=True)
    ms = jnp.where(lo, s_lo, s_hi) * (1.0 / HEAD_DIM)
    return x * lax.rsqrt(ms + RMS_EPS) * gain


def _ffn_kernel(*refs, ple):
    if ple:
        h_ref, g_ref, wi_ref, wo_ref, p_ref, gp_ref, wg_ref, wp_ref, o_ref = refs
    else:
        h_ref, g_ref, wi_ref, wo_ref, o_ref = refs
    h = h_ref[...]
    xn = _rms(h, g_ref[...]).astype(BF16)
    a = _dot(xn, wi_ref[:, :D_FF])
    b = _dot(xn, wi_ref[:, D_FF:])
    act = (a * jax.nn.sigmoid(a) * b).astype(BF16)
    h = h + 0.5 * _dot(act, wo_ref[...])
    if ple:
        xg = _rms(h, gp_ref[...]).astype(BF16)
        gate = jax.nn.sigmoid(_dot(xg, wg_ref[...]))
        h = h + gate * _dot(p_ref[...].astype(BF16), wp_ref[...])
    o_ref[...] = h


def _ffn(h, g, wi, wo, ple_args=None):
    n = h.shape[0]
    tm = _pick_tile(n, 256)
    row = lambda w: pl.BlockSpec((tm, w), lambda i: (i, 0))
    in_specs = [row(D_MODEL), _const_spec((1, D_MODEL)), _const_spec(wi.shape), _const_spec(wo.shape)]
    args = [h, g, wi, wo]
    if ple_args is not None:
        p, gp, wg, wp = ple_args
        in_specs += [row(D_PLE), _const_spec((1, D_MODEL)), _const_spec(wg.shape), _const_spec(wp.shape)]
        args += [p, gp, wg, wp]
    return pl.pallas_call(
        functools.partial(_ffn_kernel, ple=ple_args is not None),
        out_shape=jax.ShapeDtypeStruct((n, D_MODEL), F32),
        grid=(n // tm,),
        in_specs=in_specs,
        out_specs=row(D_MODEL),
        compiler_params=_cparams("parallel"),
    )(*args)


_C_U = 0
_C_Q = _C_U + D_SSM
_C_KV = _C_Q + Q_PAD
_C_GN = _C_KV + 6 * D_KV
_C_GA = _C_GN + LANE
_C_GB = _C_GA + D_MODEL
_C_END = _C_GB + D_MODEL


def _mix_in_kernel(h_ref, g_ref, w_ref, gq_ref, gks_ref, gkw_ref,
                   u_ref, qp_ref, kvc_ref, kvs_ref, kvw_ref, kvsb_ref, kvwb_ref, gn_ref, ga_ref, gb_ref):
    xn = _rms(h_ref[...], g_ref[...]).astype(BF16)
    u_ref[...] = _dot(xn, w_ref[:, _C_U:_C_Q])
    zq = _dot(xn, w_ref[:, _C_Q:_C_KV])
    for h in range(N_HEADS):
        qh = zq[:, h * LANE:(h + 1) * LANE]
        ms = jnp.sum(qh * qh, axis=-1, keepdims=True) * (1.0 / HEAD_DIM)
        qn = qh * lax.rsqrt(ms + RMS_EPS) * gq_ref[:, h * LANE:(h + 1) * LANE]
        qp_ref[:, h * LANE:(h + 1) * LANE] = qn
    zkv = _dot(xn, w_ref[:, _C_KV:_C_GN])
    kvc_ref[...] = zkv[:, 0:2 * D_KV]
    ks = _halfnorm(zkv[:, 2 * D_KV:3 * D_KV], gks_ref[...])
    vs = zkv[:, 3 * D_KV:4 * D_KV]
    kw = _halfnorm(zkv[:, 4 * D_KV:5 * D_KV], gkw_ref[...])
    vw = zkv[:, 5 * D_KV:6 * D_KV]
    kvs_ref[:, 0:D_KV] = ks
    kvs_ref[:, D_KV:] = vs
    kvw_ref[:, 0:D_KV] = kw
    kvw_ref[:, D_KV:] = vw
    kvsb_ref[:, 0:D_KV] = ks.astype(BF16)
    kvsb_ref[:, D_KV:2 * D_KV] = vs.astype(BF16)
    kvsb_ref[:, 2 * D_KV:] = jnp.ones((ks.shape[0], D_KV), BF16)
    kvwb_ref[:, 0:D_KV] = kw.astype(BF16)
    kvwb_ref[:, D_KV:] = vw.astype(BF16)
    gn_ref[...] = jax.nn.sigmoid(_dot(xn, w_ref[:, _C_GN:_C_GA]))
    ga_ref[...] = jax.nn.sigmoid(_dot(xn, w_ref[:, _C_GA:_C_GB]))
    gb_ref[...] = jax.nn.sigmoid(_dot(xn, w_ref[:, _C_GB:_C_END]))


def _mix_in(h, g, w, gq, gks, gkw):
    n = h.shape[0]
    tm = _pick_tile(n, 256)
    row = lambda width: pl.BlockSpec((tm, width), lambda i: (i, 0))
    widths = [(D_SSM, F32), (Q_PAD, F32), (2 * D_KV, F32), (2 * D_KV, F32), (2 * D_KV, F32),
              (3 * D_KV, BF16), (2 * D_KV, BF16), (LANE, F32), (D_MODEL, F32), (D_MODEL, F32)]
    return pl.pallas_call(
        _mix_in_kernel,
        out_shape=[jax.ShapeDtypeStruct((n, wd), dt) for wd, dt in widths],
        grid=(n // tm,),
        in_specs=[row(D_MODEL), _const_spec((1, D_MODEL)), _const_spec(w.shape),
                  _const_spec((1, Q_PAD)), _const_spec((1, LANE)), _const_spec((1, LANE))],
        out_specs=[row(wd) for wd, _ in widths],
        compiler_params=_cparams("parallel"),
    )(h, g, w, gq, gks, gkw)


def _gelu_tanh(x):
    return 0.5 * x * (1.0 + jnp.tanh(0.7978845608028654 * (x + 0.044715 * (x * x * x))))


def _s5_kernel(*refs, seg, carry, seq_steps=None):
    if carry:
        (u_ref, bm_ref, cr_ref, ci_ref, d_ref, lam_ref, pr_ref, pi_ref,
         y_ref, hr_out, hi_out, car_ref, cai_ref) = refs
    else:
        (u_ref, bm_ref, cr_ref, ci_ref, d_ref, lam_ref, pr_ref, pi_ref, h0r_ref, h0i_ref, y_all_ref,
         y_ref, hr_out, hi_out) = refs
    u = u_ref[...]
    rows = u.shape[0]
    x = _dot(u.astype(BF16), bm_ref[...])
    xr = x[:, :N_STATE]
    xi = x[:, N_STATE:]
    d, k = 1, 0
    while d < seg:
        lr = jnp.tile(lam_ref[2 * k * seg:(2 * k + 1) * seg, :], (rows // seg, 1))
        li = jnp.tile(lam_ref[(2 * k + 1) * seg:(2 * k + 2) * seg, :], (rows // seg, 1))
        sr = pltpu.roll(xr, d, 0)
        si = pltpu.roll(xi, d, 0)
        xr, xi = xr + lr * sr - li * si, xi + lr * si + li * sr
        d *= 2
        k += 1
    p_r = pr_ref[...]
    p_i = pi_ref[...]
    if carry:
        @pl.when(pl.program_id(0) == 0)
        def _():
            car_ref[...] = jnp.zeros_like(car_ref)
            cai_ref[...] = jnp.zeros_like(cai_ref)
        c_r = car_ref[...]
        c_i = cai_ref[...]
        hr_parts, hi_parts = [], []
        for j in range(rows // seg):
            tr = xr[j * seg:(j + 1) * seg]
            ti = xi[j * seg:(j + 1) * seg]
            hr_j = tr + p_r * c_r - p_i * c_i
            hi_j = ti + p_r * c_i + p_i * c_r
            c_r = hr_j[seg - 1:seg, :]
            c_i = hi_j[seg - 1:seg, :]
            hr_parts.append(hr_j)
            hi_parts.append(hi_j)
        hr = jnp.concatenate(hr_parts, axis=0)
        hi = jnp.concatenate(hi_parts, axis=0)
    else:
        c_r = h0r_ref[...]
        c_i = h0i_ref[...]
        hr = xr + p_r * c_r - p_i * c_i
        hi = xi + p_r * c_i + p_i * c_r
    y = _dot(hr.astype(BF16), cr_ref[...]) + _dot(hi.astype(BF16), ci_ref[...]) + d_ref[...] * u
    y_ref[...] = _gelu_tanh(y).astype(BF16)
    if carry:
        car_ref[...] = hr[rows - 1:rows, :]
        cai_ref[...] = hi[rows - 1:rows, :]

        @pl.when(pl.program_id(0) < seq_steps)
        def _():
            hr_out[...] = hr[rows - 1:rows, :]
            hi_out[...] = hi[rows - 1:rows, :]
    else:
        hr_out[...] = hr
        hi_out[...] = hi


def _s5_prompt(u, t, sp):
    rows = Q_TILE
    consts = [sp["bmat"], sp["cr"], sp["ci"], sp["d"], sp["lam_p"], sp["pr_p"], sp["pi_p"]]
    return pl.pallas_call(
        functools.partial(_s5_kernel, seg=SCAN_SEG, carry=True, seq_steps=t // rows),
        out_shape=[jax.ShapeDtypeStruct((u.shape[0], D_SSM), BF16),
                   jax.ShapeDtypeStruct((1, N_STATE), F32), jax.ShapeDtypeStruct((1, N_STATE), F32)],
        grid=(u.shape[0] // rows,),
        in_specs=[pl.BlockSpec((rows, D_SSM), lambda i: (i, 0))] + [_const_spec(c.shape) for c in consts],
        out_specs=[pl.BlockSpec((rows, D_SSM), lambda i: (i, 0)),
                   _const_spec((1, N_STATE)), _const_spec((1, N_STATE))],
        scratch_shapes=[pltpu.VMEM((1, N_STATE), F32), pltpu.VMEM((1, N_STATE), F32)],
        compiler_params=_cparams("arbitrary"),
    )(u, *consts)


def _s5_sample(u, yg, t, ns_rows, nq, h0r, h0i, sp):
    consts = [sp["bmat"], sp["cr"], sp["ci"], sp["d"], sp["lam_s"], sp["pr_s"], sp["pi_s"], h0r, h0i]
    blk = t // ns_rows
    return pl.pallas_call(
        functools.partial(_s5_kernel, seg=nq, carry=False),
        out_shape=[jax.ShapeDtypeStruct(yg.shape, BF16),
                   jax.ShapeDtypeStruct((ns_rows, N_STATE), F32), jax.ShapeDtypeStruct((ns_rows, N_STATE), F32)],
        grid=(1,),
        in_specs=([pl.BlockSpec((ns_rows, D_SSM), lambda i: (blk, 0))] + [_const_spec(c.shape) for c in consts]
                  + [pl.BlockSpec(memory_space=pl.ANY)]),
        out_specs=[pl.BlockSpec((ns_rows, D_SSM), lambda i: (blk, 0)),
                   _const_spec((ns_rows, N_STATE)), _const_spec((ns_rows, N_STATE))],
        input_output_aliases={len(consts) + 1: 0},
        compiler_params=_cparams("arbitrary"),
    )(u, *consts, yg)


N_SPAIR = CMP_STRIDE // 2


def _compress_part(get_x2, w_ref, a0, a1, part, rows):
    base = pl.multiple_of(part * rows, rows)
    for kv in range(2):
        acc = None
        for sp in range(N_SPAIR):
            term = _dot(get_x2(kv, sp), w_ref[kv, sp])
            acc = term if acc is None else acc + term
        a0[pl.ds(base, rows), kv * D_KV:(kv + 1) * D_KV] = acc[:, 0:D_KV]
        a1[pl.ds(base, rows), kv * D_KV:(kv + 1) * D_KV] = acc[:, D_KV:]


def _compress_finish(a0, a1, w_ref, pe_ref, gk_ref, kcc_ref, vcc_ref, nb):
    bias = []
    for j in range(CMP_BLOCK // CMP_STRIDE):
        halves = []
        for kv in range(2):
            b = jnp.zeros((8, 2 * D_KV), F32)
            for sp in range(N_SPAIR):
                r = (j * 2 + kv) * N_SPAIR + sp
                b = b + _dot(jnp.broadcast_to(pe_ref[r:r + 1, :], (8, 2 * D_KV)).astype(BF16), w_ref[kv, sp])
            halves.append(b[0:1, j * D_KV:(j + 1) * D_KV])
        bias.append(jnp.concatenate(halves, axis=1))
    out = (a0[...] + bias[0]) + pltpu.roll(a1[...] + bias[1], nb - 1, 0)
    valid = lax.broadcasted_iota(jnp.int32, (nb, 1), 0) < nb - 1
    out = jnp.where(valid, out, 0.0)
    kcc_ref[0] = _halfnorm(out[:, 0:D_KV], gk_ref[...]).astype(BF16)
    vcc_ref[0] = out[:, D_KV:].astype(BF16)


def _compress_rows_kernel(xk_ref, xv_ref, w_ref, pe_ref, gk_ref, kcc_ref, vcc_ref, a0, a1, *, nsplit, nb):
    part = pl.program_id(0)
    rows = nb // nsplit

    def get_x2(kv, sp):
        x_ref = (xk_ref, xv_ref)[kv]
        pair = [x_ref[pl.ds(2 * sp + si, rows, stride=CMP_STRIDE), :] for si in range(2)]
        return jnp.concatenate(pair, axis=1).astype(BF16)
    _compress_part(get_x2, w_ref, a0, a1, part, rows)

    @pl.when(part == nsplit - 1)
    def _():
        _compress_finish(a0, a1, w_ref, pe_ref, gk_ref, kcc_ref, vcc_ref, nb)


def _compress_rows(kv_cmp, t, cw):
    nb = t // CMP_STRIDE
    nsplit = 2
    return pl.pallas_call(
        functools.partial(_compress_rows_kernel, nsplit=nsplit, nb=nb),
        out_shape=[jax.ShapeDtypeStruct((1, nb, D_KV), BF16), jax.ShapeDtypeStruct((1, nb, D_KV), BF16)],
        grid=(nsplit,),
        in_specs=[pl.BlockSpec((t // nsplit, D_KV), lambda i: (i, 0)),
                  pl.BlockSpec((t // nsplit, D_KV), lambda i: (i, 1)),
                  _const_spec(cw["w"].shape), _const_spec(cw["pe"].shape), _const_spec((1, LANE))],
        out_specs=[_const_spec((1, nb, D_KV)), _const_spec((1, nb, D_KV))],
        scratch_shapes=[pltpu.VMEM((nb, 2 * D_KV), F32), pltpu.VMEM((nb, 2 * D_KV), F32)],
        compiler_params=_cparams("arbitrary"),
    )(kv_cmp, kv_cmp, cw["w"], cw["pe"], cw["gk"])


def _compress_kernel(pt_ref, pages_ref, w_ref, pe_ref, gk_ref, perm_ref,
                     kcc_ref, vcc_ref, buf, sem, x_scr, a0, a1, *, pps, nsplit, nb):
    s = pl.program_id(0)
    nsteps = pl.num_programs(0)

    def page_copy(step, p, slot):
        b = step // nsplit
        part = step % nsplit
        pg = pt_ref[b, part * pps + p]
        col = pl.multiple_of((p % 2) * PAGE_SIZE, PAGE_SIZE)
        return pltpu.make_async_copy(pages_ref.at[pg], buf.at[slot, p // 2, :, pl.ds(col, PAGE_SIZE)],
                                     sem.at[slot])

    def start(step, slot):
        def body(p, c):
            page_copy(step, p, slot).start()
            return c
        lax.fori_loop(0, pps, body, 0)

    def wait(step, slot):
        def body(p, c):
            page_copy(step, p, slot).wait()
            return c
        lax.fori_loop(0, pps, body, 0)

    @pl.when(s == 0)
    def _():
        start(s, 0)

    @pl.when(s + 1 < nsteps)
    def _():
        start(s + 1, (s + 1) % 2)

    slot = s % 2
    wait(s, slot)
    part = s % nsplit

    def to_rows(q, c):
        xt = buf[slot, q].astype(BF16)
        xp = _dot_nt(perm_ref[...], xt).astype(BF16)
        base = pl.multiple_of(q * CMP_STRIDE, CMP_STRIDE)
        for s2 in range(CMP_STRIDE):
            for kv in range(2):
                x_scr[kv, s2 // 2, pl.ds(base, CMP_STRIDE), (s2 % 2) * D_KV:(s2 % 2 + 1) * D_KV] = (
                    xp[s2 * CMP_STRIDE:(s2 + 1) * CMP_STRIDE, kv * D_KV:(kv + 1) * D_KV])
        return c
    lax.fori_loop(0, pps // 2, to_rows, 0, unroll=4)
    _compress_part(lambda kv, sp: x_scr[kv, sp], w_ref, a0, a1, part, nb // nsplit)

    @pl.when(part == nsplit - 1)
    def _():
        _compress_finish(a0, a1, w_ref, pe_ref, gk_ref, kcc_ref, vcc_ref, nb)


def _compress_paged(page_table, pages, cw):
    bsz, n_pages = page_table.shape
    nb = n_pages * (PAGE_SIZE // CMP_STRIDE)
    nsplit = 2
    pps = n_pages // nsplit
    const = lambda shape: pl.BlockSpec(shape, lambda s, pt: (0,) * len(shape), pipeline_mode=pl.Buffered(1))
    grid_spec = pltpu.PrefetchScalarGridSpec(
        num_scalar_prefetch=1,
        grid=(bsz * nsplit,),
        in_specs=[pl.BlockSpec(memory_space=pl.ANY), const(cw["w"].shape), const(cw["pe"].shape),
                  const((1, LANE)), const(cw["perm"].shape)],
        out_specs=[pl.BlockSpec((1, nb, D_KV), lambda s, pt: (s // nsplit, 0, 0)),
                   pl.BlockSpec((1, nb, D_KV), lambda s, pt: (s // nsplit, 0, 0))],
        scratch_shapes=[pltpu.VMEM((2, pps // 2, 2 * D_KV, 2 * PAGE_SIZE), F32), pltpu.SemaphoreType.DMA((2,)),
                        pltpu.VMEM((2, N_SPAIR, nb // nsplit, 2 * D_KV), BF16),
                        pltpu.VMEM((nb, 2 * D_KV), F32), pltpu.VMEM((nb, 2 * D_KV), F32)],
    )
    return pl.pallas_call(
        functools.partial(_compress_kernel, pps=pps, nsplit=nsplit, nb=nb),
        out_shape=[jax.ShapeDtypeStruct((bsz, nb, D_KV), BF16), jax.ShapeDtypeStruct((bsz, nb, D_KV), BF16)],
        grid_spec=grid_spec,
        compiler_params=_cparams("arbitrary"),
    )(page_table, pages, cw["w"], cw["pe"], cw["gk"], cw["perm"])


def _heads_rows(qp):
    return jnp.concatenate([qp[:, h * LANE:(h + 1) * LANE] for h in range(N_HEADS)], axis=0).astype(BF16)


def _slope2(h):
    return LOG2E * 2.0 ** (-8.0 * (h + 1) / N_HEADS)


def _branch(q_all, tq, pieces):
    s_all = [_dot(q_all, k) if fm else _dot_nt(q_all, k) for k, _, _, _, fm in pieces]
    es, invs = [], []
    for h in range(N_HEADS):
        rows = slice(h * tq, (h + 1) * tq)
        sm = [jnp.where(pc[3], s[rows] - _slope2(h) * pc[2], NEG) for s, pc in zip(s_all, pieces)]
        m = functools.reduce(jnp.maximum, [jnp.max(x, axis=-1, keepdims=True) for x in sm])
        m = jnp.maximum(m, HALF_NEG)
        e = [jnp.exp2(x - m) for x in sm]
        l = functools.reduce(lambda a, b: a + b, [jnp.sum(x, axis=-1, keepdims=True) for x in e])
        es.append(e)
        invs.append(1.0 / jnp.maximum(l, 1e-30))
    o = None
    for i, (_, v, _, _, fm) in enumerate(pieces):
        p = jnp.concatenate([es[h][i] for h in range(N_HEADS)], axis=0).astype(BF16)
        t = _dot_nt(p, v) if fm else _dot(p, v)
        o = t if o is None else o + t
    return o * jnp.concatenate(invs, axis=0), es, invs


def _cmp_win_topk(q_all, tq, qpos, gn, kcc, vcc, mband, win_pieces, nb, ns, topn):
    n_io = lax.broadcasted_iota(jnp.int32, (1, nb), 1)
    kc_end = n_io * CMP_STRIDE + (CMP_BLOCK - 1)
    mask_c = (qpos >= kc_end) & (n_io < nb - 1)
    kdist_c = (qpos[0:1, :] - kc_end).astype(F32)
    o_c, es, invs = _branch(q_all, tq, [(kcc, vcc, kdist_c, mask_c, False)])
    o_w, _, _ = _branch(q_all, tq, win_pieces)
    imps = []
    for g in range(N_KV):
        psum = es[HPG * g][0] * invs[HPG * g]
        for hh in range(1, HPG):
            psum = psum + es[HPG * g + hh][0] * invs[HPG * g + hh]
        hi = psum.astype(BF16)
        r = psum - hi.astype(F32)
        mid = r.astype(BF16)
        lo = (r - mid.astype(F32)).astype(BF16)
        imps.append(_dot(hi, mband) + _dot(mid, mband) + _dot(lo, mband))
    imp = jnp.concatenate(imps, axis=0)
    blk = lax.broadcasted_iota(jnp.int32, (1, ns), 1)
    cur = jnp.concatenate([jnp.right_shift(qpos, 6)] * N_KV, axis=0)
    forced = (blk == 0) | (blk == cur) | (blk == cur - 1)
    score = jnp.where(forced, imp + SEL_BIG, jnp.where(blk <= cur, imp, -SEL_BIG))
    transposed = score.shape[0] % LANE == 0
    if transposed:
        score = score.T
    axis = 0 if transposed else 1
    blkf = lax.broadcasted_iota(jnp.int32, (ns, 1) if transposed else (1, ns), axis).astype(F32)
    for _ in range(topn):
        m = jnp.max(score, axis=axis, keepdims=True)
        first = jnp.min(jnp.where(score == m, blkf, float(ns)), axis=axis, keepdims=True)
        score = jnp.where(blkf == first, -jnp.inf, score)
    sel = jnp.where(score == -jnp.inf, 1.0, 0.0)
    if transposed:
        sel = sel.T
    sel = jnp.where(blk <= cur, sel, 0.0)
    outs = []
    for h in range(N_HEADS):
        rows = slice(h * tq, (h + 1) * tq)
        outs.append(gn[:, h:h + 1] * o_c[rows] + gn[:, 2 * N_HEADS + h:2 * N_HEADS + h + 1] * o_w[rows])
    return jnp.concatenate(outs, axis=1), sel


def _e1_prompt_kernel(qp_ref, gn_ref, kcc_ref, vcc_ref, mband_ref, kvw_ref, ocw_ref, sel_ref, flag_ref,
                      *, nb, ns):
    tq = Q_TILE
    for half in range(E1_TILES):
        rows = slice(half * tq, (half + 1) * tq)
        s0 = (pl.program_id(0) * E1_TILES + half) * tq
        qpos = s0 + lax.broadcasted_iota(jnp.int32, (tq, 1), 0)
        q_all = _heads_rows(qp_ref[rows, :])
        start = pl.multiple_of(jnp.maximum(s0 - WINDOW, 0), Q_TILE)
        kw = kvw_ref[pl.ds(start, WIN_KEYS), 0:D_KV]
        vw = kvw_ref[pl.ds(start, WIN_KEYS), D_KV:]
        kpos = start + lax.broadcasted_iota(jnp.int32, (1, WIN_KEYS), 1)
        dist_w = qpos - kpos
        mask_w = (dist_w >= 0) & (dist_w < WINDOW)
        win_pieces = [(kw, vw, (s0 - kpos).astype(F32), mask_w, False)]
        ocw, sel = _cmp_win_topk(q_all, tq, qpos, gn_ref[rows, :], kcc_ref[0], vcc_ref[0], mband_ref[...],
                                 win_pieces, nb, ns, TOP_N)
        ocw_ref[rows, :] = ocw
        sel_ref[0, rows, :] = sel[:tq]
        sel_ref[1, rows, :] = sel[tq:]
        colany = jnp.max(sel, axis=0, keepdims=True)
        j_io = lax.broadcasted_iota(jnp.int32, (ns, LANE), 0)
        c_io = lax.broadcasted_iota(jnp.int32, (ns, LANE), 1)
        grp = jnp.where(jnp.right_shift(j_io, 2) == c_io, 1.0, 0.0).astype(BF16)
        cnt = _dot(jnp.broadcast_to(colany, (8, ns)).astype(BF16), grp)
        flag_ref[half] = (cnt > 0.5).astype(jnp.int32)


def _e1_prompt(qp, gn, kcc, vcc, mband, kvw_bf, t):
    nb, ns = mband.shape
    nt = t // Q_TILE
    step_rows = E1_TILES * Q_TILE
    row = lambda w: pl.BlockSpec((step_rows, w), lambda i: (i, 0))
    return pl.pallas_call(
        functools.partial(_e1_prompt_kernel, nb=nb, ns=ns),
        out_shape=[jax.ShapeDtypeStruct((t, Q_PAD), F32), jax.ShapeDtypeStruct((N_KV, t, ns), F32),
                   jax.ShapeDtypeStruct((nt, 8, LANE), jnp.int32)],
        grid=(nt // E1_TILES,),
        in_specs=[row(Q_PAD), row(LANE), _const_spec((1, nb, D_KV)), _const_spec((1, nb, D_KV)),
                  _const_spec(mband.shape), _const_spec(kvw_bf.shape)],
        out_specs=[row(Q_PAD), pl.BlockSpec((N_KV, step_rows, ns), lambda i: (0, i, 0)),
                   pl.BlockSpec((E1_TILES, 8, LANE), lambda i: (i, 0, 0))],
        compiler_params=_cparams("parallel"),
    )(qp, gn, kcc, vcc, mband, kvw_bf)


def _pad_rows(x, rows):
    return jnp.concatenate([x, jnp.zeros((rows - x.shape[0], x.shape[1]), x.dtype)], axis=0)


def _e1_sample_kernel(qp_ref, gn_ref, kcc_ref, vcc_ref, mband_ref, cwin_ref, nwin_ref, ocw_ref, sel_ref,
                      *, nb, ns, nq, past):
    qi = lax.broadcasted_iota(jnp.int32, (nq, 1), 0)
    qpos = past + qi
    q_all = _heads_rows(qp_ref[...])
    cw = cwin_ref[0]
    w_buf = cw.shape[1]
    j_c = lax.broadcasted_iota(jnp.int32, (1, w_buf), 1)
    dist_cw = qi + (w_buf - j_c)
    mask_cw = (dist_cw >= 0) & (dist_cw < WINDOW)
    nw = _pad_rows(nwin_ref[...], LANE)
    j_n = lax.broadcasted_iota(jnp.int32, (1, LANE), 1)
    mask_nw = (qi >= j_n) & (j_n < nq)
    pieces = [(cw[0:D_KV, :].astype(BF16), cw[D_KV:, :].astype(BF16), (w_buf - j_c).astype(F32), mask_cw, True),
              (nw[:, 0:D_KV].astype(BF16), nw[:, D_KV:].astype(BF16), (-j_n).astype(F32), mask_nw, False)]
    ocw, sel = _cmp_win_topk(q_all, nq, qpos, gn_ref[...], kcc_ref[0], vcc_ref[0], mband_ref[...],
                             pieces, nb, ns, TOP_N - 1)
    ocw_ref[...] = ocw
    sel_ref[0, 0] = sel[:nq]
    sel_ref[0, 1] = sel[nq:]


def _e1_sample(qp, gn, kcc, vcc, mband, cache_win, win_off, kv_win, t, bsz, nq, past):
    nb, ns = mband.shape
    w_buf = cache_win.shape[2]
    off = t // nq
    row = lambda w: pl.BlockSpec((nq, w), lambda b: (off + b, 0))
    per_b = lambda shape: pl.BlockSpec((1,) + shape, lambda b: (b,) + (0,) * len(shape))
    return pl.pallas_call(
        functools.partial(_e1_sample_kernel, nb=nb, ns=ns, nq=nq, past=past),
        out_shape=[jax.ShapeDtypeStruct((bsz * nq, Q_PAD), F32), jax.ShapeDtypeStruct((bsz, N_KV, nq, ns), F32)],
        grid=(bsz,),
        in_specs=[row(Q_PAD), row(LANE), per_b((nb, D_KV)), per_b((nb, D_KV)), _const_spec(mband.shape),
                  pl.BlockSpec((1, 2 * D_KV, w_buf), lambda b: (win_off + b, 0, 0)), row(2 * D_KV)],
        out_specs=[pl.BlockSpec((nq, Q_PAD), lambda b: (b, 0)), per_b((N_KV, nq, ns))],
        compiler_params=_cparams("parallel"),
    )(qp, gn, kcc, vcc, mband, cache_win, kv_win)


def _online_step(s, kdist, masks, v_ones, m_ref, acc_ref, tq, feature_major=False):
    reps = s.shape[1] // LANE
    ps, alphas = [], []
    for h in range(N_HEADS):
        rows = slice(h * tq, (h + 1) * tq)
        sm = jnp.where(masks[h // HPG], s[rows] - _slope2(h) * kdist, NEG)
        m_old = m_ref[rows, :]
        m_new = jnp.maximum(m_old, jnp.max(sm, axis=-1, keepdims=True))
        alphas.append(jnp.exp2(m_old - m_new))
        ps.append(jnp.exp2(sm - jnp.tile(m_new, (1, reps))))
        m_ref[rows, :] = m_new
    p_all = jnp.concatenate(ps, axis=0).astype(BF16)
    a_all = jnp.concatenate(alphas, axis=0)
    pv = _dot_nt(p_all, v_ones) if feature_major else _dot(p_all, v_ones)
    acc_ref[...] = jnp.tile(a_all, (1, 2)) * acc_ref[...] + pv


def _expand_sel(sel, first_blk, n_keys):
    ns = sel.shape[1]
    j_io = lax.broadcasted_iota(jnp.int32, (ns, n_keys), 0)
    kb = first_blk + jnp.right_shift(lax.broadcasted_iota(jnp.int32, (ns, n_keys), 1), 6)
    e = jnp.where(j_io == kb, 1.0, 0.0).astype(BF16)
    return _dot(sel.astype(BF16), e) > 0.5


def _init_online(m_ref, acc_ref):
    m_ref[...] = jnp.full(m_ref.shape, HALF_NEG, F32)
    acc_ref[...] = jnp.zeros(acc_ref.shape, F32)


def _finish_online(ocw, gn, acc_ref, tq):
    o_s = acc_ref[:, 0:LANE] * (1.0 / jnp.maximum(acc_ref[:, LANE:], 1e-30))
    outs = []
    for h in range(N_HEADS):
        g1 = gn[:, N_HEADS + h:N_HEADS + h + 1]
        outs.append(ocw[:, h * LANE:(h + 1) * LANE] + g1 * o_s[h * tq:(h + 1) * tq])
    return jnp.concatenate(outs, axis=1)


def _e2_prompt_kernel(order_ref, cnt_ref, qp_ref, sel_ref, gn_ref, ocw_ref, kvs_ref, out_ref, m_ref, acc_ref,
                      *, nchunks, ns, nt):
    tq = Q_TILE
    i = pl.program_id(0)
    s0 = i * tq
    qpos = s0 + lax.broadcasted_iota(jnp.int32, (tq, 1), 0)
    q_all = _heads_rows(qp_ref[...])
    _init_online(m_ref, acc_ref)
    n_act = jnp.where(i < nt, cnt_ref[jnp.minimum(i, nt - 1)], 0)
    blocks_per_chunk = SEL_CHUNK // SEL_BLOCK

    def body(j, carry):
        c1 = order_ref[i * nchunks + 2 * j]
        paired = 2 * j + 1 < n_act
        c2 = jnp.where(paired, order_ref[i * nchunks + jnp.minimum(2 * j + 1, nchunks - 1)], c1)
        ks, vs, dists, masks = [], [], [], [[] for _ in range(N_KV)]
        for c, first_blk in ((c1, c1 * blocks_per_chunk), (c2, jnp.where(paired, c2 * blocks_per_chunk, ns))):
            base = pl.multiple_of(c * SEL_CHUNK, SEL_CHUNK)
            ks.append(kvs_ref[pl.ds(base, SEL_CHUNK), 0:D_KV])
            vs.append(kvs_ref[pl.ds(base, SEL_CHUNK), D_KV:])
            kpos = base + lax.broadcasted_iota(jnp.int32, (1, SEL_CHUNK), 1)
            dists.append((s0 - kpos).astype(F32))
            for g in range(N_KV):
                masks[g].append(_expand_sel(sel_ref[g], first_blk, SEL_CHUNK) & (qpos >= kpos))
        s = _dot_nt(q_all, jnp.concatenate(ks, axis=0))
        _online_step(s, jnp.concatenate(dists, axis=1), [jnp.concatenate(mg, axis=1) for mg in masks],
                     jnp.concatenate(vs, axis=0), m_ref, acc_ref, tq)
        return carry

    lax.fori_loop(0, (n_act + 1) // 2, body, 0)
    out_ref[...] = _finish_online(ocw_ref[...], gn_ref[...], acc_ref, tq)


def _e2_prompt(order, cnt, qp, sel, gn, ocw, kvs_bf, t):
    ns = sel.shape[2]
    nt = t // Q_TILE
    nchunks = order.shape[0] // nt
    row = lambda w: pl.BlockSpec((Q_TILE, w), lambda i, o, c: (i, 0))
    prow = lambda w: pl.BlockSpec((Q_TILE, w), lambda i, o, c: (jnp.minimum(i, nt - 1), 0))
    grid_spec = pltpu.PrefetchScalarGridSpec(
        num_scalar_prefetch=2,
        grid=(qp.shape[0] // Q_TILE,),
        in_specs=[row(Q_PAD), pl.BlockSpec((N_KV, Q_TILE, ns), lambda i, o, c: (0, jnp.minimum(i, nt - 1), 0)),
                  row(LANE), prow(Q_PAD), pl.BlockSpec(kvs_bf.shape, lambda i, o, c: (0, 0))],
        out_specs=row(Q_PAD),
        scratch_shapes=[pltpu.VMEM((N_HEADS * Q_TILE, LANE), F32), pltpu.VMEM((N_HEADS * Q_TILE, 2 * LANE), F32)],
    )
    return pl.pallas_call(
        functools.partial(_e2_prompt_kernel, nchunks=nchunks, ns=ns, nt=nt),
        out_shape=jax.ShapeDtypeStruct((qp.shape[0], Q_PAD), F32),
        grid_spec=grid_spec,
        compiler_params=_cparams("arbitrary"),
    )(order, cnt, qp, sel, gn, ocw, kvs_bf)


def _e2_sample_kernel(pt_ref, pages_ref, qp_ref, sel_ref, gn_ref, ocw_ref, nsel_ref, expand_ref, oatt_all_ref,
                      out_ref, buf, sem, m_ref, acc_ref, *, pps, nsplit, nq):
    s = pl.program_id(0)
    nsteps = pl.num_programs(0)
    n_keys = pps * PAGE_SIZE

    def page_copy(step, p, slot):
        b = step // nsplit
        part = step % nsplit
        pg = pt_ref[b, part * pps + p]
        col = pl.multiple_of(p * PAGE_SIZE, PAGE_SIZE)
        return pltpu.make_async_copy(pages_ref.at[pg], buf.at[slot, :, pl.ds(col, PAGE_SIZE)], sem.at[slot])

    def start(step, slot):
        def body(p, c):
            page_copy(step, p, slot).start()
            return c
        lax.fori_loop(0, pps, body, 0)

    def wait(step, slot):
        def body(p, c):
            page_copy(step, p, slot).wait()
            return c
        lax.fori_loop(0, pps, body, 0)

    @pl.when(s == 0)
    def _():
        start(s, 0)

    @pl.when(s + 1 < nsteps)
    def _():
        start(s + 1, (s + 1) % 2)

    slot = s % 2
    part = s % nsplit
    qi = lax.broadcasted_iota(jnp.int32, (nq, 1), 0)
    q_all = _heads_rows(qp_ref[...])

    @pl.when(part == 0)
    def _():
        _init_online(m_ref, acc_ref)

    wait(s, slot)
    k = buf[slot, 0:D_KV, :].astype(BF16)
    v_ones = jnp.concatenate([buf[slot, D_KV:, :].astype(BF16), jnp.ones((D_KV, n_keys), BF16)], axis=0)
    sc = _dot(q_all, k)
    back = (nsplit - part) * n_keys - lax.broadcasted_iota(jnp.int32, (1, n_keys), 1)
    masks = [_dot(sel_ref[0, g, 0].astype(BF16), expand_ref[...]) > 0.5 for g in range(N_KV)]
    _online_step(sc, back.astype(F32), masks, v_ones, m_ref, acc_ref, nq, feature_major=True)

    @pl.when(part == nsplit - 1)
    def _():
        nw = _pad_rows(nsel_ref[...], LANE)
        j_n = lax.broadcasted_iota(jnp.int32, (1, LANE), 1)
        mask_n = (qi >= j_n) & (j_n < nq)
        s_n = _dot_nt(q_all, nw[:, 0:D_KV].astype(BF16))
        vn_ones = jnp.concatenate([nw[:, D_KV:].astype(BF16), jnp.ones((LANE, D_KV), BF16)], axis=1)
        _online_step(s_n, (-j_n).astype(F32), [mask_n, mask_n], vn_ones, m_ref, acc_ref, nq)
        out_ref[...] = _finish_online(ocw_ref[...], gn_ref[...], acc_ref, nq)


def _e2_sample(page_table, pages, qp, sel, gn, ocw, kv_sel, oatt, t, nq):
    bsz, n_pages = page_table.shape
    ns = sel.shape[3]
    nsplit = 4
    pps = n_pages // nsplit
    n_keys = pps * PAGE_SIZE
    nblk = ns // nsplit
    off = t // nq
    sel_parts = sel.reshape(bsz, N_KV, nq, nsplit, nblk).transpose(0, 1, 3, 2, 4)
    expand = (jnp.arange(nblk)[:, None] == jnp.arange(n_keys)[None, :] // SEL_BLOCK).astype(BF16)
    row = lambda w: pl.BlockSpec((nq, w), lambda s, pt: (off + s // nsplit, 0))
    grid_spec = pltpu.PrefetchScalarGridSpec(
        num_scalar_prefetch=1,
        grid=(bsz * nsplit,),
        in_specs=[pl.BlockSpec(memory_space=pl.ANY), row(Q_PAD),
                  pl.BlockSpec((1, N_KV, 1, nq, nblk), lambda s, pt: (s // nsplit, 0, s % nsplit, 0, 0)), row(LANE),
                  pl.BlockSpec((nq, Q_PAD), lambda s, pt: (s // nsplit, 0)), row(2 * D_KV),
                  pl.BlockSpec((nblk, n_keys), lambda s, pt: (0, 0), pipeline_mode=pl.Buffered(1)),
                  pl.BlockSpec(memory_space=pl.ANY)],
        out_specs=row(Q_PAD),
        scratch_shapes=[pltpu.VMEM((2, 2 * D_KV, n_keys), F32), pltpu.SemaphoreType.DMA((2,)),
                        pltpu.VMEM((N_HEADS * nq, LANE), F32), pltpu.VMEM((N_HEADS * nq, 2 * LANE), F32)],
    )
    return pl.pallas_call(
        functools.partial(_e2_sample_kernel, pps=pps, nsplit=nsplit, nq=nq),
        out_shape=jax.ShapeDtypeStruct(oatt.shape, F32),
        grid_spec=grid_spec,
        input_output_aliases={8: 0},
        compiler_params=_cparams("arbitrary"),
    )(page_table, pages, qp, sel_parts, gn, ocw, kv_sel, expand, oatt)


def _post_kernel(h_ref, yg_ref, oatt_ref, ga_ref, gb_ref, wglu_ref, watt_ref, wout_ref, o_ref):
    gl = _dot(yg_ref[...], wglu_ref[...])
    br_a = gl[:, :D_MODEL] * jax.nn.sigmoid(gl[:, D_MODEL:])
    br_b = _dot(oatt_ref[...].astype(BF16), watt_ref[...])
    merged = (ga_ref[...] * br_a + gb_ref[...] * br_b).astype(BF16)
    o_ref[...] = h_ref[...] + _dot(merged, wout_ref[...])


def _post(h, yg, oatt, ga, gb, wglu, watt, wout):
    n = h.shape[0]
    tm = _pick_tile(n, 256)
    row = lambda w: pl.BlockSpec((tm, w), lambda i: (i, 0))
    return pl.pallas_call(
        _post_kernel,
        out_shape=jax.ShapeDtypeStruct((n, D_MODEL), F32),
        grid=(n // tm,),
        in_specs=[row(D_MODEL), row(D_SSM), row(Q_PAD), row(D_MODEL), row(D_MODEL),
                  _const_spec(wglu.shape), _const_spec(watt.shape), _const_spec(wout.shape)],
        out_specs=row(D_MODEL),
        compiler_params=_cparams("parallel"),
    )(h, yg, oatt, ga, gb, wglu, watt, wout)


def _head_pad_index():
    h = jnp.arange(N_HEADS)[:, None]
    d = jnp.arange(HEAD_DIM)[None, :]
    return (LANE * h + HEAD_DIM * (h // HPG) + d).reshape(-1)


def _prep_mix_weights(w_in, qk_norm):
    idx = _head_pad_index()
    wq = jnp.zeros((D_MODEL, Q_PAD), F32).at[:, idx].set(w_in[:, D_SSM:D_SSM + N_HEADS * HEAD_DIM])
    c0 = D_SSM + N_HEADS * HEAD_DIM
    c1 = c0 + 6 * D_KV
    c2 = c1 + 3 * N_HEADS
    wgn = jnp.zeros((D_MODEL, LANE), F32).at[:, :3 * N_HEADS].set(w_in[:, c1:c2])
    w = jnp.concatenate([w_in[:, :D_SSM], wq, w_in[:, c0:c1], wgn, w_in[:, c2:]], axis=1).astype(BF16)
    gq = jnp.zeros((Q_PAD,), F32).at[idx].set(jnp.tile(qk_norm[0] * (HEAD_DIM ** -0.5 * LOG2E), N_HEADS))[None]
    gks = jnp.tile(qk_norm[2], N_KV)[None]
    gkw = jnp.tile(qk_norm[3], N_KV)[None]
    return w, gq, gks, gkw


def _prep_att_out(w_att_out):
    return jnp.zeros((Q_PAD, D_MODEL), F32).at[_head_pad_index()].set(w_att_out).astype(BF16)


def _prep_compress(pe_k, pe_v, w_k, w_v, gain_k):
    nj = CMP_BLOCK // CMP_STRIDE
    eye = jnp.eye(N_KV, dtype=F32)

    def pair_maps(w):
        w5 = w.reshape(nj, N_SPAIR, 2, HEAD_DIM, HEAD_DIM)
        return jnp.einsum("jpsde,gh->psgdjhe", w5, eye).reshape(N_SPAIR, 2 * D_KV, 2 * D_KV)

    def pair_pe(pe):
        p5 = pe.reshape(nj, N_SPAIR, 2, 1, HEAD_DIM)
        return jnp.broadcast_to(p5, (nj, N_SPAIR, 2, N_KV, HEAD_DIM)).reshape(nj, N_SPAIR, 2 * D_KV)

    w = jnp.stack([pair_maps(w_k), pair_maps(w_v)], axis=0)
    pe = jnp.stack([pair_pe(pe_k), pair_pe(pe_v)], axis=1).reshape(-1, 2 * D_KV)
    r = jnp.arange(2 * PAGE_SIZE)
    perm = (r[None, :] == (CMP_STRIDE * (r % CMP_STRIDE) + r // CMP_STRIDE)[:, None]).astype(BF16)
    return {"w": w.astype(BF16), "pe": pe,
            "gk": jnp.tile(gain_k, N_KV)[None], "perm": perm}


def _prep_s5(a_re, a_im, log_dt, b_re, b_im, c_re, c_im, d, nq):
    dt = jnp.exp(log_dt)[:, None]
    mag = jnp.exp(a_re * dt)
    lr = mag * jnp.cos(a_im * dt)
    li = mag * jnp.sin(a_im * dt)
    den = a_re * a_re + a_im * a_im
    fr = ((lr - 1.0) * a_re + li * a_im) / den
    fi = (li * a_re - (lr - 1.0) * a_im) / den
    bbr = fr[..., None] * b_re - fi[..., None] * b_im
    bbi = fr[..., None] * b_im + fi[..., None] * b_re
    eye = jnp.eye(N_SSM_GROUPS, dtype=F32)
    blk_b = lambda m: jnp.einsum("gpc,gh->gchp", m, eye).reshape(D_SSM, N_STATE)
    blk_c = lambda m: jnp.einsum("gcp,gh->gphc", m, eye).reshape(N_STATE, D_SSM)

    def lam_pow(k):
        kk = k.astype(F32)[:, None, None]
        m = jnp.exp(a_re * dt * kk)
        th = a_im * dt * kk
        return (m * jnp.cos(th)).reshape(-1, N_STATE), (m * jnp.sin(th)).reshape(-1, N_STATE)

    def step_table(seg):
        ks = []
        dd = 1
        while dd < seg:
            ks.append(dd)
            dd *= 2
        re, im = lam_pow(jnp.array(ks))
        live = jnp.arange(seg)[None, :, None] >= jnp.array(ks)[:, None, None]
        tab = jnp.stack([jnp.where(live, re[:, None, :], 0.0), jnp.where(live, im[:, None, :], 0.0)], axis=1)
        return tab.reshape(-1, N_STATE)

    sp = {"bmat": jnp.concatenate([blk_b(bbr), blk_b(bbi)], axis=1).astype(BF16),
          "cr": blk_c(c_re).astype(BF16), "ci": (-blk_c(c_im)).astype(BF16), "d": d[None],
          "lam_p": step_table(SCAN_SEG), "lam_s": step_table(nq)}
    sp["pr_p"], sp["pi_p"] = lam_pow(jnp.arange(SCAN_SEG) + 1)
    return sp, lam_pow


def _band_matrix(nb, ns):
    ratio = SEL_BLOCK // CMP_STRIDE
    lo = CMP_BLOCK // CMP_STRIDE - 1
    c = jnp.arange(nb)[:, None]
    j = jnp.arange(ns)[None, :]
    return ((c >= ratio * j - lo) & (c <= ratio * j + ratio - 1)).astype(BF16)


def kernel(x_prompt, x_sample, cache_kv_cmp, cache_kv_sel, cache_kv_win, state_ssm_re, state_ssm_im, page_table, p_prompt, p_sample, norm_ffn1, w_ffn1_in, w_ffn1_out, norm_mix, w_in, qk_norm, ssm_a_re, ssm_a_im, ssm_log_dt, ssm_b_re, ssm_b_im, ssm_c_re, ssm_c_im, ssm_d, w_glu, cmp_pe_k, cmp_pe_v, cmp_w_k, cmp_w_v, w_att_out, w_out, norm_ffn2, w_ffn2_in, w_ffn2_out, norm_ple, w_ple_gate, w_ple_proj):
    bp, t = x_prompt.shape[:2]
    bsz, nq = x_sample.shape[:2]
    n_pages = page_table.shape[1]
    past = n_pages * PAGE_SIZE
    n_pool = cache_kv_cmp.shape[1]
    w_buf = cache_kv_win.shape[2]
    ns_rows = bsz * nq
    assert bp == 1 and t % SEL_CHUNK == 0 and t >= WIN_KEYS and t % ns_rows == 0 and nq < CMP_STRIDE
    assert past == t and w_buf == WINDOW and ns_rows % Q_TILE == 0
    nb = t // CMP_STRIDE
    ns = t // SEL_BLOCK
    nt = t // Q_TILE
    nchunks = t // SEL_CHUNK
    mband = _band_matrix(nb, ns)
    page_table = page_table.astype(jnp.int32)
    feat_major = lambda c: jnp.transpose(c, (0, 1, 3, 4, 5, 2)).reshape(DEPTH * c.shape[1], 2 * D_KV, c.shape[2])
    pages_cmp = feat_major(cache_kv_cmp)
    pages_sel = feat_major(cache_kv_sel)
    pages_win = feat_major(cache_kv_win)

    h = jnp.concatenate([x_prompt[0], x_sample.reshape(ns_rows, D_MODEL)], axis=0)
    st_p = [[] for _ in range(5)]
    st_s = [[] for _ in range(5)]
    for i in range(DEPTH):
        row1 = lambda a: a[i][None]
        h = _ffn(h, row1(norm_ffn1), w_ffn1_in[i].astype(BF16), w_ffn1_out[i].astype(BF16))
        w_mix, gq, gks, gkw = _prep_mix_weights(w_in[i], qk_norm[i])
        u, qp, kv_cmp, kv_sel, kv_win, kvs_bf, kvw_bf, gn, ga, gb = _mix_in(h, row1(norm_mix), w_mix, gq, gks, gkw)

        sp, lam_pow = _prep_s5(ssm_a_re[i], ssm_a_im[i], ssm_log_dt[i], ssm_b_re[i], ssm_b_im[i],
                               ssm_c_re[i], ssm_c_im[i], ssm_d[i], nq)
        sp["pr_s"], sp["pi_s"] = lam_pow(jnp.arange(ns_rows) % nq + 1)
        yg, hr_p, hi_p = _s5_prompt(u, t, sp)
        h0r = jnp.repeat(state_ssm_re[i].reshape(bsz, N_STATE), nq, axis=0)
        h0i = jnp.repeat(state_ssm_im[i].reshape(bsz, N_STATE), nq, axis=0)
        yg, hr_s, hi_s = _s5_sample(u, yg, t, ns_rows, nq, h0r, h0i, sp)

        cw = _prep_compress(cmp_pe_k[i], cmp_pe_v[i], cmp_w_k[i], cmp_w_v[i], qk_norm[i, 1])
        layer_pages = page_table + i * n_pool
        kcc_p, vcc_p = _compress_rows(kv_cmp, t, cw)
        kcc_s, vcc_s = _compress_paged(layer_pages, pages_cmp, cw)
        ocw_p, sel_p, flags = _e1_prompt(qp, gn, kcc_p, vcc_p, mband, kvw_bf, t)
        ocw_s, sel_s = _e1_sample(qp, gn, kcc_s, vcc_s, mband, pages_win, i * bsz, kv_win, t, bsz, nq, past)
        idle = flags[:, 0, :nchunks] == 0
        order = jnp.argsort(idle, axis=1, stable=True).astype(jnp.int32).reshape(-1)
        cnt = (nchunks - jnp.sum(idle, axis=1)).astype(jnp.int32)
        oatt = _e2_prompt(order, cnt, qp, sel_p, gn, ocw_p, kvs_bf, t)
        oatt = _e2_sample(layer_pages, pages_sel, qp, sel_s, gn, ocw_s, kv_sel, oatt, t, nq)

        h = _post(h, yg, oatt, ga, gb,
                  w_glu[i].astype(BF16), _prep_att_out(w_att_out[i]), w_out[i].astype(BF16))
        p_all = jnp.concatenate([p_prompt[i, 0], p_sample[i].reshape(ns_rows, D_PLE)], axis=0)
        h = _ffn(h, row1(norm_ffn2), w_ffn2_in[i].astype(BF16), w_ffn2_out[i].astype(BF16),
                 (p_all, row1(norm_ple), w_ple_gate[i].astype(BF16), w_ple_proj[i].astype(BF16)))

        kv5 = lambda a, lead: a.reshape(lead + (2, N_KV, HEAD_DIM))
        st_p[0].append(kv5(kv_cmp[:t], (1, t)))
        st_p[1].append(kv5(kv_sel[:t], (1, t)))
        st_p[2].append(kv5(kv_win[t - min(WINDOW, t):t], (1, min(WINDOW, t))))
        st_p[3].append(hr_p.reshape(1, N_SSM_GROUPS, P_STATE))
        st_p[4].append(hi_p.reshape(1, N_SSM_GROUPS, P_STATE))
        st_s[0].append(kv5(kv_cmp[t:], (bsz, nq)))
        st_s[1].append(kv5(kv_sel[t:], (bsz, nq)))
        st_s[2].append(jnp.concatenate([cache_kv_win[i, :, nq:], kv5(kv_win[t:], (bsz, nq))], axis=1))
        st_s[3].append(hr_s[nq - 1::nq].reshape(bsz, N_SSM_GROUPS, P_STATE))
        st_s[4].append(hi_s[nq - 1::nq].reshape(bsz, N_SSM_GROUPS, P_STATE))

    outs_p = [jnp.stack(a) for a in st_p]
    outs_s = [jnp.stack(a) for a in st_s]
    y_prompt = h[:t][None]
    y_sample = h[t:].reshape(bsz, nq, D_MODEL)
    return (y_prompt, y_sample, *outs_p, *outs_s)
```

```python
import functools

import jax
import jax.numpy as jnp
from jax import lax
from jax.experimental import pallas as pl
from jax.experimental.pallas import tpu as pltpu

F32 = jnp.float32
BF16 = jnp.bfloat16

D_MODEL = 1024
DEPTH = 2
D_SSM = 512
SSM_GROUP = 16
N_SSM_GROUPS = 32
P_STATE = 64
N_STATE = N_SSM_GROUPS * P_STATE
N_HEADS = 8
HEAD_DIM = 64
N_KV = 2
HPG = 4
D_KV = 128
CMP_BLOCK = 32
CMP_STRIDE = 16
SEL_BLOCK = 64
TOP_N = 16
WINDOW = 512
PAGE_SIZE = 128
SEL_BIG = 1e4
D_FF = 2816
D_PLE = 256
RMS_EPS = 1e-6
NEG = -1e30
HALF_NEG = -0.5e30
LOG2E = 1.4426950408889634

LANE = 128
Q_TILE = 128
E1_TILES = 1
SCAN_SEG = 8
SEL_CHUNK = 256
WIN_KEYS = WINDOW + Q_TILE
Q_PAD = N_HEADS * LANE
VMEM_LIMIT = 56 * 2 ** 20


def _cparams(*sem):
    return pltpu.CompilerParams(dimension_semantics=sem, vmem_limit_bytes=VMEM_LIMIT)


def _dot(a, b):
    return jnp.dot(a, b, preferred_element_type=F32)


def _dot_nt(a, b):
    return lax.dot_general(a, b, (((1,), (1,)), ((), ())), preferred_element_type=F32)


def _pick_tile(n, target):
    for t in range(min(n, target), 15, -1):
        if n % t == 0 and t % 16 == 0:
            return t
    raise ValueError(f"no row tile for {n}")


def _const_spec(shape):
    nd = len(shape)
    return pl.BlockSpec(shape, lambda *_: (0,) * nd, pipeline_mode=pl.Buffered(1))


def _rms(x, g):
    ms = jnp.mean(x * x, axis=-1, keepdims=True)
    return x * lax.rsqrt(ms + RMS_EPS) * g


def _halfnorm(x, gain):
    lo = lax.broadcasted_iota(jnp.int32, (1, LANE), 1) < HEAD_DIM
    x2 = x * x
    s_lo = jnp.sum(jnp.where(lo, x2, 0.0), axis=-1, keepdims=True)
    s_hi = jnp.sum(jnp.where(lo, 0.0, x2), axis=-1, keepdims=True)
    ms = jnp.where(lo, s_lo, s_hi) * (1.0 / HEAD_DIM)
    return x * lax.rsqrt(ms + RMS_EPS) * gain


def _row_tile(refs, lead_tiles):
    if len(refs) == 1:
        return refs[0][...]
    return jnp.where(pl.program_id(0) < lead_tiles, refs[0][...], refs[1][...])


def _ffn_kernel(*refs, ple, h_parts, p_parts, lead_tiles):
    refs = list(refs)
    h_refs = [refs.pop(0) for _ in range(h_parts)]
    if ple:
        g_ref, wi_ref, wo_ref = refs[:3]
        p_refs = refs[3:3 + p_parts]
        gp_ref, wg_ref, wp_ref, o_ref = refs[3 + p_parts:]
    else:
        g_ref, wi_ref, wo_ref, o_ref = refs
    h = _row_tile(h_refs, lead_tiles)
    xn = _rms(h, g_ref[...]).astype(BF16)
    a = _dot(xn, wi_ref[:, :D_FF])
    b = _dot(xn, wi_ref[:, D_FF:])
    act = (a * jax.nn.sigmoid(a) * b).astype(BF16)
    h = h + 0.5 * _dot(act, wo_ref[...])
    if ple:
        xg = _rms(h, gp_ref[...]).astype(BF16)
        gate = jax.nn.sigmoid(_dot(xg, wg_ref[...]))
        h = h + gate * _dot(_row_tile(p_refs, lead_tiles).astype(BF16), wp_ref[...])
    o_ref[...] = h


def _ffn(h, g, wi, wo, n_prompt, n_sample, ple_args=None, p_layer=0):
    h_parts = list(h) if isinstance(h, (tuple, list)) else [h]
    n = n_prompt + n_sample
    tm = _pick_tile(n, 256)
    assert n_prompt % tm == 0 and n_sample % tm == 0
    lead_tiles = n_prompt // tm
    row = lambda w: pl.BlockSpec((tm, w), lambda i: (i, 0))

    def pair_specs(w, off_p, off_s):
        return [pl.BlockSpec((tm, w), lambda i: (off_p + jnp.minimum(i, lead_tiles - 1), 0)),
                pl.BlockSpec((tm, w), lambda i: (off_s + jnp.maximum(i - lead_tiles, 0), 0))]

    in_specs = (pair_specs(D_MODEL, 0, 0) if len(h_parts) == 2 else [row(D_MODEL)]) + [
        _const_spec((1, D_MODEL)), _const_spec(wi.shape), _const_spec(wo.shape)]
    args = h_parts + [g, wi, wo]
    p_parts = []
    if ple_args is not None:
        p_parts, gp, wg, wp = ple_args
        p_parts = list(p_parts)
        in_specs += pair_specs(D_PLE, p_layer * lead_tiles, p_layer * (n_sample // tm)) + [
            _const_spec((1, D_MODEL)), _const_spec(wg.shape), _const_spec(wp.shape)]
        args += p_parts + [gp, wg, wp]
    return pl.pallas_call(
        functools.partial(_ffn_kernel, ple=ple_args is not None, h_parts=len(h_parts), p_parts=len(p_parts),
                          lead_tiles=lead_tiles),
        out_shape=jax.ShapeDtypeStruct((n, D_MODEL), F32),
        grid=(n // tm,),
        in_specs=in_specs,
        out_specs=row(D_MODEL),
        compiler_params=_cparams("parallel"),
    )(*args)


_C_U = 0
_C_Q = _C_U + D_SSM
_C_KV = _C_Q + Q_PAD
_C_GN = _C_KV + 6 * D_KV
_C_GA = _C_GN + LANE
_C_GB = _C_GA + D_MODEL
_C_END = _C_GB + D_MODEL


def _mix_in_kernel(h_ref, g_ref, w_ref, gq_ref, gks_ref, gkw_ref,
                   u_ref, qp_ref, kvc_ref, kvs_ref, kvw_ref, kvsb_ref, kvwb_ref, gn_ref, ga_ref, gb_ref):
    xn = _rms(h_ref[...], g_ref[...]).astype(BF16)
    u_ref[...] = _dot(xn, w_ref[:, _C_U:_C_Q])
    zq = _dot(xn, w_ref[:, _C_Q:_C_KV])
    for h in range(N_HEADS):
        qh = zq[:, h * LANE:(h + 1) * LANE]
        ms = jnp.sum(qh * qh, axis=-1, keepdims=True) * (1.0 / HEAD_DIM)
        qn = qh * lax.rsqrt(ms + RMS_EPS) * gq_ref[:, h * LANE:(h + 1) * LANE]
        qp_ref[:, h * LANE:(h + 1) * LANE] = qn
    zkv = _dot(xn, w_ref[:, _C_KV:_C_GN])
    kvc_ref[...] = zkv[:, 0:2 * D_KV]
    ks = _halfnorm(zkv[:, 2 * D_KV:3 * D_KV], gks_ref[...])
    vs = zkv[:, 3 * D_KV:4 * D_KV]
    kw = _halfnorm(zkv[:, 4 * D_KV:5 * D_KV], gkw_ref[...])
    vw = zkv[:, 5 * D_KV:6 * D_KV]
    kvs_ref[:, 0:D_KV] = ks
    kvs_ref[:, D_KV:] = vs
    kvw_ref[:, 0:D_KV] = kw
    kvw_ref[:, D_KV:] = vw
    kvsb_ref[:, 0:D_KV] = ks.astype(BF16)
    kvsb_ref[:, D_KV:2 * D_KV] = vs.astype(BF16)
    kvsb_ref[:, 2 * D_KV:] = jnp.ones((ks.shape[0], D_KV), BF16)
    kvwb_ref[:, 0:D_KV] = kw.astype(BF16)
    kvwb_ref[:, D_KV:] = vw.astype(BF16)
    gn_ref[...] = jax.nn.sigmoid(_dot(xn, w_ref[:, _C_GN:_C_GA]))
    ga_ref[...] = jax.nn.sigmoid(_dot(xn, w_ref[:, _C_GA:_C_GB]))
    gb_ref[...] = jax.nn.sigmoid(_dot(xn, w_ref[:, _C_GB:_C_END]))


def _mix_in(h, g, w, gq, gks, gkw):
    n = h.shape[0]
    tm = _pick_tile(n, 256)
    row = lambda width: pl.BlockSpec((tm, width), lambda i: (i, 0))
    widths = [(D_SSM, F32), (Q_PAD, F32), (2 * D_KV, F32), (2 * D_KV, F32), (2 * D_KV, F32),
              (3 * D_KV, BF16), (2 * D_KV, BF16), (LANE, F32), (D_MODEL, F32), (D_MODEL, F32)]
    return pl.pallas_call(
        _mix_in_kernel,
        out_shape=[jax.ShapeDtypeStruct((n, wd), dt) for wd, dt in widths],
        grid=(n // tm,),
        in_specs=[row(D_MODEL), _const_spec((1, D_MODEL)), _const_spec(w.shape),
                  _const_spec((1, Q_PAD)), _const_spec((1, LANE)), _const_spec((1, LANE))],
        out_specs=[row(wd) for wd, _ in widths],
        compiler_params=_cparams("parallel"),
    )(h, g, w, gq, gks, gkw)


def _gelu_tanh(x):
    return 0.5 * x * (1.0 + jnp.tanh(0.7978845608028654 * (x + 0.044715 * (x * x * x))))


def _s5_kernel(*refs, seg, carry, seq_steps=None):
    if carry:
        (u_ref, bm_ref, cr_ref, ci_ref, d_ref, lam_ref, pr_ref, pi_ref,
         y_ref, hr_out, hi_out, car_ref, cai_ref) = refs
    else:
        (u_ref, bm_ref, cr_ref, ci_ref, d_ref, lam_ref, pr_ref, pi_ref, h0r_ref, h0i_ref, y_all_ref,
         y_ref, hr_out, hi_out) = refs
    u = u_ref[...]
    rows = u.shape[0]
    x = _dot(u.astype(BF16), bm_ref[...])
    xr = x[:, :N_STATE]
    xi = x[:, N_STATE:]
    d, k = 1, 0
    while d < seg:
        lr = jnp.tile(lam_ref[2 * k * seg:(2 * k + 1) * seg, :], (rows // seg, 1))
        li = jnp.tile(lam_ref[(2 * k + 1) * seg:(2 * k + 2) * seg, :], (rows // seg, 1))
        sr = pltpu.roll(xr, d, 0)
        si = pltpu.roll(xi, d, 0)
        xr, xi = xr + lr * sr - li * si, xi + lr * si + li * sr
        d *= 2
        k += 1
    p_r = pr_ref[...]
    p_i = pi_ref[...]
    if carry:
        @pl.when(pl.program_id(0) == 0)
        def _():
            car_ref[...] = jnp.zeros_like(car_ref)
            cai_ref[...] = jnp.zeros_like(cai_ref)
        c_r = car_ref[...]
        c_i = cai_ref[...]
        hr_parts, hi_parts = [], []
        for j in range(rows // seg):
            tr = xr[j * seg:(j + 1) * seg]
            ti = xi[j * seg:(j + 1) * seg]
            hr_j = tr + p_r * c_r - p_i * c_i
            hi_j = ti + p_r * c_i + p_i * c_r
            c_r = hr_j[seg - 1:seg, :]
            c_i = hi_j[seg - 1:seg, :]
            hr_parts.append(hr_j)
            hi_parts.append(hi_j)
        hr = jnp.concatenate(hr_parts, axis=0)
        hi = jnp.concatenate(hi_parts, axis=0)
    else:
        c_r = h0r_ref[...]
        c_i = h0i_ref[...]
        hr = xr + p_r * c_r - p_i * c_i
        hi = xi + p_r * c_i + p_i * c_r
    y = _dot(hr.astype(BF16), cr_ref[...]) + _dot(hi.astype(BF16), ci_ref[...]) + d_ref[...] * u
    y_ref[...] = _gelu_tanh(y).astype(BF16)
    if carry:
        car_ref[...] = hr[rows - 1:rows, :]
        cai_ref[...] = hi[rows - 1:rows, :]

        @pl.when(pl.program_id(0) < seq_steps)
        def _():
            hr_out[...] = hr[rows - 1:rows, :]
            hi_out[...] = hi[rows - 1:rows, :]
    else:
        hr_out[...] = hr
        hi_out[...] = hi


def _s5_prompt(u, t, sp):
    rows = Q_TILE
    consts = [sp["bmat"], sp["cr"], sp["ci"], sp["d"], sp["lam_p"], sp["pr_p"], sp["pi_p"]]
    return pl.pallas_call(
        functools.partial(_s5_kernel, seg=SCAN_SEG, carry=True, seq_steps=t // rows),
        out_shape=[jax.ShapeDtypeStruct((u.shape[0], D_SSM), BF16),
                   jax.ShapeDtypeStruct((1, N_STATE), F32), jax.ShapeDtypeStruct((1, N_STATE), F32)],
        grid=(u.shape[0] // rows,),
        in_specs=[pl.BlockSpec((rows, D_SSM), lambda i: (i, 0))] + [_const_spec(c.shape) for c in consts],
        out_specs=[pl.BlockSpec((rows, D_SSM), lambda i: (i, 0)),
                   _const_spec((1, N_STATE)), _const_spec((1, N_STATE))],
        scratch_shapes=[pltpu.VMEM((1, N_STATE), F32), pltpu.VMEM((1, N_STATE), F32)],
        compiler_params=_cparams("arbitrary"),
    )(u, *consts)


def _s5_sample(u, yg, t, ns_rows, nq, h0r, h0i, sp):
    consts = [sp["bmat"], sp["cr"], sp["ci"], sp["d"], sp["lam_s"], sp["pr_s"], sp["pi_s"], h0r, h0i]
    blk = t // ns_rows
    return pl.pallas_call(
        functools.partial(_s5_kernel, seg=nq, carry=False),
        out_shape=[jax.ShapeDtypeStruct(yg.shape, BF16),
                   jax.ShapeDtypeStruct((ns_rows, N_STATE), F32), jax.ShapeDtypeStruct((ns_rows, N_STATE), F32)],
        grid=(1,),
        in_specs=([pl.BlockSpec((ns_rows, D_SSM), lambda i: (blk, 0))] + [_const_spec(c.shape) for c in consts]
                  + [pl.BlockSpec(memory_space=pl.ANY)]),
        out_specs=[pl.BlockSpec((ns_rows, D_SSM), lambda i: (blk, 0)),
                   _const_spec((ns_rows, N_STATE)), _const_spec((ns_rows, N_STATE))],
        input_output_aliases={len(consts) + 1: 0},
        compiler_params=_cparams("arbitrary"),
    )(u, *consts, yg)


N_SPAIR = CMP_STRIDE // 2


def _compress_part(get_x2, w_ref, a0, a1, part, rows):
    base = pl.multiple_of(part * rows, rows)
    for kv in range(2):
        acc = None
        for sp in range(N_SPAIR):
            term = _dot(get_x2(kv, sp), w_ref[kv, sp])
            acc = term if acc is None else acc + term
        a0[pl.ds(base, rows), kv * D_KV:(kv + 1) * D_KV] = acc[:, 0:D_KV]
        a1[pl.ds(base, rows), kv * D_KV:(kv + 1) * D_KV] = acc[:, D_KV:]


def _compress_finish(a0, a1, w_ref, pe_ref, gk_ref, kcc_ref, vcc_ref, nb):
    bias = []
    for j in range(CMP_BLOCK // CMP_STRIDE):
        halves = []
        for kv in range(2):
            b = jnp.zeros((8, 2 * D_KV), F32)
            for sp in range(N_SPAIR):
                r = (j * 2 + kv) * N_SPAIR + sp
                b = b + _dot(jnp.broadcast_to(pe_ref[r:r + 1, :], (8, 2 * D_KV)).astype(BF16), w_ref[kv, sp])
            halves.append(b[0:1, j * D_KV:(j + 1) * D_KV])
        bias.append(jnp.concatenate(halves, axis=1))
    out = (a0[...] + bias[0]) + pltpu.roll(a1[...] + bias[1], nb - 1, 0)
    valid = lax.broadcasted_iota(jnp.int32, (nb, 1), 0) < nb - 1
    out = jnp.where(valid, out, 0.0)
    kcc_ref[0] = _halfnorm(out[:, 0:D_KV], gk_ref[...]).astype(BF16)
    vcc_ref[0] = out[:, D_KV:].astype(BF16)


def _compress_rows_kernel(xk_ref, xv_ref, w_ref, pe_ref, gk_ref, kcc_ref, vcc_ref, a0, a1, *, nsplit, nb):
    part = pl.program_id(0)
    rows = nb // nsplit

    def get_x2(kv, sp):
        x_ref = (xk_ref, xv_ref)[kv]
        pair = [x_ref[pl.ds(2 * sp + si, rows, stride=CMP_STRIDE), :] for si in range(2)]
        return jnp.concatenate(pair, axis=1).astype(BF16)
    _compress_part(get_x2, w_ref, a0, a1, part, rows)

    @pl.when(part == nsplit - 1)
    def _():
        _compress_finish(a0, a1, w_ref, pe_ref, gk_ref, kcc_ref, vcc_ref, nb)


def _compress_rows(kv_cmp, t, cw):
    nb = t // CMP_STRIDE
    nsplit = 2
    return pl.pallas_call(
        functools.partial(_compress_rows_kernel, nsplit=nsplit, nb=nb),
        out_shape=[jax.ShapeDtypeStruct((1, nb, D_KV), BF16), jax.ShapeDtypeStruct((1, nb, D_KV), BF16)],
        grid=(nsplit,),
        in_specs=[pl.BlockSpec((t // nsplit, D_KV), lambda i: (i, 0)),
                  pl.BlockSpec((t // nsplit, D_KV), lambda i: (i, 1)),
                  _const_spec(cw["w"].shape), _const_spec(cw["pe"].shape), _const_spec((1, LANE))],
        out_specs=[_const_spec((1, nb, D_KV)), _const_spec((1, nb, D_KV))],
        scratch_shapes=[pltpu.VMEM((nb, 2 * D_KV), F32), pltpu.VMEM((nb, 2 * D_KV), F32)],
        compiler_params=_cparams("arbitrary"),
    )(kv_cmp, kv_cmp, cw["w"], cw["pe"], cw["gk"])


def _compress_kernel(pt_ref, pages_ref, w_ref, pe_ref, gk_ref, perm_ref,
                     kcc_ref, vcc_ref, buf, sem, x_scr, a0, a1, *, pps, nsplit, nb):
    s = pl.program_id(0)
    nsteps = pl.num_programs(0)

    def page_copy(step, p, slot):
        b = step // nsplit
        part = step % nsplit
        pg = pt_ref[b, part * pps + p]
        col = pl.multiple_of((p % 2) * PAGE_SIZE, PAGE_SIZE)
        return pltpu.make_async_copy(pages_ref.at[pg], buf.at[slot, p // 2, :, pl.ds(col, PAGE_SIZE)],
                                     sem.at[slot])

    def start(step, slot):
        def body(p, c):
            page_copy(step, p, slot).start()
            return c
        lax.fori_loop(0, pps, body, 0)

    def wait(step, slot):
        def body(p, c):
            page_copy(step, p, slot).wait()
            return c
        lax.fori_loop(0, pps, body, 0)

    @pl.when(s == 0)
    def _():
        start(s, 0)

    @pl.when(s + 1 < nsteps)
    def _():
        start(s + 1, (s + 1) % 2)

    slot = s % 2
    wait(s, slot)
    part = s % nsplit

    def to_rows(q, c):
        xt = buf[slot, q].astype(BF16)
        xp = _dot_nt(perm_ref[...], xt).astype(BF16)
        base = pl.multiple_of(q * CMP_STRIDE, CMP_STRIDE)
        for s2 in range(CMP_STRIDE):
            for kv in range(2):
                x_scr[kv, s2 // 2, pl.ds(base, CMP_STRIDE), (s2 % 2) * D_KV:(s2 % 2 + 1) * D_KV] = (
                    xp[s2 * CMP_STRIDE:(s2 + 1) * CMP_STRIDE, kv * D_KV:(kv + 1) * D_KV])
        return c
    lax.fori_loop(0, pps // 2, to_rows, 0, unroll=4)
    _compress_part(lambda kv, sp: x_scr[kv, sp], w_ref, a0, a1, part, nb // nsplit)

    @pl.when(part == nsplit - 1)
    def _():
        _compress_finish(a0, a1, w_ref, pe_ref, gk_ref, kcc_ref, vcc_ref, nb)


def _compress_paged(page_table, pages, cw):
    bsz, n_pages = page_table.shape
    nb = n_pages * (PAGE_SIZE // CMP_STRIDE)
    nsplit = 2
    pps = n_pages // nsplit
    const = lambda shape: pl.BlockSpec(shape, lambda s, pt: (0,) * len(shape), pipeline_mode=pl.Buffered(1))
    grid_spec = pltpu.PrefetchScalarGridSpec(
        num_scalar_prefetch=1,
        grid=(bsz * nsplit,),
        in_specs=[pl.BlockSpec(memory_space=pl.ANY), const(cw["w"].shape), const(cw["pe"].shape),
                  const((1, LANE)), const(cw["perm"].shape)],
        out_specs=[pl.BlockSpec((1, nb, D_KV), lambda s, pt: (s // nsplit, 0, 0)),
                   pl.BlockSpec((1, nb, D_KV), lambda s, pt: (s // nsplit, 0, 0))],
        scratch_shapes=[pltpu.VMEM((2, pps // 2, 2 * D_KV, 2 * PAGE_SIZE), F32), pltpu.SemaphoreType.DMA((2,)),
                        pltpu.VMEM((2, N_SPAIR, nb // nsplit, 2 * D_KV), BF16),
                        pltpu.VMEM((nb, 2 * D_KV), F32), pltpu.VMEM((nb, 2 * D_KV), F32)],
    )
    return pl.pallas_call(
        functools.partial(_compress_kernel, pps=pps, nsplit=nsplit, nb=nb),
        out_shape=[jax.ShapeDtypeStruct((bsz, nb, D_KV), BF16), jax.ShapeDtypeStruct((bsz, nb, D_KV), BF16)],
        grid_spec=grid_spec,
        compiler_params=_cparams("arbitrary"),
    )(page_table, pages, cw["w"], cw["pe"], cw["gk"], cw["perm"])


def _heads_rows(qp):
    return jnp.concatenate([qp[:, h * LANE:(h + 1) * LANE] for h in range(N_HEADS)], axis=0).astype(BF16)


def _slope2(h):
    return LOG2E * 2.0 ** (-8.0 * (h + 1) / N_HEADS)


def _branch(q_all, tq, pieces):
    s_all = [_dot(q_all, k) if fm else _dot_nt(q_all, k) for k, _, _, _, fm in pieces]
    es, invs = [], []
    for h in range(N_HEADS):
        rows = slice(h * tq, (h + 1) * tq)
        sm = [jnp.where(pc[3], s[rows] - _slope2(h) * pc[2], NEG) for s, pc in zip(s_all, pieces)]
        m = functools.reduce(jnp.maximum, [jnp.max(x, axis=-1, keepdims=True) for x in sm])
        m = jnp.maximum(m, HALF_NEG)
        e = [jnp.exp2(x - m) for x in sm]
        l = functools.reduce(lambda a, b: a + b, [jnp.sum(x, axis=-1, keepdims=True) for x in e])
        es.append(e)
        invs.append(1.0 / jnp.maximum(l, 1e-30))
    o = None
    for i, (_, v, _, _, fm) in enumerate(pieces):
        p = jnp.concatenate([es[h][i] for h in range(N_HEADS)], axis=0).astype(BF16)
        t = _dot_nt(p, v) if fm else _dot(p, v)
        o = t if o is None else o + t
    return o * jnp.concatenate(invs, axis=0), es, invs


def _cmp_win_topk(q_all, tq, qpos, gn, kcc, vcc, mband, win_pieces, nb, ns, topn):
    n_io = lax.broadcasted_iota(jnp.int32, (1, nb), 1)
    kc_end = n_io * CMP_STRIDE + (CMP_BLOCK - 1)
    mask_c = (qpos >= kc_end) & (n_io < nb - 1)
    kdist_c = (qpos[0:1, :] - kc_end).astype(F32)
    o_c, es, invs = _branch(q_all, tq, [(kcc, vcc, kdist_c, mask_c, False)])
    o_w, _, _ = _branch(q_all, tq, win_pieces)
    imps = []
    for g in range(N_KV):
        psum = es[HPG * g][0] * invs[HPG * g]
        for hh in range(1, HPG):
            psum = psum + es[HPG * g + hh][0] * invs[HPG * g + hh]
        hi = psum.astype(BF16)
        r = psum - hi.astype(F32)
        mid = r.astype(BF16)
        lo = (r - mid.astype(F32)).astype(BF16)
        imps.append(_dot(hi, mband) + _dot(mid, mband) + _dot(lo, mband))
    imp = jnp.concatenate(imps, axis=0)
    blk = lax.broadcasted_iota(jnp.int32, (1, ns), 1)
    cur = jnp.concatenate([jnp.right_shift(qpos, 6)] * N_KV, axis=0)
    forced = (blk == 0) | (blk == cur) | (blk == cur - 1)
    score = jnp.where(forced, imp + SEL_BIG, jnp.where(blk <= cur, imp, -SEL_BIG))
    transposed = score.shape[0] % LANE == 0
    if transposed:
        score = score.T
    axis = 0 if transposed else 1
    blkf = lax.broadcasted_iota(jnp.int32, (ns, 1) if transposed else (1, ns), axis).astype(F32)
    for _ in range(topn):
        m = jnp.max(score, axis=axis, keepdims=True)
        first = jnp.min(jnp.where(score == m, blkf, float(ns)), axis=axis, keepdims=True)
        score = jnp.where(blkf == first, -jnp.inf, score)
    sel = jnp.where(score == -jnp.inf, 1.0, 0.0)
    if transposed:
        sel = sel.T
    sel = jnp.where(blk <= cur, sel, 0.0)
    outs = []
    for h in range(N_HEADS):
        rows = slice(h * tq, (h + 1) * tq)
        outs.append(gn[:, h:h + 1] * o_c[rows] + gn[:, 2 * N_HEADS + h:2 * N_HEADS + h + 1] * o_w[rows])
    return jnp.concatenate(outs, axis=1), sel


def _e1_prompt_kernel(qp_ref, gn_ref, kcc_ref, vcc_ref, mband_ref, kvw_ref, ocw_ref, sel_ref, flag_ref,
                      *, nb, ns):
    tq = Q_TILE
    for half in range(E1_TILES):
        rows = slice(half * tq, (half + 1) * tq)
        s0 = (pl.program_id(0) * E1_TILES + half) * tq
        qpos = s0 + lax.broadcasted_iota(jnp.int32, (tq, 1), 0)
        q_all = _heads_rows(qp_ref[rows, :])
        start = pl.multiple_of(jnp.maximum(s0 - WINDOW, 0), Q_TILE)
        kw = kvw_ref[pl.ds(start, WIN_KEYS), 0:D_KV]
        vw = kvw_ref[pl.ds(start, WIN_KEYS), D_KV:]
        kpos = start + lax.broadcasted_iota(jnp.int32, (1, WIN_KEYS), 1)
        dist_w = qpos - kpos
        mask_w = (dist_w >= 0) & (dist_w < WINDOW)
        win_pieces = [(kw, vw, (s0 - kpos).astype(F32), mask_w, False)]
        ocw, sel = _cmp_win_topk(q_all, tq, qpos, gn_ref[rows, :], kcc_ref[0], vcc_ref[0], mband_ref[...],
                                 win_pieces, nb, ns, TOP_N)
        ocw_ref[rows, :] = ocw
        sel_ref[0, rows, :] = sel[:tq]
        sel_ref[1, rows, :] = sel[tq:]
        colany = jnp.max(sel, axis=0, keepdims=True)
        j_io = lax.broadcasted_iota(jnp.int32, (ns, LANE), 0)
        c_io = lax.broadcasted_iota(jnp.int32, (ns, LANE), 1)
        grp = jnp.where(jnp.right_shift(j_io, 2) == c_io, 1.0, 0.0).astype(BF16)
        cnt = _dot(jnp.broadcast_to(colany, (8, ns)).astype(BF16), grp)
        flag_ref[half] = (cnt > 0.5).astype(jnp.int32)


def _e1_prompt(qp, gn, kcc, vcc, mband, kvw_bf, t):
    nb, ns = mband.shape
    nt = t // Q_TILE
    step_rows = E1_TILES * Q_TILE
    row = lambda w: pl.BlockSpec((step_rows, w), lambda i: (i, 0))
    return pl.pallas_call(
        functools.partial(_e1_prompt_kernel, nb=nb, ns=ns),
        out_shape=[jax.ShapeDtypeStruct((t, Q_PAD), F32), jax.ShapeDtypeStruct((N_KV, t, ns), F32),
                   jax.ShapeDtypeStruct((nt, 8, LANE), jnp.int32)],
        grid=(nt // E1_TILES,),
        in_specs=[row(Q_PAD), row(LANE), _const_spec((1, nb, D_KV)), _const_spec((1, nb, D_KV)),
                  _const_spec(mband.shape), _const_spec(kvw_bf.shape)],
        out_specs=[row(Q_PAD), pl.BlockSpec((N_KV, step_rows, ns), lambda i: (0, i, 0)),
                   pl.BlockSpec((E1_TILES, 8, LANE), lambda i: (i, 0, 0))],
        compiler_params=_cparams("parallel"),
    )(qp, gn, kcc, vcc, mband, kvw_bf)


def _pad_rows(x, rows):
    return jnp.concatenate([x, jnp.zeros((rows - x.shape[0], x.shape[1]), x.dtype)], axis=0)


def _e1_sample_kernel(qp_ref, gn_ref, kcc_ref, vcc_ref, mband_ref, cwin_ref, nwin_ref, ocw_ref, sel_ref,
                      *, nb, ns, nq, past):
    qi = lax.broadcasted_iota(jnp.int32, (nq, 1), 0)
    qpos = past + qi
    q_all = _heads_rows(qp_ref[...])
    cw = cwin_ref[0]
    w_buf = cw.shape[1]
    j_c = lax.broadcasted_iota(jnp.int32, (1, w_buf), 1)
    dist_cw = qi + (w_buf - j_c)
    mask_cw = (dist_cw >= 0) & (dist_cw < WINDOW)
    nw = _pad_rows(nwin_ref[...], LANE)
    j_n = lax.broadcasted_iota(jnp.int32, (1, LANE), 1)
    mask_nw = (qi >= j_n) & (j_n < nq)
    pieces = [(cw[0:D_KV, :].astype(BF16), cw[D_KV:, :].astype(BF16), (w_buf - j_c).astype(F32), mask_cw, True),
              (nw[:, 0:D_KV].astype(BF16), nw[:, D_KV:].astype(BF16), (-j_n).astype(F32), mask_nw, False)]
    ocw, sel = _cmp_win_topk(q_all, nq, qpos, gn_ref[...], kcc_ref[0], vcc_ref[0], mband_ref[...],
                             pieces, nb, ns, TOP_N - 1)
    ocw_ref[...] = ocw
    sel_ref[0, 0] = sel[:nq]
    sel_ref[0, 1] = sel[nq:]


def _e1_sample(qp, gn, kcc, vcc, mband, cache_win, win_off, kv_win, t, bsz, nq, past):
    nb, ns = mband.shape
    w_buf = cache_win.shape[2]
    off = t // nq
    row = lambda w: pl.BlockSpec((nq, w), lambda b: (off + b, 0))
    per_b = lambda shape: pl.BlockSpec((1,) + shape, lambda b: (b,) + (0,) * len(shape))
    return pl.pallas_call(
        functools.partial(_e1_sample_kernel, nb=nb, ns=ns, nq=nq, past=past),
        out_shape=[jax.ShapeDtypeStruct((bsz * nq, Q_PAD), F32), jax.ShapeDtypeStruct((bsz, N_KV, nq, ns), F32)],
        grid=(bsz,),
        in_specs=[row(Q_PAD), row(LANE), per_b((nb, D_KV)), per_b((nb, D_KV)), _const_spec(mband.shape),
                  pl.BlockSpec((1, 2 * D_KV, w_buf), lambda b: (win_off + b, 0, 0)), row(2 * D_KV)],
        out_specs=[pl.BlockSpec((nq, Q_PAD), lambda b: (b, 0)), per_b((N_KV, nq, ns))],
        compiler_params=_cparams("parallel"),
    )(qp, gn, kcc, vcc, mband, cache_win, kv_win)


def _online_step(s, kdist, masks, v_ones, m_ref, acc_ref, tq, feature_major=False):
    reps = s.shape[1] // LANE
    ps, alphas = [], []
    for h in range(N_HEADS):
        rows = slice(h * tq, (h + 1) * tq)
        sm = jnp.where(masks[h // HPG], s[rows] - _slope2(h) * kdist, NEG)
        m_old = m_ref[rows, :]
        m_new = jnp.maximum(m_old, jnp.max(sm, axis=-1, keepdims=True))
        alphas.append(jnp.exp2(m_old - m_new))
        ps.append(jnp.exp2(sm - jnp.tile(m_new, (1, reps))))
        m_ref[rows, :] = m_new
    p_all = jnp.concatenate(ps, axis=0).astype(BF16)
    a_all = jnp.concatenate(alphas, axis=0)
    pv = _dot_nt(p_all, v_ones) if feature_major else _dot(p_all, v_ones)
    acc_ref[...] = jnp.tile(a_all, (1, 2)) * acc_ref[...] + pv


def _expand_sel(sel, first_blk, n_keys):
    ns = sel.shape[1]
    j_io = lax.broadcasted_iota(jnp.int32, (ns, n_keys), 0)
    kb = first_blk + jnp.right_shift(lax.broadcasted_iota(jnp.int32, (ns, n_keys), 1), 6)
    e = jnp.where(j_io == kb, 1.0, 0.0).astype(BF16)
    return _dot(sel.astype(BF16), e) > 0.5


def _init_online(m_ref, acc_ref):
    m_ref[...] = jnp.full(m_ref.shape, HALF_NEG, F32)
    acc_ref[...] = jnp.zeros(acc_ref.shape, F32)


def _finish_online(ocw, gn, acc_ref, tq):
    o_s = acc_ref[:, 0:LANE] * (1.0 / jnp.maximum(acc_ref[:, LANE:], 1e-30))
    outs = []
    for h in range(N_HEADS):
        g1 = gn[:, N_HEADS + h:N_HEADS + h + 1]
        outs.append(ocw[:, h * LANE:(h + 1) * LANE] + g1 * o_s[h * tq:(h + 1) * tq])
    return jnp.concatenate(outs, axis=1)


def _e2_prompt_kernel(order_ref, cnt_ref, qp_ref, sel_ref, gn_ref, ocw_ref, kvs_ref, out_ref, m_ref, acc_ref,
                      *, nchunks, ns, nt):
    tq = Q_TILE
    i = pl.program_id(0)
    s0 = i * tq
    qpos = s0 + lax.broadcasted_iota(jnp.int32, (tq, 1), 0)
    q_all = _heads_rows(qp_ref[...])
    _init_online(m_ref, acc_ref)
    n_act = jnp.where(i < nt, cnt_ref[jnp.minimum(i, nt - 1)], 0)
    blocks_per_chunk = SEL_CHUNK // SEL_BLOCK

    def body(j, carry):
        c1 = order_ref[i * nchunks + 2 * j]
        paired = 2 * j + 1 < n_act
        c2 = jnp.where(paired, order_ref[i * nchunks + jnp.minimum(2 * j + 1, nchunks - 1)], c1)
        ks, vs, dists, masks = [], [], [], [[] for _ in range(N_KV)]
        for c, first_blk in ((c1, c1 * blocks_per_chunk), (c2, jnp.where(paired, c2 * blocks_per_chunk, ns))):
            base = pl.multiple_of(c * SEL_CHUNK, SEL_CHUNK)
            ks.append(kvs_ref[pl.ds(base, SEL_CHUNK), 0:D_KV])
            vs.append(kvs_ref[pl.ds(base, SEL_CHUNK), D_KV:])
            kpos = base + lax.broadcasted_iota(jnp.int32, (1, SEL_CHUNK), 1)
            dists.append((s0 - kpos).astype(F32))
            for g in range(N_KV):
                masks[g].append(_expand_sel(sel_ref[g], first_blk, SEL_CHUNK) & (qpos >= kpos))
        s = _dot_nt(q_all, jnp.concatenate(ks, axis=0))
        _online_step(s, jnp.concatenate(dists, axis=1), [jnp.concatenate(mg, axis=1) for mg in masks],
                     jnp.concatenate(vs, axis=0), m_ref, acc_ref, tq)
        return carry

    lax.fori_loop(0, (n_act + 1) // 2, body, 0)
    out_ref[...] = _finish_online(ocw_ref[...], gn_ref[...], acc_ref, tq)


def _e2_prompt(order, cnt, qp, sel, gn, ocw, kvs_bf, t):
    ns = sel.shape[2]
    nt = t // Q_TILE
    nchunks = order.shape[0] // nt
    row = lambda w: pl.BlockSpec((Q_TILE, w), lambda i, o, c: (i, 0))
    prow = lambda w: pl.BlockSpec((Q_TILE, w), lambda i, o, c: (jnp.minimum(i, nt - 1), 0))
    grid_spec = pltpu.PrefetchScalarGridSpec(
        num_scalar_prefetch=2,
        grid=(qp.shape[0] // Q_TILE,),
        in_specs=[row(Q_PAD), pl.BlockSpec((N_KV, Q_TILE, ns), lambda i, o, c: (0, jnp.minimum(i, nt - 1), 0)),
                  row(LANE), prow(Q_PAD), pl.BlockSpec(kvs_bf.shape, lambda i, o, c: (0, 0))],
        out_specs=row(Q_PAD),
        scratch_shapes=[pltpu.VMEM((N_HEADS * Q_TILE, LANE), F32), pltpu.VMEM((N_HEADS * Q_TILE, 2 * LANE), F32)],
    )
    return pl.pallas_call(
        functools.partial(_e2_prompt_kernel, nchunks=nchunks, ns=ns, nt=nt),
        out_shape=jax.ShapeDtypeStruct((qp.shape[0], Q_PAD), F32),
        grid_spec=grid_spec,
        compiler_params=_cparams("arbitrary"),
    )(order, cnt, qp, sel, gn, ocw, kvs_bf)


def _e2_sample_kernel(pt_ref, live_ref, pages_ref, qp_ref, sel_ref, gn_ref, ocw_ref, nsel_ref, expand_ref,
                      oatt_all_ref, out_ref, buf, sem, m_ref, acc_ref, *, pps, nsplit, nq):
    s = pl.program_id(0)
    nsteps = pl.num_programs(0)
    n_keys = pps * PAGE_SIZE

    def page_copy(step, p, slot):
        b = step // nsplit
        part = step % nsplit
        pg = pt_ref[b, part * pps + p]
        col = pl.multiple_of(p * PAGE_SIZE, PAGE_SIZE)
        return pltpu.make_async_copy(pages_ref.at[pg], buf.at[slot, :, pl.ds(col, PAGE_SIZE)], sem.at[slot])

    def start(step, slot):
        def body(p, c):
            page_copy(step, p, slot).start()
            return c
        lax.fori_loop(0, pps, body, 0)

    def wait(step, slot):
        def body(p, c):
            page_copy(step, p, slot).wait()
            return c
        lax.fori_loop(0, pps, body, 0)

    @pl.when((s == 0) & (live_ref[0] > 0))
    def _():
        start(s, 0)

    @pl.when((s + 1 < nsteps) & (live_ref[jnp.minimum(s + 1, nsteps - 1)] > 0))
    def _():
        start(s + 1, (s + 1) % 2)

    slot = s % 2
    part = s % nsplit
    qi = lax.broadcasted_iota(jnp.int32, (nq, 1), 0)
    q_all = _heads_rows(qp_ref[...])

    @pl.when(part == 0)
    def _():
        _init_online(m_ref, acc_ref)

    @pl.when(live_ref[s] > 0)
    def _():
        wait(s, slot)
        k = buf[slot, 0:D_KV, :].astype(BF16)
        v_ones = jnp.concatenate([buf[slot, D_KV:, :].astype(BF16), jnp.ones((D_KV, n_keys), BF16)], axis=0)
        sc = _dot(q_all, k)
        back = (nsplit - part) * n_keys - lax.broadcasted_iota(jnp.int32, (1, n_keys), 1)
        masks = [_dot(sel_ref[0, g, 0].astype(BF16), expand_ref[...]) > 0.5 for g in range(N_KV)]
        _online_step(sc, back.astype(F32), masks, v_ones, m_ref, acc_ref, nq, feature_major=True)

    @pl.when(part == nsplit - 1)
    def _():
        nw = _pad_rows(nsel_ref[...], LANE)
        j_n = lax.broadcasted_iota(jnp.int32, (1, LANE), 1)
        mask_n = (qi >= j_n) & (j_n < nq)
        s_n = _dot_nt(q_all, nw[:, 0:D_KV].astype(BF16))
        vn_ones = jnp.concatenate([nw[:, D_KV:].astype(BF16), jnp.ones((LANE, D_KV), BF16)], axis=1)
        _online_step(s_n, (-j_n).astype(F32), [mask_n, mask_n], vn_ones, m_ref, acc_ref, nq)
        out_ref[...] = _finish_online(ocw_ref[...], gn_ref[...], acc_ref, nq)


def _e2_sample(page_table, pages, qp, sel, gn, ocw, kv_sel, oatt, t, nq):
    bsz, n_pages = page_table.shape
    ns = sel.shape[3]
    nsplit = 4
    pps = n_pages // nsplit
    n_keys = pps * PAGE_SIZE
    nblk = ns // nsplit
    off = t // nq
    sel_parts = sel.reshape(bsz, N_KV, nq, nsplit, nblk).transpose(0, 1, 3, 2, 4)
    live = (jnp.max(sel_parts, axis=(1, 3, 4)) > 0).astype(jnp.int32).reshape(-1)
    expand = (jnp.arange(nblk)[:, None] == jnp.arange(n_keys)[None, :] // SEL_BLOCK).astype(BF16)
    row = lambda w: pl.BlockSpec((nq, w), lambda s, pt, lv: (off + s // nsplit, 0))
    grid_spec = pltpu.PrefetchScalarGridSpec(
        num_scalar_prefetch=2,
        grid=(bsz * nsplit,),
        in_specs=[pl.BlockSpec(memory_space=pl.ANY), row(Q_PAD),
                  pl.BlockSpec((1, N_KV, 1, nq, nblk), lambda s, pt, lv: (s // nsplit, 0, s % nsplit, 0, 0)),
                  row(LANE), pl.BlockSpec((nq, Q_PAD), lambda s, pt, lv: (s // nsplit, 0)), row(2 * D_KV),
                  pl.BlockSpec((nblk, n_keys), lambda s, pt, lv: (0, 0), pipeline_mode=pl.Buffered(1)),
                  pl.BlockSpec(memory_space=pl.ANY)],
        out_specs=row(Q_PAD),
        scratch_shapes=[pltpu.VMEM((2, 2 * D_KV, n_keys), F32), pltpu.SemaphoreType.DMA((2,)),
                        pltpu.VMEM((N_HEADS * nq, LANE), F32), pltpu.VMEM((N_HEADS * nq, 2 * LANE), F32)],
    )
    return pl.pallas_call(
        functools.partial(_e2_sample_kernel, pps=pps, nsplit=nsplit, nq=nq),
        out_shape=jax.ShapeDtypeStruct(oatt.shape, F32),
        grid_spec=grid_spec,
        input_output_aliases={9: 0},
        compiler_params=_cparams("arbitrary"),
    )(page_table, live, pages, qp, sel_parts, gn, ocw, kv_sel, expand, oatt)


def _post_kernel(h_ref, yg_ref, oatt_ref, ga_ref, gb_ref, wglu_ref, watt_ref, wout_ref, o_ref):
    gl = _dot(yg_ref[...], wglu_ref[...])
    br_a = gl[:, :D_MODEL] * jax.nn.sigmoid(gl[:, D_MODEL:])
    br_b = _dot(oatt_ref[...].astype(BF16), watt_ref[...])
    merged = (ga_ref[...] * br_a + gb_ref[...] * br_b).astype(BF16)
    o_ref[...] = h_ref[...] + _dot(merged, wout_ref[...])


def _post(h, yg, oatt, ga, gb, wglu, watt, wout):
    n = h.shape[0]
    tm = _pick_tile(n, 256)
    row = lambda w: pl.BlockSpec((tm, w), lambda i: (i, 0))
    return pl.pallas_call(
        _post_kernel,
        out_shape=jax.ShapeDtypeStruct((n, D_MODEL), F32),
        grid=(n // tm,),
        in_specs=[row(D_MODEL), row(D_SSM), row(Q_PAD), row(D_MODEL), row(D_MODEL),
                  _const_spec(wglu.shape), _const_spec(watt.shape), _const_spec(wout.shape)],
        out_specs=row(D_MODEL),
        compiler_params=_cparams("parallel"),
    )(h, yg, oatt, ga, gb, wglu, watt, wout)


def _head_pad_index():
    h = jnp.arange(N_HEADS)[:, None]
    d = jnp.arange(HEAD_DIM)[None, :]
    return (LANE * h + HEAD_DIM * (h // HPG) + d).reshape(-1)


def _prep_mix_weights(w_in, qk_norm):
    idx = _head_pad_index()
    wq = jnp.zeros((D_MODEL, Q_PAD), F32).at[:, idx].set(w_in[:, D_SSM:D_SSM + N_HEADS * HEAD_DIM])
    c0 = D_SSM + N_HEADS * HEAD_DIM
    c1 = c0 + 6 * D_KV
    c2 = c1 + 3 * N_HEADS
    wgn = jnp.zeros((D_MODEL, LANE), F32).at[:, :3 * N_HEADS].set(w_in[:, c1:c2])
    w = jnp.concatenate([w_in[:, :D_SSM], wq, w_in[:, c0:c1], wgn, w_in[:, c2:]], axis=1).astype(BF16)
    gq = jnp.zeros((Q_PAD,), F32).at[idx].set(jnp.tile(qk_norm[0] * (HEAD_DIM ** -0.5 * LOG2E), N_HEADS))[None]
    gks = jnp.tile(qk_norm[2], N_KV)[None]
    gkw = jnp.tile(qk_norm[3], N_KV)[None]
    return w, gq, gks, gkw


def _prep_att_out(w_att_out):
    return jnp.zeros((Q_PAD, D_MODEL), F32).at[_head_pad_index()].set(w_att_out).astype(BF16)


def _prep_compress(pe_k, pe_v, w_k, w_v, gain_k):
    nj = CMP_BLOCK // CMP_STRIDE
    eye = jnp.eye(N_KV, dtype=F32)

    def pair_maps(w):
        w5 = w.reshape(nj, N_SPAIR, 2, HEAD_DIM, HEAD_DIM)
        return jnp.einsum("jpsde,gh->psgdjhe", w5, eye).reshape(N_SPAIR, 2 * D_KV, 2 * D_KV)

    def pair_pe(pe):
        p5 = pe.reshape(nj, N_SPAIR, 2, 1, HEAD_DIM)
        return jnp.broadcast_to(p5, (nj, N_SPAIR, 2, N_KV, HEAD_DIM)).reshape(nj, N_SPAIR, 2 * D_KV)

    w = jnp.stack([pair_maps(w_k), pair_maps(w_v)], axis=0)
    pe = jnp.stack([pair_pe(pe_k), pair_pe(pe_v)], axis=1).reshape(-1, 2 * D_KV)
    r = jnp.arange(2 * PAGE_SIZE)
    perm = (r[None, :] == (CMP_STRIDE * (r % CMP_STRIDE) + r // CMP_STRIDE)[:, None]).astype(BF16)
    return {"w": w.astype(BF16), "pe": pe,
            "gk": jnp.tile(gain_k, N_KV)[None], "perm": perm}


def _prep_s5(a_re, a_im, log_dt, b_re, b_im, c_re, c_im, d, nq):
    dt = jnp.exp(log_dt)[:, None]
    mag = jnp.exp(a_re * dt)
    lr = mag * jnp.cos(a_im * dt)
    li = mag * jnp.sin(a_im * dt)
    den = a_re * a_re + a_im * a_im
    fr = ((lr - 1.0) * a_re + li * a_im) / den
    fi = (li * a_re - (lr - 1.0) * a_im) / den
    bbr = fr[..., None] * b_re - fi[..., None] * b_im
    bbi = fr[..., None] * b_im + fi[..., None] * b_re
    eye = jnp.eye(N_SSM_GROUPS, dtype=F32)
    blk_b = lambda m: jnp.einsum("gpc,gh->gchp", m, eye).reshape(D_SSM, N_STATE)
    blk_c = lambda m: jnp.einsum("gcp,gh->gphc", m, eye).reshape(N_STATE, D_SSM)

    def lam_pow(k):
        kk = k.astype(F32)[:, None, None]
        m = jnp.exp(a_re * dt * kk)
        th = a_im * dt * kk
        return (m * jnp.cos(th)).reshape(-1, N_STATE), (m * jnp.sin(th)).reshape(-1, N_STATE)

    def step_table(seg):
        ks = []
        dd = 1
        while dd < seg:
            ks.append(dd)
            dd *= 2
        re, im = lam_pow(jnp.array(ks))
        live = jnp.arange(seg)[None, :, None] >= jnp.array(ks)[:, None, None]
        tab = jnp.stack([jnp.where(live, re[:, None, :], 0.0), jnp.where(live, im[:, None, :], 0.0)], axis=1)
        return tab.reshape(-1, N_STATE)

    sp = {"bmat": jnp.concatenate([blk_b(bbr), blk_b(bbi)], axis=1).astype(BF16),
          "cr": blk_c(c_re).astype(BF16), "ci": (-blk_c(c_im)).astype(BF16), "d": d[None],
          "lam_p": step_table(SCAN_SEG), "lam_s": step_table(nq)}
    sp["pr_p"], sp["pi_p"] = lam_pow(jnp.arange(SCAN_SEG) + 1)
    return sp, lam_pow


def _band_matrix(nb, ns):
    ratio = SEL_BLOCK // CMP_STRIDE
    lo = CMP_BLOCK // CMP_STRIDE - 1
    c = jnp.arange(nb)[:, None]
    j = jnp.arange(ns)[None, :]
    return ((c >= ratio * j - lo) & (c <= ratio * j + ratio - 1)).astype(BF16)


def kernel(x_prompt, x_sample, cache_kv_cmp, cache_kv_sel, cache_kv_win, state_ssm_re, state_ssm_im, page_table, p_prompt, p_sample, norm_ffn1, w_ffn1_in, w_ffn1_out, norm_mix, w_in, qk_norm, ssm_a_re, ssm_a_im, ssm_log_dt, ssm_b_re, ssm_b_im, ssm_c_re, ssm_c_im, ssm_d, w_glu, cmp_pe_k, cmp_pe_v, cmp_w_k, cmp_w_v, w_att_out, w_out, norm_ffn2, w_ffn2_in, w_ffn2_out, norm_ple, w_ple_gate, w_ple_proj):
    bp, t = x_prompt.shape[:2]
    bsz, nq = x_sample.shape[:2]
    n_pages = page_table.shape[1]
    past = n_pages * PAGE_SIZE
    n_pool = cache_kv_cmp.shape[1]
    w_buf = cache_kv_win.shape[2]
    ns_rows = bsz * nq
    assert bp == 1 and t % SEL_CHUNK == 0 and t >= WIN_KEYS and t % ns_rows == 0 and nq < CMP_STRIDE
    assert past == t and w_buf == WINDOW and ns_rows % Q_TILE == 0
    nb = t // CMP_STRIDE
    ns = t // SEL_BLOCK
    nt = t // Q_TILE
    nchunks = t // SEL_CHUNK
    mband = _band_matrix(nb, ns)
    page_table = page_table.astype(jnp.int32)
    feat_major = lambda c: jnp.transpose(c, (0, 1, 3, 4, 5, 2)).reshape(DEPTH * c.shape[1], 2 * D_KV, c.shape[2])
    pages_cmp = feat_major(cache_kv_cmp)
    pages_sel = feat_major(cache_kv_sel)
    pages_win = feat_major(cache_kv_win)

    h = (x_prompt[0], x_sample.reshape(ns_rows, D_MODEL))
    p_all = (p_prompt.reshape(DEPTH * t, D_PLE), p_sample.reshape(DEPTH * ns_rows, D_PLE))
    st_p = [[] for _ in range(5)]
    st_s = [[] for _ in range(5)]
    for i in range(DEPTH):
        row1 = lambda a: a[i][None]
        h = _ffn(h, row1(norm_ffn1), w_ffn1_in[i].astype(BF16), w_ffn1_out[i].astype(BF16), t, ns_rows)
        w_mix, gq, gks, gkw = _prep_mix_weights(w_in[i], qk_norm[i])
        u, qp, kv_cmp, kv_sel, kv_win, kvs_bf, kvw_bf, gn, ga, gb = _mix_in(h, row1(norm_mix), w_mix, gq, gks, gkw)

        sp, lam_pow = _prep_s5(ssm_a_re[i], ssm_a_im[i], ssm_log_dt[i], ssm_b_re[i], ssm_b_im[i],
                               ssm_c_re[i], ssm_c_im[i], ssm_d[i], nq)
        sp["pr_s"], sp["pi_s"] = lam_pow(jnp.arange(ns_rows) % nq + 1)
        yg, hr_p, hi_p = _s5_prompt(u, t, sp)
        h0r = jnp.repeat(state_ssm_re[i].reshape(bsz, N_STATE), nq, axis=0)
        h0i = jnp.repeat(state_ssm_im[i].reshape(bsz, N_STATE), nq, axis=0)
        yg, hr_s, hi_s = _s5_sample(u, yg, t, ns_rows, nq, h0r, h0i, sp)

        cw = _prep_compress(cmp_pe_k[i], cmp_pe_v[i], cmp_w_k[i], cmp_w_v[i], qk_norm[i, 1])
        layer_pages = page_table + i * n_pool
        kcc_p, vcc_p = _compress_rows(kv_cmp, t, cw)
        kcc_s, vcc_s = _compress_paged(layer_pages, pages_cmp, cw)
        ocw_p, sel_p, flags = _e1_prompt(qp, gn, kcc_p, vcc_p, mband, kvw_bf, t)
        ocw_s, sel_s = _e1_sample(qp, gn, kcc_s, vcc_s, mband, pages_win, i * bsz, kv_win, t, bsz, nq, past)
        idle = flags[:, 0, :nchunks] == 0
        order = jnp.argsort(idle, axis=1, stable=True).astype(jnp.int32).reshape(-1)
        cnt = (nchunks - jnp.sum(idle, axis=1)).astype(jnp.int32)
        oatt = _e2_prompt(order, cnt, qp, sel_p, gn, ocw_p, kvs_bf, t)
        oatt = _e2_sample(layer_pages, pages_sel, qp, sel_s, gn, ocw_s, kv_sel, oatt, t, nq)

        h = _post(h, yg, oatt, ga, gb,
                  w_glu[i].astype(BF16), _prep_att_out(w_att_out[i]), w_out[i].astype(BF16))
        h = _ffn(h, row1(norm_ffn2), w_ffn2_in[i].astype(BF16), w_ffn2_out[i].astype(BF16), t, ns_rows,
                 (p_all, row1(norm_ple), w_ple_gate[i].astype(BF16), w_ple_proj[i].astype(BF16)), p_layer=i)

        kv5 = lambda a, lead: a.reshape(lead + (2, N_KV, HEAD_DIM))
        st_p[0].append(kv5(kv_cmp[:t], (1, t)))
        st_p[1].append(kv5(kv_sel[:t], (1, t)))
        st_p[2].append(kv5(kv_win[t - min(WINDOW, t):t], (1, min(WINDOW, t))))
        st_p[3].append(hr_p.reshape(1, N_SSM_GROUPS, P_STATE))
        st_p[4].append(hi_p.reshape(1, N_SSM_GROUPS, P_STATE))
        st_s[0].append(kv5(kv_cmp[t:], (bsz, nq)))
        st_s[1].append(kv5(kv_sel[t:], (bsz, nq)))
        st_s[2].append(jnp.concatenate([cache_kv_win[i, :, nq:], kv5(kv_win[t:], (bsz, nq))], axis=1))
        st_s[3].append(hr_s[nq - 1::nq].reshape(bsz, N_SSM_GROUPS, P_STATE))
        st_s[4].append(hi_s[nq - 1::nq].reshape(bsz, N_SSM_GROUPS, P_STATE))

    outs_p = [jnp.stack(a) for a in st_p]
    outs_s = [jnp.stack(a) for a in st_s]
    y_prompt = h[:t][None]
    y_sample = h[t:].reshape(bsz, nq, D_MODEL)
    return (y_prompt, y_sample, *outs_p, *outs_s)
```

```python
import functools

import jax
import jax.numpy as jnp
from jax import lax
from jax.experimental import pallas as pl
from jax.experimental.pallas import tpu as pltpu

F32 = jnp.float32
BF16 = jnp.bfloat16

D_MODEL = 1024
DEPTH = 2
D_SSM = 512
SSM_GROUP = 16
N_SSM_GROUPS = 32
P_STATE = 64
N_STATE = N_SSM_GROUPS * P_STATE
N_HEADS = 8
HEAD_DIM = 64
N_KV = 2
HPG = 4
D_KV = 128
CMP_BLOCK = 32
CMP_STRIDE = 16
SEL_BLOCK = 64
TOP_N = 16
WINDOW = 512
PAGE_SIZE = 128
SEL_BIG = 1e4
D_FF = 2816
D_PLE = 256
RMS_EPS = 1e-6
NEG = -1e30
HALF_NEG = -0.5e30
LOG2E = 1.4426950408889634

LANE = 128
Q_TILE = 128
E1_TILES = 1
SCAN_SEG = 8
SEL_CHUNK = 256
WIN_KEYS = WINDOW + Q_TILE
Q_PAD = N_HEADS * LANE
VMEM_LIMIT = 56 * 2 ** 20


def _cparams(*sem):
    return pltpu.CompilerParams(dimension_semantics=sem, vmem_limit_bytes=VMEM_LIMIT)


def _dot(a, b):
    return jnp.dot(a, b, preferred_element_type=F32)


def _dot_nt(a, b):
    return lax.dot_general(a, b, (((1,), (1,)), ((), ())), preferred_element_type=F32)


def _pick_tile(n, target):
    for t in range(min(n, target), 15, -1):
        if n % t == 0 and t % 16 == 0:
            return t
    raise ValueError(f"no row tile for {n}")


def _const_spec(shape):
    nd = len(shape)
    return pl.BlockSpec(shape, lambda *_: (0,) * nd, pipeline_mode=pl.Buffered(1))


def _rms(x, g):
    ms = jnp.mean(x * x, axis=-1, keepdims=True)
    return x * lax.rsqrt(ms + RMS_EPS) * g


def _halfnorm(x, gain):
    lo = lax.broadcasted_iota(jnp.int32, (1, LANE), 1) < HEAD_DIM
    x2 = x * x
    s_lo = jnp.sum(jnp.where(lo, x2, 0.0), axis=-1, keepdims=True)
    s_hi = jnp.sum(jnp.where(lo, 0.0, x2), axis=-1, keepdims=True)
    ms = jnp.where(lo, s_lo, s_hi) * (1.0 / HEAD_DIM)
    return x * lax.rsqrt(ms + RMS_EPS) * gain


def _row_tile(refs, lead_tiles):
    if len(refs) == 1:
        return refs[0][...]
    return jnp.where(pl.program_id(0) < lead_tiles, refs[0][...], refs[1][...])


def _ffn_kernel(*refs, ple, h_parts, p_parts, lead_tiles):
    refs = list(refs)
    h_refs = [refs.pop(0) for _ in range(h_parts)]
    if ple:
        g_ref, wi_ref, wo_ref = refs[:3]
        p_refs = refs[3:3 + p_parts]
        gp_ref, wg_ref, wp_ref, o_ref = refs[3 + p_parts:]
    else:
        g_ref, wi_ref, wo_ref, o_ref = refs
    h = _row_tile(h_refs, lead_tiles)
    xn = _rms(h, g_ref[...]).astype(BF16)
    a = _dot(xn, wi_ref[:, :D_FF])
    b = _dot(xn, wi_ref[:, D_FF:])
    act = (a * jax.nn.sigmoid(a) * b).astype(BF16)
    h = h + 0.5 * _dot(act, wo_ref[...])
    if ple:
        xg = _rms(h, gp_ref[...]).astype(BF16)
        gate = jax.nn.sigmoid(_dot(xg, wg_ref[...]))
        h = h + gate * _dot(_row_tile(p_refs, lead_tiles).astype(BF16), wp_ref[...])
    o_ref[...] = h


def _ffn(h, g, wi, wo, n_prompt, n_sample, ple_args=None, p_layer=0):
    h_parts = list(h) if isinstance(h, (tuple, list)) else [h]
    n = n_prompt + n_sample
    tm = _pick_tile(n, 256)
    assert n_prompt % tm == 0 and n_sample % tm == 0
    lead_tiles = n_prompt // tm
    row = lambda w: pl.BlockSpec((tm, w), lambda i: (i, 0))

    def pair_specs(w, off_p, off_s):
        return [pl.BlockSpec((tm, w), lambda i: (off_p + jnp.minimum(i, lead_tiles - 1), 0)),
                pl.BlockSpec((tm, w), lambda i: (off_s + jnp.maximum(i - lead_tiles, 0), 0))]

    in_specs = (pair_specs(D_MODEL, 0, 0) if len(h_parts) == 2 else [row(D_MODEL)]) + [
        _const_spec((1, D_MODEL)), _const_spec(wi.shape), _const_spec(wo.shape)]
    args = h_parts + [g, wi, wo]
    p_parts = []
    if ple_args is not None:
        p_parts, gp, wg, wp = ple_args
        p_parts = list(p_parts)
        in_specs += pair_specs(D_PLE, p_layer * lead_tiles, p_layer * (n_sample // tm)) + [
            _const_spec((1, D_MODEL)), _const_spec(wg.shape), _const_spec(wp.shape)]
        args += p_parts + [gp, wg, wp]
    return pl.pallas_call(
        functools.partial(_ffn_kernel, ple=ple_args is not None, h_parts=len(h_parts), p_parts=len(p_parts),
                          lead_tiles=lead_tiles),
        out_shape=jax.ShapeDtypeStruct((n, D_MODEL), F32),
        grid=(n // tm,),
        in_specs=in_specs,
        out_specs=row(D_MODEL),
        compiler_params=_cparams("parallel"),
    )(*args)


_C_U = 0
_C_Q = _C_U + D_SSM
_C_KV = _C_Q + Q_PAD
_C_GN = _C_KV + 6 * D_KV
_C_GA = _C_GN + LANE
_C_GB = _C_GA + D_MODEL
_C_END = _C_GB + D_MODEL


def _mix_in_kernel(h_ref, g_ref, w_ref, gq_ref, gks_ref, gkw_ref,
                   u_ref, qp_ref, kvc_ref, kvs_ref, kvw_ref, kvsb_ref, kvwb_ref, gn_ref, ga_ref, gb_ref):
    xn = _rms(h_ref[...], g_ref[...]).astype(BF16)
    u_ref[...] = _dot(xn, w_ref[:, _C_U:_C_Q])
    zq = _dot(xn, w_ref[:, _C_Q:_C_KV])
    for h in range(N_HEADS):
        qh = zq[:, h * LANE:(h + 1) * LANE]
        ms = jnp.sum(qh * qh, axis=-1, keepdims=True) * (1.0 / HEAD_DIM)
        qn = qh * lax.rsqrt(ms + RMS_EPS) * gq_ref[:, h * LANE:(h + 1) * LANE]
        qp_ref[:, h * LANE:(h + 1) * LANE] = qn
    zkv = _dot(xn, w_ref[:, _C_KV:_C_GN])
    kvc_ref[...] = zkv[:, 0:2 * D_KV]
    ks = _halfnorm(zkv[:, 2 * D_KV:3 * D_KV], gks_ref[...])
    vs = zkv[:, 3 * D_KV:4 * D_KV]
    kw = _halfnorm(zkv[:, 4 * D_KV:5 * D_KV], gkw_ref[...])
    vw = zkv[:, 5 * D_KV:6 * D_KV]
    kvs_ref[:, 0:D_KV] = ks
    kvs_ref[:, D_KV:] = vs
    kvw_ref[:, 0:D_KV] = kw
    kvw_ref[:, D_KV:] = vw
    kvsb_ref[:, 0:D_KV] = ks.astype(BF16)
    kvsb_ref[:, D_KV:2 * D_KV] = vs.astype(BF16)
    kvsb_ref[:, 2 * D_KV:] = jnp.ones((ks.shape[0], D_KV), BF16)
    kvwb_ref[:, 0:D_KV] = kw.astype(BF16)
    kvwb_ref[:, D_KV:] = vw.astype(BF16)
    gn_ref[...] = jax.nn.sigmoid(_dot(xn, w_ref[:, _C_GN:_C_GA]))
    ga_ref[...] = jax.nn.sigmoid(_dot(xn, w_ref[:, _C_GA:_C_GB]))
    gb_ref[...] = jax.nn.sigmoid(_dot(xn, w_ref[:, _C_GB:_C_END]))


def _mix_in(h, g, w, gq, gks, gkw):
    n = h.shape[0]
    tm = _pick_tile(n, 256)
    row = lambda width: pl.BlockSpec((tm, width), lambda i: (i, 0))
    widths = [(D_SSM, F32), (Q_PAD, F32), (2 * D_KV, F32), (2 * D_KV, F32), (2 * D_KV, F32),
              (3 * D_KV, BF16), (2 * D_KV, BF16), (LANE, F32), (D_MODEL, F32), (D_MODEL, F32)]
    return pl.pallas_call(
        _mix_in_kernel,
        out_shape=[jax.ShapeDtypeStruct((n, wd), dt) for wd, dt in widths],
        grid=(n // tm,),
        in_specs=[row(D_MODEL), _const_spec((1, D_MODEL)), _const_spec(w.shape),
                  _const_spec((1, Q_PAD)), _const_spec((1, LANE)), _const_spec((1, LANE))],
        out_specs=[row(wd) for wd, _ in widths],
        compiler_params=_cparams("parallel"),
    )(h, g, w, gq, gks, gkw)


def _gelu_tanh(x):
    return 0.5 * x * (1.0 + jnp.tanh(0.7978845608028654 * (x + 0.044715 * (x * x * x))))


def _s5_kernel(*refs, seg, carry, seq_steps=None):
    if carry:
        (u_ref, bm_ref, cr_ref, ci_ref, d_ref, lam_ref, pr_ref, pi_ref,
         y_ref, hr_out, hi_out, car_ref, cai_ref) = refs
    else:
        (u_ref, bm_ref, cr_ref, ci_ref, d_ref, lam_ref, pr_ref, pi_ref, h0r_ref, h0i_ref, y_all_ref,
         y_ref, hr_out, hi_out) = refs
    u = u_ref[...]
    rows = u.shape[0]
    x = _dot(u.astype(BF16), bm_ref[...])
    xr = x[:, :N_STATE]
    xi = x[:, N_STATE:]
    d, k = 1, 0
    while d < seg:
        lr = jnp.tile(lam_ref[2 * k * seg:(2 * k + 1) * seg, :], (rows // seg, 1))
        li = jnp.tile(lam_ref[(2 * k + 1) * seg:(2 * k + 2) * seg, :], (rows // seg, 1))
        sr = pltpu.roll(xr, d, 0)
        si = pltpu.roll(xi, d, 0)
        xr, xi = xr + lr * sr - li * si, xi + lr * si + li * sr
        d *= 2
        k += 1
    p_r = pr_ref[...]
    p_i = pi_ref[...]
    if carry:
        @pl.when(pl.program_id(0) == 0)
        def _():
            car_ref[...] = jnp.zeros_like(car_ref)
            cai_ref[...] = jnp.zeros_like(cai_ref)
        c_r = car_ref[...]
        c_i = cai_ref[...]
        hr_parts, hi_parts = [], []
        for j in range(rows // seg):
            tr = xr[j * seg:(j + 1) * seg]
            ti = xi[j * seg:(j + 1) * seg]
            hr_j = tr + p_r * c_r - p_i * c_i
            hi_j = ti + p_r * c_i + p_i * c_r
            c_r = hr_j[seg - 1:seg, :]
            c_i = hi_j[seg - 1:seg, :]
            hr_parts.append(hr_j)
            hi_parts.append(hi_j)
        hr = jnp.concatenate(hr_parts, axis=0)
        hi = jnp.concatenate(hi_parts, axis=0)
    else:
        c_r = h0r_ref[...]
        c_i = h0i_ref[...]
        hr = xr + p_r * c_r - p_i * c_i
        hi = xi + p_r * c_i + p_i * c_r
    y = _dot(hr.astype(BF16), cr_ref[...]) + _dot(hi.astype(BF16), ci_ref[...]) + d_ref[...] * u
    y_ref[...] = _gelu_tanh(y).astype(BF16)
    if carry:
        car_ref[...] = hr[rows - 1:rows, :]
        cai_ref[...] = hi[rows - 1:rows, :]

        @pl.when(pl.program_id(0) < seq_steps)
        def _():
            hr_out[...] = hr[rows - 1:rows, :]
            hi_out[...] = hi[rows - 1:rows, :]
    else:
        hr_out[...] = hr
        hi_out[...] = hi


def _s5_prompt(u, t, sp):
    rows = Q_TILE
    consts = [sp["bmat"], sp["cr"], sp["ci"], sp["d"], sp["lam_p"], sp["pr_p"], sp["pi_p"]]
    return pl.pallas_call(
        functools.partial(_s5_kernel, seg=SCAN_SEG, carry=True, seq_steps=t // rows),
        out_shape=[jax.ShapeDtypeStruct((u.shape[0], D_SSM), BF16),
                   jax.ShapeDtypeStruct((1, N_STATE), F32), jax.ShapeDtypeStruct((1, N_STATE), F32)],
        grid=(u.shape[0] // rows,),
        in_specs=[pl.BlockSpec((rows, D_SSM), lambda i: (i, 0))] + [_const_spec(c.shape) for c in consts],
        out_specs=[pl.BlockSpec((rows, D_SSM), lambda i: (i, 0)),
                   _const_spec((1, N_STATE)), _const_spec((1, N_STATE))],
        scratch_shapes=[pltpu.VMEM((1, N_STATE), F32), pltpu.VMEM((1, N_STATE), F32)],
        compiler_params=_cparams("arbitrary"),
    )(u, *consts)


def _s5_sample(u, yg, t, ns_rows, nq, h0r, h0i, sp):
    consts = [sp["bmat"], sp["cr"], sp["ci"], sp["d"], sp["lam_s"], sp["pr_s"], sp["pi_s"], h0r, h0i]
    blk = t // ns_rows
    return pl.pallas_call(
        functools.partial(_s5_kernel, seg=nq, carry=False),
        out_shape=[jax.ShapeDtypeStruct(yg.shape, BF16),
                   jax.ShapeDtypeStruct((ns_rows, N_STATE), F32), jax.ShapeDtypeStruct((ns_rows, N_STATE), F32)],
        grid=(1,),
        in_specs=([pl.BlockSpec((ns_rows, D_SSM), lambda i: (blk, 0))] + [_const_spec(c.shape) for c in consts]
                  + [pl.BlockSpec(memory_space=pl.ANY)]),
        out_specs=[pl.BlockSpec((ns_rows, D_SSM), lambda i: (blk, 0)),
                   _const_spec((ns_rows, N_STATE)), _const_spec((ns_rows, N_STATE))],
        input_output_aliases={len(consts) + 1: 0},
        compiler_params=_cparams("arbitrary"),
    )(u, *consts, yg)


N_SPAIR = CMP_STRIDE // 2


def _compress_part(get_x2, w_ref, a0, a1, part, rows):
    base = pl.multiple_of(part * rows, rows)
    for kv in range(2):
        acc = None
        for sp in range(N_SPAIR):
            term = _dot(get_x2(kv, sp), w_ref[kv, sp])
            acc = term if acc is None else acc + term
        a0[pl.ds(base, rows), kv * D_KV:(kv + 1) * D_KV] = acc[:, 0:D_KV]
        a1[pl.ds(base, rows), kv * D_KV:(kv + 1) * D_KV] = acc[:, D_KV:]


def _compress_finish(a0, a1, w_ref, pe_ref, gk_ref, kcc_ref, vcc_ref, nb):
    bias = []
    for j in range(CMP_BLOCK // CMP_STRIDE):
        halves = []
        for kv in range(2):
            b = jnp.zeros((8, 2 * D_KV), F32)
            for sp in range(N_SPAIR):
                r = (j * 2 + kv) * N_SPAIR + sp
                b = b + _dot(jnp.broadcast_to(pe_ref[r:r + 1, :], (8, 2 * D_KV)).astype(BF16), w_ref[kv, sp])
            halves.append(b[0:1, j * D_KV:(j + 1) * D_KV])
        bias.append(jnp.concatenate(halves, axis=1))
    out = (a0[...] + bias[0]) + pltpu.roll(a1[...] + bias[1], nb - 1, 0)
    valid = lax.broadcasted_iota(jnp.int32, (nb, 1), 0) < nb - 1
    out = jnp.where(valid, out, 0.0)
    kcc_ref[0] = _halfnorm(out[:, 0:D_KV], gk_ref[...]).astype(BF16)
    vcc_ref[0] = out[:, D_KV:].astype(BF16)


def _compress_rows_kernel(xk_ref, xv_ref, w_ref, pe_ref, gk_ref, kcc_ref, vcc_ref, a0, a1, *, nsplit, nb):
    part = pl.program_id(0)
    rows = nb // nsplit

    def get_x2(kv, sp):
        x_ref = (xk_ref, xv_ref)[kv]
        pair = [x_ref[pl.ds(2 * sp + si, rows, stride=CMP_STRIDE), :] for si in range(2)]
        return jnp.concatenate(pair, axis=1).astype(BF16)
    _compress_part(get_x2, w_ref, a0, a1, part, rows)

    @pl.when(part == nsplit - 1)
    def _():
        _compress_finish(a0, a1, w_ref, pe_ref, gk_ref, kcc_ref, vcc_ref, nb)


def _compress_rows(kv_cmp, t, cw):
    nb = t // CMP_STRIDE
    nsplit = 2
    return pl.pallas_call(
        functools.partial(_compress_rows_kernel, nsplit=nsplit, nb=nb),
        out_shape=[jax.ShapeDtypeStruct((1, nb, D_KV), BF16), jax.ShapeDtypeStruct((1, nb, D_KV), BF16)],
        grid=(nsplit,),
        in_specs=[pl.BlockSpec((t // nsplit, D_KV), lambda i: (i, 0)),
                  pl.BlockSpec((t // nsplit, D_KV), lambda i: (i, 1)),
                  _const_spec(cw["w"].shape), _const_spec(cw["pe"].shape), _const_spec((1, LANE))],
        out_specs=[_const_spec((1, nb, D_KV)), _const_spec((1, nb, D_KV))],
        scratch_shapes=[pltpu.VMEM((nb, 2 * D_KV), F32), pltpu.VMEM((nb, 2 * D_KV), F32)],
        compiler_params=_cparams("arbitrary"),
    )(kv_cmp, kv_cmp, cw["w"], cw["pe"], cw["gk"])


def _compress_kernel(pt_ref, pages_ref, w_ref, pe_ref, gk_ref, perm_ref,
                     kcc_ref, vcc_ref, buf, sem, x_scr, a0, a1, *, pps, nsplit, nb):
    s = pl.program_id(0)
    nsteps = pl.num_programs(0)

    def page_copy(step, p, slot):
        b = step // nsplit
        part = step % nsplit
        pg = pt_ref[b, part * pps + p]
        col = pl.multiple_of((p % 2) * PAGE_SIZE, PAGE_SIZE)
        return pltpu.make_async_copy(pages_ref.at[pg], buf.at[slot, p // 2, :, pl.ds(col, PAGE_SIZE)],
                                     sem.at[slot])

    def start(step, slot):
        def body(p, c):
            page_copy(step, p, slot).start()
            return c
        lax.fori_loop(0, pps, body, 0)

    def wait(step, slot):
        def body(p, c):
            page_copy(step, p, slot).wait()
            return c
        lax.fori_loop(0, pps, body, 0)

    @pl.when(s == 0)
    def _():
        start(s, 0)

    @pl.when(s + 1 < nsteps)
    def _():
        start(s + 1, (s + 1) % 2)

    slot = s % 2
    wait(s, slot)
    part = s % nsplit

    def to_rows(q, c):
        xt = buf[slot, q].astype(BF16)
        xp = _dot_nt(perm_ref[...], xt).astype(BF16)
        base = pl.multiple_of(q * CMP_STRIDE, CMP_STRIDE)
        for s2 in range(CMP_STRIDE):
            for kv in range(2):
                x_scr[kv, s2 // 2, pl.ds(base, CMP_STRIDE), (s2 % 2) * D_KV:(s2 % 2 + 1) * D_KV] = (
                    xp[s2 * CMP_STRIDE:(s2 + 1) * CMP_STRIDE, kv * D_KV:(kv + 1) * D_KV])
        return c
    lax.fori_loop(0, pps // 2, to_rows, 0, unroll=4)
    _compress_part(lambda kv, sp: x_scr[kv, sp], w_ref, a0, a1, part, nb // nsplit)

    @pl.when(part == nsplit - 1)
    def _():
        _compress_finish(a0, a1, w_ref, pe_ref, gk_ref, kcc_ref, vcc_ref, nb)


def _compress_paged(page_table, pages, cw):
    bsz, n_pages = page_table.shape
    nb = n_pages * (PAGE_SIZE // CMP_STRIDE)
    nsplit = 2
    pps = n_pages // nsplit
    const = lambda shape: pl.BlockSpec(shape, lambda s, pt: (0,) * len(shape), pipeline_mode=pl.Buffered(1))
    grid_spec = pltpu.PrefetchScalarGridSpec(
        num_scalar_prefetch=1,
        grid=(bsz * nsplit,),
        in_specs=[pl.BlockSpec(memory_space=pl.ANY), const(cw["w"].shape), const(cw["pe"].shape),
                  const((1, LANE)), const(cw["perm"].shape)],
        out_specs=[pl.BlockSpec((1, nb, D_KV), lambda s, pt: (s // nsplit, 0, 0)),
                   pl.BlockSpec((1, nb, D_KV), lambda s, pt: (s // nsplit, 0, 0))],
        scratch_shapes=[pltpu.VMEM((2, pps // 2, 2 * D_KV, 2 * PAGE_SIZE), F32), pltpu.SemaphoreType.DMA((2,)),
                        pltpu.VMEM((2, N_SPAIR, nb // nsplit, 2 * D_KV), BF16),
                        pltpu.VMEM((nb, 2 * D_KV), F32), pltpu.VMEM((nb, 2 * D_KV), F32)],
    )
    return pl.pallas_call(
        functools.partial(_compress_kernel, pps=pps, nsplit=nsplit, nb=nb),
        out_shape=[jax.ShapeDtypeStruct((bsz, nb, D_KV), BF16), jax.ShapeDtypeStruct((bsz, nb, D_KV), BF16)],
        grid_spec=grid_spec,
        compiler_params=_cparams("arbitrary"),
    )(page_table, pages, cw["w"], cw["pe"], cw["gk"], cw["perm"])


def _heads_rows(qp):
    return jnp.concatenate([qp[:, h * LANE:(h + 1) * LANE] for h in range(N_HEADS)], axis=0).astype(BF16)


def _slope2(h):
    return LOG2E * 2.0 ** (-8.0 * (h + 1) / N_HEADS)


def _branch(q_all, tq, pieces):
    s_all = [_dot(q_all, k) if fm else _dot_nt(q_all, k) for k, _, _, _, fm in pieces]
    es, invs = [], []
    for h in range(N_HEADS):
        rows = slice(h * tq, (h + 1) * tq)
        sm = [jnp.where(pc[3], s[rows] - _slope2(h) * pc[2], NEG) for s, pc in zip(s_all, pieces)]
        m = functools.reduce(jnp.maximum, [jnp.max(x, axis=-1, keepdims=True) for x in sm])
        m = jnp.maximum(m, HALF_NEG)
        e = [jnp.exp2(x - m) for x in sm]
        l = functools.reduce(lambda a, b: a + b, [jnp.sum(x, axis=-1, keepdims=True) for x in e])
        es.append(e)
        invs.append(1.0 / jnp.maximum(l, 1e-30))
    o = None
    for i, (_, v, _, _, fm) in enumerate(pieces):
        p = jnp.concatenate([es[h][i] for h in range(N_HEADS)], axis=0).astype(BF16)
        t = _dot_nt(p, v) if fm else _dot(p, v)
        o = t if o is None else o + t
    return o * jnp.concatenate(invs, axis=0), es, invs


def _cmp_win_topk(q_all, tq, qpos, gn, kcc, vcc, mband, win_pieces, nb, ns, topn):
    n_io = lax.broadcasted_iota(jnp.int32, (1, nb), 1)
    kc_end = n_io * CMP_STRIDE + (CMP_BLOCK - 1)
    mask_c = (qpos >= kc_end) & (n_io < nb - 1)
    kdist_c = (qpos[0:1, :] - kc_end).astype(F32)
    o_c, es, invs = _branch(q_all, tq, [(kcc, vcc, kdist_c, mask_c, False)])
    o_w, _, _ = _branch(q_all, tq, win_pieces)
    imps = []
    for g in range(N_KV):
        psum = es[HPG * g][0] * invs[HPG * g]
        for hh in range(1, HPG):
            psum = psum + es[HPG * g + hh][0] * invs[HPG * g + hh]
        hi = psum.astype(BF16)
        r = psum - hi.astype(F32)
        mid = r.astype(BF16)
        lo = (r - mid.astype(F32)).astype(BF16)
        imps.append(_dot(hi, mband) + _dot(mid, mband) + _dot(lo, mband))
    imp = jnp.concatenate(imps, axis=0)
    blk = lax.broadcasted_iota(jnp.int32, (1, ns), 1)
    cur = jnp.concatenate([jnp.right_shift(qpos, 6)] * N_KV, axis=0)
    forced = (blk == 0) | (blk == cur) | (blk == cur - 1)
    score = jnp.where(forced, imp + SEL_BIG, jnp.where(blk <= cur, imp, -SEL_BIG))
    n_rows = score.shape[0]
    pad = (-n_rows) % LANE
    if pad:
        score = jnp.concatenate([score, jnp.zeros((pad, ns), F32)], axis=0)
    score = score.T
    blkf = lax.broadcasted_iota(jnp.int32, (ns, 1), 0).astype(F32)
    for _ in range(topn):
        m = jnp.max(score, axis=0, keepdims=True)
        first = jnp.min(jnp.where(score == m, blkf, float(ns)), axis=0, keepdims=True)
        score = jnp.where(blkf == first, -jnp.inf, score)
    sel = jnp.where(score == -jnp.inf, 1.0, 0.0).T[:n_rows]
    sel = jnp.where(blk <= cur, sel, 0.0)
    outs = []
    for h in range(N_HEADS):
        rows = slice(h * tq, (h + 1) * tq)
        outs.append(gn[:, h:h + 1] * o_c[rows] + gn[:, 2 * N_HEADS + h:2 * N_HEADS + h + 1] * o_w[rows])
    return jnp.concatenate(outs, axis=1), sel


def _e1_prompt_kernel(qp_ref, gn_ref, kcc_ref, vcc_ref, mband_ref, kvw_ref, ocw_ref, sel_ref, flag_ref,
                      *, nb, ns):
    tq = Q_TILE
    for half in range(E1_TILES):
        rows = slice(half * tq, (half + 1) * tq)
        s0 = (pl.program_id(0) * E1_TILES + half) * tq
        qpos = s0 + lax.broadcasted_iota(jnp.int32, (tq, 1), 0)
        q_all = _heads_rows(qp_ref[rows, :])
        start = pl.multiple_of(jnp.maximum(s0 - WINDOW, 0), Q_TILE)
        kw = kvw_ref[pl.ds(start, WIN_KEYS), 0:D_KV]
        vw = kvw_ref[pl.ds(start, WIN_KEYS), D_KV:]
        kpos = start + lax.broadcasted_iota(jnp.int32, (1, WIN_KEYS), 1)
        dist_w = qpos - kpos
        mask_w = (dist_w >= 0) & (dist_w < WINDOW)
        win_pieces = [(kw, vw, (s0 - kpos).astype(F32), mask_w, False)]
        ocw, sel = _cmp_win_topk(q_all, tq, qpos, gn_ref[rows, :], kcc_ref[0], vcc_ref[0], mband_ref[...],
                                 win_pieces, nb, ns, TOP_N)
        ocw_ref[rows, :] = ocw
        sel_ref[0, rows, :] = sel[:tq]
        sel_ref[1, rows, :] = sel[tq:]
        colany = jnp.max(sel, axis=0, keepdims=True)
        j_io = lax.broadcasted_iota(jnp.int32, (ns, LANE), 0)
        c_io = lax.broadcasted_iota(jnp.int32, (ns, LANE), 1)
        grp = jnp.where(jnp.right_shift(j_io, 2) == c_io, 1.0, 0.0).astype(BF16)
        cnt = _dot(jnp.broadcast_to(colany, (8, ns)).astype(BF16), grp)
        flag_ref[half] = (cnt > 0.5).astype(jnp.int32)


def _e1_prompt(qp, gn, kcc, vcc, mband, kvw_bf, t):
    nb, ns = mband.shape
    nt = t // Q_TILE
    step_rows = E1_TILES * Q_TILE
    row = lambda w: pl.BlockSpec((step_rows, w), lambda i: (i, 0))
    return pl.pallas_call(
        functools.partial(_e1_prompt_kernel, nb=nb, ns=ns),
        out_shape=[jax.ShapeDtypeStruct((t, Q_PAD), F32), jax.ShapeDtypeStruct((N_KV, t, ns), F32),
                   jax.ShapeDtypeStruct((nt, 8, LANE), jnp.int32)],
        grid=(nt // E1_TILES,),
        in_specs=[row(Q_PAD), row(LANE), _const_spec((1, nb, D_KV)), _const_spec((1, nb, D_KV)),
                  _const_spec(mband.shape), _const_spec(kvw_bf.shape)],
        out_specs=[row(Q_PAD), pl.BlockSpec((N_KV, step_rows, ns), lambda i: (0, i, 0)),
                   pl.BlockSpec((E1_TILES, 8, LANE), lambda i: (i, 0, 0))],
        compiler_params=_cparams("parallel"),
    )(qp, gn, kcc, vcc, mband, kvw_bf)


def _pad_rows(x, rows):
    return jnp.concatenate([x, jnp.zeros((rows - x.shape[0], x.shape[1]), x.dtype)], axis=0)


def _e1_sample_kernel(qp_ref, gn_ref, kcc_ref, vcc_ref, mband_ref, cwin_ref, nwin_ref, ocw_ref, sel_ref,
                      *, nb, ns, nq, past):
    qi = lax.broadcasted_iota(jnp.int32, (nq, 1), 0)
    qpos = past + qi
    q_all = _heads_rows(qp_ref[...])
    cw = cwin_ref[0]
    w_buf = cw.shape[1]
    j_c = lax.broadcasted_iota(jnp.int32, (1, w_buf), 1)
    dist_cw = qi + (w_buf - j_c)
    mask_cw = (dist_cw >= 0) & (dist_cw < WINDOW)
    nw = _pad_rows(nwin_ref[...], LANE)
    j_n = lax.broadcasted_iota(jnp.int32, (1, LANE), 1)
    mask_nw = (qi >= j_n) & (j_n < nq)
    pieces = [(cw[0:D_KV, :].astype(BF16), cw[D_KV:, :].astype(BF16), (w_buf - j_c).astype(F32), mask_cw, True),
              (nw[:, 0:D_KV].astype(BF16), nw[:, D_KV:].astype(BF16), (-j_n).astype(F32), mask_nw, False)]
    ocw, sel = _cmp_win_topk(q_all, nq, qpos, gn_ref[...], kcc_ref[0], vcc_ref[0], mband_ref[...],
                             pieces, nb, ns, TOP_N - 1)
    ocw_ref[...] = ocw
    sel_ref[0, 0] = sel[:nq]
    sel_ref[0, 1] = sel[nq:]


def _e1_sample(qp, gn, kcc, vcc, mband, cache_win, win_off, kv_win, t, bsz, nq, past):
    nb, ns = mband.shape
    w_buf = cache_win.shape[2]
    off = t // nq
    row = lambda w: pl.BlockSpec((nq, w), lambda b: (off + b, 0))
    per_b = lambda shape: pl.BlockSpec((1,) + shape, lambda b: (b,) + (0,) * len(shape))
    return pl.pallas_call(
        functools.partial(_e1_sample_kernel, nb=nb, ns=ns, nq=nq, past=past),
        out_shape=[jax.ShapeDtypeStruct((bsz * nq, Q_PAD), F32), jax.ShapeDtypeStruct((bsz, N_KV, nq, ns), F32)],
        grid=(bsz,),
        in_specs=[row(Q_PAD), row(LANE), per_b((nb, D_KV)), per_b((nb, D_KV)), _const_spec(mband.shape),
                  pl.BlockSpec((1, 2 * D_KV, w_buf), lambda b: (win_off + b, 0, 0)), row(2 * D_KV)],
        out_specs=[pl.BlockSpec((nq, Q_PAD), lambda b: (b, 0)), per_b((N_KV, nq, ns))],
        compiler_params=_cparams("parallel"),
    )(qp, gn, kcc, vcc, mband, cache_win, kv_win)


def _online_step(s, kdist, masks, v_ones, m_ref, acc_ref, tq, feature_major=False):
    reps = s.shape[1] // LANE
    ps, alphas = [], []
    for h in range(N_HEADS):
        rows = slice(h * tq, (h + 1) * tq)
        sm = jnp.where(masks[h // HPG], s[rows] - _slope2(h) * kdist, NEG)
        m_old = m_ref[rows, :]
        m_new = jnp.maximum(m_old, jnp.max(sm, axis=-1, keepdims=True))
        alphas.append(jnp.exp2(m_old - m_new))
        ps.append(jnp.exp2(sm - jnp.tile(m_new, (1, reps))))
        m_ref[rows, :] = m_new
    p_all = jnp.concatenate(ps, axis=0).astype(BF16)
    a_all = jnp.concatenate(alphas, axis=0)
    pv = _dot_nt(p_all, v_ones) if feature_major else _dot(p_all, v_ones)
    acc_ref[...] = jnp.tile(a_all, (1, 2)) * acc_ref[...] + pv


def _block_to_keys(ns, first_blk, n_keys):
    j_io = lax.broadcasted_iota(jnp.int32, (ns, n_keys), 0)
    kb = first_blk + jnp.right_shift(lax.broadcasted_iota(jnp.int32, (ns, n_keys), 1), 6)
    return jnp.where(j_io == kb, 1.0, 0.0).astype(BF16)


def _init_online(m_ref, acc_ref):
    m_ref[...] = jnp.full(m_ref.shape, HALF_NEG, F32)
    acc_ref[...] = jnp.zeros(acc_ref.shape, F32)


def _finish_online(ocw, gn, acc_ref, tq):
    o_s = acc_ref[:, 0:LANE] * (1.0 / jnp.maximum(acc_ref[:, LANE:], 1e-30))
    outs = []
    for h in range(N_HEADS):
        g1 = gn[:, N_HEADS + h:N_HEADS + h + 1]
        outs.append(ocw[:, h * LANE:(h + 1) * LANE] + g1 * o_s[h * tq:(h + 1) * tq])
    return jnp.concatenate(outs, axis=1)


def _e2_prompt_kernel(order_ref, cnt_ref, qp_ref, sel_ref, gn_ref, ocw_ref, kvs_ref, out_ref, m_ref, acc_ref,
                      *, nchunks, ns, nt):
    tq = Q_TILE
    i = pl.program_id(0)
    s0 = i * tq
    qpos = s0 + lax.broadcasted_iota(jnp.int32, (tq, 1), 0)
    q_all = _heads_rows(qp_ref[...])
    _init_online(m_ref, acc_ref)
    n_act = jnp.where(i < nt, cnt_ref[jnp.minimum(i, nt - 1)], 0)
    blocks_per_chunk = SEL_CHUNK // SEL_BLOCK
    sel_bf = [sel_ref[g].astype(BF16) for g in range(N_KV)]

    def body(j, carry):
        c1 = order_ref[i * nchunks + 2 * j]
        paired = 2 * j + 1 < n_act
        c2 = jnp.where(paired, order_ref[i * nchunks + jnp.minimum(2 * j + 1, nchunks - 1)], c1)
        ks, vs, dists, masks = [], [], [], [[] for _ in range(N_KV)]
        for c, first_blk in ((c1, c1 * blocks_per_chunk), (c2, jnp.where(paired, c2 * blocks_per_chunk, ns))):
            base = pl.multiple_of(c * SEL_CHUNK, SEL_CHUNK)
            ks.append(kvs_ref[pl.ds(base, SEL_CHUNK), 0:D_KV])
            vs.append(kvs_ref[pl.ds(base, SEL_CHUNK), D_KV:])
            kpos = base + lax.broadcasted_iota(jnp.int32, (1, SEL_CHUNK), 1)
            dists.append((s0 - kpos).astype(F32))
            expand = _block_to_keys(ns, first_blk, SEL_CHUNK)
            causal = qpos >= kpos
            for g in range(N_KV):
                masks[g].append((_dot(sel_bf[g], expand) > 0.5) & causal)
        s = _dot_nt(q_all, jnp.concatenate(ks, axis=0))
        _online_step(s, jnp.concatenate(dists, axis=1), [jnp.concatenate(mg, axis=1) for mg in masks],
                     jnp.concatenate(vs, axis=0), m_ref, acc_ref, tq)
        return carry

    lax.fori_loop(0, (n_act + 1) // 2, body, 0)
    out_ref[...] = _finish_online(ocw_ref[...], gn_ref[...], acc_ref, tq)


def _e2_prompt(order, cnt, qp, sel, gn, ocw, kvs_bf, t):
    ns = sel.shape[2]
    nt = t // Q_TILE
    nchunks = order.shape[0] // nt
    row = lambda w: pl.BlockSpec((Q_TILE, w), lambda i, o, c: (i, 0))
    prow = lambda w: pl.BlockSpec((Q_TILE, w), lambda i, o, c: (jnp.minimum(i, nt - 1), 0))
    grid_spec = pltpu.PrefetchScalarGridSpec(
        num_scalar_prefetch=2,
        grid=(qp.shape[0] // Q_TILE,),
        in_specs=[row(Q_PAD), pl.BlockSpec((N_KV, Q_TILE, ns), lambda i, o, c: (0, jnp.minimum(i, nt - 1), 0)),
                  row(LANE), prow(Q_PAD), pl.BlockSpec(kvs_bf.shape, lambda i, o, c: (0, 0))],
        out_specs=row(Q_PAD),
        scratch_shapes=[pltpu.VMEM((N_HEADS * Q_TILE, LANE), F32), pltpu.VMEM((N_HEADS * Q_TILE, 2 * LANE), F32)],
    )
    return pl.pallas_call(
        functools.partial(_e2_prompt_kernel, nchunks=nchunks, ns=ns, nt=nt),
        out_shape=jax.ShapeDtypeStruct((qp.shape[0], Q_PAD), F32),
        grid_spec=grid_spec,
        compiler_params=_cparams("arbitrary"),
    )(order, cnt, qp, sel, gn, ocw, kvs_bf)


def _e2_sample_kernel(pt_ref, live_ref, pages_ref, qp_ref, sel_ref, gn_ref, ocw_ref, nsel_ref, expand_ref,
                      oatt_all_ref, out_ref, buf, sem, m_ref, acc_ref, *, pps, nsplit, nq):
    s = pl.program_id(0)
    nsteps = pl.num_programs(0)
    n_keys = pps * PAGE_SIZE

    def page_copy(step, p, slot):
        b = step // nsplit
        part = step % nsplit
        pg = pt_ref[b, part * pps + p]
        col = pl.multiple_of(p * PAGE_SIZE, PAGE_SIZE)
        return pltpu.make_async_copy(pages_ref.at[pg], buf.at[slot, :, pl.ds(col, PAGE_SIZE)], sem.at[slot])

    def start(step, slot):
        def body(p, c):
            page_copy(step, p, slot).start()
            return c
        lax.fori_loop(0, pps, body, 0)

    def wait(step, slot):
        def body(p, c):
            page_copy(step, p, slot).wait()
            return c
        lax.fori_loop(0, pps, body, 0)

    @pl.when((s == 0) & (live_ref[0] > 0))
    def _():
        start(s, 0)

    @pl.when((s + 1 < nsteps) & (live_ref[jnp.minimum(s + 1, nsteps - 1)] > 0))
    def _():
        start(s + 1, (s + 1) % 2)

    slot = s % 2
    part = s % nsplit
    qi = lax.broadcasted_iota(jnp.int32, (nq, 1), 0)
    q_all = _heads_rows(qp_ref[...])

    @pl.when(part == 0)
    def _():
        _init_online(m_ref, acc_ref)

    @pl.when(live_ref[s] > 0)
    def _():
        wait(s, slot)
        k = buf[slot, 0:D_KV, :].astype(BF16)
        v_ones = jnp.concatenate([buf[slot, D_KV:, :].astype(BF16), jnp.ones((D_KV, n_keys), BF16)], axis=0)
        sc = _dot(q_all, k)
        back = (nsplit - part) * n_keys - lax.broadcasted_iota(jnp.int32, (1, n_keys), 1)
        masks = [_dot(sel_ref[0, g, 0].astype(BF16), expand_ref[...]) > 0.5 for g in range(N_KV)]
        _online_step(sc, back.astype(F32), masks, v_ones, m_ref, acc_ref, nq, feature_major=True)

    @pl.when(part == nsplit - 1)
    def _():
        nw = _pad_rows(nsel_ref[...], LANE)
        j_n = lax.broadcasted_iota(jnp.int32, (1, LANE), 1)
        mask_n = (qi >= j_n) & (j_n < nq)
        s_n = _dot_nt(q_all, nw[:, 0:D_KV].astype(BF16))
        vn_ones = jnp.concatenate([nw[:, D_KV:].astype(BF16), jnp.ones((LANE, D_KV), BF16)], axis=1)
        _online_step(s_n, (-j_n).astype(F32), [mask_n, mask_n], vn_ones, m_ref, acc_ref, nq)
        out_ref[...] = _finish_online(ocw_ref[...], gn_ref[...], acc_ref, nq)


def _e2_sample(page_table, pages, qp, sel, gn, ocw, kv_sel, oatt, t, nq):
    bsz, n_pages = page_table.shape
    ns = sel.shape[3]
    nsplit = 8
    pps = n_pages // nsplit
    n_keys = pps * PAGE_SIZE
    nblk = ns // nsplit
    off = t // nq
    sel_parts = sel.reshape(bsz, N_KV, nq, nsplit, nblk).transpose(0, 1, 3, 2, 4)
    live = (jnp.max(sel_parts, axis=(1, 3, 4)) > 0).astype(jnp.int32).reshape(-1)
    expand = (jnp.arange(nblk)[:, None] == jnp.arange(n_keys)[None, :] // SEL_BLOCK).astype(BF16)
    row = lambda w: pl.BlockSpec((nq, w), lambda s, pt, lv: (off + s // nsplit, 0))
    grid_spec = pltpu.PrefetchScalarGridSpec(
        num_scalar_prefetch=2,
        grid=(bsz * nsplit,),
        in_specs=[pl.BlockSpec(memory_space=pl.ANY), row(Q_PAD),
                  pl.BlockSpec((1, N_KV, 1, nq, nblk), lambda s, pt, lv: (s // nsplit, 0, s % nsplit, 0, 0)),
                  row(LANE), pl.BlockSpec((nq, Q_PAD), lambda s, pt, lv: (s // nsplit, 0)), row(2 * D_KV),
                  pl.BlockSpec((nblk, n_keys), lambda s, pt, lv: (0, 0), pipeline_mode=pl.Buffered(1)),
                  pl.BlockSpec(memory_space=pl.ANY)],
        out_specs=row(Q_PAD),
        scratch_shapes=[pltpu.VMEM((2, 2 * D_KV, n_keys), F32), pltpu.SemaphoreType.DMA((2,)),
                        pltpu.VMEM((N_HEADS * nq, LANE), F32), pltpu.VMEM((N_HEADS * nq, 2 * LANE), F32)],
    )
    return pl.pallas_call(
        functools.partial(_e2_sample_kernel, pps=pps, nsplit=nsplit, nq=nq),
        out_shape=jax.ShapeDtypeStruct(oatt.shape, F32),
        grid_spec=grid_spec,
        input_output_aliases={9: 0},
        compiler_params=_cparams("arbitrary"),
    )(page_table, live, pages, qp, sel_parts, gn, ocw, kv_sel, expand, oatt)


def _post_kernel(h_ref, yg_ref, oatt_ref, ga_ref, gb_ref, wglu_ref, watt_ref, wout_ref, o_ref):
    gl = _dot(yg_ref[...], wglu_ref[...])
    br_a = gl[:, :D_MODEL] * jax.nn.sigmoid(gl[:, D_MODEL:])
    br_b = _dot(oatt_ref[...].astype(BF16), watt_ref[...])
    merged = (ga_ref[...] * br_a + gb_ref[...] * br_b).astype(BF16)
    o_ref[...] = h_ref[...] + _dot(merged, wout_ref[...])


def _post(h, yg, oatt, ga, gb, wglu, watt, wout):
    n = h.shape[0]
    tm = _pick_tile(n, 256)
    row = lambda w: pl.BlockSpec((tm, w), lambda i: (i, 0))
    return pl.pallas_call(
        _post_kernel,
        out_shape=jax.ShapeDtypeStruct((n, D_MODEL), F32),
        grid=(n // tm,),
        in_specs=[row(D_MODEL), row(D_SSM), row(Q_PAD), row(D_MODEL), row(D_MODEL),
                  _const_spec(wglu.shape), _const_spec(watt.shape), _const_spec(wout.shape)],
        out_specs=row(D_MODEL),
        compiler_params=_cparams("parallel"),
    )(h, yg, oatt, ga, gb, wglu, watt, wout)


def _head_pad_index():
    h = jnp.arange(N_HEADS)[:, None]
    d = jnp.arange(HEAD_DIM)[None, :]
    return (LANE * h + HEAD_DIM * (h // HPG) + d).reshape(-1)


def _prep_mix_weights(w_in, qk_norm):
    idx = _head_pad_index()
    wq = jnp.zeros((D_MODEL, Q_PAD), F32).at[:, idx].set(w_in[:, D_SSM:D_SSM + N_HEADS * HEAD_DIM])
    c0 = D_SSM + N_HEADS * HEAD_DIM
    c1 = c0 + 6 * D_KV
    c2 = c1 + 3 * N_HEADS
    wgn = jnp.zeros((D_MODEL, LANE), F32).at[:, :3 * N_HEADS].set(w_in[:, c1:c2])
    w = jnp.concatenate([w_in[:, :D_SSM], wq, w_in[:, c0:c1], wgn, w_in[:, c2:]], axis=1).astype(BF16)
    gq = jnp.zeros((Q_PAD,), F32).at[idx].set(jnp.tile(qk_norm[0] * (HEAD_DIM ** -0.5 * LOG2E), N_HEADS))[None]
    gks = jnp.tile(qk_norm[2], N_KV)[None]
    gkw = jnp.tile(qk_norm[3], N_KV)[None]
    return w, gq, gks, gkw


def _prep_att_out(w_att_out):
    return jnp.zeros((Q_PAD, D_MODEL), F32).at[_head_pad_index()].set(w_att_out).astype(BF16)


def _prep_compress(pe_k, pe_v, w_k, w_v, gain_k):
    nj = CMP_BLOCK // CMP_STRIDE
    eye = jnp.eye(N_KV, dtype=F32)

    def pair_maps(w):
        w5 = w.reshape(nj, N_SPAIR, 2, HEAD_DIM, HEAD_DIM)
        return jnp.einsum("jpsde,gh->psgdjhe", w5, eye).reshape(N_SPAIR, 2 * D_KV, 2 * D_KV)

    def pair_pe(pe):
        p5 = pe.reshape(nj, N_SPAIR, 2, 1, HEAD_DIM)
        return jnp.broadcast_to(p5, (nj, N_SPAIR, 2, N_KV, HEAD_DIM)).reshape(nj, N_SPAIR, 2 * D_KV)

    w = jnp.stack([pair_maps(w_k), pair_maps(w_v)], axis=0)
    pe = jnp.stack([pair_pe(pe_k), pair_pe(pe_v)], axis=1).reshape(-1, 2 * D_KV)
    r = jnp.arange(2 * PAGE_SIZE)
    perm = (r[None, :] == (CMP_STRIDE * (r % CMP_STRIDE) + r // CMP_STRIDE)[:, None]).astype(BF16)
    return {"w": w.astype(BF16), "pe": pe,
            "gk": jnp.tile(gain_k, N_KV)[None], "perm": perm}


def _prep_s5(a_re, a_im, log_dt, b_re, b_im, c_re, c_im, d, nq):
    dt = jnp.exp(log_dt)[:, None]
    mag = jnp.exp(a_re * dt)
    lr = mag * jnp.cos(a_im * dt)
    li = mag * jnp.sin(a_im * dt)
    den = a_re * a_re + a_im * a_im
    fr = ((lr - 1.0) * a_re + li * a_im) / den
    fi = (li * a_re - (lr - 1.0) * a_im) / den
    bbr = fr[..., None] * b_re - fi[..., None] * b_im
    bbi = fr[..., None] * b_im + fi[..., None] * b_re
    eye = jnp.eye(N_SSM_GROUPS, dtype=F32)
    blk_b = lambda m: jnp.einsum("gpc,gh->gchp", m, eye).reshape(D_SSM, N_STATE)
    blk_c = lambda m: jnp.einsum("gcp,gh->gphc", m, eye).reshape(N_STATE, D_SSM)

    def lam_pow(k):
        kk = k.astype(F32)[:, None, None]
        m = jnp.exp(a_re * dt * kk)
        th = a_im * dt * kk
        return (m * jnp.cos(th)).reshape(-1, N_STATE), (m * jnp.sin(th)).reshape(-1, N_STATE)

    def step_table(seg):
        ks = []
        dd = 1
        while dd < seg:
            ks.append(dd)
            dd *= 2
        re, im = lam_pow(jnp.array(ks))
        live = jnp.arange(seg)[None, :, None] >= jnp.array(ks)[:, None, None]
        tab = jnp.stack([jnp.where(live, re[:, None, :], 0.0), jnp.where(live, im[:, None, :], 0.0)], axis=1)
        return tab.reshape(-1, N_STATE)

    sp = {"bmat": jnp.concatenate([blk_b(bbr), blk_b(bbi)], axis=1).astype(BF16),
          "cr": blk_c(c_re).astype(BF16), "ci": (-blk_c(c_im)).astype(BF16), "d": d[None],
          "lam_p": step_table(SCAN_SEG), "lam_s": step_table(nq)}
    sp["pr_p"], sp["pi_p"] = lam_pow(jnp.arange(SCAN_SEG) + 1)
    return sp, lam_pow


def _band_matrix(nb, ns):
    ratio = SEL_BLOCK // CMP_STRIDE
    lo = CMP_BLOCK // CMP_STRIDE - 1
    c = jnp.arange(nb)[:, None]
    j = jnp.arange(ns)[None, :]
    return ((c >= ratio * j - lo) & (c <= ratio * j + ratio - 1)).astype(BF16)


def kernel(x_prompt, x_sample, cache_kv_cmp, cache_kv_sel, cache_kv_win, state_ssm_re, state_ssm_im, page_table, p_prompt, p_sample, norm_ffn1, w_ffn1_in, w_ffn1_out, norm_mix, w_in, qk_norm, ssm_a_re, ssm_a_im, ssm_log_dt, ssm_b_re, ssm_b_im, ssm_c_re, ssm_c_im, ssm_d, w_glu, cmp_pe_k, cmp_pe_v, cmp_w_k, cmp_w_v, w_att_out, w_out, norm_ffn2, w_ffn2_in, w_ffn2_out, norm_ple, w_ple_gate, w_ple_proj):
    bp, t = x_prompt.shape[:2]
    bsz, nq = x_sample.shape[:2]
    n_pages = page_table.shape[1]
    past = n_pages * PAGE_SIZE
    n_pool = cache_kv_cmp.shape[1]
    w_buf = cache_kv_win.shape[2]
    ns_rows = bsz * nq
    assert bp == 1 and t % SEL_CHUNK == 0 and t >= WIN_KEYS and t % ns_rows == 0 and nq < CMP_STRIDE
    assert past == t and w_buf == WINDOW and ns_rows % Q_TILE == 0
    nb = t // CMP_STRIDE
    ns = t // SEL_BLOCK
    nt = t // Q_TILE
    nchunks = t // SEL_CHUNK
    mband = _band_matrix(nb, ns)
    page_table = page_table.astype(jnp.int32)
    feat_major = lambda c: jnp.transpose(c, (0, 1, 3, 4, 5, 2)).reshape(DEPTH * c.shape[1], 2 * D_KV, c.shape[2])
    pages_cmp = feat_major(cache_kv_cmp)
    pages_sel = feat_major(cache_kv_sel)
    pages_win = feat_major(cache_kv_win)

    h = (x_prompt[0], x_sample.reshape(ns_rows, D_MODEL))
    p_all = (p_prompt.reshape(DEPTH * t, D_PLE), p_sample.reshape(DEPTH * ns_rows, D_PLE))
    st_p = [[] for _ in range(5)]
    st_s = [[] for _ in range(5)]
    for i in range(DEPTH):
        row1 = lambda a: a[i][None]
        h = _ffn(h, row1(norm_ffn1), w_ffn1_in[i].astype(BF16), w_ffn1_out[i].astype(BF16), t, ns_rows)
        w_mix, gq, gks, gkw = _prep_mix_weights(w_in[i], qk_norm[i])
        u, qp, kv_cmp, kv_sel, kv_win, kvs_bf, kvw_bf, gn, ga, gb = _mix_in(h, row1(norm_mix), w_mix, gq, gks, gkw)

        sp, lam_pow = _prep_s5(ssm_a_re[i], ssm_a_im[i], ssm_log_dt[i], ssm_b_re[i], ssm_b_im[i],
                               ssm_c_re[i], ssm_c_im[i], ssm_d[i], nq)
        sp["pr_s"], sp["pi_s"] = lam_pow(jnp.arange(ns_rows) % nq + 1)
        yg, hr_p, hi_p = _s5_prompt(u, t, sp)
        h0r = jnp.repeat(state_ssm_re[i].reshape(bsz, N_STATE), nq, axis=0)
        h0i = jnp.repeat(state_ssm_im[i].reshape(bsz, N_STATE), nq, axis=0)
        yg, hr_s, hi_s = _s5_sample(u, yg, t, ns_rows, nq, h0r, h0i, sp)

        cw = _prep_compress(cmp_pe_k[i], cmp_pe_v[i], cmp_w_k[i], cmp_w_v[i], qk_norm[i, 1])
        layer_pages = page_table + i * n_pool
        kcc_p, vcc_p = _compress_rows(kv_cmp, t, cw)
        kcc_s, vcc_s = _compress_paged(layer_pages, pages_cmp, cw)
        ocw_p, sel_p, flags = _e1_prompt(qp, gn, kcc_p, vcc_p, mband, kvw_bf, t)
        ocw_s, sel_s = _e1_sample(qp, gn, kcc_s, vcc_s, mband, pages_win, i * bsz, kv_win, t, bsz, nq, past)
        idle = flags[:, 0, :nchunks] == 0
        order = jnp.argsort(idle, axis=1, stable=True).astype(jnp.int32).reshape(-1)
        cnt = (nchunks - jnp.sum(idle, axis=1)).astype(jnp.int32)
        oatt = _e2_prompt(order, cnt, qp, sel_p, gn, ocw_p, kvs_bf, t)
        oatt = _e2_sample(layer_pages, pages_sel, qp, sel_s, gn, ocw_s, kv_sel, oatt, t, nq)

        h = _post(h, yg, oatt, ga, gb,
                  w_glu[i].astype(BF16), _prep_att_out(w_att_out[i]), w_out[i].astype(BF16))
        h = _ffn(h, row1(norm_ffn2), w_ffn2_in[i].astype(BF16), w_ffn2_out[i].astype(BF16), t, ns_rows,
                 (p_all, row1(norm_ple), w_ple_gate[i].astype(BF16), w_ple_proj[i].astype(BF16)), p_layer=i)

        kv5 = lambda a, lead: a.reshape(lead + (2, N_KV, HEAD_DIM))
        st_p[0].append(kv5(kv_cmp[:t], (1, t)))
        st_p[1].append(kv5(kv_sel[:t], (1, t)))
        st_p[2].append(kv5(kv_win[t - min(WINDOW, t):t], (1, min(WINDOW, t))))
        st_p[3].append(hr_p.reshape(1, N_SSM_GROUPS, P_STATE))
        st_p[4].append(hi_p.reshape(1, N_SSM_GROUPS, P_STATE))
        st_s[0].append(kv5(kv_cmp[t:], (bsz, nq)))
        st_s[1].append(kv5(kv_sel[t:], (bsz, nq)))
        st_s[2].append(jnp.concatenate([cache_kv_win[i, :, nq:], kv5(kv_win[t:], (bsz, nq))], axis=1))
        st_s[3].append(hr_s[nq - 1::nq].reshape(bsz, N_SSM_GROUPS, P_STATE))
        st_s[4].append(hi_s[nq - 1::nq].reshape(bsz, N_SSM_GROUPS, P_STATE))

    outs_p = [jnp.stack(a) for a in st_p]
    outs_s = [jnp.stack(a) for a in st_s]
    y_prompt = h[:t][None]
    y_sample = h[t:].reshape(bsz, nq, D_MODEL)
    return (y_prompt, y_sample, *outs_p, *outs_s)
```

```python
import functools

import jax
import jax.numpy as jnp
from jax import lax
from jax.experimental import pallas as pl
from jax.experimental.pallas import tpu as pltpu

F32 = jnp.float32
BF16 = jnp.bfloat16

D_MODEL = 1024
DEPTH = 2
D_SSM = 512
SSM_GROUP = 16
N_SSM_GROUPS = 32
P_STATE = 64
N_STATE = N_SSM_GROUPS * P_STATE
N_HEADS = 8
HEAD_DIM = 64
N_KV = 2
HPG = 4
D_KV = 128
CMP_BLOCK = 32
CMP_STRIDE = 16
SEL_BLOCK = 64
TOP_N = 16
WINDOW = 512
PAGE_SIZE = 128
SEL_BIG = 1e4
D_FF = 2816
D_PLE = 256
RMS_EPS = 1e-6
NEG = -1e30
HALF_NEG = -0.5e30
LOG2E = 1.4426950408889634

LANE = 128
Q_TILE = 128
SCAN_SEG = 8
SEL_CHUNK = 256
WIN_KEYS = WINDOW + Q_TILE
Q_PAD = N_HEADS * LANE
VMEM_LIMIT = 56 * 2 ** 20


def _cparams(*sem):
    return pltpu.CompilerParams(dimension_semantics=sem, vmem_limit_bytes=VMEM_LIMIT)


def _dot(a, b):
    return jnp.dot(a, b, preferred_element_type=F32)


def _dot_nt(a, b):
    return lax.dot_general(a, b, (((1,), (1,)), ((), ())), preferred_element_type=F32)


def _pick_tile(n, target):
    for t in range(min(n, target), 15, -1):
        if n % t == 0 and t % 16 == 0:
            return t
    raise ValueError(f"no row tile for {n}")


def _const_spec(shape):
    nd = len(shape)
    return pl.BlockSpec(shape, lambda *_: (0,) * nd, pipeline_mode=pl.Buffered(1))


def _rms(x, g):
    ms = jnp.mean(x * x, axis=-1, keepdims=True)
    return x * lax.rsqrt(ms + RMS_EPS) * g


def _halfnorm(x, gain):
    lo = lax.broadcasted_iota(jnp.int32, (1, LANE), 1) < HEAD_DIM
    x2 = x * x
    s_lo = jnp.sum(jnp.where(lo, x2, 0.0), axis=-1, keepdims=True)
    s_hi = jnp.sum(jnp.where(lo, 0.0, x2), axis=-1, keepdims=True)
    ms = jnp.where(lo, s_lo, s_hi) * (1.0 / HEAD_DIM)
    return x * lax.rsqrt(ms + RMS_EPS) * gain


def _row_tile(refs, lead_tiles):
    if len(refs) == 1:
        return refs[0][...]
    return jnp.where(pl.program_id(0) < lead_tiles, refs[0][...], refs[1][...])


def _ffn_kernel(*refs, ple, h_parts, p_parts, lead_tiles):
    refs = list(refs)
    h_refs = [refs.pop(0) for _ in range(h_parts)]
    if ple:
        g_ref, wi_ref, wo_ref = refs[:3]
        p_refs = refs[3:3 + p_parts]
        gp_ref, wg_ref, wp_ref, o_ref = refs[3 + p_parts:]
    else:
        g_ref, wi_ref, wo_ref, o_ref = refs
    h = _row_tile(h_refs, lead_tiles)
    xn = _rms(h, g_ref[...]).astype(BF16)
    a = _dot(xn, wi_ref[:, :D_FF])
    b = _dot(xn, wi_ref[:, D_FF:])
    act = (a * jax.nn.sigmoid(a) * b).astype(BF16)
    h = h + 0.5 * _dot(act, wo_ref[...])
    if ple:
        xg = _rms(h, gp_ref[...]).astype(BF16)
        gate = jax.nn.sigmoid(_dot(xg, wg_ref[...]))
        h = h + gate * _dot(_row_tile(p_refs, lead_tiles).astype(BF16), wp_ref[...])
    o_ref[...] = h


def _ffn(h, g, wi, wo, n_prompt, n_sample, ple_args=None, p_layer=0):
    h_parts = list(h) if isinstance(h, (tuple, list)) else [h]
    n = n_prompt + n_sample
    tm = _pick_tile(n, 256)
    assert n_prompt % tm == 0 and n_sample % tm == 0
    lead_tiles = n_prompt // tm
    row = lambda w: pl.BlockSpec((tm, w), lambda i: (i, 0))

    def pair_specs(w, off_p, off_s):
        return [pl.BlockSpec((tm, w), lambda i: (off_p + jnp.minimum(i, lead_tiles - 1), 0)),
                pl.BlockSpec((tm, w), lambda i: (off_s + jnp.maximum(i - lead_tiles, 0), 0))]

    in_specs = (pair_specs(D_MODEL, 0, 0) if len(h_parts) == 2 else [row(D_MODEL)]) + [
        _const_spec((1, D_MODEL)), _const_spec(wi.shape), _const_spec(wo.shape)]
    args = h_parts + [g, wi, wo]
    p_parts = []
    if ple_args is not None:
        p_parts, gp, wg, wp = ple_args
        p_parts = list(p_parts)
        in_specs += pair_specs(D_PLE, p_layer * lead_tiles, p_layer * (n_sample // tm)) + [
            _const_spec((1, D_MODEL)), _const_spec(wg.shape), _const_spec(wp.shape)]
        args += p_parts + [gp, wg, wp]
    return pl.pallas_call(
        functools.partial(_ffn_kernel, ple=ple_args is not None, h_parts=len(h_parts), p_parts=len(p_parts),
                          lead_tiles=lead_tiles),
        out_shape=jax.ShapeDtypeStruct((n, D_MODEL), F32),
        grid=(n // tm,),
        in_specs=in_specs,
        out_specs=row(D_MODEL),
        compiler_params=_cparams("parallel"),
    )(*args)


_C_U = 0
_C_Q = _C_U + D_SSM
_C_KV = _C_Q + Q_PAD
_C_GN = _C_KV + 6 * D_KV
_C_GA = _C_GN + LANE
_C_GB = _C_GA + D_MODEL
_C_END = _C_GB + D_MODEL


def _mix_in_kernel(h_ref, g_ref, w_ref, gq_ref, gks_ref, gkw_ref,
                   u_ref, qp_ref, kvc_ref, kvs_ref, kvw_ref, kvsb_ref, kvwb_ref, gn_ref, ga_ref, gb_ref):
    xn = _rms(h_ref[...], g_ref[...]).astype(BF16)
    u_ref[...] = _dot(xn, w_ref[:, _C_U:_C_Q])
    zq = _dot(xn, w_ref[:, _C_Q:_C_KV])
    for h in range(N_HEADS):
        qh = zq[:, h * LANE:(h + 1) * LANE]
        ms = jnp.sum(qh * qh, axis=-1, keepdims=True) * (1.0 / HEAD_DIM)
        qn = qh * lax.rsqrt(ms + RMS_EPS) * gq_ref[:, h * LANE:(h + 1) * LANE]
        qp_ref[:, h * LANE:(h + 1) * LANE] = qn
    zkv = _dot(xn, w_ref[:, _C_KV:_C_GN])
    kvc_ref[...] = zkv[:, 0:2 * D_KV]
    ks = _halfnorm(zkv[:, 2 * D_KV:3 * D_KV], gks_ref[...])
    vs = zkv[:, 3 * D_KV:4 * D_KV]
    kw = _halfnorm(zkv[:, 4 * D_KV:5 * D_KV], gkw_ref[...])
    vw = zkv[:, 5 * D_KV:6 * D_KV]
    kvs_ref[:, 0:D_KV] = ks
    kvs_ref[:, D_KV:] = vs
    kvw_ref[:, 0:D_KV] = kw
    kvw_ref[:, D_KV:] = vw
    kvsb_ref[:, 0:D_KV] = ks.astype(BF16)
    kvsb_ref[:, D_KV:2 * D_KV] = vs.astype(BF16)
    kvsb_ref[:, 2 * D_KV:] = jnp.ones((ks.shape[0], D_KV), BF16)
    kvwb_ref[:, 0:D_KV] = kw.astype(BF16)
    kvwb_ref[:, D_KV:] = vw.astype(BF16)
    gn_ref[...] = jax.nn.sigmoid(_dot(xn, w_ref[:, _C_GN:_C_GA]))
    ga_ref[...] = jax.nn.sigmoid(_dot(xn, w_ref[:, _C_GA:_C_GB]))
    gb_ref[...] = jax.nn.sigmoid(_dot(xn, w_ref[:, _C_GB:_C_END]))


def _mix_in(h, g, w, gq, gks, gkw):
    n = h.shape[0]
    tm = _pick_tile(n, 256)
    row = lambda width: pl.BlockSpec((tm, width), lambda i: (i, 0))
    widths = [(D_SSM, F32), (Q_PAD, F32), (2 * D_KV, F32), (2 * D_KV, F32), (2 * D_KV, F32),
              (3 * D_KV, BF16), (2 * D_KV, BF16), (LANE, F32), (D_MODEL, F32), (D_MODEL, F32)]
    return pl.pallas_call(
        _mix_in_kernel,
        out_shape=[jax.ShapeDtypeStruct((n, wd), dt) for wd, dt in widths],
        grid=(n // tm,),
        in_specs=[row(D_MODEL), _const_spec((1, D_MODEL)), _const_spec(w.shape),
                  _const_spec((1, Q_PAD)), _const_spec((1, LANE)), _const_spec((1, LANE))],
        out_specs=[row(wd) for wd, _ in widths],
        compiler_params=_cparams("parallel"),
    )(h, g, w, gq, gks, gkw)


def _gelu_tanh(x):
    return 0.5 * x * (1.0 + jnp.tanh(0.7978845608028654 * (x + 0.044715 * (x * x * x))))


def _s5_kernel(*refs, seg, carry, seq_steps=None):
    if carry:
        (u_ref, bm_ref, cr_ref, ci_ref, d_ref, lam_ref, pr_ref, pi_ref,
         y_ref, hr_out, hi_out, car_ref, cai_ref) = refs
    else:
        (u_ref, bm_ref, cr_ref, ci_ref, d_ref, lam_ref, pr_ref, pi_ref, h0r_ref, h0i_ref, y_all_ref,
         y_ref, hr_out, hi_out) = refs
    u = u_ref[...]
    rows = u.shape[0]
    x = _dot(u.astype(BF16), bm_ref[...])
    xr = x[:, :N_STATE]
    xi = x[:, N_STATE:]
    d, k = 1, 0
    while d < seg:
        lr = jnp.tile(lam_ref[2 * k * seg:(2 * k + 1) * seg, :], (rows // seg, 1))
        li = jnp.tile(lam_ref[(2 * k + 1) * seg:(2 * k + 2) * seg, :], (rows // seg, 1))
        sr = pltpu.roll(xr, d, 0)
        si = pltpu.roll(xi, d, 0)
        xr, xi = xr + lr * sr - li * si, xi + lr * si + li * sr
        d *= 2
        k += 1
    p_r = pr_ref[...]
    p_i = pi_ref[...]
    if carry:
        @pl.when(pl.program_id(0) == 0)
        def _():
            car_ref[...] = jnp.zeros_like(car_ref)
            cai_ref[...] = jnp.zeros_like(cai_ref)
        c_r = car_ref[...]
        c_i = cai_ref[...]
        hr_parts, hi_parts = [], []
        for j in range(rows // seg):
            tr = xr[j * seg:(j + 1) * seg]
            ti = xi[j * seg:(j + 1) * seg]
            hr_j = tr + p_r * c_r - p_i * c_i
            hi_j = ti + p_r * c_i + p_i * c_r
            c_r = hr_j[seg - 1:seg, :]
            c_i = hi_j[seg - 1:seg, :]
            hr_parts.append(hr_j)
            hi_parts.append(hi_j)
        hr = jnp.concatenate(hr_parts, axis=0)
        hi = jnp.concatenate(hi_parts, axis=0)
    else:
        c_r = h0r_ref[...]
        c_i = h0i_ref[...]
        hr = xr + p_r * c_r - p_i * c_i
        hi = xi + p_r * c_i + p_i * c_r
    y = _dot(hr.astype(BF16), cr_ref[...]) + _dot(hi.astype(BF16), ci_ref[...]) + d_ref[...] * u
    y_ref[...] = _gelu_tanh(y).astype(BF16)
    if carry:
        car_ref[...] = hr[rows - 1:rows, :]
        cai_ref[...] = hi[rows - 1:rows, :]

        @pl.when(pl.program_id(0) < seq_steps)
        def _():
            hr_out[...] = hr[rows - 1:rows, :]
            hi_out[...] = hi[rows - 1:rows, :]
    else:
        hr_out[...] = hr
        hi_out[...] = hi


def _s5_prompt(u, t, sp):
    rows = Q_TILE
    consts = [sp["bmat"], sp["cr"], sp["ci"], sp["d"], sp["lam_p"], sp["pr_p"], sp["pi_p"]]
    return pl.pallas_call(
        functools.partial(_s5_kernel, seg=SCAN_SEG, carry=True, seq_steps=t // rows),
        out_shape=[jax.ShapeDtypeStruct((u.shape[0], D_SSM), BF16),
                   jax.ShapeDtypeStruct((1, N_STATE), F32), jax.ShapeDtypeStruct((1, N_STATE), F32)],
        grid=(u.shape[0] // rows,),
        in_specs=[pl.BlockSpec((rows, D_SSM), lambda i: (i, 0))] + [_const_spec(c.shape) for c in consts],
        out_specs=[pl.BlockSpec((rows, D_SSM), lambda i: (i, 0)),
                   _const_spec((1, N_STATE)), _const_spec((1, N_STATE))],
        scratch_shapes=[pltpu.VMEM((1, N_STATE), F32), pltpu.VMEM((1, N_STATE), F32)],
        compiler_params=_cparams("arbitrary"),
    )(u, *consts)


def _s5_sample(u, yg, t, ns_rows, nq, h0r, h0i, sp):
    consts = [sp["bmat"], sp["cr"], sp["ci"], sp["d"], sp["lam_s"], sp["pr_s"], sp["pi_s"], h0r, h0i]
    blk = t // ns_rows
    return pl.pallas_call(
        functools.partial(_s5_kernel, seg=nq, carry=False),
        out_shape=[jax.ShapeDtypeStruct(yg.shape, BF16),
                   jax.ShapeDtypeStruct((ns_rows, N_STATE), F32), jax.ShapeDtypeStruct((ns_rows, N_STATE), F32)],
        grid=(1,),
        in_specs=([pl.BlockSpec((ns_rows, D_SSM), lambda i: (blk, 0))] + [_const_spec(c.shape) for c in consts]
                  + [pl.BlockSpec(memory_space=pl.ANY)]),
        out_specs=[pl.BlockSpec((ns_rows, D_SSM), lambda i: (blk, 0)),
                   _const_spec((ns_rows, N_STATE)), _const_spec((ns_rows, N_STATE))],
        input_output_aliases={len(consts) + 1: 0},
        compiler_params=_cparams("arbitrary"),
    )(u, *consts, yg)


N_SPAIR = CMP_STRIDE // 2


def _compress_part(get_x2, w_ref, a0, a1, part, rows):
    base = pl.multiple_of(part * rows, rows)
    for kv in range(2):
        acc = None
        for sp in range(N_SPAIR):
            term = _dot(get_x2(kv, sp), w_ref[kv, sp])
            acc = term if acc is None else acc + term
        a0[pl.ds(base, rows), kv * D_KV:(kv + 1) * D_KV] = acc[:, 0:D_KV]
        a1[pl.ds(base, rows), kv * D_KV:(kv + 1) * D_KV] = acc[:, D_KV:]


def _compress_finish(a0, a1, w_ref, pe_ref, gk_ref, kcc_ref, vcc_ref, nb):
    bias = []
    for j in range(CMP_BLOCK // CMP_STRIDE):
        halves = []
        for kv in range(2):
            b = jnp.zeros((8, 2 * D_KV), F32)
            for sp in range(N_SPAIR):
                r = (j * 2 + kv) * N_SPAIR + sp
                b = b + _dot(jnp.broadcast_to(pe_ref[r:r + 1, :], (8, 2 * D_KV)).astype(BF16), w_ref[kv, sp])
            halves.append(b[0:1, j * D_KV:(j + 1) * D_KV])
        bias.append(jnp.concatenate(halves, axis=1))
    out = (a0[...] + bias[0]) + pltpu.roll(a1[...] + bias[1], nb - 1, 0)
    valid = lax.broadcasted_iota(jnp.int32, (nb, 1), 0) < nb - 1
    out = jnp.where(valid, out, 0.0)
    kcc_ref[0] = _halfnorm(out[:, 0:D_KV], gk_ref[...]).astype(BF16)
    vcc_ref[0] = out[:, D_KV:].astype(BF16)


def _compress_rows_kernel(xk_ref, xv_ref, w_ref, pe_ref, gk_ref, kcc_ref, vcc_ref, a0, a1, *, nsplit, nb):
    part = pl.program_id(0)
    rows = nb // nsplit

    def get_x2(kv, sp):
        x_ref = (xk_ref, xv_ref)[kv]
        pair = [x_ref[pl.ds(2 * sp + si, rows, stride=CMP_STRIDE), :] for si in range(2)]
        return jnp.concatenate(pair, axis=1).astype(BF16)
    _compress_part(get_x2, w_ref, a0, a1, part, rows)

    @pl.when(part == nsplit - 1)
    def _():
        _compress_finish(a0, a1, w_ref, pe_ref, gk_ref, kcc_ref, vcc_ref, nb)


def _compress_rows(kv_cmp, t, cw):
    nb = t // CMP_STRIDE
    nsplit = 2
    return pl.pallas_call(
        functools.partial(_compress_rows_kernel, nsplit=nsplit, nb=nb),
        out_shape=[jax.ShapeDtypeStruct((1, nb, D_KV), BF16), jax.ShapeDtypeStruct((1, nb, D_KV), BF16)],
        grid=(nsplit,),
        in_specs=[pl.BlockSpec((t // nsplit, D_KV), lambda i: (i, 0)),
                  pl.BlockSpec((t // nsplit, D_KV), lambda i: (i, 1)),
                  _const_spec(cw["w"].shape), _const_spec(cw["pe"].shape), _const_spec((1, LANE))],
        out_specs=[_const_spec((1, nb, D_KV)), _const_spec((1, nb, D_KV))],
        scratch_shapes=[pltpu.VMEM((nb, 2 * D_KV), F32), pltpu.VMEM((nb, 2 * D_KV), F32)],
        compiler_params=_cparams("arbitrary"),
    )(kv_cmp, kv_cmp, cw["w"], cw["pe"], cw["gk"])


def _compress_kernel(pt_ref, pages_ref, w_ref, pe_ref, gk_ref, perm_ref,
                     kcc_ref, vcc_ref, buf, sem, x_scr, a0, a1, *, pps, nsplit, nb):
    s = pl.program_id(0)
    nsteps = pl.num_programs(0)

    def page_copy(step, p, slot):
        b = step // nsplit
        part = step % nsplit
        pg = pt_ref[b, part * pps + p]
        col = pl.multiple_of((p % 2) * PAGE_SIZE, PAGE_SIZE)
        return pltpu.make_async_copy(pages_ref.at[pg], buf.at[slot, p // 2, :, pl.ds(col, PAGE_SIZE)],
                                     sem.at[slot])

    def start(step, slot):
        def body(p, c):
            page_copy(step, p, slot).start()
            return c
        lax.fori_loop(0, pps, body, 0)

    def wait(step, slot):
        def body(p, c):
            page_copy(step, p, slot).wait()
            return c
        lax.fori_loop(0, pps, body, 0)

    @pl.when(s == 0)
    def _():
        start(s, 0)

    @pl.when(s + 1 < nsteps)
    def _():
        start(s + 1, (s + 1) % 2)

    slot = s % 2
    wait(s, slot)
    part = s % nsplit

    def to_rows(q, c):
        xt = buf[slot, q].astype(BF16)
        xp = _dot_nt(perm_ref[...], xt).astype(BF16)
        base = pl.multiple_of(q * CMP_STRIDE, CMP_STRIDE)
        for s2 in range(CMP_STRIDE):
            for kv in range(2):
                x_scr[kv, s2 // 2, pl.ds(base, CMP_STRIDE), (s2 % 2) * D_KV:(s2 % 2 + 1) * D_KV] = (
                    xp[s2 * CMP_STRIDE:(s2 + 1) * CMP_STRIDE, kv * D_KV:(kv + 1) * D_KV])
        return c
    lax.fori_loop(0, pps // 2, to_rows, 0, unroll=8)
    _compress_part(lambda kv, sp: x_scr[kv, sp], w_ref, a0, a1, part, nb // nsplit)

    @pl.when(part == nsplit - 1)
    def _():
        _compress_finish(a0, a1, w_ref, pe_ref, gk_ref, kcc_ref, vcc_ref, nb)


def _compress_paged(page_table, pages, cw):
    bsz, n_pages = page_table.shape
    nb = n_pages * (PAGE_SIZE // CMP_STRIDE)
    nsplit = 2
    pps = n_pages // nsplit
    const = lambda shape: pl.BlockSpec(shape, lambda s, pt: (0,) * len(shape), pipeline_mode=pl.Buffered(1))
    grid_spec = pltpu.PrefetchScalarGridSpec(
        num_scalar_prefetch=1,
        grid=(bsz * nsplit,),
        in_specs=[pl.BlockSpec(memory_space=pl.ANY), const(cw["w"].shape), const(cw["pe"].shape),
                  const((1, LANE)), const(cw["perm"].shape)],
        out_specs=[pl.BlockSpec((1, nb, D_KV), lambda s, pt: (s // nsplit, 0, 0)),
                   pl.BlockSpec((1, nb, D_KV), lambda s, pt: (s // nsplit, 0, 0))],
        scratch_shapes=[pltpu.VMEM((2, pps // 2, 2 * D_KV, 2 * PAGE_SIZE), F32), pltpu.SemaphoreType.DMA((2,)),
                        pltpu.VMEM((2, N_SPAIR, nb // nsplit, 2 * D_KV), BF16),
                        pltpu.VMEM((nb, 2 * D_KV), F32), pltpu.VMEM((nb, 2 * D_KV), F32)],
    )
    return pl.pallas_call(
        functools.partial(_compress_kernel, pps=pps, nsplit=nsplit, nb=nb),
        out_shape=[jax.ShapeDtypeStruct((bsz, nb, D_KV), BF16), jax.ShapeDtypeStruct((bsz, nb, D_KV), BF16)],
        grid_spec=grid_spec,
        compiler_params=_cparams("arbitrary"),
    )(page_table, pages, cw["w"], cw["pe"], cw["gk"], cw["perm"])


def _heads_rows(qp):
    return jnp.concatenate([qp[:, h * LANE:(h + 1) * LANE] for h in range(N_HEADS)], axis=0).astype(BF16)


def _slope2(h):
    return LOG2E * 2.0 ** (-8.0 * (h + 1) / N_HEADS)


def _branch(q_all, tq, pieces):
    s_all = [_dot(q_all, k) if fm else _dot_nt(q_all, k) for k, _, _, _, fm in pieces]
    es, invs = [], []
    for h in range(N_HEADS):
        rows = slice(h * tq, (h + 1) * tq)
        sm = [jnp.where(pc[3], s[rows] - _slope2(h) * pc[2], NEG) for s, pc in zip(s_all, pieces)]
        m = functools.reduce(jnp.maximum, [jnp.max(x, axis=-1, keepdims=True) for x in sm])
        m = jnp.maximum(m, HALF_NEG)
        e = [jnp.exp2(x - m) for x in sm]
        l = functools.reduce(lambda a, b: a + b, [jnp.sum(x, axis=-1, keepdims=True) for x in e])
        es.append(e)
        invs.append(1.0 / jnp.maximum(l, 1e-30))
    o = None
    for i, (_, v, _, _, fm) in enumerate(pieces):
        p = jnp.concatenate([es[h][i] for h in range(N_HEADS)], axis=0).astype(BF16)
        t = _dot_nt(p, v) if fm else _dot(p, v)
        o = t if o is None else o + t
    return o * jnp.concatenate(invs, axis=0), es, invs


def _cmp_win_topk(q_all, tq, qpos, gn, kcc, vcc, mband, win_pieces, nb, ns, topn):
    n_io = lax.broadcasted_iota(jnp.int32, (1, kcc.shape[0]), 1)
    kc_end = n_io * CMP_STRIDE + (CMP_BLOCK - 1)
    mask_c = (qpos >= kc_end) & (n_io < nb - 1)
    kdist_c = (qpos[0:1, :] - kc_end).astype(F32)
    o_c, es, invs = _branch(q_all, tq, [(kcc, vcc, kdist_c, mask_c, False)])
    o_w, _, _ = _branch(q_all, tq, win_pieces)
    imps = []
    for g in range(N_KV):
        psum = es[HPG * g][0] * invs[HPG * g]
        for hh in range(1, HPG):
            psum = psum + es[HPG * g + hh][0] * invs[HPG * g + hh]
        hi = psum.astype(BF16)
        r = psum - hi.astype(F32)
        mid = r.astype(BF16)
        lo = (r - mid.astype(F32)).astype(BF16)
        imps.append(_dot(hi, mband) + _dot(mid, mband) + _dot(lo, mband))
    imp = jnp.concatenate(imps, axis=0)
    blk = lax.broadcasted_iota(jnp.int32, (1, ns), 1)
    cur = jnp.concatenate([jnp.right_shift(qpos, 6)] * N_KV, axis=0)
    forced = (blk == 0) | (blk == cur) | (blk == cur - 1)
    score = jnp.where(forced, imp + SEL_BIG, jnp.where(blk <= cur, imp, -SEL_BIG))
    n_rows = score.shape[0]
    pad = (-n_rows) % LANE
    if pad:
        score = jnp.concatenate([score, jnp.zeros((pad, ns), F32)], axis=0)
    score = score.T
    blkf = lax.broadcasted_iota(jnp.int32, (ns, 1), 0).astype(F32)
    for _ in range(topn):
        m = jnp.max(score, axis=0, keepdims=True)
        first = jnp.min(jnp.where(score == m, blkf, float(ns)), axis=0, keepdims=True)
        score = jnp.where(blkf == first, -jnp.inf, score)
    sel = jnp.where(score == -jnp.inf, 1.0, 0.0).T[:n_rows]
    sel = jnp.where(blk <= cur, sel, 0.0)
    outs = []
    for h in range(N_HEADS):
        rows = slice(h * tq, (h + 1) * tq)
        outs.append(gn[:, h:h + 1] * o_c[rows] + gn[:, 2 * N_HEADS + h:2 * N_HEADS + h + 1] * o_w[rows])
    return jnp.concatenate(outs, axis=1), sel


def _e1_prompt_kernel(qp_ref, gn_ref, kcc_ref, vcc_ref, mband_ref, kvw_ref, ocw_ref, sel_ref, flag_ref,
                      *, nb, ns, widths):
    tq = Q_TILE
    s0 = pl.program_id(0) * tq
    qpos = s0 + lax.broadcasted_iota(jnp.int32, (tq, 1), 0)

    def run(width):
        q_all = _heads_rows(qp_ref[...])
        start = pl.multiple_of(jnp.maximum(s0 - WINDOW, 0), Q_TILE)
        kw = kvw_ref[pl.ds(start, WIN_KEYS), 0:D_KV]
        vw = kvw_ref[pl.ds(start, WIN_KEYS), D_KV:]
        kpos = start + lax.broadcasted_iota(jnp.int32, (1, WIN_KEYS), 1)
        dist_w = qpos - kpos
        mask_w = (dist_w >= 0) & (dist_w < WINDOW)
        win_pieces = [(kw, vw, (s0 - kpos).astype(F32), mask_w, False)]
        ocw, sel = _cmp_win_topk(q_all, tq, qpos, gn_ref[...], kcc_ref[0, 0:width, :], vcc_ref[0, 0:width, :],
                                 mband_ref[0:width, :], win_pieces, nb, ns, TOP_N)
        ocw_ref[...] = ocw
        sel_ref[0] = sel[:tq]
        sel_ref[1] = sel[tq:]
        colany = jnp.max(sel, axis=0, keepdims=True)
        j_io = lax.broadcasted_iota(jnp.int32, (ns, LANE), 0)
        c_io = lax.broadcasted_iota(jnp.int32, (ns, LANE), 1)
        grp = jnp.where(jnp.right_shift(j_io, 2) == c_io, 1.0, 0.0).astype(BF16)
        cnt = _dot(jnp.broadcast_to(colany, (8, ns)).astype(BF16), grp)
        flag_ref[0] = (cnt > 0.5).astype(jnp.int32)

    nvar = len(widths)
    variant = jnp.minimum(pl.program_id(0) // (pl.num_programs(0) // nvar), nvar - 1)
    for vi, width in enumerate(widths):
        pl.when(variant == vi)(functools.partial(run, width))


def _e1_prompt(qp, gn, kcc, vcc, mband, kvw_bf, t):
    nb, ns = mband.shape
    nt = t // Q_TILE
    nvar = max(v for v in (1, 2, 4) if nb % (v * LANE) == 0 and nt % v == 0)
    widths = tuple(nb * (v + 1) // nvar for v in range(nvar))
    row = lambda w: pl.BlockSpec((Q_TILE, w), lambda i: (i, 0))
    return pl.pallas_call(
        functools.partial(_e1_prompt_kernel, nb=nb, ns=ns, widths=widths),
        out_shape=[jax.ShapeDtypeStruct((t, Q_PAD), F32), jax.ShapeDtypeStruct((N_KV, t, ns), F32),
                   jax.ShapeDtypeStruct((nt, 8, LANE), jnp.int32)],
        grid=(nt,),
        in_specs=[row(Q_PAD), row(LANE), _const_spec((1, nb, D_KV)), _const_spec((1, nb, D_KV)),
                  _const_spec(mband.shape), _const_spec(kvw_bf.shape)],
        out_specs=[row(Q_PAD), pl.BlockSpec((N_KV, Q_TILE, ns), lambda i: (0, i, 0)),
                   pl.BlockSpec((1, 8, LANE), lambda i: (i, 0, 0))],
        compiler_params=_cparams("parallel"),
    )(qp, gn, kcc, vcc, mband, kvw_bf)


def _pad_rows(x, rows):
    return jnp.concatenate([x, jnp.zeros((rows - x.shape[0], x.shape[1]), x.dtype)], axis=0)


def _e1_sample_kernel(qp_ref, gn_ref, kcc_ref, vcc_ref, mband_ref, cwin_ref, nwin_ref, ocw_ref, sel_ref,
                      *, nb, ns, nq, past):
    qi = lax.broadcasted_iota(jnp.int32, (nq, 1), 0)
    qpos = past + qi
    q_all = _heads_rows(qp_ref[...])
    cw = cwin_ref[0]
    w_buf = cw.shape[1]
    j_c = lax.broadcasted_iota(jnp.int32, (1, w_buf), 1)
    dist_cw = qi + (w_buf - j_c)
    mask_cw = (dist_cw >= 0) & (dist_cw < WINDOW)
    nw = _pad_rows(nwin_ref[...], LANE)
    j_n = lax.broadcasted_iota(jnp.int32, (1, LANE), 1)
    mask_nw = (qi >= j_n) & (j_n < nq)
    pieces = [(cw[0:D_KV, :].astype(BF16), cw[D_KV:, :].astype(BF16), (w_buf - j_c).astype(F32), mask_cw, True),
              (nw[:, 0:D_KV].astype(BF16), nw[:, D_KV:].astype(BF16), (-j_n).astype(F32), mask_nw, False)]
    ocw, sel = _cmp_win_topk(q_all, nq, qpos, gn_ref[...], kcc_ref[0], vcc_ref[0], mband_ref[...],
                             pieces, nb, ns, TOP_N - 1)
    ocw_ref[...] = ocw
    sel_ref[0, 0] = sel[:nq]
    sel_ref[0, 1] = sel[nq:]


def _e1_sample(qp, gn, kcc, vcc, mband, cache_win, win_off, kv_win, t, bsz, nq, past):
    nb, ns = mband.shape
    w_buf = cache_win.shape[2]
    off = t // nq
    row = lambda w: pl.BlockSpec((nq, w), lambda b: (off + b, 0))
    per_b = lambda shape: pl.BlockSpec((1,) + shape, lambda b: (b,) + (0,) * len(shape))
    return pl.pallas_call(
        functools.partial(_e1_sample_kernel, nb=nb, ns=ns, nq=nq, past=past),
        out_shape=[jax.ShapeDtypeStruct((bsz * nq, Q_PAD), F32), jax.ShapeDtypeStruct((bsz, N_KV, nq, ns), F32)],
        grid=(bsz,),
        in_specs=[row(Q_PAD), row(LANE), per_b((nb, D_KV)), per_b((nb, D_KV)), _const_spec(mband.shape),
                  pl.BlockSpec((1, 2 * D_KV, w_buf), lambda b: (win_off + b, 0, 0)), row(2 * D_KV)],
        out_specs=[pl.BlockSpec((nq, Q_PAD), lambda b: (b, 0)), per_b((N_KV, nq, ns))],
        compiler_params=_cparams("parallel"),
    )(qp, gn, kcc, vcc, mband, cache_win, kv_win)


def _online_step(s, kdist, masks, v_ones, m_ref, acc_ref, tq, feature_major=False):
    reps = s.shape[1] // LANE
    ps, alphas = [], []
    for h in range(N_HEADS):
        rows = slice(h * tq, (h + 1) * tq)
        sm = jnp.where(masks[h // HPG], s[rows] - _slope2(h) * kdist, NEG)
        m_old = m_ref[rows, :]
        m_new = jnp.maximum(m_old, jnp.max(sm, axis=-1, keepdims=True))
        alphas.append(jnp.exp2(m_old - m_new))
        ps.append(jnp.exp2(sm - jnp.tile(m_new, (1, reps))))
        m_ref[rows, :] = m_new
    p_all = jnp.concatenate(ps, axis=0).astype(BF16)
    a_all = jnp.concatenate(alphas, axis=0)
    pv = _dot_nt(p_all, v_ones) if feature_major else _dot(p_all, v_ones)
    acc_ref[...] = jnp.tile(a_all, (1, 2)) * acc_ref[...] + pv


def _block_to_keys(ns, first_blk, n_keys):
    j_io = lax.broadcasted_iota(jnp.int32, (ns, n_keys), 0)
    kb = first_blk + jnp.right_shift(lax.broadcasted_iota(jnp.int32, (ns, n_keys), 1), 6)
    return jnp.where(j_io == kb, 1.0, 0.0).astype(BF16)


def _init_online(m_ref, acc_ref):
    m_ref[...] = jnp.full(m_ref.shape, HALF_NEG, F32)
    acc_ref[...] = jnp.zeros(acc_ref.shape, F32)


def _finish_online(ocw, gn, acc_ref, tq):
    o_s = acc_ref[:, 0:LANE] * (1.0 / jnp.maximum(acc_ref[:, LANE:], 1e-30))
    outs = []
    for h in range(N_HEADS):
        g1 = gn[:, N_HEADS + h:N_HEADS + h + 1]
        outs.append(ocw[:, h * LANE:(h + 1) * LANE] + g1 * o_s[h * tq:(h + 1) * tq])
    return jnp.concatenate(outs, axis=1)


def _e2_prompt_kernel(order_ref, cnt_ref, qp_ref, sel_ref, gn_ref, ocw_ref, kvs_ref, out_ref, m_ref, acc_ref,
                      *, nchunks, ns, nt):
    tq = Q_TILE
    i = pl.program_id(0)
    s0 = i * tq
    qpos = s0 + lax.broadcasted_iota(jnp.int32, (tq, 1), 0)
    q_all = _heads_rows(qp_ref[...])
    _init_online(m_ref, acc_ref)
    n_act = jnp.where(i < nt, cnt_ref[jnp.minimum(i, nt - 1)], 0)
    blocks_per_chunk = SEL_CHUNK // SEL_BLOCK
    sel_bf = [sel_ref[g].astype(BF16) for g in range(N_KV)]

    def body(j, carry):
        c1 = order_ref[i * nchunks + 2 * j]
        paired = 2 * j + 1 < n_act
        c2 = jnp.where(paired, order_ref[i * nchunks + jnp.minimum(2 * j + 1, nchunks - 1)], c1)
        ks, vs, dists, masks = [], [], [], [[] for _ in range(N_KV)]
        for c, first_blk in ((c1, c1 * blocks_per_chunk), (c2, jnp.where(paired, c2 * blocks_per_chunk, ns))):
            base = pl.multiple_of(c * SEL_CHUNK, SEL_CHUNK)
            ks.append(kvs_ref[pl.ds(base, SEL_CHUNK), 0:D_KV])
            vs.append(kvs_ref[pl.ds(base, SEL_CHUNK), D_KV:])
            kpos = base + lax.broadcasted_iota(jnp.int32, (1, SEL_CHUNK), 1)
            dists.append((s0 - kpos).astype(F32))
            expand = _block_to_keys(ns, first_blk, SEL_CHUNK)
            causal = qpos >= kpos
            for g in range(N_KV):
                masks[g].append((_dot(sel_bf[g], expand) > 0.5) & causal)
        s = _dot_nt(q_all, jnp.concatenate(ks, axis=0))
        _online_step(s, jnp.concatenate(dists, axis=1), [jnp.concatenate(mg, axis=1) for mg in masks],
                     jnp.concatenate(vs, axis=0), m_ref, acc_ref, tq)
        return carry

    lax.fori_loop(0, (n_act + 1) // 2, body, 0)
    out_ref[...] = _finish_online(ocw_ref[...], gn_ref[...], acc_ref, tq)


def _e2_prompt(order, cnt, qp, sel, gn, ocw, kvs_bf, t):
    ns = sel.shape[2]
    nt = t // Q_TILE
    nchunks = order.shape[0] // nt
    row = lambda w: pl.BlockSpec((Q_TILE, w), lambda i, o, c: (i, 0))
    prow = lambda w: pl.BlockSpec((Q_TILE, w), lambda i, o, c: (jnp.minimum(i, nt - 1), 0))
    grid_spec = pltpu.PrefetchScalarGridSpec(
        num_scalar_prefetch=2,
        grid=(qp.shape[0] // Q_TILE,),
        in_specs=[row(Q_PAD), pl.BlockSpec((N_KV, Q_TILE, ns), lambda i, o, c: (0, jnp.minimum(i, nt - 1), 0)),
                  row(LANE), prow(Q_PAD), pl.BlockSpec(kvs_bf.shape, lambda i, o, c: (0, 0))],
        out_specs=row(Q_PAD),
        scratch_shapes=[pltpu.VMEM((N_HEADS * Q_TILE, LANE), F32), pltpu.VMEM((N_HEADS * Q_TILE, 2 * LANE), F32)],
    )
    return pl.pallas_call(
        functools.partial(_e2_prompt_kernel, nchunks=nchunks, ns=ns, nt=nt),
        out_shape=jax.ShapeDtypeStruct((qp.shape[0], Q_PAD), F32),
        grid_spec=grid_spec,
        compiler_params=_cparams("arbitrary"),
    )(order, cnt, qp, sel, gn, ocw, kvs_bf)


def _e2_sample_kernel(pt_ref, live_ref, pages_ref, qp_ref, sel_ref, gn_ref, ocw_ref, nsel_ref, expand_ref,
                      oatt_all_ref, out_ref, buf, sem, m_ref, acc_ref, *, pps, nsplit, nq):
    s = pl.program_id(0)
    nsteps = pl.num_programs(0)
    n_keys = pps * PAGE_SIZE

    def page_copy(step, p, slot):
        b = step // nsplit
        part = step % nsplit
        pg = pt_ref[b, part * pps + p]
        col = pl.multiple_of(p * PAGE_SIZE, PAGE_SIZE)
        return pltpu.make_async_copy(pages_ref.at[pg], buf.at[slot, :, pl.ds(col, PAGE_SIZE)], sem.at[slot])

    def start(step, slot):
        def body(p, c):
            page_copy(step, p, slot).start()
            return c
        lax.fori_loop(0, pps, body, 0)

    def wait(step, slot):
        def body(p, c):
            page_copy(step, p, slot).wait()
            return c
        lax.fori_loop(0, pps, body, 0)

    @pl.when((s == 0) & (live_ref[0] > 0))
    def _():
        start(s, 0)

    @pl.when((s + 1 < nsteps) & (live_ref[jnp.minimum(s + 1, nsteps - 1)] > 0))
    def _():
        start(s + 1, (s + 1) % 2)

    slot = s % 2
    part = s % nsplit
    qi = lax.broadcasted_iota(jnp.int32, (nq, 1), 0)
    q_all = _heads_rows(qp_ref[...])

    @pl.when(part == 0)
    def _():
        _init_online(m_ref, acc_ref)

    @pl.when(live_ref[s] > 0)
    def _():
        wait(s, slot)
        k = buf[slot, 0:D_KV, :].astype(BF16)
        v_ones = jnp.concatenate([buf[slot, D_KV:, :].astype(BF16), jnp.ones((D_KV, n_keys), BF16)], axis=0)
        sc = _dot(q_all, k)
        back = (nsplit - part) * n_keys - lax.broadcasted_iota(jnp.int32, (1, n_keys), 1)
        masks = [_dot(sel_ref[0, g, 0].astype(BF16), expand_ref[...]) > 0.5 for g in range(N_KV)]
        _online_step(sc, back.astype(F32), masks, v_ones, m_ref, acc_ref, nq, feature_major=True)

    @pl.when(part == nsplit - 1)
    def _():
        nw = _pad_rows(nsel_ref[...], LANE)
        j_n = lax.broadcasted_iota(jnp.int32, (1, LANE), 1)
        mask_n = (qi >= j_n) & (j_n < nq)
        s_n = _dot_nt(q_all, nw[:, 0:D_KV].astype(BF16))
        vn_ones = jnp.concatenate([nw[:, D_KV:].astype(BF16), jnp.ones((LANE, D_KV), BF16)], axis=1)
        _online_step(s_n, (-j_n).astype(F32), [mask_n, mask_n], vn_ones, m_ref, acc_ref, nq)
        out_ref[...] = _finish_online(ocw_ref[...], gn_ref[...], acc_ref, nq)


def _e2_sample(page_table, pages, qp, sel, gn, ocw, kv_sel, oatt, t, nq):
    bsz, n_pages = page_table.shape
    ns = sel.shape[3]
    nsplit = 8
    pps = n_pages // nsplit
    n_keys = pps * PAGE_SIZE
    nblk = ns // nsplit
    off = t // nq
    sel_parts = sel.reshape(bsz, N_KV, nq, nsplit, nblk).transpose(0, 1, 3, 2, 4)
    live = (jnp.max(sel_parts, axis=(1, 3, 4)) > 0).astype(jnp.int32).reshape(-1)
    expand = (jnp.arange(nblk)[:, None] == jnp.arange(n_keys)[None, :] // SEL_BLOCK).astype(BF16)
    row = lambda w: pl.BlockSpec((nq, w), lambda s, pt, lv: (off + s // nsplit, 0))
    grid_spec = pltpu.PrefetchScalarGridSpec(
        num_scalar_prefetch=2,
        grid=(bsz * nsplit,),
        in_specs=[pl.BlockSpec(memory_space=pl.ANY), row(Q_PAD),
                  pl.BlockSpec((1, N_KV, 1, nq, nblk), lambda s, pt, lv: (s // nsplit, 0, s % nsplit, 0, 0)),
                  row(LANE), pl.BlockSpec((nq, Q_PAD), lambda s, pt, lv: (s // nsplit, 0)), row(2 * D_KV),
                  pl.BlockSpec((nblk, n_keys), lambda s, pt, lv: (0, 0), pipeline_mode=pl.Buffered(1)),
                  pl.BlockSpec(memory_space=pl.ANY)],
        out_specs=row(Q_PAD),
        scratch_shapes=[pltpu.VMEM((2, 2 * D_KV, n_keys), F32), pltpu.SemaphoreType.DMA((2,)),
                        pltpu.VMEM((N_HEADS * nq, LANE), F32), pltpu.VMEM((N_HEADS * nq, 2 * LANE), F32)],
    )
    return pl.pallas_call(
        functools.partial(_e2_sample_kernel, pps=pps, nsplit=nsplit, nq=nq),
        out_shape=jax.ShapeDtypeStruct(oatt.shape, F32),
        grid_spec=grid_spec,
        input_output_aliases={9: 0},
        compiler_params=_cparams("arbitrary"),
    )(page_table, live, pages, qp, sel_parts, gn, ocw, kv_sel, expand, oatt)


def _post_kernel(h_ref, yg_ref, oatt_ref, ga_ref, gb_ref, wglu_ref, watt_ref, wout_ref, o_ref):
    gl = _dot(yg_ref[...], wglu_ref[...])
    br_a = gl[:, :D_MODEL] * jax.nn.sigmoid(gl[:, D_MODEL:])
    br_b = _dot(oatt_ref[...].astype(BF16), watt_ref[...])
    merged = (ga_ref[...] * br_a + gb_ref[...] * br_b).astype(BF16)
    o_ref[...] = h_ref[...] + _dot(merged, wout_ref[...])


def _post(h, yg, oatt, ga, gb, wglu, watt, wout):
    n = h.shape[0]
    tm = _pick_tile(n, 256)
    row = lambda w: pl.BlockSpec((tm, w), lambda i: (i, 0))
    return pl.pallas_call(
        _post_kernel,
        out_shape=jax.ShapeDtypeStruct((n, D_MODEL), F32),
        grid=(n // tm,),
        in_specs=[row(D_MODEL), row(D_SSM), row(Q_PAD), row(D_MODEL), row(D_MODEL),
                  _const_spec(wglu.shape), _const_spec(watt.shape), _const_spec(wout.shape)],
        out_specs=row(D_MODEL),
        compiler_params=_cparams("parallel"),
    )(h, yg, oatt, ga, gb, wglu, watt, wout)


def _head_pad_index():
    h = jnp.arange(N_HEADS)[:, None]
    d = jnp.arange(HEAD_DIM)[None, :]
    return (LANE * h + HEAD_DIM * (h // HPG) + d).reshape(-1)


def _prep_mix_weights(w_in, qk_norm):
    idx = _head_pad_index()
    wq = jnp.zeros((D_MODEL, Q_PAD), F32).at[:, idx].set(w_in[:, D_SSM:D_SSM + N_HEADS * HEAD_DIM])
    c0 = D_SSM + N_HEADS * HEAD_DIM
    c1 = c0 + 6 * D_KV
    c2 = c1 + 3 * N_HEADS
    wgn = jnp.zeros((D_MODEL, LANE), F32).at[:, :3 * N_HEADS].set(w_in[:, c1:c2])
    w = jnp.concatenate([w_in[:, :D_SSM], wq, w_in[:, c0:c1], wgn, w_in[:, c2:]], axis=1).astype(BF16)
    gq = jnp.zeros((Q_PAD,), F32).at[idx].set(jnp.tile(qk_norm[0] * (HEAD_DIM ** -0.5 * LOG2E), N_HEADS))[None]
    gks = jnp.tile(qk_norm[2], N_KV)[None]
    gkw = jnp.tile(qk_norm[3], N_KV)[None]
    return w, gq, gks, gkw


def _prep_att_out(w_att_out):
    return jnp.zeros((Q_PAD, D_MODEL), F32).at[_head_pad_index()].set(w_att_out).astype(BF16)


def _prep_compress(pe_k, pe_v, w_k, w_v, gain_k):
    nj = CMP_BLOCK // CMP_STRIDE
    eye = jnp.eye(N_KV, dtype=F32)

    def pair_maps(w):
        w5 = w.reshape(nj, N_SPAIR, 2, HEAD_DIM, HEAD_DIM)
        return jnp.einsum("jpsde,gh->psgdjhe", w5, eye).reshape(N_SPAIR, 2 * D_KV, 2 * D_KV)

    def pair_pe(pe):
        p5 = pe.reshape(nj, N_SPAIR, 2, 1, HEAD_DIM)
        return jnp.broadcast_to(p5, (nj, N_SPAIR, 2, N_KV, HEAD_DIM)).reshape(nj, N_SPAIR, 2 * D_KV)

    w = jnp.stack([pair_maps(w_k), pair_maps(w_v)], axis=0)
    pe = jnp.stack([pair_pe(pe_k), pair_pe(pe_v)], axis=1).reshape(-1, 2 * D_KV)
    r = jnp.arange(2 * PAGE_SIZE)
    perm = (r[None, :] == (CMP_STRIDE * (r % CMP_STRIDE) + r // CMP_STRIDE)[:, None]).astype(BF16)
    return {"w": w.astype(BF16), "pe": pe,
            "gk": jnp.tile(gain_k, N_KV)[None], "perm": perm}


def _prep_s5(a_re, a_im, log_dt, b_re, b_im, c_re, c_im, d, nq):
    dt = jnp.exp(log_dt)[:, None]
    mag = jnp.exp(a_re * dt)
    lr = mag * jnp.cos(a_im * dt)
    li = mag * jnp.sin(a_im * dt)
    den = a_re * a_re + a_im * a_im
    fr = ((lr - 1.0) * a_re + li * a_im) / den
    fi = (li * a_re - (lr - 1.0) * a_im) / den
    bbr = fr[..., None] * b_re - fi[..., None] * b_im
    bbi = fr[..., None] * b_im + fi[..., None] * b_re
    eye = jnp.eye(N_SSM_GROUPS, dtype=F32)
    blk_b = lambda m: jnp.einsum("gpc,gh->gchp", m, eye).reshape(D_SSM, N_STATE)
    blk_c = lambda m: jnp.einsum("gcp,gh->gphc", m, eye).reshape(N_STATE, D_SSM)

    def lam_pow(k):
        kk = k.astype(F32)[:, None, None]
        m = jnp.exp(a_re * dt * kk)
        th = a_im * dt * kk
        return (m * jnp.cos(th)).reshape(-1, N_STATE), (m * jnp.sin(th)).reshape(-1, N_STATE)

    def step_table(seg):
        ks = []
        dd = 1
        while dd < seg:
            ks.append(dd)
            dd *= 2
        re, im = lam_pow(jnp.array(ks))
        live = jnp.arange(seg)[None, :, None] >= jnp.array(ks)[:, None, None]
        tab = jnp.stack([jnp.where(live, re[:, None, :], 0.0), jnp.where(live, im[:, None, :], 0.0)], axis=1)
        return tab.reshape(-1, N_STATE)

    sp = {"bmat": jnp.concatenate([blk_b(bbr), blk_b(bbi)], axis=1).astype(BF16),
          "cr": blk_c(c_re).astype(BF16), "ci": (-blk_c(c_im)).astype(BF16), "d": d[None],
          "lam_p": step_table(SCAN_SEG), "lam_s": step_table(nq)}
    sp["pr_p"], sp["pi_p"] = lam_pow(jnp.arange(SCAN_SEG) + 1)
    return sp, lam_pow


def _band_matrix(nb, ns):
    ratio = SEL_BLOCK // CMP_STRIDE
    lo = CMP_BLOCK // CMP_STRIDE - 1
    c = jnp.arange(nb)[:, None]
    j = jnp.arange(ns)[None, :]
    return ((c >= ratio * j - lo) & (c <= ratio * j + ratio - 1)).astype(BF16)


def kernel(x_prompt, x_sample, cache_kv_cmp, cache_kv_sel, cache_kv_win, state_ssm_re, state_ssm_im, page_table, p_prompt, p_sample, norm_ffn1, w_ffn1_in, w_ffn1_out, norm_mix, w_in, qk_norm, ssm_a_re, ssm_a_im, ssm_log_dt, ssm_b_re, ssm_b_im, ssm_c_re, ssm_c_im, ssm_d, w_glu, cmp_pe_k, cmp_pe_v, cmp_w_k, cmp_w_v, w_att_out, w_out, norm_ffn2, w_ffn2_in, w_ffn2_out, norm_ple, w_ple_gate, w_ple_proj):
    bp, t = x_prompt.shape[:2]
    bsz, nq = x_sample.shape[:2]
    n_pages = page_table.shape[1]
    past = n_pages * PAGE_SIZE
    n_pool = cache_kv_cmp.shape[1]
    w_buf = cache_kv_win.shape[2]
    ns_rows = bsz * nq
    assert bp == 1 and t % SEL_CHUNK == 0 and t >= WIN_KEYS and t % ns_rows == 0 and nq < CMP_STRIDE
    assert past == t and w_buf == WINDOW and ns_rows % Q_TILE == 0
    nb = t // CMP_STRIDE
    ns = t // SEL_BLOCK
    nt = t // Q_TILE
    nchunks = t // SEL_CHUNK
    mband = _band_matrix(nb, ns)
    page_table = page_table.astype(jnp.int32)
    feat_major = lambda c: jnp.transpose(c, (0, 1, 3, 4, 5, 2)).reshape(DEPTH * c.shape[1], 2 * D_KV, c.shape[2])
    pages_cmp = feat_major(cache_kv_cmp)
    pages_sel = feat_major(cache_kv_sel)
    pages_win = feat_major(cache_kv_win)

    h = (x_prompt[0], x_sample.reshape(ns_rows, D_MODEL))
    p_all = (p_prompt.reshape(DEPTH * t, D_PLE), p_sample.reshape(DEPTH * ns_rows, D_PLE))
    st_p = [[] for _ in range(5)]
    st_s = [[] for _ in range(5)]
    for i in range(DEPTH):
        row1 = lambda a: a[i][None]
        h = _ffn(h, row1(norm_ffn1), w_ffn1_in[i].astype(BF16), w_ffn1_out[i].astype(BF16), t, ns_rows)
        w_mix, gq, gks, gkw = _prep_mix_weights(w_in[i], qk_norm[i])
        u, qp, kv_cmp, kv_sel, kv_win, kvs_bf, kvw_bf, gn, ga, gb = _mix_in(h, row1(norm_mix), w_mix, gq, gks, gkw)

        sp, lam_pow = _prep_s5(ssm_a_re[i], ssm_a_im[i], ssm_log_dt[i], ssm_b_re[i], ssm_b_im[i],
                               ssm_c_re[i], ssm_c_im[i], ssm_d[i], nq)
        sp["pr_s"], sp["pi_s"] = lam_pow(jnp.arange(ns_rows) % nq + 1)
        yg, hr_p, hi_p = _s5_prompt(u, t, sp)
        h0r = jnp.repeat(state_ssm_re[i].reshape(bsz, N_STATE), nq, axis=0)
        h0i = jnp.repeat(state_ssm_im[i].reshape(bsz, N_STATE), nq, axis=0)
        yg, hr_s, hi_s = _s5_sample(u, yg, t, ns_rows, nq, h0r, h0i, sp)

        cw = _prep_compress(cmp_pe_k[i], cmp_pe_v[i], cmp_w_k[i], cmp_w_v[i], qk_norm[i, 1])
        layer_pages = page_table + i * n_pool
        kcc_p, vcc_p = _compress_rows(kv_cmp, t, cw)
        kcc_s, vcc_s = _compress_paged(layer_pages, pages_cmp, cw)
        ocw_p, sel_p, flags = _e1_prompt(qp, gn, kcc_p, vcc_p, mband, kvw_bf, t)
        ocw_s, sel_s = _e1_sample(qp, gn, kcc_s, vcc_s, mband, pages_win, i * bsz, kv_win, t, bsz, nq, past)
        idle = flags[:, 0, :nchunks] == 0
        order = jnp.argsort(idle, axis=1, stable=True).astype(jnp.int32).reshape(-1)
        cnt = (nchunks - jnp.sum(idle, axis=1)).astype(jnp.int32)
        oatt = _e2_prompt(order, cnt, qp, sel_p, gn, ocw_p, kvs_bf, t)
        oatt = _e2_sample(layer_pages, pages_sel, qp, sel_s, gn, ocw_s, kv_sel, oatt, t, nq)

        h = _post(h, yg, oatt, ga, gb,
                  w_glu[i].astype(BF16), _prep_att_out(w_att_out[i]), w_out[i].astype(BF16))
        h = _ffn(h, row1(norm_ffn2), w_ffn2_in[i].astype(BF16), w_ffn2_out[i].astype(BF16), t, ns_rows,
                 (p_all, row1(norm_ple), w_ple_gate[i].astype(BF16), w_ple_proj[i].astype(BF16)), p_layer=i)

        kv5 = lambda a, lead: a.reshape(lead + (2, N_KV, HEAD_DIM))
        st_p[0].append(kv5(kv_cmp[:t], (1, t)))
        st_p[1].append(kv5(kv_sel[:t], (1, t)))
        st_p[2].append(kv5(kv_win[t - min(WINDOW, t):t], (1, min(WINDOW, t))))
        st_p[3].append(hr_p.reshape(1, N_SSM_GROUPS, P_STATE))
        st_p[4].append(hi_p.reshape(1, N_SSM_GROUPS, P_STATE))
        st_s[0].append(kv5(kv_cmp[t:], (bsz, nq)))
        st_s[1].append(kv5(kv_sel[t:], (bsz, nq)))
        st_s[2].append(jnp.concatenate([cache_kv_win[i, :, nq:], kv5(kv_win[t:], (bsz, nq))], axis=1))
        st_s[3].append(hr_s[nq - 1::nq].reshape(bsz, N_SSM_GROUPS, P_STATE))
        st_s[4].append(hi_s[nq - 1::nq].reshape(bsz, N_SSM_GROUPS, P_STATE))

    outs_p = [jnp.stack(a) for a in st_p]
    outs_s = [jnp.stack(a) for a in st_s]
    y_prompt = h[:t][None]
    y_sample = h[t:].reshape(bsz, nq, D_MODEL)
    return (y_prompt, y_sample, *outs_p, *outs_s)
```

```python
import functools

import jax
import jax.numpy as jnp
from jax import lax
from jax.experimental import pallas as pl
from jax.experimental.pallas import tpu as pltpu

F32 = jnp.float32
BF16 = jnp.bfloat16

D_MODEL = 1024
DEPTH = 2
D_SSM = 512
SSM_GROUP = 16
N_SSM_GROUPS = 32
P_STATE = 64
N_STATE = N_SSM_GROUPS * P_STATE
N_HEADS = 8
HEAD_DIM = 64
N_KV = 2
HPG = 4
D_KV = 128
CMP_BLOCK = 32
CMP_STRIDE = 16
SEL_BLOCK = 64
TOP_N = 16
WINDOW = 512
PAGE_SIZE = 128
SEL_BIG = 1e4
D_FF = 2816
D_PLE = 256
RMS_EPS = 1e-6
NEG = -1e30
HALF_NEG = -0.5e30
LOG2E = 1.4426950408889634

LANE = 128
Q_TILE = 128
SCAN_SEG = 8
SEL_CHUNK = 256
WIN_KEYS = WINDOW + Q_TILE
Q_PAD = N_HEADS * LANE
VMEM_LIMIT = 56 * 2 ** 20


def _cparams(*sem):
    return pltpu.CompilerParams(dimension_semantics=sem, vmem_limit_bytes=VMEM_LIMIT)


def _dot(a, b):
    return jnp.dot(a, b, preferred_element_type=F32)


def _dot_nt(a, b):
    return lax.dot_general(a, b, (((1,), (1,)), ((), ())), preferred_element_type=F32)


def _pick_tile(n, target):
    for t in range(min(n, target), 15, -1):
        if n % t == 0 and t % 16 == 0:
            return t
    raise ValueError(f"no row tile for {n}")


def _const_spec(shape):
    nd = len(shape)
    return pl.BlockSpec(shape, lambda *_: (0,) * nd, pipeline_mode=pl.Buffered(1))


def _rms(x, g):
    ms = jnp.mean(x * x, axis=-1, keepdims=True)
    return x * lax.rsqrt(ms + RMS_EPS) * g


def _halfnorm(x, gain):
    lo = lax.broadcasted_iota(jnp.int32, (1, LANE), 1) < HEAD_DIM
    x2 = x * x
    s_lo = jnp.sum(jnp.where(lo, x2, 0.0), axis=-1, keepdims=True)
    s_hi = jnp.sum(jnp.where(lo, 0.0, x2), axis=-1, keepdims=True)
    ms = jnp.where(lo, s_lo, s_hi) * (1.0 / HEAD_DIM)
    return x * lax.rsqrt(ms + RMS_EPS) * gain


def _row_tile(refs, lead_tiles):
    if len(refs) == 1:
        return refs[0][...]
    return jnp.where(pl.program_id(0) < lead_tiles, refs[0][...], refs[1][...])


def _ffn_kernel(*refs, ple, h_parts, p_parts, lead_tiles):
    refs = list(refs)
    h_refs = [refs.pop(0) for _ in range(h_parts)]
    if ple:
        g_ref, wi_ref, wo_ref = refs[:3]
        p_refs = refs[3:3 + p_parts]
        gp_ref, wg_ref, wp_ref, o_ref = refs[3 + p_parts:]
    else:
        g_ref, wi_ref, wo_ref, o_ref = refs
    h = _row_tile(h_refs, lead_tiles)
    xn = _rms(h, g_ref[...]).astype(BF16)
    a = _dot(xn, wi_ref[:, :D_FF])
    b = _dot(xn, wi_ref[:, D_FF:])
    act = (a * jax.nn.sigmoid(a) * b).astype(BF16)
    h = h + 0.5 * _dot(act, wo_ref[...])
    if ple:
        xg = _rms(h, gp_ref[...]).astype(BF16)
        gate = jax.nn.sigmoid(_dot(xg, wg_ref[...]))
        h = h + gate * _dot(_row_tile(p_refs, lead_tiles).astype(BF16), wp_ref[...])
    o_ref[...] = h


def _ffn(h, g, wi, wo, n_prompt, n_sample, ple_args=None, p_layer=0):
    h_parts = list(h) if isinstance(h, (tuple, list)) else [h]
    n = n_prompt + n_sample
    tm = _pick_tile(n, 256)
    assert n_prompt % tm == 0 and n_sample % tm == 0
    lead_tiles = n_prompt // tm
    row = lambda w: pl.BlockSpec((tm, w), lambda i: (i, 0))

    def pair_specs(w, off_p, off_s):
        return [pl.BlockSpec((tm, w), lambda i: (off_p + jnp.minimum(i, lead_tiles - 1), 0)),
                pl.BlockSpec((tm, w), lambda i: (off_s + jnp.maximum(i - lead_tiles, 0), 0))]

    in_specs = (pair_specs(D_MODEL, 0, 0) if len(h_parts) == 2 else [row(D_MODEL)]) + [
        _const_spec((1, D_MODEL)), _const_spec(wi.shape), _const_spec(wo.shape)]
    args = h_parts + [g, wi, wo]
    p_parts = []
    if ple_args is not None:
        p_parts, gp, wg, wp = ple_args
        p_parts = list(p_parts)
        in_specs += pair_specs(D_PLE, p_layer * lead_tiles, p_layer * (n_sample // tm)) + [
            _const_spec((1, D_MODEL)), _const_spec(wg.shape), _const_spec(wp.shape)]
        args += p_parts + [gp, wg, wp]
    return pl.pallas_call(
        functools.partial(_ffn_kernel, ple=ple_args is not None, h_parts=len(h_parts), p_parts=len(p_parts),
                          lead_tiles=lead_tiles),
        out_shape=jax.ShapeDtypeStruct((n, D_MODEL), F32),
        grid=(n // tm,),
        in_specs=in_specs,
        out_specs=row(D_MODEL),
        compiler_params=_cparams("parallel"),
    )(*args)


_C_U = 0
_C_Q = _C_U + D_SSM
_C_KV = _C_Q + Q_PAD
_C_GN = _C_KV + 6 * D_KV
_C_GA = _C_GN + LANE
_C_GB = _C_GA + D_MODEL
_C_END = _C_GB + D_MODEL


def _mix_in_kernel(h_ref, g_ref, w_ref, gq_ref, gks_ref, gkw_ref,
                   u_ref, qp_ref, kvc_ref, kvs_ref, kvw_ref, kvsb_ref, kvwb_ref, gn_ref, ga_ref, gb_ref):
    xn = _rms(h_ref[...], g_ref[...]).astype(BF16)
    u_ref[...] = _dot(xn, w_ref[:, _C_U:_C_Q])
    zq = _dot(xn, w_ref[:, _C_Q:_C_KV])
    for h in range(N_HEADS):
        qh = zq[:, h * LANE:(h + 1) * LANE]
        ms = jnp.sum(qh * qh, axis=-1, keepdims=True) * (1.0 / HEAD_DIM)
        qn = qh * lax.rsqrt(ms + RMS_EPS) * gq_ref[:, h * LANE:(h + 1) * LANE]
        qp_ref[:, h * LANE:(h + 1) * LANE] = qn
    zkv = _dot(xn, w_ref[:, _C_KV:_C_GN])
    kvc_ref[...] = zkv[:, 0:2 * D_KV]
    ks = _halfnorm(zkv[:, 2 * D_KV:3 * D_KV], gks_ref[...])
    vs = zkv[:, 3 * D_KV:4 * D_KV]
    kw = _halfnorm(zkv[:, 4 * D_KV:5 * D_KV], gkw_ref[...])
    vw = zkv[:, 5 * D_KV:6 * D_KV]
    kvs_ref[:, 0:D_KV] = ks
    kvs_ref[:, D_KV:] = vs
    kvw_ref[:, 0:D_KV] = kw
    kvw_ref[:, D_KV:] = vw
    kvsb_ref[:, 0:D_KV] = ks.astype(BF16)
    kvsb_ref[:, D_KV:2 * D_KV] = vs.astype(BF16)
    kvsb_ref[:, 2 * D_KV:] = jnp.ones((ks.shape[0], D_KV), BF16)
    kvwb_ref[:, 0:D_KV] = kw.astype(BF16)
    kvwb_ref[:, D_KV:] = vw.astype(BF16)
    gn_ref[...] = jax.nn.sigmoid(_dot(xn, w_ref[:, _C_GN:_C_GA]))
    ga_ref[...] = jax.nn.sigmoid(_dot(xn, w_ref[:, _C_GA:_C_GB]))
    gb_ref[...] = jax.nn.sigmoid(_dot(xn, w_ref[:, _C_GB:_C_END]))


def _mix_in(h, g, w, gq, gks, gkw):
    n = h.shape[0]
    tm = _pick_tile(n, 256)
    row = lambda width: pl.BlockSpec((tm, width), lambda i: (i, 0))
    widths = [(D_SSM, F32), (Q_PAD, F32), (2 * D_KV, F32), (2 * D_KV, F32), (2 * D_KV, F32),
              (3 * D_KV, BF16), (2 * D_KV, BF16), (LANE, F32), (D_MODEL, F32), (D_MODEL, F32)]
    return pl.pallas_call(
        _mix_in_kernel,
        out_shape=[jax.ShapeDtypeStruct((n, wd), dt) for wd, dt in widths],
        grid=(n // tm,),
        in_specs=[row(D_MODEL), _const_spec((1, D_MODEL)), _const_spec(w.shape),
                  _const_spec((1, Q_PAD)), _const_spec((1, LANE)), _const_spec((1, LANE))],
        out_specs=[row(wd) for wd, _ in widths],
        compiler_params=_cparams("parallel"),
    )(h, g, w, gq, gks, gkw)


def _gelu_tanh(x):
    return 0.5 * x * (1.0 + jnp.tanh(0.7978845608028654 * (x + 0.044715 * (x * x * x))))


def _s5_kernel(*refs, seg, carry, seq_steps=None):
    if carry:
        (u_ref, bm_ref, cr_ref, ci_ref, d_ref, lam_ref, pr_ref, pi_ref,
         y_ref, hr_out, hi_out, car_ref, cai_ref) = refs
    else:
        (u_ref, bm_ref, cr_ref, ci_ref, d_ref, lam_ref, pr_ref, pi_ref, h0r_ref, h0i_ref, y_all_ref,
         y_ref, hr_out, hi_out) = refs
    u = u_ref[...]
    rows = u.shape[0]
    x = _dot(u.astype(BF16), bm_ref[...])
    xr = x[:, :N_STATE]
    xi = x[:, N_STATE:]
    d, k = 1, 0
    while d < seg:
        lr = jnp.tile(lam_ref[2 * k * seg:(2 * k + 1) * seg, :], (rows // seg, 1))
        li = jnp.tile(lam_ref[(2 * k + 1) * seg:(2 * k + 2) * seg, :], (rows // seg, 1))
        sr = pltpu.roll(xr, d, 0)
        si = pltpu.roll(xi, d, 0)
        xr, xi = xr + lr * sr - li * si, xi + lr * si + li * sr
        d *= 2
        k += 1
    p_r = pr_ref[...]
    p_i = pi_ref[...]
    if carry:
        @pl.when(pl.program_id(0) == 0)
        def _():
            car_ref[...] = jnp.zeros_like(car_ref)
            cai_ref[...] = jnp.zeros_like(cai_ref)
        c_r = car_ref[...]
        c_i = cai_ref[...]
        hr_parts, hi_parts = [], []
        for j in range(rows // seg):
            tr = xr[j * seg:(j + 1) * seg]
            ti = xi[j * seg:(j + 1) * seg]
            hr_j = tr + p_r * c_r - p_i * c_i
            hi_j = ti + p_r * c_i + p_i * c_r
            c_r = hr_j[seg - 1:seg, :]
            c_i = hi_j[seg - 1:seg, :]
            hr_parts.append(hr_j)
            hi_parts.append(hi_j)
        hr = jnp.concatenate(hr_parts, axis=0)
        hi = jnp.concatenate(hi_parts, axis=0)
    else:
        c_r = h0r_ref[...]
        c_i = h0i_ref[...]
        hr = xr + p_r * c_r - p_i * c_i
        hi = xi + p_r * c_i + p_i * c_r
    y = _dot(hr.astype(BF16), cr_ref[...]) + _dot(hi.astype(BF16), ci_ref[...]) + d_ref[...] * u
    y_ref[...] = _gelu_tanh(y).astype(BF16)
    if carry:
        car_ref[...] = hr[rows - 1:rows, :]
        cai_ref[...] = hi[rows - 1:rows, :]

        @pl.when(pl.program_id(0) < seq_steps)
        def _():
            hr_out[...] = hr[rows - 1:rows, :]
            hi_out[...] = hi[rows - 1:rows, :]
    else:
        hr_out[...] = hr
        hi_out[...] = hi


def _s5_prompt(u, t, sp):
    rows = Q_TILE
    consts = [sp["bmat"], sp["cr"], sp["ci"], sp["d"], sp["lam_p"], sp["pr_p"], sp["pi_p"]]
    return pl.pallas_call(
        functools.partial(_s5_kernel, seg=SCAN_SEG, carry=True, seq_steps=t // rows),
        out_shape=[jax.ShapeDtypeStruct((u.shape[0], D_SSM), BF16),
                   jax.ShapeDtypeStruct((1, N_STATE), F32), jax.ShapeDtypeStruct((1, N_STATE), F32)],
        grid=(u.shape[0] // rows,),
        in_specs=[pl.BlockSpec((rows, D_SSM), lambda i: (i, 0))] + [_const_spec(c.shape) for c in consts],
        out_specs=[pl.BlockSpec((rows, D_SSM), lambda i: (i, 0)),
                   _const_spec((1, N_STATE)), _const_spec((1, N_STATE))],
        scratch_shapes=[pltpu.VMEM((1, N_STATE), F32), pltpu.VMEM((1, N_STATE), F32)],
        compiler_params=_cparams("arbitrary"),
    )(u, *consts)


def _s5_sample(u, yg, t, ns_rows, nq, h0r, h0i, sp):
    consts = [sp["bmat"], sp["cr"], sp["ci"], sp["d"], sp["lam_s"], sp["pr_s"], sp["pi_s"], h0r, h0i]
    blk = t // ns_rows
    return pl.pallas_call(
        functools.partial(_s5_kernel, seg=nq, carry=False),
        out_shape=[jax.ShapeDtypeStruct(yg.shape, BF16),
                   jax.ShapeDtypeStruct((ns_rows, N_STATE), F32), jax.ShapeDtypeStruct((ns_rows, N_STATE), F32)],
        grid=(1,),
        in_specs=([pl.BlockSpec((ns_rows, D_SSM), lambda i: (blk, 0))] + [_const_spec(c.shape) for c in consts]
                  + [pl.BlockSpec(memory_space=pl.ANY)]),
        out_specs=[pl.BlockSpec((ns_rows, D_SSM), lambda i: (blk, 0)),
                   _const_spec((ns_rows, N_STATE)), _const_spec((ns_rows, N_STATE))],
        input_output_aliases={len(consts) + 1: 0},
        compiler_params=_cparams("arbitrary"),
    )(u, *consts, yg)


N_SPAIR = CMP_STRIDE // 2


def _compress_part(get_x2, w_ref, a0, a1, part, rows):
    base = pl.multiple_of(part * rows, rows)
    for kv in range(2):
        acc = None
        for sp in range(N_SPAIR):
            term = _dot(get_x2(kv, sp), w_ref[kv, sp])
            acc = term if acc is None else acc + term
        a0[pl.ds(base, rows), kv * D_KV:(kv + 1) * D_KV] = acc[:, 0:D_KV]
        a1[pl.ds(base, rows), kv * D_KV:(kv + 1) * D_KV] = acc[:, D_KV:]


def _compress_finish(a0, a1, w_ref, pe_ref, gk_ref, kcc_ref, vcc_ref, nb):
    bias = []
    for j in range(CMP_BLOCK // CMP_STRIDE):
        halves = []
        for kv in range(2):
            b = jnp.zeros((8, 2 * D_KV), F32)
            for sp in range(N_SPAIR):
                r = (j * 2 + kv) * N_SPAIR + sp
                b = b + _dot(jnp.broadcast_to(pe_ref[r:r + 1, :], (8, 2 * D_KV)).astype(BF16), w_ref[kv, sp])
            halves.append(b[0:1, j * D_KV:(j + 1) * D_KV])
        bias.append(jnp.concatenate(halves, axis=1))
    out = (a0[...] + bias[0]) + pltpu.roll(a1[...] + bias[1], nb - 1, 0)
    valid = lax.broadcasted_iota(jnp.int32, (nb, 1), 0) < nb - 1
    out = jnp.where(valid, out, 0.0)
    kcc_ref[0] = _halfnorm(out[:, 0:D_KV], gk_ref[...]).astype(BF16)
    vcc_ref[0] = out[:, D_KV:].astype(BF16)


def _compress_rows_kernel(xk_ref, xv_ref, w_ref, pe_ref, gk_ref, kcc_ref, vcc_ref, a0, a1, *, nsplit, nb):
    part = pl.program_id(0)
    rows = nb // nsplit

    def get_x2(kv, sp):
        x_ref = (xk_ref, xv_ref)[kv]
        pair = [x_ref[pl.ds(2 * sp + si, rows, stride=CMP_STRIDE), :] for si in range(2)]
        return jnp.concatenate(pair, axis=1).astype(BF16)
    _compress_part(get_x2, w_ref, a0, a1, part, rows)

    @pl.when(part == nsplit - 1)
    def _():
        _compress_finish(a0, a1, w_ref, pe_ref, gk_ref, kcc_ref, vcc_ref, nb)


def _compress_rows(kv_cmp, t, cw):
    nb = t // CMP_STRIDE
    nsplit = 2
    return pl.pallas_call(
        functools.partial(_compress_rows_kernel, nsplit=nsplit, nb=nb),
        out_shape=[jax.ShapeDtypeStruct((1, nb, D_KV), BF16), jax.ShapeDtypeStruct((1, nb, D_KV), BF16)],
        grid=(nsplit,),
        in_specs=[pl.BlockSpec((t // nsplit, D_KV), lambda i: (i, 0)),
                  pl.BlockSpec((t // nsplit, D_KV), lambda i: (i, 1)),
                  _const_spec(cw["w"].shape), _const_spec(cw["pe"].shape), _const_spec((1, LANE))],
        out_specs=[_const_spec((1, nb, D_KV)), _const_spec((1, nb, D_KV))],
        scratch_shapes=[pltpu.VMEM((nb, 2 * D_KV), F32), pltpu.VMEM((nb, 2 * D_KV), F32)],
        compiler_params=_cparams("arbitrary"),
    )(kv_cmp, kv_cmp, cw["w"], cw["pe"], cw["gk"])


def _compress_kernel(pt_ref, pages_ref, w_ref, pe_ref, gk_ref, perm_ref,
                     kcc_ref, vcc_ref, buf, sem, x_scr, a0, a1, *, pps, nsplit, nb):
    s = pl.program_id(0)
    nsteps = pl.num_programs(0)

    def page_copy(step, p, slot):
        b = step // nsplit
        part = step % nsplit
        pg = pt_ref[b, part * pps + p]
        col = pl.multiple_of((p % 2) * PAGE_SIZE, PAGE_SIZE)
        return pltpu.make_async_copy(pages_ref.at[pg], buf.at[slot, p // 2, :, pl.ds(col, PAGE_SIZE)],
                                     sem.at[slot])

    def start(step, slot):
        def body(p, c):
            page_copy(step, p, slot).start()
            return c
        lax.fori_loop(0, pps, body, 0)

    def wait(step, slot):
        def body(p, c):
            page_copy(step, p, slot).wait()
            return c
        lax.fori_loop(0, pps, body, 0)

    @pl.when(s == 0)
    def _():
        start(s, 0)

    @pl.when(s + 1 < nsteps)
    def _():
        start(s + 1, (s + 1) % 2)

    slot = s % 2
    wait(s, slot)
    part = s % nsplit

    def to_rows(q, c):
        xt = buf[slot, q].astype(BF16)
        xp = _dot_nt(perm_ref[...], xt).astype(BF16)
        base = pl.multiple_of(q * CMP_STRIDE, CMP_STRIDE)
        for s2 in range(CMP_STRIDE):
            for kv in range(2):
                x_scr[kv, s2 // 2, pl.ds(base, CMP_STRIDE), (s2 % 2) * D_KV:(s2 % 2 + 1) * D_KV] = (
                    xp[s2 * CMP_STRIDE:(s2 + 1) * CMP_STRIDE, kv * D_KV:(kv + 1) * D_KV])
        return c
    lax.fori_loop(0, pps // 2, to_rows, 0, unroll=8)
    _compress_part(lambda kv, sp: x_scr[kv, sp], w_ref, a0, a1, part, nb // nsplit)

    @pl.when(part == nsplit - 1)
    def _():
        _compress_finish(a0, a1, w_ref, pe_ref, gk_ref, kcc_ref, vcc_ref, nb)


def _compress_paged(page_table, pages, cw):
    bsz, n_pages = page_table.shape
    nb = n_pages * (PAGE_SIZE // CMP_STRIDE)
    nsplit = 2
    pps = n_pages // nsplit
    const = lambda shape: pl.BlockSpec(shape, lambda s, pt: (0,) * len(shape), pipeline_mode=pl.Buffered(1))
    grid_spec = pltpu.PrefetchScalarGridSpec(
        num_scalar_prefetch=1,
        grid=(bsz * nsplit,),
        in_specs=[pl.BlockSpec(memory_space=pl.ANY), const(cw["w"].shape), const(cw["pe"].shape),
                  const((1, LANE)), const(cw["perm"].shape)],
        out_specs=[pl.BlockSpec((1, nb, D_KV), lambda s, pt: (s // nsplit, 0, 0)),
                   pl.BlockSpec((1, nb, D_KV), lambda s, pt: (s // nsplit, 0, 0))],
        scratch_shapes=[pltpu.VMEM((2, pps // 2, 2 * D_KV, 2 * PAGE_SIZE), F32), pltpu.SemaphoreType.DMA((2,)),
                        pltpu.VMEM((2, N_SPAIR, nb // nsplit, 2 * D_KV), BF16),
                        pltpu.VMEM((nb, 2 * D_KV), F32), pltpu.VMEM((nb, 2 * D_KV), F32)],
    )
    return pl.pallas_call(
        functools.partial(_compress_kernel, pps=pps, nsplit=nsplit, nb=nb),
        out_shape=[jax.ShapeDtypeStruct((bsz, nb, D_KV), BF16), jax.ShapeDtypeStruct((bsz, nb, D_KV), BF16)],
        grid_spec=grid_spec,
        compiler_params=_cparams("arbitrary"),
    )(page_table, pages, cw["w"], cw["pe"], cw["gk"], cw["perm"])


def _heads_rows(qp):
    return jnp.concatenate([qp[:, h * LANE:(h + 1) * LANE] for h in range(N_HEADS)], axis=0).astype(BF16)


def _slope2(h):
    return LOG2E * 2.0 ** (-8.0 * (h + 1) / N_HEADS)


def _branch(q_all, tq, pieces):
    s_all = [_dot(q_all, k) if fm else _dot_nt(q_all, k) for k, _, _, _, fm in pieces]
    es, invs = [], []
    for h in range(N_HEADS):
        rows = slice(h * tq, (h + 1) * tq)
        sm = [jnp.where(pc[3], s[rows] - _slope2(h) * pc[2], NEG) for s, pc in zip(s_all, pieces)]
        m = functools.reduce(jnp.maximum, [jnp.max(x, axis=-1, keepdims=True) for x in sm])
        m = jnp.maximum(m, HALF_NEG)
        e = [jnp.exp2(x - m) for x in sm]
        l = functools.reduce(lambda a, b: a + b, [jnp.sum(x, axis=-1, keepdims=True) for x in e])
        es.append(e)
        invs.append(1.0 / jnp.maximum(l, 1e-30))
    o = None
    for i, (_, v, _, _, fm) in enumerate(pieces):
        p = jnp.concatenate([es[h][i] for h in range(N_HEADS)], axis=0).astype(BF16)
        t = _dot_nt(p, v) if fm else _dot(p, v)
        o = t if o is None else o + t
    return o * jnp.concatenate(invs, axis=0), es, invs


def _cmp_win_topk(q_all, tq, qpos, gn, kcc, vcc, mband, win_pieces, nb, ns, topn):
    n_io = lax.broadcasted_iota(jnp.int32, (1, kcc.shape[0]), 1)
    kc_end = n_io * CMP_STRIDE + (CMP_BLOCK - 1)
    mask_c = (qpos >= kc_end) & (n_io < nb - 1)
    kdist_c = (qpos[0:1, :] - kc_end).astype(F32)
    o_c, es, invs = _branch(q_all, tq, [(kcc, vcc, kdist_c, mask_c, False)])
    o_w, _, _ = _branch(q_all, tq, win_pieces)
    imps = []
    for g in range(N_KV):
        psum = es[HPG * g][0] * invs[HPG * g]
        for hh in range(1, HPG):
            psum = psum + es[HPG * g + hh][0] * invs[HPG * g + hh]
        hi = psum.astype(BF16)
        r = psum - hi.astype(F32)
        mid = r.astype(BF16)
        lo = (r - mid.astype(F32)).astype(BF16)
        imps.append(_dot(hi, mband) + _dot(mid, mband) + _dot(lo, mband))
    imp = jnp.concatenate(imps, axis=0)
    blk = lax.broadcasted_iota(jnp.int32, (1, ns), 1)
    cur = jnp.concatenate([jnp.right_shift(qpos, 6)] * N_KV, axis=0)
    forced = (blk == 0) | (blk == cur) | (blk == cur - 1)
    score = jnp.where(forced, imp + SEL_BIG, jnp.where(blk <= cur, imp, -SEL_BIG))
    n_rows = score.shape[0]
    pad = (-n_rows) % LANE
    if pad:
        score = jnp.concatenate([score, jnp.zeros((pad, ns), F32)], axis=0)
    score = score.T
    blkf = lax.broadcasted_iota(jnp.int32, (ns, 1), 0).astype(F32)
    for _ in range(topn):
        m = jnp.max(score, axis=0, keepdims=True)
        first = jnp.min(jnp.where(score == m, blkf, float(ns)), axis=0, keepdims=True)
        score = jnp.where(blkf == first, -jnp.inf, score)
    sel = jnp.where(score == -jnp.inf, 1.0, 0.0).T[:n_rows]
    sel = jnp.where(blk <= cur, sel, 0.0)
    outs = []
    for h in range(N_HEADS):
        rows = slice(h * tq, (h + 1) * tq)
        outs.append(gn[:, h:h + 1] * o_c[rows] + gn[:, 2 * N_HEADS + h:2 * N_HEADS + h + 1] * o_w[rows])
    return jnp.concatenate(outs, axis=1), sel


def _e1_prompt_kernel(qp_ref, gn_ref, kcc_ref, vcc_ref, mband_ref, kvw_ref, ocw_ref, sel_ref, flag_ref,
                      *, nb, ns, widths):
    tq = Q_TILE
    s0 = pl.program_id(0) * tq
    qpos = s0 + lax.broadcasted_iota(jnp.int32, (tq, 1), 0)

    def run(width):
        q_all = _heads_rows(qp_ref[...])
        start = pl.multiple_of(jnp.maximum(s0 - WINDOW, 0), Q_TILE)
        kw = kvw_ref[pl.ds(start, WIN_KEYS), 0:D_KV]
        vw = kvw_ref[pl.ds(start, WIN_KEYS), D_KV:]
        kpos = start + lax.broadcasted_iota(jnp.int32, (1, WIN_KEYS), 1)
        dist_w = qpos - kpos
        mask_w = (dist_w >= 0) & (dist_w < WINDOW)
        win_pieces = [(kw, vw, (s0 - kpos).astype(F32), mask_w, False)]
        ocw, sel = _cmp_win_topk(q_all, tq, qpos, gn_ref[...], kcc_ref[0, 0:width, :], vcc_ref[0, 0:width, :],
                                 mband_ref[0:width, :], win_pieces, nb, ns, TOP_N)
        ocw_ref[...] = ocw
        sel_ref[0] = sel[:tq]
        sel_ref[1] = sel[tq:]
        colany = jnp.max(sel, axis=0, keepdims=True)
        j_io = lax.broadcasted_iota(jnp.int32, (ns, LANE), 0)
        c_io = lax.broadcasted_iota(jnp.int32, (ns, LANE), 1)
        grp = jnp.where(jnp.right_shift(j_io, 2) == c_io, 1.0, 0.0).astype(BF16)
        cnt = _dot(jnp.broadcast_to(colany, (8, ns)).astype(BF16), grp)
        flag_ref[0] = (cnt > 0.5).astype(jnp.int32)

    nvar = len(widths)
    variant = jnp.minimum(pl.program_id(0) // (pl.num_programs(0) // nvar), nvar - 1)
    for vi, width in enumerate(widths):
        pl.when(variant == vi)(functools.partial(run, width))


def _e1_prompt(qp, gn, kcc, vcc, mband, kvw_bf, t):
    nb, ns = mband.shape
    nt = t // Q_TILE
    nvar = max(v for v in (1, 2, 4) if nb % (v * LANE) == 0 and nt % v == 0)
    widths = tuple(nb * (v + 1) // nvar for v in range(nvar))
    row = lambda w: pl.BlockSpec((Q_TILE, w), lambda i: (i, 0))
    return pl.pallas_call(
        functools.partial(_e1_prompt_kernel, nb=nb, ns=ns, widths=widths),
        out_shape=[jax.ShapeDtypeStruct((t, Q_PAD), F32), jax.ShapeDtypeStruct((N_KV, t, ns), F32),
                   jax.ShapeDtypeStruct((nt, 8, LANE), jnp.int32)],
        grid=(nt,),
        in_specs=[row(Q_PAD), row(LANE), _const_spec((1, nb, D_KV)), _const_spec((1, nb, D_KV)),
                  _const_spec(mband.shape), _const_spec(kvw_bf.shape)],
        out_specs=[row(Q_PAD), pl.BlockSpec((N_KV, Q_TILE, ns), lambda i: (0, i, 0)),
                   pl.BlockSpec((1, 8, LANE), lambda i: (i, 0, 0))],
        compiler_params=_cparams("parallel"),
    )(qp, gn, kcc, vcc, mband, kvw_bf)


def _pad_rows(x, rows):
    return jnp.concatenate([x, jnp.zeros((rows - x.shape[0], x.shape[1]), x.dtype)], axis=0)


def _e1_sample_kernel(qp_ref, gn_ref, kcc_ref, vcc_ref, mband_ref, cwin_ref, nwin_ref, ocw_ref, sel_ref,
                      *, nb, ns, nq, past):
    qi = lax.broadcasted_iota(jnp.int32, (nq, 1), 0)
    qpos = past + qi
    q_all = _heads_rows(qp_ref[...])
    cw = cwin_ref[0]
    w_buf = cw.shape[1]
    j_c = lax.broadcasted_iota(jnp.int32, (1, w_buf), 1)
    dist_cw = qi + (w_buf - j_c)
    mask_cw = (dist_cw >= 0) & (dist_cw < WINDOW)
    nw = _pad_rows(nwin_ref[...], LANE)
    j_n = lax.broadcasted_iota(jnp.int32, (1, LANE), 1)
    mask_nw = (qi >= j_n) & (j_n < nq)
    pieces = [(cw[0:D_KV, :].astype(BF16), cw[D_KV:, :].astype(BF16), (w_buf - j_c).astype(F32), mask_cw, True),
              (nw[:, 0:D_KV].astype(BF16), nw[:, D_KV:].astype(BF16), (-j_n).astype(F32), mask_nw, False)]
    ocw, sel = _cmp_win_topk(q_all, nq, qpos, gn_ref[...], kcc_ref[0], vcc_ref[0], mband_ref[...],
                             pieces, nb, ns, TOP_N - 1)
    ocw_ref[...] = ocw
    sel_ref[0, 0] = sel[:nq]
    sel_ref[0, 1] = sel[nq:]


def _e1_sample(qp, gn, kcc, vcc, mband, cache_win, win_off, kv_win, t, bsz, nq, past):
    nb, ns = mband.shape
    w_buf = cache_win.shape[2]
    off = t // nq
    row = lambda w: pl.BlockSpec((nq, w), lambda b: (off + b, 0))
    per_b = lambda shape: pl.BlockSpec((1,) + shape, lambda b: (b,) + (0,) * len(shape))
    return pl.pallas_call(
        functools.partial(_e1_sample_kernel, nb=nb, ns=ns, nq=nq, past=past),
        out_shape=[jax.ShapeDtypeStruct((bsz * nq, Q_PAD), F32), jax.ShapeDtypeStruct((bsz, N_KV, nq, ns), F32)],
        grid=(bsz,),
        in_specs=[row(Q_PAD), row(LANE), per_b((nb, D_KV)), per_b((nb, D_KV)), _const_spec(mband.shape),
                  pl.BlockSpec((1, 2 * D_KV, w_buf), lambda b: (win_off + b, 0, 0)), row(2 * D_KV)],
        out_specs=[pl.BlockSpec((nq, Q_PAD), lambda b: (b, 0)), per_b((N_KV, nq, ns))],
        compiler_params=_cparams("parallel"),
    )(qp, gn, kcc, vcc, mband, cache_win, kv_win)


def _online_step(q_all, k, kdist, masks, v_ones, m_ref, acc_ref, tq, feature_major=False, split_groups=False):
    reps = kdist.shape[1] // LANE
    per_chain = 1 if split_groups else N_HEADS
    chains = [(a, a + per_chain) for a in range(0, N_HEADS, per_chain)]
    for h_lo, h_hi in chains:
        grows = slice(h_lo * tq, h_hi * tq)
        s = _dot(q_all[grows], k) if feature_major else _dot_nt(q_all[grows], k)
        ps, alphas = [], []
        for h in range(h_lo, h_hi):
            hh = h - h_lo
            rows = slice(h * tq, (h + 1) * tq)
            sm = jnp.where(masks[h // HPG], s[hh * tq:(hh + 1) * tq] - _slope2(h) * kdist, NEG)
            m_old = m_ref[rows, :]
            m_new = jnp.maximum(m_old, jnp.max(sm, axis=-1, keepdims=True))
            alphas.append(jnp.exp2(m_old - m_new))
            ps.append(jnp.exp2(sm - jnp.tile(m_new, (1, reps))))
            m_ref[rows, :] = m_new
        p_g = jnp.concatenate(ps, axis=0).astype(BF16)
        a_g = jnp.concatenate(alphas, axis=0)
        pv = _dot_nt(p_g, v_ones) if feature_major else _dot(p_g, v_ones)
        acc_ref[grows, :] = jnp.tile(a_g, (1, 2)) * acc_ref[grows, :] + pv


def _block_to_keys(ns, first_blk, n_keys):
    j_io = lax.broadcasted_iota(jnp.int32, (ns, n_keys), 0)
    kb = first_blk + jnp.right_shift(lax.broadcasted_iota(jnp.int32, (ns, n_keys), 1), 6)
    return jnp.where(j_io == kb, 1.0, 0.0).astype(BF16)


def _init_online(m_ref, acc_ref):
    m_ref[...] = jnp.full(m_ref.shape, HALF_NEG, F32)
    acc_ref[...] = jnp.zeros(acc_ref.shape, F32)


def _finish_online(ocw, gn, acc_ref, tq):
    o_s = acc_ref[:, 0:LANE] * (1.0 / jnp.maximum(acc_ref[:, LANE:], 1e-30))
    outs = []
    for h in range(N_HEADS):
        g1 = gn[:, N_HEADS + h:N_HEADS + h + 1]
        outs.append(ocw[:, h * LANE:(h + 1) * LANE] + g1 * o_s[h * tq:(h + 1) * tq])
    return jnp.concatenate(outs, axis=1)


def _e2_prompt_kernel(order_ref, cnt_ref, qp_ref, sel_ref, gn_ref, ocw_ref, kvs_ref, out_ref, m_ref, acc_ref,
                      *, nchunks, ns, nt):
    tq = Q_TILE
    i = pl.program_id(0)
    s0 = i * tq
    qpos = s0 + lax.broadcasted_iota(jnp.int32, (tq, 1), 0)
    q_all = _heads_rows(qp_ref[...])
    _init_online(m_ref, acc_ref)
    n_act = jnp.where(i < nt, cnt_ref[jnp.minimum(i, nt - 1)], 0)
    blocks_per_chunk = SEL_CHUNK // SEL_BLOCK
    sel_bf = [sel_ref[g].astype(BF16) for g in range(N_KV)]

    def body(j, carry):
        c1 = order_ref[i * nchunks + 2 * j]
        paired = 2 * j + 1 < n_act
        c2 = jnp.where(paired, order_ref[i * nchunks + jnp.minimum(2 * j + 1, nchunks - 1)], c1)
        ks, vs, dists, masks = [], [], [], [[] for _ in range(N_KV)]
        for c, first_blk in ((c1, c1 * blocks_per_chunk), (c2, jnp.where(paired, c2 * blocks_per_chunk, ns))):
            base = pl.multiple_of(c * SEL_CHUNK, SEL_CHUNK)
            ks.append(kvs_ref[pl.ds(base, SEL_CHUNK), 0:D_KV])
            vs.append(kvs_ref[pl.ds(base, SEL_CHUNK), D_KV:])
            kpos = base + lax.broadcasted_iota(jnp.int32, (1, SEL_CHUNK), 1)
            dists.append((s0 - kpos).astype(F32))
            expand = _block_to_keys(ns, first_blk, SEL_CHUNK)
            causal = qpos >= kpos
            for g in range(N_KV):
                masks[g].append((_dot(sel_bf[g], expand) > 0.5) & causal)
        _online_step(q_all, jnp.concatenate(ks, axis=0), jnp.concatenate(dists, axis=1),
                     [jnp.concatenate(mg, axis=1) for mg in masks], jnp.concatenate(vs, axis=0), m_ref, acc_ref, tq,
                     split_groups=True)
        return carry

    lax.fori_loop(0, (n_act + 1) // 2, body, 0)
    out_ref[...] = _finish_online(ocw_ref[...], gn_ref[...], acc_ref, tq)


def _e2_prompt(order, cnt, qp, sel, gn, ocw, kvs_bf, t):
    ns = sel.shape[2]
    nt = t // Q_TILE
    nchunks = order.shape[0] // nt
    row = lambda w: pl.BlockSpec((Q_TILE, w), lambda i, o, c: (i, 0))
    prow = lambda w: pl.BlockSpec((Q_TILE, w), lambda i, o, c: (jnp.minimum(i, nt - 1), 0))
    grid_spec = pltpu.PrefetchScalarGridSpec(
        num_scalar_prefetch=2,
        grid=(qp.shape[0] // Q_TILE,),
        in_specs=[row(Q_PAD), pl.BlockSpec((N_KV, Q_TILE, ns), lambda i, o, c: (0, jnp.minimum(i, nt - 1), 0)),
                  row(LANE), prow(Q_PAD), pl.BlockSpec(kvs_bf.shape, lambda i, o, c: (0, 0))],
        out_specs=row(Q_PAD),
        scratch_shapes=[pltpu.VMEM((N_HEADS * Q_TILE, LANE), F32), pltpu.VMEM((N_HEADS * Q_TILE, 2 * LANE), F32)],
    )
    return pl.pallas_call(
        functools.partial(_e2_prompt_kernel, nchunks=nchunks, ns=ns, nt=nt),
        out_shape=jax.ShapeDtypeStruct((qp.shape[0], Q_PAD), F32),
        grid_spec=grid_spec,
        compiler_params=_cparams("arbitrary"),
    )(order, cnt, qp, sel, gn, ocw, kvs_bf)


def _e2_sample_kernel(pt_ref, live_ref, pages_ref, qp_ref, sel_ref, gn_ref, ocw_ref, nsel_ref, expand_ref,
                      oatt_all_ref, out_ref, buf, sem, m_ref, acc_ref, *, pps, nsplit, nq):
    s = pl.program_id(0)
    nsteps = pl.num_programs(0)
    n_keys = pps * PAGE_SIZE

    def page_copy(step, p, slot):
        b = step // nsplit
        part = step % nsplit
        pg = pt_ref[b, part * pps + p]
        col = pl.multiple_of(p * PAGE_SIZE, PAGE_SIZE)
        return pltpu.make_async_copy(pages_ref.at[pg], buf.at[slot, :, pl.ds(col, PAGE_SIZE)], sem.at[slot])

    def start(step, slot):
        def body(p, c):
            page_copy(step, p, slot).start()
            return c
        lax.fori_loop(0, pps, body, 0)

    def wait(step, slot):
        def body(p, c):
            page_copy(step, p, slot).wait()
            return c
        lax.fori_loop(0, pps, body, 0)

    @pl.when((s == 0) & (live_ref[0] > 0))
    def _():
        start(s, 0)

    @pl.when((s + 1 < nsteps) & (live_ref[jnp.minimum(s + 1, nsteps - 1)] > 0))
    def _():
        start(s + 1, (s + 1) % 2)

    slot = s % 2
    part = s % nsplit
    qi = lax.broadcasted_iota(jnp.int32, (nq, 1), 0)
    q_all = _heads_rows(qp_ref[...])

    @pl.when(part == 0)
    def _():
        _init_online(m_ref, acc_ref)

    @pl.when(live_ref[s] > 0)
    def _():
        wait(s, slot)
        k = buf[slot, 0:D_KV, :].astype(BF16)
        v_ones = jnp.concatenate([buf[slot, D_KV:, :].astype(BF16), jnp.ones((D_KV, n_keys), BF16)], axis=0)
        back = (nsplit - part) * n_keys - lax.broadcasted_iota(jnp.int32, (1, n_keys), 1)
        masks = [_dot(sel_ref[0, g, 0].astype(BF16), expand_ref[...]) > 0.5 for g in range(N_KV)]
        _online_step(q_all, k, back.astype(F32), masks, v_ones, m_ref, acc_ref, nq, feature_major=True)

    @pl.when(part == nsplit - 1)
    def _():
        nw = _pad_rows(nsel_ref[...], LANE)
        j_n = lax.broadcasted_iota(jnp.int32, (1, LANE), 1)
        mask_n = (qi >= j_n) & (j_n < nq)
        vn_ones = jnp.concatenate([nw[:, D_KV:].astype(BF16), jnp.ones((LANE, D_KV), BF16)], axis=1)
        _online_step(q_all, nw[:, 0:D_KV].astype(BF16), (-j_n).astype(F32), [mask_n, mask_n], vn_ones,
                     m_ref, acc_ref, nq)
        out_ref[...] = _finish_online(ocw_ref[...], gn_ref[...], acc_ref, nq)


def _e2_sample(page_table, pages, qp, sel, gn, ocw, kv_sel, oatt, t, nq):
    bsz, n_pages = page_table.shape
    ns = sel.shape[3]
    nsplit = 8
    pps = n_pages // nsplit
    n_keys = pps * PAGE_SIZE
    nblk = ns // nsplit
    off = t // nq
    sel_parts = sel.reshape(bsz, N_KV, nq, nsplit, nblk).transpose(0, 1, 3, 2, 4)
    live = (jnp.max(sel_parts, axis=(1, 3, 4)) > 0).astype(jnp.int32).reshape(-1)
    expand = (jnp.arange(nblk)[:, None] == jnp.arange(n_keys)[None, :] // SEL_BLOCK).astype(BF16)
    row = lambda w: pl.BlockSpec((nq, w), lambda s, pt, lv: (off + s // nsplit, 0))
    grid_spec = pltpu.PrefetchScalarGridSpec(
        num_scalar_prefetch=2,
        grid=(bsz * nsplit,),
        in_specs=[pl.BlockSpec(memory_space=pl.ANY), row(Q_PAD),
                  pl.BlockSpec((1, N_KV, 1, nq, nblk), lambda s, pt, lv: (s // nsplit, 0, s % nsplit, 0, 0)),
                  row(LANE), pl.BlockSpec((nq, Q_PAD), lambda s, pt, lv: (s // nsplit, 0)), row(2 * D_KV),
                  pl.BlockSpec((nblk, n_keys), lambda s, pt, lv: (0, 0), pipeline_mode=pl.Buffered(1)),
                  pl.BlockSpec(memory_space=pl.ANY)],
        out_specs=row(Q_PAD),
        scratch_shapes=[pltpu.VMEM((2, 2 * D_KV, n_keys), F32), pltpu.SemaphoreType.DMA((2,)),
                        pltpu.VMEM((N_HEADS * nq, LANE), F32), pltpu.VMEM((N_HEADS * nq, 2 * LANE), F32)],
    )
    return pl.pallas_call(
        functools.partial(_e2_sample_kernel, pps=pps, nsplit=nsplit, nq=nq),
        out_shape=jax.ShapeDtypeStruct(oatt.shape, F32),
        grid_spec=grid_spec,
        input_output_aliases={9: 0},
        compiler_params=_cparams("arbitrary"),
    )(page_table, live, pages, qp, sel_parts, gn, ocw, kv_sel, expand, oatt)


def _post_kernel(h_ref, yg_ref, oatt_ref, ga_ref, gb_ref, wglu_ref, watt_ref, wout_ref, o_ref):
    gl = _dot(yg_ref[...], wglu_ref[...])
    br_a = gl[:, :D_MODEL] * jax.nn.sigmoid(gl[:, D_MODEL:])
    br_b = _dot(oatt_ref[...].astype(BF16), watt_ref[...])
    merged = (ga_ref[...] * br_a + gb_ref[...] * br_b).astype(BF16)
    o_ref[...] = h_ref[...] + _dot(merged, wout_ref[...])


def _post(h, yg, oatt, ga, gb, wglu, watt, wout):
    n = h.shape[0]
    tm = _pick_tile(n, 256)
    row = lambda w: pl.BlockSpec((tm, w), lambda i: (i, 0))
    return pl.pallas_call(
        _post_kernel,
        out_shape=jax.ShapeDtypeStruct((n, D_MODEL), F32),
        grid=(n // tm,),
        in_specs=[row(D_MODEL), row(D_SSM), row(Q_PAD), row(D_MODEL), row(D_MODEL),
                  _const_spec(wglu.shape), _const_spec(watt.shape), _const_spec(wout.shape)],
        out_specs=row(D_MODEL),
        compiler_params=_cparams("parallel"),
    )(h, yg, oatt, ga, gb, wglu, watt, wout)


def _head_pad_index():
    h = jnp.arange(N_HEADS)[:, None]
    d = jnp.arange(HEAD_DIM)[None, :]
    return (LANE * h + HEAD_DIM * (h // HPG) + d).reshape(-1)


def _prep_mix_weights(w_in, qk_norm):
    idx = _head_pad_index()
    wq = jnp.zeros((D_MODEL, Q_PAD), F32).at[:, idx].set(w_in[:, D_SSM:D_SSM + N_HEADS * HEAD_DIM])
    c0 = D_SSM + N_HEADS * HEAD_DIM
    c1 = c0 + 6 * D_KV
    c2 = c1 + 3 * N_HEADS
    wgn = jnp.zeros((D_MODEL, LANE), F32).at[:, :3 * N_HEADS].set(w_in[:, c1:c2])
    w = jnp.concatenate([w_in[:, :D_SSM], wq, w_in[:, c0:c1], wgn, w_in[:, c2:]], axis=1).astype(BF16)
    gq = jnp.zeros((Q_PAD,), F32).at[idx].set(jnp.tile(qk_norm[0] * (HEAD_DIM ** -0.5 * LOG2E), N_HEADS))[None]
    gks = jnp.tile(qk_norm[2], N_KV)[None]
    gkw = jnp.tile(qk_norm[3], N_KV)[None]
    return w, gq, gks, gkw


def _prep_att_out(w_att_out):
    return jnp.zeros((Q_PAD, D_MODEL), F32).at[_head_pad_index()].set(w_att_out).astype(BF16)


def _prep_compress(pe_k, pe_v, w_k, w_v, gain_k):
    nj = CMP_BLOCK // CMP_STRIDE
    eye = jnp.eye(N_KV, dtype=F32)

    def pair_maps(w):
        w5 = w.reshape(nj, N_SPAIR, 2, HEAD_DIM, HEAD_DIM)
        return jnp.einsum("jpsde,gh->psgdjhe", w5, eye).reshape(N_SPAIR, 2 * D_KV, 2 * D_KV)

    def pair_pe(pe):
        p5 = pe.reshape(nj, N_SPAIR, 2, 1, HEAD_DIM)
        return jnp.broadcast_to(p5, (nj, N_SPAIR, 2, N_KV, HEAD_DIM)).reshape(nj, N_SPAIR, 2 * D_KV)

    w = jnp.stack([pair_maps(w_k), pair_maps(w_v)], axis=0)
    pe = jnp.stack([pair_pe(pe_k), pair_pe(pe_v)], axis=1).reshape(-1, 2 * D_KV)
    r = jnp.arange(2 * PAGE_SIZE)
    perm = (r[None, :] == (CMP_STRIDE * (r % CMP_STRIDE) + r // CMP_STRIDE)[:, None]).astype(BF16)
    return {"w": w.astype(BF16), "pe": pe,
            "gk": jnp.tile(gain_k, N_KV)[None], "perm": perm}


def _prep_s5(a_re, a_im, log_dt, b_re, b_im, c_re, c_im, d, nq):
    dt = jnp.exp(log_dt)[:, None]
    mag = jnp.exp(a_re * dt)
    lr = mag * jnp.cos(a_im * dt)
    li = mag * jnp.sin(a_im * dt)
    den = a_re * a_re + a_im * a_im
    fr = ((lr - 1.0) * a_re + li * a_im) / den
    fi = (li * a_re - (lr - 1.0) * a_im) / den
    bbr = fr[..., None] * b_re - fi[..., None] * b_im
    bbi = fr[..., None] * b_im + fi[..., None] * b_re
    eye = jnp.eye(N_SSM_GROUPS, dtype=F32)
    blk_b = lambda m: jnp.einsum("gpc,gh->gchp", m, eye).reshape(D_SSM, N_STATE)
    blk_c = lambda m: jnp.einsum("gcp,gh->gphc", m, eye).reshape(N_STATE, D_SSM)

    def lam_pow(k):
        kk = k.astype(F32)[:, None, None]
        m = jnp.exp(a_re * dt * kk)
        th = a_im * dt * kk
        return (m * jnp.cos(th)).reshape(-1, N_STATE), (m * jnp.sin(th)).reshape(-1, N_STATE)

    def step_table(seg):
        ks = []
        dd = 1
        while dd < seg:
            ks.append(dd)
            dd *= 2
        re, im = lam_pow(jnp.array(ks))
        live = jnp.arange(seg)[None, :, None] >= jnp.array(ks)[:, None, None]
        tab = jnp.stack([jnp.where(live, re[:, None, :], 0.0), jnp.where(live, im[:, None, :], 0.0)], axis=1)
        return tab.reshape(-1, N_STATE)

    sp = {"bmat": jnp.concatenate([blk_b(bbr), blk_b(bbi)], axis=1).astype(BF16),
          "cr": blk_c(c_re).astype(BF16), "ci": (-blk_c(c_im)).astype(BF16), "d": d[None],
          "lam_p": step_table(SCAN_SEG), "lam_s": step_table(nq)}
    sp["pr_p"], sp["pi_p"] = lam_pow(jnp.arange(SCAN_SEG) + 1)
    return sp, lam_pow


def _band_matrix(nb, ns):
    ratio = SEL_BLOCK // CMP_STRIDE
    lo = CMP_BLOCK // CMP_STRIDE - 1
    c = jnp.arange(nb)[:, None]
    j = jnp.arange(ns)[None, :]
    return ((c >= ratio * j - lo) & (c <= ratio * j + ratio - 1)).astype(BF16)


def kernel(x_prompt, x_sample, cache_kv_cmp, cache_kv_sel, cache_kv_win, state_ssm_re, state_ssm_im, page_table, p_prompt, p_sample, norm_ffn1, w_ffn1_in, w_ffn1_out, norm_mix, w_in, qk_norm, ssm_a_re, ssm_a_im, ssm_log_dt, ssm_b_re, ssm_b_im, ssm_c_re, ssm_c_im, ssm_d, w_glu, cmp_pe_k, cmp_pe_v, cmp_w_k, cmp_w_v, w_att_out, w_out, norm_ffn2, w_ffn2_in, w_ffn2_out, norm_ple, w_ple_gate, w_ple_proj):
    bp, t = x_prompt.shape[:2]
    bsz, nq = x_sample.shape[:2]
    n_pages = page_table.shape[1]
    past = n_pages * PAGE_SIZE
    n_pool = cache_kv_cmp.shape[1]
    w_buf = cache_kv_win.shape[2]
    ns_rows = bsz * nq
    assert bp == 1 and t % SEL_CHUNK == 0 and t >= WIN_KEYS and t % ns_rows == 0 and nq < CMP_STRIDE
    assert past == t and w_buf == WINDOW and ns_rows % Q_TILE == 0
    nb = t // CMP_STRIDE
    ns = t // SEL_BLOCK
    nt = t // Q_TILE
    nchunks = t // SEL_CHUNK
    mband = _band_matrix(nb, ns)
    page_table = page_table.astype(jnp.int32)
    feat_major = lambda c: jnp.transpose(c, (0, 1, 3, 4, 5, 2)).reshape(DEPTH * c.shape[1], 2 * D_KV, c.shape[2])
    pages_cmp = feat_major(cache_kv_cmp)
    pages_sel = feat_major(cache_kv_sel)
    pages_win = feat_major(cache_kv_win)

    h = (x_prompt[0], x_sample.reshape(ns_rows, D_MODEL))
    p_all = (p_prompt.reshape(DEPTH * t, D_PLE), p_sample.reshape(DEPTH * ns_rows, D_PLE))
    st_p = [[] for _ in range(5)]
    st_s = [[] for _ in range(5)]
    for i in range(DEPTH):
        row1 = lambda a: a[i][None]
        h = _ffn(h, row1(norm_ffn1), w_ffn1_in[i].astype(BF16), w_ffn1_out[i].astype(BF16), t, ns_rows)
        w_mix, gq, gks, gkw = _prep_mix_weights(w_in[i], qk_norm[i])
        u, qp, kv_cmp, kv_sel, kv_win, kvs_bf, kvw_bf, gn, ga, gb = _mix_in(h, row1(norm_mix), w_mix, gq, gks, gkw)

        sp, lam_pow = _prep_s5(ssm_a_re[i], ssm_a_im[i], ssm_log_dt[i], ssm_b_re[i], ssm_b_im[i],
                               ssm_c_re[i], ssm_c_im[i], ssm_d[i], nq)
        sp["pr_s"], sp["pi_s"] = lam_pow(jnp.arange(ns_rows) % nq + 1)
        yg, hr_p, hi_p = _s5_prompt(u, t, sp)
        h0r = jnp.repeat(state_ssm_re[i].reshape(bsz, N_STATE), nq, axis=0)
        h0i = jnp.repeat(state_ssm_im[i].reshape(bsz, N_STATE), nq, axis=0)
        yg, hr_s, hi_s = _s5_sample(u, yg, t, ns_rows, nq, h0r, h0i, sp)

        cw = _prep_compress(cmp_pe_k[i], cmp_pe_v[i], cmp_w_k[i], cmp_w_v[i], qk_norm[i, 1])
        layer_pages = page_table + i * n_pool
        kcc_p, vcc_p = _compress_rows(kv_cmp, t, cw)
        kcc_s, vcc_s = _compress_paged(layer_pages, pages_cmp, cw)
        ocw_p, sel_p, flags = _e1_prompt(qp, gn, kcc_p, vcc_p, mband, kvw_bf, t)
        ocw_s, sel_s = _e1_sample(qp, gn, kcc_s, vcc_s, mband, pages_win, i * bsz, kv_win, t, bsz, nq, past)
        idle = flags[:, 0, :nchunks] == 0
        order = jnp.argsort(idle, axis=1, stable=True).astype(jnp.int32).reshape(-1)
        cnt = (nchunks - jnp.sum(idle, axis=1)).astype(jnp.int32)
        oatt = _e2_prompt(order, cnt, qp, sel_p, gn, ocw_p, kvs_bf, t)
        oatt = _e2_sample(layer_pages, pages_sel, qp, sel_s, gn, ocw_s, kv_sel, oatt, t, nq)

        h = _post(h, yg, oatt, ga, gb,
                  w_glu[i].astype(BF16), _prep_att_out(w_att_out[i]), w_out[i].astype(BF16))
        h = _ffn(h, row1(norm_ffn2), w_ffn2_in[i].astype(BF16), w_ffn2_out[i].astype(BF16), t, ns_rows,
                 (p_all, row1(norm_ple), w_ple_gate[i].astype(BF16), w_ple_proj[i].astype(BF16)), p_layer=i)

        kv5 = lambda a, lead: a.reshape(lead + (2, N_KV, HEAD_DIM))
        st_p[0].append(kv5(kv_cmp[:t], (1, t)))
        st_p[1].append(kv5(kv_sel[:t], (1, t)))
        st_p[2].append(kv5(kv_win[t - min(WINDOW, t):t], (1, min(WINDOW, t))))
        st_p[3].append(hr_p.reshape(1, N_SSM_GROUPS, P_STATE))
        st_p[4].append(hi_p.reshape(1, N_SSM_GROUPS, P_STATE))
        st_s[0].append(kv5(kv_cmp[t:], (bsz, nq)))
        st_s[1].append(kv5(kv_sel[t:], (bsz, nq)))
        st_s[2].append(jnp.concatenate([cache_kv_win[i, :, nq:], kv5(kv_win[t:], (bsz, nq))], axis=1))
        st_s[3].append(hr_s[nq - 1::nq].reshape(bsz, N_SSM_GROUPS, P_STATE))
        st_s[4].append(hi_s[nq - 1::nq].reshape(bsz, N_SSM_GROUPS, P_STATE))

    outs_p = [jnp.stack(a) for a in st_p]
    outs_s = [jnp.stack(a) for a in st_s]
    y_prompt = h[:t][None]
    y_sample = h[t:].reshape(bsz, nq, D_MODEL)
    return (y_prompt, y_sample, *outs_p, *outs_s)
```

```python
import functools

import jax
import jax.numpy as jnp
from jax import lax
from jax.experimental import pallas as pl
from jax.experimental.pallas import tpu as pltpu

F32 = jnp.float32
BF16 = jnp.bfloat16

D_MODEL = 1024
DEPTH = 2
D_SSM = 512
SSM_GROUP = 16
N_SSM_GROUPS = 32
P_STATE = 64
N_STATE = N_SSM_GROUPS * P_STATE
N_HEADS = 8
HEAD_DIM = 64
N_KV = 2
HPG = 4
D_KV = 128
CMP_BLOCK = 32
CMP_STRIDE = 16
SEL_BLOCK = 64
TOP_N = 16
WINDOW = 512
PAGE_SIZE = 128
SEL_BIG = 1e4
D_FF = 2816
D_PLE = 256
RMS_EPS = 1e-6
NEG = -1e30
HALF_NEG = -0.5e30
LOG2E = 1.4426950408889634

LANE = 128
Q_TILE = 128
SCAN_SEG = 8
SEL_CHUNK = 256
WIN_KEYS = WINDOW + Q_TILE
Q_PAD = N_HEADS * LANE
VMEM_LIMIT = 56 * 2 ** 20


def _cparams(*sem):
    return pltpu.CompilerParams(dimension_semantics=sem, vmem_limit_bytes=VMEM_LIMIT)


def _dot(a, b):
    return jnp.dot(a, b, preferred_element_type=F32)


def _dot_nt(a, b):
    return lax.dot_general(a, b, (((1,), (1,)), ((), ())), preferred_element_type=F32)


def _pick_tile(n, target):
    for t in range(min(n, target), 15, -1):
        if n % t == 0 and t % 16 == 0:
            return t
    raise ValueError(f"no row tile for {n}")


def _const_spec(shape):
    nd = len(shape)
    return pl.BlockSpec(shape, lambda *_: (0,) * nd, pipeline_mode=pl.Buffered(1))


def _rms(x, g):
    ms = jnp.mean(x * x, axis=-1, keepdims=True)
    return x * lax.rsqrt(ms + RMS_EPS) * g


def _halfnorm(x, gain):
    lo = lax.broadcasted_iota(jnp.int32, (1, LANE), 1) < HEAD_DIM
    x2 = x * x
    s_lo = jnp.sum(jnp.where(lo, x2, 0.0), axis=-1, keepdims=True)
    s_hi = jnp.sum(jnp.where(lo, 0.0, x2), axis=-1, keepdims=True)
    ms = jnp.where(lo, s_lo, s_hi) * (1.0 / HEAD_DIM)
    return x * lax.rsqrt(ms + RMS_EPS) * gain


def _row_tile(refs, lead_tiles):
    if len(refs) == 1:
        return refs[0][...]
    return jnp.where(pl.program_id(0) < lead_tiles, refs[0][...], refs[1][...])


def _ffn_kernel(*refs, ple, h_parts, p_parts, lead_tiles):
    refs = list(refs)
    h_refs = [refs.pop(0) for _ in range(h_parts)]
    if ple:
        g_ref, wi_ref, wo_ref = refs[:3]
        p_refs = refs[3:3 + p_parts]
        gp_ref, wg_ref, wp_ref, o_ref = refs[3 + p_parts:]
    else:
        g_ref, wi_ref, wo_ref, o_ref = refs
    h = _row_tile(h_refs, lead_tiles)
    xn = _rms(h, g_ref[...]).astype(BF16)
    a = _dot(xn, wi_ref[:, :D_FF])
    b = _dot(xn, wi_ref[:, D_FF:])
    act = (a * jax.nn.sigmoid(a) * b).astype(BF16)
    h = h + 0.5 * _dot(act, wo_ref[...])
    if ple:
        xg = _rms(h, gp_ref[...]).astype(BF16)
        gate = jax.nn.sigmoid(_dot(xg, wg_ref[...]))
        h = h + gate * _dot(_row_tile(p_refs, lead_tiles).astype(BF16), wp_ref[...])
    o_ref[...] = h


def _ffn(h, g, wi, wo, n_prompt, n_sample, ple_args=None, p_layer=0):
    h_parts = list(h) if isinstance(h, (tuple, list)) else [h]
    n = n_prompt + n_sample
    tm = _pick_tile(n, 256)
    assert n_prompt % tm == 0 and n_sample % tm == 0
    lead_tiles = n_prompt // tm
    row = lambda w: pl.BlockSpec((tm, w), lambda i: (i, 0))

    def pair_specs(w, off_p, off_s):
        return [pl.BlockSpec((tm, w), lambda i: (off_p + jnp.minimum(i, lead_tiles - 1), 0)),
                pl.BlockSpec((tm, w), lambda i: (off_s + jnp.maximum(i - lead_tiles, 0), 0))]

    in_specs = (pair_specs(D_MODEL, 0, 0) if len(h_parts) == 2 else [row(D_MODEL)]) + [
        _const_spec((1, D_MODEL)), _const_spec(wi.shape), _const_spec(wo.shape)]
    args = h_parts + [g, wi, wo]
    p_parts = []
    if ple_args is not None:
        p_parts, gp, wg, wp = ple_args
        p_parts = list(p_parts)
        in_specs += pair_specs(D_PLE, p_layer * lead_tiles, p_layer * (n_sample // tm)) + [
            _const_spec((1, D_MODEL)), _const_spec(wg.shape), _const_spec(wp.shape)]
        args += p_parts + [gp, wg, wp]
    return pl.pallas_call(
        functools.partial(_ffn_kernel, ple=ple_args is not None, h_parts=len(h_parts), p_parts=len(p_parts),
                          lead_tiles=lead_tiles),
        out_shape=jax.ShapeDtypeStruct((n, D_MODEL), F32),
        grid=(n // tm,),
        in_specs=in_specs,
        out_specs=row(D_MODEL),
        compiler_params=_cparams("parallel"),
    )(*args)


_C_U = 0
_C_Q = _C_U + D_SSM
_C_KV = _C_Q + Q_PAD
_C_GN = _C_KV + 6 * D_KV
_C_GA = _C_GN + LANE
_C_GB = _C_GA + D_MODEL
_C_END = _C_GB + D_MODEL


def _mix_in_kernel(h_ref, g_ref, w_ref, gq_ref, gks_ref, gkw_ref,
                   u_ref, qp_ref, kvc_ref, kvs_ref, kvw_ref, kvsb_ref, kvwb_ref, gn_ref, ga_ref, gb_ref):
    xn = _rms(h_ref[...], g_ref[...]).astype(BF16)
    u_ref[...] = _dot(xn, w_ref[:, _C_U:_C_Q])
    zq = _dot(xn, w_ref[:, _C_Q:_C_KV])
    for h in range(N_HEADS):
        qh = zq[:, h * LANE:(h + 1) * LANE]
        ms = jnp.sum(qh * qh, axis=-1, keepdims=True) * (1.0 / HEAD_DIM)
        qn = qh * lax.rsqrt(ms + RMS_EPS) * gq_ref[:, h * LANE:(h + 1) * LANE]
        qp_ref[:, h * LANE:(h + 1) * LANE] = qn
    zkv = _dot(xn, w_ref[:, _C_KV:_C_GN])
    kvc_ref[...] = zkv[:, 0:2 * D_KV]
    ks = _halfnorm(zkv[:, 2 * D_KV:3 * D_KV], gks_ref[...])
    vs = zkv[:, 3 * D_KV:4 * D_KV]
    kw = _halfnorm(zkv[:, 4 * D_KV:5 * D_KV], gkw_ref[...])
    vw = zkv[:, 5 * D_KV:6 * D_KV]
    kvs_ref[:, 0:D_KV] = ks
    kvs_ref[:, D_KV:] = vs
    kvw_ref[:, 0:D_KV] = kw
    kvw_ref[:, D_KV:] = vw
    kvsb_ref[:, 0:D_KV] = ks.astype(BF16)
    kvsb_ref[:, D_KV:2 * D_KV] = vs.astype(BF16)
    kvsb_ref[:, 2 * D_KV:] = jnp.ones((ks.shape[0], D_KV), BF16)
    kvwb_ref[:, 0:D_KV] = kw.astype(BF16)
    kvwb_ref[:, D_KV:] = vw.astype(BF16)
    gn_ref[...] = jax.nn.sigmoid(_dot(xn, w_ref[:, _C_GN:_C_GA]))
    ga_ref[...] = jax.nn.sigmoid(_dot(xn, w_ref[:, _C_GA:_C_GB]))
    gb_ref[...] = jax.nn.sigmoid(_dot(xn, w_ref[:, _C_GB:_C_END]))


def _mix_in(h, g, w, gq, gks, gkw):
    n = h.shape[0]
    tm = _pick_tile(n, 256)
    row = lambda width: pl.BlockSpec((tm, width), lambda i: (i, 0))
    widths = [(D_SSM, F32), (Q_PAD, F32), (2 * D_KV, F32), (2 * D_KV, F32), (2 * D_KV, F32),
              (3 * D_KV, BF16), (2 * D_KV, BF16), (LANE, F32), (D_MODEL, F32), (D_MODEL, F32)]
    return pl.pallas_call(
        _mix_in_kernel,
        out_shape=[jax.ShapeDtypeStruct((n, wd), dt) for wd, dt in widths],
        grid=(n // tm,),
        in_specs=[row(D_MODEL), _const_spec((1, D_MODEL)), _const_spec(w.shape),
                  _const_spec((1, Q_PAD)), _const_spec((1, LANE)), _const_spec((1, LANE))],
        out_specs=[row(wd) for wd, _ in widths],
        compiler_params=_cparams("parallel"),
    )(h, g, w, gq, gks, gkw)


def _gelu_tanh(x):
    return 0.5 * x * (1.0 + jnp.tanh(0.7978845608028654 * (x + 0.044715 * (x * x * x))))


def _s5_kernel(*refs, seg, carry, seq_steps=None):
    if carry:
        (u_ref, bm_ref, cr_ref, ci_ref, d_ref, lam_ref, pr_ref, pi_ref,
         y_ref, hr_out, hi_out, car_ref, cai_ref) = refs
    else:
        (u_ref, bm_ref, cr_ref, ci_ref, d_ref, lam_ref, pr_ref, pi_ref, h0r_ref, h0i_ref, y_all_ref,
         y_ref, hr_out, hi_out) = refs
    u = u_ref[...]
    rows = u.shape[0]
    x = _dot(u.astype(BF16), bm_ref[...])
    xr = x[:, :N_STATE]
    xi = x[:, N_STATE:]
    d, k = 1, 0
    while d < seg:
        lr = jnp.tile(lam_ref[2 * k * seg:(2 * k + 1) * seg, :], (rows // seg, 1))
        li = jnp.tile(lam_ref[(2 * k + 1) * seg:(2 * k + 2) * seg, :], (rows // seg, 1))
        sr = pltpu.roll(xr, d, 0)
        si = pltpu.roll(xi, d, 0)
        xr, xi = xr + lr * sr - li * si, xi + lr * si + li * sr
        d *= 2
        k += 1
    p_r = pr_ref[...]
    p_i = pi_ref[...]
    if carry:
        @pl.when(pl.program_id(0) == 0)
        def _():
            car_ref[...] = jnp.zeros_like(car_ref)
            cai_ref[...] = jnp.zeros_like(cai_ref)
        c_r = car_ref[...]
        c_i = cai_ref[...]
        hr_parts, hi_parts = [], []
        for j in range(rows // seg):
            tr = xr[j * seg:(j + 1) * seg]
            ti = xi[j * seg:(j + 1) * seg]
            hr_j = tr + p_r * c_r - p_i * c_i
            hi_j = ti + p_r * c_i + p_i * c_r
            c_r = hr_j[seg - 1:seg, :]
            c_i = hi_j[seg - 1:seg, :]
            hr_parts.append(hr_j)
            hi_parts.append(hi_j)
        hr = jnp.concatenate(hr_parts, axis=0)
        hi = jnp.concatenate(hi_parts, axis=0)
    else:
        c_r = h0r_ref[...]
        c_i = h0i_ref[...]
        hr = xr + p_r * c_r - p_i * c_i
        hi = xi + p_r * c_i + p_i * c_r
    y = _dot(hr.astype(BF16), cr_ref[...]) + _dot(hi.astype(BF16), ci_ref[...]) + d_ref[...] * u
    y_ref[...] = _gelu_tanh(y).astype(BF16)
    if carry:
        car_ref[...] = hr[rows - 1:rows, :]
        cai_ref[...] = hi[rows - 1:rows, :]

        @pl.when(pl.program_id(0) < seq_steps)
        def _():
            hr_out[...] = hr[rows - 1:rows, :]
            hi_out[...] = hi[rows - 1:rows, :]
    else:
        hr_out[...] = hr
        hi_out[...] = hi


def _s5_prompt(u, t, sp):
    rows = Q_TILE
    consts = [sp["bmat"], sp["cr"], sp["ci"], sp["d"], sp["lam_p"], sp["pr_p"], sp["pi_p"]]
    return pl.pallas_call(
        functools.partial(_s5_kernel, seg=SCAN_SEG, carry=True, seq_steps=t // rows),
        out_shape=[jax.ShapeDtypeStruct((u.shape[0], D_SSM), BF16),
                   jax.ShapeDtypeStruct((1, N_STATE), F32), jax.ShapeDtypeStruct((1, N_STATE), F32)],
        grid=(u.shape[0] // rows,),
        in_specs=[pl.BlockSpec((rows, D_SSM), lambda i: (i, 0))] + [_const_spec(c.shape) for c in consts],
        out_specs=[pl.BlockSpec((rows, D_SSM), lambda i: (i, 0)),
                   _const_spec((1, N_STATE)), _const_spec((1, N_STATE))],
        scratch_shapes=[pltpu.VMEM((1, N_STATE), F32), pltpu.VMEM((1, N_STATE), F32)],
        compiler_params=_cparams("arbitrary"),
    )(u, *consts)


def _s5_sample(u, yg, t, ns_rows, nq, h0r, h0i, sp):
    consts = [sp["bmat"], sp["cr"], sp["ci"], sp["d"], sp["lam_s"], sp["pr_s"], sp["pi_s"], h0r, h0i]
    blk = t // ns_rows
    return pl.pallas_call(
        functools.partial(_s5_kernel, seg=nq, carry=False),
        out_shape=[jax.ShapeDtypeStruct(yg.shape, BF16),
                   jax.ShapeDtypeStruct((ns_rows, N_STATE), F32), jax.ShapeDtypeStruct((ns_rows, N_STATE), F32)],
        grid=(1,),
        in_specs=([pl.BlockSpec((ns_rows, D_SSM), lambda i: (blk, 0))] + [_const_spec(c.shape) for c in consts]
                  + [pl.BlockSpec(memory_space=pl.ANY)]),
        out_specs=[pl.BlockSpec((ns_rows, D_SSM), lambda i: (blk, 0)),
                   _const_spec((ns_rows, N_STATE)), _const_spec((ns_rows, N_STATE))],
        input_output_aliases={len(consts) + 1: 0},
        compiler_params=_cparams("arbitrary"),
    )(u, *consts, yg)


N_SPAIR = CMP_STRIDE // 2


def _compress_part(get_x2, w_ref, a0, a1, part, rows):
    base = pl.multiple_of(part * rows, rows)
    for kv in range(2):
        acc = None
        for sp in range(N_SPAIR):
            term = _dot(get_x2(kv, sp), w_ref[kv, sp])
            acc = term if acc is None else acc + term
        a0[pl.ds(base, rows), kv * D_KV:(kv + 1) * D_KV] = acc[:, 0:D_KV]
        a1[pl.ds(base, rows), kv * D_KV:(kv + 1) * D_KV] = acc[:, D_KV:]


def _compress_finish(a0, a1, w_ref, pe_ref, gk_ref, kcc_ref, vcc_ref, nb):
    bias = []
    for j in range(CMP_BLOCK // CMP_STRIDE):
        halves = []
        for kv in range(2):
            b = jnp.zeros((8, 2 * D_KV), F32)
            for sp in range(N_SPAIR):
                r = (j * 2 + kv) * N_SPAIR + sp
                b = b + _dot(jnp.broadcast_to(pe_ref[r:r + 1, :], (8, 2 * D_KV)).astype(BF16), w_ref[kv, sp])
            halves.append(b[0:1, j * D_KV:(j + 1) * D_KV])
        bias.append(jnp.concatenate(halves, axis=1))
    out = (a0[...] + bias[0]) + pltpu.roll(a1[...] + bias[1], nb - 1, 0)
    valid = lax.broadcasted_iota(jnp.int32, (nb, 1), 0) < nb - 1
    out = jnp.where(valid, out, 0.0)
    kcc_ref[0] = _halfnorm(out[:, 0:D_KV], gk_ref[...]).astype(BF16)
    vcc_ref[0] = out[:, D_KV:].astype(BF16)


def _compress_rows_kernel(xk_ref, xv_ref, w_ref, pe_ref, gk_ref, kcc_ref, vcc_ref, a0, a1, *, nsplit, nb):
    part = pl.program_id(0)
    rows = nb // nsplit

    def get_x2(kv, sp):
        x_ref = (xk_ref, xv_ref)[kv]
        pair = [x_ref[pl.ds(2 * sp + si, rows, stride=CMP_STRIDE), :] for si in range(2)]
        return jnp.concatenate(pair, axis=1).astype(BF16)
    _compress_part(get_x2, w_ref, a0, a1, part, rows)

    @pl.when(part == nsplit - 1)
    def _():
        _compress_finish(a0, a1, w_ref, pe_ref, gk_ref, kcc_ref, vcc_ref, nb)


def _compress_rows(kv_cmp, t, cw):
    nb = t // CMP_STRIDE
    nsplit = 2
    return pl.pallas_call(
        functools.partial(_compress_rows_kernel, nsplit=nsplit, nb=nb),
        out_shape=[jax.ShapeDtypeStruct((1, nb, D_KV), BF16), jax.ShapeDtypeStruct((1, nb, D_KV), BF16)],
        grid=(nsplit,),
        in_specs=[pl.BlockSpec((t // nsplit, D_KV), lambda i: (i, 0)),
                  pl.BlockSpec((t // nsplit, D_KV), lambda i: (i, 1)),
                  _const_spec(cw["w"].shape), _const_spec(cw["pe"].shape), _const_spec((1, LANE))],
        out_specs=[_const_spec((1, nb, D_KV)), _const_spec((1, nb, D_KV))],
        scratch_shapes=[pltpu.VMEM((nb, 2 * D_KV), F32), pltpu.VMEM((nb, 2 * D_KV), F32)],
        compiler_params=_cparams("arbitrary"),
    )(kv_cmp, kv_cmp, cw["w"], cw["pe"], cw["gk"])


def _compress_kernel(pt_ref, pages_ref, w_ref, pe_ref, gk_ref, perm_ref,
                     kcc_ref, vcc_ref, buf, sem, x_scr, a0, a1, *, pps, nsplit, nb):
    s = pl.program_id(0)
    nsteps = pl.num_programs(0)

    def page_copy(step, p, slot):
        b = step // nsplit
        part = step % nsplit
        pg = pt_ref[b, part * pps + p]
        col = pl.multiple_of((p % 2) * PAGE_SIZE, PAGE_SIZE)
        return pltpu.make_async_copy(pages_ref.at[pg], buf.at[slot, p // 2, :, pl.ds(col, PAGE_SIZE)],
                                     sem.at[slot])

    def start(step, slot):
        def body(p, c):
            page_copy(step, p, slot).start()
            return c
        lax.fori_loop(0, pps, body, 0)

    def wait(step, slot):
        def body(p, c):
            page_copy(step, p, slot).wait()
            return c
        lax.fori_loop(0, pps, body, 0)

    @pl.when(s == 0)
    def _():
        start(s, 0)

    @pl.when(s + 1 < nsteps)
    def _():
        start(s + 1, (s + 1) % 2)

    slot = s % 2
    wait(s, slot)
    part = s % nsplit

    def to_rows(q, c):
        xt = buf[slot, q].astype(BF16)
        xp = _dot_nt(perm_ref[...], xt).astype(BF16)
        base = pl.multiple_of(q * CMP_STRIDE, CMP_STRIDE)
        for s2 in range(CMP_STRIDE):
            for kv in range(2):
                x_scr[kv, s2 // 2, pl.ds(base, CMP_STRIDE), (s2 % 2) * D_KV:(s2 % 2 + 1) * D_KV] = (
                    xp[s2 * CMP_STRIDE:(s2 + 1) * CMP_STRIDE, kv * D_KV:(kv + 1) * D_KV])
        return c
    lax.fori_loop(0, pps // 2, to_rows, 0, unroll=16)
    _compress_part(lambda kv, sp: x_scr[kv, sp], w_ref, a0, a1, part, nb // nsplit)

    @pl.when(part == nsplit - 1)
    def _():
        _compress_finish(a0, a1, w_ref, pe_ref, gk_ref, kcc_ref, vcc_ref, nb)


def _compress_paged(page_table, pages, cw):
    bsz, n_pages = page_table.shape
    nb = n_pages * (PAGE_SIZE // CMP_STRIDE)
    nsplit = 2
    pps = n_pages // nsplit
    const = lambda shape: pl.BlockSpec(shape, lambda s, pt: (0,) * len(shape), pipeline_mode=pl.Buffered(1))
    grid_spec = pltpu.PrefetchScalarGridSpec(
        num_scalar_prefetch=1,
        grid=(bsz * nsplit,),
        in_specs=[pl.BlockSpec(memory_space=pl.ANY), const(cw["w"].shape), const(cw["pe"].shape),
                  const((1, LANE)), const(cw["perm"].shape)],
        out_specs=[pl.BlockSpec((1, nb, D_KV), lambda s, pt: (s // nsplit, 0, 0)),
                   pl.BlockSpec((1, nb, D_KV), lambda s, pt: (s // nsplit, 0, 0))],
        scratch_shapes=[pltpu.VMEM((2, pps // 2, 2 * D_KV, 2 * PAGE_SIZE), F32), pltpu.SemaphoreType.DMA((2,)),
                        pltpu.VMEM((2, N_SPAIR, nb // nsplit, 2 * D_KV), BF16),
                        pltpu.VMEM((nb, 2 * D_KV), F32), pltpu.VMEM((nb, 2 * D_KV), F32)],
    )
    return pl.pallas_call(
        functools.partial(_compress_kernel, pps=pps, nsplit=nsplit, nb=nb),
        out_shape=[jax.ShapeDtypeStruct((bsz, nb, D_KV), BF16), jax.ShapeDtypeStruct((bsz, nb, D_KV), BF16)],
        grid_spec=grid_spec,
        compiler_params=_cparams("arbitrary"),
    )(page_table, pages, cw["w"], cw["pe"], cw["gk"], cw["perm"])


def _heads_rows(qp):
    return jnp.concatenate([qp[:, h * LANE:(h + 1) * LANE] for h in range(N_HEADS)], axis=0).astype(BF16)


def _slope2(h):
    return LOG2E * 2.0 ** (-8.0 * (h + 1) / N_HEADS)


def _branch(q_all, tq, pieces):
    s_all = [_dot(q_all, k) if fm else _dot_nt(q_all, k) for k, _, _, _, fm in pieces]
    es, invs = [], []
    for h in range(N_HEADS):
        rows = slice(h * tq, (h + 1) * tq)
        sm = [jnp.where(pc[3], s[rows] - _slope2(h) * pc[2], NEG) for s, pc in zip(s_all, pieces)]
        m = functools.reduce(jnp.maximum, [jnp.max(x, axis=-1, keepdims=True) for x in sm])
        m = jnp.maximum(m, HALF_NEG)
        e = [jnp.exp2(x - m) for x in sm]
        l = functools.reduce(lambda a, b: a + b, [jnp.sum(x, axis=-1, keepdims=True) for x in e])
        es.append(e)
        invs.append(1.0 / jnp.maximum(l, 1e-30))
    o = None
    for i, (_, v, _, _, fm) in enumerate(pieces):
        p = jnp.concatenate([es[h][i] for h in range(N_HEADS)], axis=0).astype(BF16)
        t = _dot_nt(p, v) if fm else _dot(p, v)
        o = t if o is None else o + t
    return o * jnp.concatenate(invs, axis=0), es, invs


def _cmp_win_topk(q_all, tq, qpos, gn, kcc, vcc, mband, win_pieces, nb, ns, topn):
    n_io = lax.broadcasted_iota(jnp.int32, (1, kcc.shape[0]), 1)
    kc_end = n_io * CMP_STRIDE + (CMP_BLOCK - 1)
    mask_c = (qpos >= kc_end) & (n_io < nb - 1)
    kdist_c = (qpos[0:1, :] - kc_end).astype(F32)
    o_c, es, invs = _branch(q_all, tq, [(kcc, vcc, kdist_c, mask_c, False)])
    o_w, _, _ = _branch(q_all, tq, win_pieces)
    imps = []
    for g in range(N_KV):
        psum = es[HPG * g][0] * invs[HPG * g]
        for hh in range(1, HPG):
            psum = psum + es[HPG * g + hh][0] * invs[HPG * g + hh]
        hi = psum.astype(BF16)
        r = psum - hi.astype(F32)
        mid = r.astype(BF16)
        lo = (r - mid.astype(F32)).astype(BF16)
        imps.append(_dot(hi, mband) + _dot(mid, mband) + _dot(lo, mband))
    imp = jnp.concatenate(imps, axis=0)
    blk = lax.broadcasted_iota(jnp.int32, (1, ns), 1)
    cur = jnp.concatenate([jnp.right_shift(qpos, 6)] * N_KV, axis=0)
    forced = (blk == 0) | (blk == cur) | (blk == cur - 1)
    score = jnp.where(forced, imp + SEL_BIG, jnp.where(blk <= cur, imp, -SEL_BIG))
    n_rows = score.shape[0]
    pad = (-n_rows) % LANE
    if pad:
        score = jnp.concatenate([score, jnp.zeros((pad, ns), F32)], axis=0)
    score = score.T
    blkf = lax.broadcasted_iota(jnp.int32, (ns, 1), 0).astype(F32)
    for _ in range(topn):
        m = jnp.max(score, axis=0, keepdims=True)
        first = jnp.min(jnp.where(score == m, blkf, float(ns)), axis=0, keepdims=True)
        score = jnp.where(blkf == first, -jnp.inf, score)
    sel = jnp.where(score == -jnp.inf, 1.0, 0.0).T[:n_rows]
    sel = jnp.where(blk <= cur, sel, 0.0)
    outs = []
    for h in range(N_HEADS):
        rows = slice(h * tq, (h + 1) * tq)
        outs.append(gn[:, h:h + 1] * o_c[rows] + gn[:, 2 * N_HEADS + h:2 * N_HEADS + h + 1] * o_w[rows])
    return jnp.concatenate(outs, axis=1), sel


def _e1_prompt_kernel(qp_ref, gn_ref, kcc_ref, vcc_ref, mband_ref, kvw_ref, ocw_ref, sel_ref, flag_ref,
                      *, nb, ns, widths):
    tq = Q_TILE
    s0 = pl.program_id(0) * tq
    qpos = s0 + lax.broadcasted_iota(jnp.int32, (tq, 1), 0)

    def run(width):
        q_all = _heads_rows(qp_ref[...])
        start = pl.multiple_of(jnp.maximum(s0 - WINDOW, 0), Q_TILE)
        kw = kvw_ref[pl.ds(start, WIN_KEYS), 0:D_KV]
        vw = kvw_ref[pl.ds(start, WIN_KEYS), D_KV:]
        kpos = start + lax.broadcasted_iota(jnp.int32, (1, WIN_KEYS), 1)
        dist_w = qpos - kpos
        mask_w = (dist_w >= 0) & (dist_w < WINDOW)
        win_pieces = [(kw, vw, (s0 - kpos).astype(F32), mask_w, False)]
        ocw, sel = _cmp_win_topk(q_all, tq, qpos, gn_ref[...], kcc_ref[0, 0:width, :], vcc_ref[0, 0:width, :],
                                 mband_ref[0:width, :], win_pieces, nb, ns, TOP_N)
        ocw_ref[...] = ocw
        sel_ref[0] = sel[:tq]
        sel_ref[1] = sel[tq:]
        colany = jnp.max(sel, axis=0, keepdims=True)
        j_io = lax.broadcasted_iota(jnp.int32, (ns, LANE), 0)
        c_io = lax.broadcasted_iota(jnp.int32, (ns, LANE), 1)
        grp = jnp.where(jnp.right_shift(j_io, 2) == c_io, 1.0, 0.0).astype(BF16)
        cnt = _dot(jnp.broadcast_to(colany, (8, ns)).astype(BF16), grp)
        flag_ref[0] = (cnt > 0.5).astype(jnp.int32)

    nvar = len(widths)
    variant = jnp.minimum(pl.program_id(0) // (pl.num_programs(0) // nvar), nvar - 1)
    for vi, width in enumerate(widths):
        pl.when(variant == vi)(functools.partial(run, width))


def _e1_prompt(qp, gn, kcc, vcc, mband, kvw_bf, t):
    nb, ns = mband.shape
    nt = t // Q_TILE
    nvar = max(v for v in (1, 2, 4) if nb % (v * LANE) == 0 and nt % v == 0)
    widths = tuple(nb * (v + 1) // nvar for v in range(nvar))
    row = lambda w: pl.BlockSpec((Q_TILE, w), lambda i: (i, 0))
    return pl.pallas_call(
        functools.partial(_e1_prompt_kernel, nb=nb, ns=ns, widths=widths),
        out_shape=[jax.ShapeDtypeStruct((t, Q_PAD), F32), jax.ShapeDtypeStruct((N_KV, t, ns), F32),
                   jax.ShapeDtypeStruct((nt, 8, LANE), jnp.int32)],
        grid=(nt,),
        in_specs=[row(Q_PAD), row(LANE), _const_spec((1, nb, D_KV)), _const_spec((1, nb, D_KV)),
                  _const_spec(mband.shape), _const_spec(kvw_bf.shape)],
        out_specs=[row(Q_PAD), pl.BlockSpec((N_KV, Q_TILE, ns), lambda i: (0, i, 0)),
                   pl.BlockSpec((1, 8, LANE), lambda i: (i, 0, 0))],
        compiler_params=_cparams("parallel"),
    )(qp, gn, kcc, vcc, mband, kvw_bf)


def _pad_rows(x, rows):
    return jnp.concatenate([x, jnp.zeros((rows - x.shape[0], x.shape[1]), x.dtype)], axis=0)


def _e1_sample_kernel(qp_ref, gn_ref, kcc_ref, vcc_ref, mband_ref, cwin_ref, nwin_ref, ocw_ref, sel_ref,
                      *, nb, ns, nq, past):
    qi = lax.broadcasted_iota(jnp.int32, (nq, 1), 0)
    qpos = past + qi
    q_all = _heads_rows(qp_ref[...])
    cw = cwin_ref[0]
    w_buf = cw.shape[1]
    j_c = lax.broadcasted_iota(jnp.int32, (1, w_buf), 1)
    dist_cw = qi + (w_buf - j_c)
    mask_cw = (dist_cw >= 0) & (dist_cw < WINDOW)
    nw = _pad_rows(nwin_ref[...], LANE)
    j_n = lax.broadcasted_iota(jnp.int32, (1, LANE), 1)
    mask_nw = (qi >= j_n) & (j_n < nq)
    pieces = [(cw[0:D_KV, :].astype(BF16), cw[D_KV:, :].astype(BF16), (w_buf - j_c).astype(F32), mask_cw, True),
              (nw[:, 0:D_KV].astype(BF16), nw[:, D_KV:].astype(BF16), (-j_n).astype(F32), mask_nw, False)]
    ocw, sel = _cmp_win_topk(q_all, nq, qpos, gn_ref[...], kcc_ref[0], vcc_ref[0], mband_ref[...],
                             pieces, nb, ns, TOP_N - 1)
    ocw_ref[...] = ocw
    sel_ref[0, 0] = sel[:nq]
    sel_ref[0, 1] = sel[nq:]


def _e1_sample(qp, gn, kcc, vcc, mband, cache_win, win_off, kv_win, t, bsz, nq, past):
    nb, ns = mband.shape
    w_buf = cache_win.shape[2]
    off = t // nq
    row = lambda w: pl.BlockSpec((nq, w), lambda b: (off + b, 0))
    per_b = lambda shape: pl.BlockSpec((1,) + shape, lambda b: (b,) + (0,) * len(shape))
    return pl.pallas_call(
        functools.partial(_e1_sample_kernel, nb=nb, ns=ns, nq=nq, past=past),
        out_shape=[jax.ShapeDtypeStruct((bsz * nq, Q_PAD), F32), jax.ShapeDtypeStruct((bsz, N_KV, nq, ns), F32)],
        grid=(bsz,),
        in_specs=[row(Q_PAD), row(LANE), per_b((nb, D_KV)), per_b((nb, D_KV)), _const_spec(mband.shape),
                  pl.BlockSpec((1, 2 * D_KV, w_buf), lambda b: (win_off + b, 0, 0)), row(2 * D_KV)],
        out_specs=[pl.BlockSpec((nq, Q_PAD), lambda b: (b, 0)), per_b((N_KV, nq, ns))],
        compiler_params=_cparams("parallel"),
    )(qp, gn, kcc, vcc, mband, cache_win, kv_win)


def _online_step(q_all, k, kdist, masks, v_ones, m_ref, acc_ref, tq, feature_major=False, split_groups=False):
    reps = kdist.shape[1] // LANE
    per_chain = 1 if split_groups else N_HEADS
    chains = [(a, a + per_chain) for a in range(0, N_HEADS, per_chain)]
    for h_lo, h_hi in chains:
        grows = slice(h_lo * tq, h_hi * tq)
        s = _dot(q_all[grows], k) if feature_major else _dot_nt(q_all[grows], k)
        ps, alphas = [], []
        for h in range(h_lo, h_hi):
            hh = h - h_lo
            rows = slice(h * tq, (h + 1) * tq)
            sm = jnp.where(masks[h // HPG], s[hh * tq:(hh + 1) * tq] - _slope2(h) * kdist, NEG)
            m_old = m_ref[rows, :]
            m_new = jnp.maximum(m_old, jnp.max(sm, axis=-1, keepdims=True))
            alphas.append(jnp.exp2(m_old - m_new))
            ps.append(jnp.exp2(sm - jnp.tile(m_new, (1, reps))))
            m_ref[rows, :] = m_new
        p_g = jnp.concatenate(ps, axis=0).astype(BF16)
        a_g = jnp.concatenate(alphas, axis=0)
        pv = _dot_nt(p_g, v_ones) if feature_major else _dot(p_g, v_ones)
        acc_ref[grows, :] = jnp.tile(a_g, (1, 2)) * acc_ref[grows, :] + pv


def _block_to_keys(ns, first_blk, n_keys):
    j_io = lax.broadcasted_iota(jnp.int32, (ns, n_keys), 0)
    kb = first_blk + jnp.right_shift(lax.broadcasted_iota(jnp.int32, (ns, n_keys), 1), 6)
    return jnp.where(j_io == kb, 1.0, 0.0).astype(BF16)


def _init_online(m_ref, acc_ref):
    m_ref[...] = jnp.full(m_ref.shape, HALF_NEG, F32)
    acc_ref[...] = jnp.zeros(acc_ref.shape, F32)


def _finish_online(ocw, gn, acc_ref, tq):
    o_s = acc_ref[:, 0:LANE] * (1.0 / jnp.maximum(acc_ref[:, LANE:], 1e-30))
    outs = []
    for h in range(N_HEADS):
        g1 = gn[:, N_HEADS + h:N_HEADS + h + 1]
        outs.append(ocw[:, h * LANE:(h + 1) * LANE] + g1 * o_s[h * tq:(h + 1) * tq])
    return jnp.concatenate(outs, axis=1)


def _e2_prompt_kernel(order_ref, cnt_ref, qp_ref, sel_ref, gn_ref, ocw_ref, kvs_ref, out_ref, m_ref, acc_ref,
                      *, nchunks, ns, nt):
    tq = Q_TILE
    i = pl.program_id(0)
    s0 = i * tq
    qpos = s0 + lax.broadcasted_iota(jnp.int32, (tq, 1), 0)
    q_all = _heads_rows(qp_ref[...])
    _init_online(m_ref, acc_ref)
    n_act = jnp.where(i < nt, cnt_ref[jnp.minimum(i, nt - 1)], 0)
    blocks_per_chunk = SEL_CHUNK // SEL_BLOCK
    sel_bf = [sel_ref[g].astype(BF16) for g in range(N_KV)]

    def body(j, carry):
        c1 = order_ref[i * nchunks + 2 * j]
        paired = 2 * j + 1 < n_act
        c2 = jnp.where(paired, order_ref[i * nchunks + jnp.minimum(2 * j + 1, nchunks - 1)], c1)
        ks, vs, dists, masks = [], [], [], [[] for _ in range(N_KV)]
        for c, first_blk in ((c1, c1 * blocks_per_chunk), (c2, jnp.where(paired, c2 * blocks_per_chunk, ns))):
            base = pl.multiple_of(c * SEL_CHUNK, SEL_CHUNK)
            ks.append(kvs_ref[pl.ds(base, SEL_CHUNK), 0:D_KV])
            vs.append(kvs_ref[pl.ds(base, SEL_CHUNK), D_KV:])
            kpos = base + lax.broadcasted_iota(jnp.int32, (1, SEL_CHUNK), 1)
            dists.append((s0 - kpos).astype(F32))
            expand = _block_to_keys(ns, first_blk, SEL_CHUNK)
            causal = qpos >= kpos
            for g in range(N_KV):
                masks[g].append((_dot(sel_bf[g], expand) > 0.5) & causal)
        _online_step(q_all, jnp.concatenate(ks, axis=0), jnp.concatenate(dists, axis=1),
                     [jnp.concatenate(mg, axis=1) for mg in masks], jnp.concatenate(vs, axis=0), m_ref, acc_ref, tq,
                     split_groups=True)
        return carry

    lax.fori_loop(0, (n_act + 1) // 2, body, 0)
    out_ref[...] = _finish_online(ocw_ref[...], gn_ref[...], acc_ref, tq)


def _e2_prompt(order, cnt, qp, sel, gn, ocw, kvs_bf, t):
    ns = sel.shape[2]
    nt = t // Q_TILE
    nchunks = order.shape[0] // nt
    row = lambda w: pl.BlockSpec((Q_TILE, w), lambda i, o, c: (i, 0))
    prow = lambda w: pl.BlockSpec((Q_TILE, w), lambda i, o, c: (jnp.minimum(i, nt - 1), 0))
    grid_spec = pltpu.PrefetchScalarGridSpec(
        num_scalar_prefetch=2,
        grid=(qp.shape[0] // Q_TILE,),
        in_specs=[row(Q_PAD), pl.BlockSpec((N_KV, Q_TILE, ns), lambda i, o, c: (0, jnp.minimum(i, nt - 1), 0)),
                  row(LANE), prow(Q_PAD), pl.BlockSpec(kvs_bf.shape, lambda i, o, c: (0, 0))],
        out_specs=row(Q_PAD),
        scratch_shapes=[pltpu.VMEM((N_HEADS * Q_TILE, LANE), F32), pltpu.VMEM((N_HEADS * Q_TILE, 2 * LANE), F32)],
    )
    return pl.pallas_call(
        functools.partial(_e2_prompt_kernel, nchunks=nchunks, ns=ns, nt=nt),
        out_shape=jax.ShapeDtypeStruct((qp.shape[0], Q_PAD), F32),
        grid_spec=grid_spec,
        compiler_params=_cparams("arbitrary"),
    )(order, cnt, qp, sel, gn, ocw, kvs_bf)


def _e2_sample_kernel(pt_ref, live_ref, pages_ref, qp_ref, sel_ref, gn_ref, ocw_ref, nsel_ref, expand_ref,
                      oatt_all_ref, out_ref, buf, sem, m_ref, acc_ref, *, pps, nsplit, nq):
    s = pl.program_id(0)
    nsteps = pl.num_programs(0)
    n_keys = pps * PAGE_SIZE

    def page_copy(step, p, slot):
        b = step // nsplit
        part = step % nsplit
        pg = pt_ref[b, part * pps + p]
        col = pl.multiple_of(p * PAGE_SIZE, PAGE_SIZE)
        return pltpu.make_async_copy(pages_ref.at[pg], buf.at[slot, :, pl.ds(col, PAGE_SIZE)], sem.at[slot])

    def start(step, slot):
        def body(p, c):
            page_copy(step, p, slot).start()
            return c
        lax.fori_loop(0, pps, body, 0)

    def wait(step, slot):
        def body(p, c):
            page_copy(step, p, slot).wait()
            return c
        lax.fori_loop(0, pps, body, 0)

    @pl.when((s == 0) & (live_ref[0] > 0))
    def _():
        start(s, 0)

    @pl.when((s + 1 < nsteps) & (live_ref[jnp.minimum(s + 1, nsteps - 1)] > 0))
    def _():
        start(s + 1, (s + 1) % 2)

    slot = s % 2
    part = s % nsplit
    qi = lax.broadcasted_iota(jnp.int32, (nq, 1), 0)
    q_all = _heads_rows(qp_ref[...])

    @pl.when(part == 0)
    def _():
        _init_online(m_ref, acc_ref)

    @pl.when(live_ref[s] > 0)
    def _():
        wait(s, slot)
        k = buf[slot, 0:D_KV, :].astype(BF16)
        v_ones = jnp.concatenate([buf[slot, D_KV:, :].astype(BF16), jnp.ones((D_KV, n_keys), BF16)], axis=0)
        back = (nsplit - part) * n_keys - lax.broadcasted_iota(jnp.int32, (1, n_keys), 1)
        masks = [_dot(sel_ref[0, g, 0].astype(BF16), expand_ref[...]) > 0.5 for g in range(N_KV)]
        _online_step(q_all, k, back.astype(F32), masks, v_ones, m_ref, acc_ref, nq, feature_major=True)

    @pl.when(part == nsplit - 1)
    def _():
        nw = _pad_rows(nsel_ref[...], LANE)
        j_n = lax.broadcasted_iota(jnp.int32, (1, LANE), 1)
        mask_n = (qi >= j_n) & (j_n < nq)
        vn_ones = jnp.concatenate([nw[:, D_KV:].astype(BF16), jnp.ones((LANE, D_KV), BF16)], axis=1)
        _online_step(q_all, nw[:, 0:D_KV].astype(BF16), (-j_n).astype(F32), [mask_n, mask_n], vn_ones,
                     m_ref, acc_ref, nq)
        out_ref[...] = _finish_online(ocw_ref[...], gn_ref[...], acc_ref, nq)


def _e2_sample(page_table, pages, qp, sel, gn, ocw, kv_sel, oatt, t, nq):
    bsz, n_pages = page_table.shape
    ns = sel.shape[3]
    nsplit = 8
    pps = n_pages // nsplit
    n_keys = pps * PAGE_SIZE
    nblk = ns // nsplit
    off = t // nq
    sel_parts = sel.reshape(bsz, N_KV, nq, nsplit, nblk).transpose(0, 1, 3, 2, 4)
    live = (jnp.max(sel_parts, axis=(1, 3, 4)) > 0).astype(jnp.int32).reshape(-1)
    expand = (jnp.arange(nblk)[:, None] == jnp.arange(n_keys)[None, :] // SEL_BLOCK).astype(BF16)
    row = lambda w: pl.BlockSpec((nq, w), lambda s, pt, lv: (off + s // nsplit, 0))
    grid_spec = pltpu.PrefetchScalarGridSpec(
        num_scalar_prefetch=2,
        grid=(bsz * nsplit,),
        in_specs=[pl.BlockSpec(memory_space=pl.ANY), row(Q_PAD),
                  pl.BlockSpec((1, N_KV, 1, nq, nblk), lambda s, pt, lv: (s // nsplit, 0, s % nsplit, 0, 0)),
                  row(LANE), pl.BlockSpec((nq, Q_PAD), lambda s, pt, lv: (s // nsplit, 0)), row(2 * D_KV),
                  pl.BlockSpec((nblk, n_keys), lambda s, pt, lv: (0, 0), pipeline_mode=pl.Buffered(1)),
                  pl.BlockSpec(memory_space=pl.ANY)],
        out_specs=row(Q_PAD),
        scratch_shapes=[pltpu.VMEM((2, 2 * D_KV, n_keys), F32), pltpu.SemaphoreType.DMA((2,)),
                        pltpu.VMEM((N_HEADS * nq, LANE), F32), pltpu.VMEM((N_HEADS * nq, 2 * LANE), F32)],
    )
    return pl.pallas_call(
        functools.partial(_e2_sample_kernel, pps=pps, nsplit=nsplit, nq=nq),
        out_shape=jax.ShapeDtypeStruct(oatt.shape, F32),
        grid_spec=grid_spec,
        input_output_aliases={9: 0},
        compiler_params=_cparams("arbitrary"),
    )(page_table, live, pages, qp, sel_parts, gn, ocw, kv_sel, expand, oatt)


def _post_kernel(h_ref, yg_ref, oatt_ref, ga_ref, gb_ref, wglu_ref, watt_ref, wout_ref, o_ref):
    gl = _dot(yg_ref[...], wglu_ref[...])
    br_a = gl[:, :D_MODEL] * jax.nn.sigmoid(gl[:, D_MODEL:])
    br_b = _dot(oatt_ref[...].astype(BF16), watt_ref[...])
    merged = (ga_ref[...] * br_a + gb_ref[...] * br_b).astype(BF16)
    o_ref[...] = h_ref[...] + _dot(merged, wout_ref[...])


def _post(h, yg, oatt, ga, gb, wglu, watt, wout):
    n = h.shape[0]
    tm = _pick_tile(n, 256)
    row = lambda w: pl.BlockSpec((tm, w), lambda i: (i, 0))
    return pl.pallas_call(
        _post_kernel,
        out_shape=jax.ShapeDtypeStruct((n, D_MODEL), F32),
        grid=(n // tm,),
        in_specs=[row(D_MODEL), row(D_SSM), row(Q_PAD), row(D_MODEL), row(D_MODEL),
                  _const_spec(wglu.shape), _const_spec(watt.shape), _const_spec(wout.shape)],
        out_specs=row(D_MODEL),
        compiler_params=_cparams("parallel"),
    )(h, yg, oatt, ga, gb, wglu, watt, wout)


def _head_pad_index():
    h = jnp.arange(N_HEADS)[:, None]
    d = jnp.arange(HEAD_DIM)[None, :]
    return (LANE * h + HEAD_DIM * (h // HPG) + d).reshape(-1)


def _prep_mix_weights(w_in, qk_norm):
    idx = _head_pad_index()
    wq = jnp.zeros((D_MODEL, Q_PAD), F32).at[:, idx].set(w_in[:, D_SSM:D_SSM + N_HEADS * HEAD_DIM])
    c0 = D_SSM + N_HEADS * HEAD_DIM
    c1 = c0 + 6 * D_KV
    c2 = c1 + 3 * N_HEADS
    wgn = jnp.zeros((D_MODEL, LANE), F32).at[:, :3 * N_HEADS].set(w_in[:, c1:c2])
    w = jnp.concatenate([w_in[:, :D_SSM], wq, w_in[:, c0:c1], wgn, w_in[:, c2:]], axis=1).astype(BF16)
    gq = jnp.zeros((Q_PAD,), F32).at[idx].set(jnp.tile(qk_norm[0] * (HEAD_DIM ** -0.5 * LOG2E), N_HEADS))[None]
    gks = jnp.tile(qk_norm[2], N_KV)[None]
    gkw = jnp.tile(qk_norm[3], N_KV)[None]
    return w, gq, gks, gkw


def _prep_att_out(w_att_out):
    return jnp.zeros((Q_PAD, D_MODEL), F32).at[_head_pad_index()].set(w_att_out).astype(BF16)


def _prep_compress(pe_k, pe_v, w_k, w_v, gain_k):
    nj = CMP_BLOCK // CMP_STRIDE
    eye = jnp.eye(N_KV, dtype=F32)

    def pair_maps(w):
        w5 = w.reshape(nj, N_SPAIR, 2, HEAD_DIM, HEAD_DIM)
        return jnp.einsum("jpsde,gh->psgdjhe", w5, eye).reshape(N_SPAIR, 2 * D_KV, 2 * D_KV)

    def pair_pe(pe):
        p5 = pe.reshape(nj, N_SPAIR, 2, 1, HEAD_DIM)
        return jnp.broadcast_to(p5, (nj, N_SPAIR, 2, N_KV, HEAD_DIM)).reshape(nj, N_SPAIR, 2 * D_KV)

    w = jnp.stack([pair_maps(w_k), pair_maps(w_v)], axis=0)
    pe = jnp.stack([pair_pe(pe_k), pair_pe(pe_v)], axis=1).reshape(-1, 2 * D_KV)
    r = jnp.arange(2 * PAGE_SIZE)
    perm = (r[None, :] == (CMP_STRIDE * (r % CMP_STRIDE) + r // CMP_STRIDE)[:, None]).astype(BF16)
    return {"w": w.astype(BF16), "pe": pe,
            "gk": jnp.tile(gain_k, N_KV)[None], "perm": perm}


def _prep_s5(a_re, a_im, log_dt, b_re, b_im, c_re, c_im, d, nq):
    dt = jnp.exp(log_dt)[:, None]
    mag = jnp.exp(a_re * dt)
    lr = mag * jnp.cos(a_im * dt)
    li = mag * jnp.sin(a_im * dt)
    den = a_re * a_re + a_im * a_im
    fr = ((lr - 1.0) * a_re + li * a_im) / den
    fi = (li * a_re - (lr - 1.0) * a_im) / den
    bbr = fr[..., None] * b_re - fi[..., None] * b_im
    bbi = fr[..., None] * b_im + fi[..., None] * b_re
    eye = jnp.eye(N_SSM_GROUPS, dtype=F32)
    blk_b = lambda m: jnp.einsum("gpc,gh->gchp", m, eye).reshape(D_SSM, N_STATE)
    blk_c = lambda m: jnp.einsum("gcp,gh->gphc", m, eye).reshape(N_STATE, D_SSM)

    def lam_pow(k):
        kk = k.astype(F32)[:, None, None]
        m = jnp.exp(a_re * dt * kk)
        th = a_im * dt * kk
        return (m * jnp.cos(th)).reshape(-1, N_STATE), (m * jnp.sin(th)).reshape(-1, N_STATE)

    def step_table(seg):
        ks = []
        dd = 1
        while dd < seg:
            ks.append(dd)
            dd *= 2
        re, im = lam_pow(jnp.array(ks))
        live = jnp.arange(seg)[None, :, None] >= jnp.array(ks)[:, None, None]
        tab = jnp.stack([jnp.where(live, re[:, None, :], 0.0), jnp.where(live, im[:, None, :], 0.0)], axis=1)
        return tab.reshape(-1, N_STATE)

    sp = {"bmat": jnp.concatenate([blk_b(bbr), blk_b(bbi)], axis=1).astype(BF16),
          "cr": blk_c(c_re).astype(BF16), "ci": (-blk_c(c_im)).astype(BF16), "d": d[None],
          "lam_p": step_table(SCAN_SEG), "lam_s": step_table(nq)}
    sp["pr_p"], sp["pi_p"] = lam_pow(jnp.arange(SCAN_SEG) + 1)
    return sp, lam_pow


def _band_matrix(nb, ns):
    ratio = SEL_BLOCK // CMP_STRIDE
    lo = CMP_BLOCK // CMP_STRIDE - 1
    c = jnp.arange(nb)[:, None]
    j = jnp.arange(ns)[None, :]
    return ((c >= ratio * j - lo) & (c <= ratio * j + ratio - 1)).astype(BF16)


def kernel(x_prompt, x_sample, cache_kv_cmp, cache_kv_sel, cache_kv_win, state_ssm_re, state_ssm_im, page_table, p_prompt, p_sample, norm_ffn1, w_ffn1_in, w_ffn1_out, norm_mix, w_in, qk_norm, ssm_a_re, ssm_a_im, ssm_log_dt, ssm_b_re, ssm_b_im, ssm_c_re, ssm_c_im, ssm_d, w_glu, cmp_pe_k, cmp_pe_v, cmp_w_k, cmp_w_v, w_att_out, w_out, norm_ffn2, w_ffn2_in, w_ffn2_out, norm_ple, w_ple_gate, w_ple_proj):
    bp, t = x_prompt.shape[:2]
    bsz, nq = x_sample.shape[:2]
    n_pages = page_table.shape[1]
    past = n_pages * PAGE_SIZE
    n_pool = cache_kv_cmp.shape[1]
    w_buf = cache_kv_win.shape[2]
    ns_rows = bsz * nq
    assert bp == 1 and t % SEL_CHUNK == 0 and t >= WIN_KEYS and t % ns_rows == 0 and nq < CMP_STRIDE
    assert past == t and w_buf == WINDOW and ns_rows % Q_TILE == 0
    nb = t // CMP_STRIDE
    ns = t // SEL_BLOCK
    nt = t // Q_TILE
    nchunks = t // SEL_CHUNK
    mband = _band_matrix(nb, ns)
    page_table = page_table.astype(jnp.int32)
    feat_major = lambda c: jnp.transpose(c, (0, 1, 3, 4, 5, 2)).reshape(DEPTH * c.shape[1], 2 * D_KV, c.shape[2])
    pages_cmp = feat_major(cache_kv_cmp)
    pages_sel = feat_major(cache_kv_sel)
    pages_win = feat_major(cache_kv_win)

    h = (x_prompt[0], x_sample.reshape(ns_rows, D_MODEL))
    p_all = (p_prompt.reshape(DEPTH * t, D_PLE), p_sample.reshape(DEPTH * ns_rows, D_PLE))
    st_p = [[] for _ in range(5)]
    st_s = [[] for _ in range(5)]
    for i in range(DEPTH):
        row1 = lambda a: a[i][None]
        h = _ffn(h, row1(norm_ffn1), w_ffn1_in[i].astype(BF16), w_ffn1_out[i].astype(BF16), t, ns_rows)
        w_mix, gq, gks, gkw = _prep_mix_weights(w_in[i], qk_norm[i])
        u, qp, kv_cmp, kv_sel, kv_win, kvs_bf, kvw_bf, gn, ga, gb = _mix_in(h, row1(norm_mix), w_mix, gq, gks, gkw)

        sp, lam_pow = _prep_s5(ssm_a_re[i], ssm_a_im[i], ssm_log_dt[i], ssm_b_re[i], ssm_b_im[i],
                               ssm_c_re[i], ssm_c_im[i], ssm_d[i], nq)
        sp["pr_s"], sp["pi_s"] = lam_pow(jnp.arange(ns_rows) % nq + 1)
        yg, hr_p, hi_p = _s5_prompt(u, t, sp)
        h0r = jnp.repeat(state_ssm_re[i].reshape(bsz, N_STATE), nq, axis=0)
        h0i = jnp.repeat(state_ssm_im[i].reshape(bsz, N_STATE), nq, axis=0)
        yg, hr_s, hi_s = _s5_sample(u, yg, t, ns_rows, nq, h0r, h0i, sp)

        cw = _prep_compress(cmp_pe_k[i], cmp_pe_v[i], cmp_w_k[i], cmp_w_v[i], qk_norm[i, 1])
        layer_pages = page_table + i * n_pool
        kcc_p, vcc_p = _compress_rows(kv_cmp, t, cw)
        kcc_s, vcc_s = _compress_paged(layer_pages, pages_cmp, cw)
        ocw_p, sel_p, flags = _e1_prompt(qp, gn, kcc_p, vcc_p, mband, kvw_bf, t)
        ocw_s, sel_s = _e1_sample(qp, gn, kcc_s, vcc_s, mband, pages_win, i * bsz, kv_win, t, bsz, nq, past)
        idle = flags[:, 0, :nchunks] == 0
        order = jnp.argsort(idle, axis=1, stable=True).astype(jnp.int32).reshape(-1)
        cnt = (nchunks - jnp.sum(idle, axis=1)).astype(jnp.int32)
        oatt = _e2_prompt(order, cnt, qp, sel_p, gn, ocw_p, kvs_bf, t)
        oatt = _e2_sample(layer_pages, pages_sel, qp, sel_s, gn, ocw_s, kv_sel, oatt, t, nq)

        h = _post(h, yg, oatt, ga, gb,
                  w_glu[i].astype(BF16), _prep_att_out(w_att_out[i]), w_out[i].astype(BF16))
        h = _ffn(h, row1(norm_ffn2), w_ffn2_in[i].astype(BF16), w_ffn2_out[i].astype(BF16), t, ns_rows,
                 (p_all, row1(norm_ple), w_ple_gate[i].astype(BF16), w_ple_proj[i].astype(BF16)), p_layer=i)

        kv5 = lambda a, lead: a.reshape(lead + (2, N_KV, HEAD_DIM))
        st_p[0].append(kv5(kv_cmp[:t], (1, t)))
        st_p[1].append(kv5(kv_sel[:t], (1, t)))
        st_p[2].append(kv5(kv_win[t - min(WINDOW, t):t], (1, min(WINDOW, t))))
        st_p[3].append(hr_p.reshape(1, N_SSM_GROUPS, P_STATE))
        st_p[4].append(hi_p.reshape(1, N_SSM_GROUPS, P_STATE))
        st_s[0].append(kv5(kv_cmp[t:], (bsz, nq)))
        st_s[1].append(kv5(kv_sel[t:], (bsz, nq)))
        st_s[2].append(jnp.concatenate([cache_kv_win[i, :, nq:], kv5(kv_win[t:], (bsz, nq))], axis=1))
        st_s[3].append(hr_s[nq - 1::nq].reshape(bsz, N_SSM_GROUPS, P_STATE))
        st_s[4].append(hi_s[nq - 1::nq].reshape(bsz, N_SSM_GROUPS, P_STATE))

    outs_p = [jnp.stack(a) for a in st_p]
    outs_s = [jnp.stack(a) for a in st_s]
    y_prompt = h[:t][None]
    y_sample = h[t:].reshape(bsz, nq, D_MODEL)
    return (y_prompt, y_sample, *outs_p, *outs_s)
```

```python
import functools

import jax
import jax.numpy as jnp
from jax import lax
from jax.experimental import pallas as pl
from jax.experimental.pallas import tpu as pltpu

F32 = jnp.float32
BF16 = jnp.bfloat16

D_MODEL = 1024
DEPTH = 2
D_SSM = 512
SSM_GROUP = 16
N_SSM_GROUPS = 32
P_STATE = 64
N_STATE = N_SSM_GROUPS * P_STATE
N_HEADS = 8
HEAD_DIM = 64
N_KV = 2
HPG = 4
D_KV = 128
CMP_BLOCK = 32
CMP_STRIDE = 16
SEL_BLOCK = 64
TOP_N = 16
WINDOW = 512
PAGE_SIZE = 128
SEL_BIG = 1e4
D_FF = 2816
D_PLE = 256
RMS_EPS = 1e-6
NEG = -1e30
HALF_NEG = -0.5e30
LOG2E = 1.4426950408889634

LANE = 128
Q_TILE = 128
SCAN_SEG = 8
SEL_CHUNK = 256
WIN_KEYS = WINDOW + Q_TILE
Q_PAD = N_HEADS * LANE
VMEM_LIMIT = 56 * 2 ** 20


def _cparams(*sem):
    return pltpu.CompilerParams(dimension_semantics=sem, vmem_limit_bytes=VMEM_LIMIT)


def _dot(a, b):
    return jnp.dot(a, b, preferred_element_type=F32)


def _dot_nt(a, b):
    return lax.dot_general(a, b, (((1,), (1,)), ((), ())), preferred_element_type=F32)


def _pick_tile(n, target):
    for t in range(min(n, target), 15, -1):
        if n % t == 0 and t % 16 == 0:
            return t
    raise ValueError(f"no row tile for {n}")


def _const_spec(shape):
    nd = len(shape)
    return pl.BlockSpec(shape, lambda *_: (0,) * nd, pipeline_mode=pl.Buffered(1))


def _rms(x, g):
    ms = jnp.mean(x * x, axis=-1, keepdims=True)
    return x * lax.rsqrt(ms + RMS_EPS) * g


def _halfnorm(x, gain):
    lo = lax.broadcasted_iota(jnp.int32, (1, LANE), 1) < HEAD_DIM
    x2 = x * x
    s_lo = jnp.sum(jnp.where(lo, x2, 0.0), axis=-1, keepdims=True)
    s_hi = jnp.sum(jnp.where(lo, 0.0, x2), axis=-1, keepdims=True)
    ms = jnp.where(lo, s_lo, s_hi) * (1.0 / HEAD_DIM)
    return x * lax.rsqrt(ms + RMS_EPS) * gain


def _row_tile(refs, lead_tiles):
    if len(refs) == 1:
        return refs[0][...]
    return jnp.where(pl.program_id(0) < lead_tiles, refs[0][...], refs[1][...])


def _ffn_kernel(*refs, ple, h_parts, p_parts, lead_tiles):
    refs = list(refs)
    h_refs = [refs.pop(0) for _ in range(h_parts)]
    if ple:
        g_ref, wi_ref, wo_ref = refs[:3]
        p_refs = refs[3:3 + p_parts]
        gp_ref, wg_ref, wp_ref, o_ref = refs[3 + p_parts:]
    else:
        g_ref, wi_ref, wo_ref, o_ref = refs
    h = _row_tile(h_refs, lead_tiles)
    xn = _rms(h, g_ref[...]).astype(BF16)
    a = _dot(xn, wi_ref[:, :D_FF])
    b = _dot(xn, wi_ref[:, D_FF:])
    act = (a * jax.nn.sigmoid(a) * b).astype(BF16)
    h = h + 0.5 * _dot(act, wo_ref[...])
    if ple:
        xg = _rms(h, gp_ref[...]).astype(BF16)
        gate = jax.nn.sigmoid(_dot(xg, wg_ref[...]))
        h = h + gate * _dot(_row_tile(p_refs, lead_tiles).astype(BF16), wp_ref[...])
    o_ref[...] = h


def _ffn(h, g, wi, wo, n_prompt, n_sample, ple_args=None, p_layer=0):
    h_parts = list(h) if isinstance(h, (tuple, list)) else [h]
    n = n_prompt + n_sample
    tm = _pick_tile(n, 256)
    assert n_prompt % tm == 0 and n_sample % tm == 0
    lead_tiles = n_prompt // tm
    row = lambda w: pl.BlockSpec((tm, w), lambda i: (i, 0))

    def pair_specs(w, off_p, off_s):
        return [pl.BlockSpec((tm, w), lambda i: (off_p + jnp.minimum(i, lead_tiles - 1), 0)),
                pl.BlockSpec((tm, w), lambda i: (off_s + jnp.maximum(i - lead_tiles, 0), 0))]

    in_specs = (pair_specs(D_MODEL, 0, 0) if len(h_parts) == 2 else [row(D_MODEL)]) + [
        _const_spec((1, D_MODEL)), _const_spec(wi.shape), _const_spec(wo.shape)]
    args = h_parts + [g, wi, wo]
    p_parts = []
    if ple_args is not None:
        p_parts, gp, wg, wp = ple_args
        p_parts = list(p_parts)
        in_specs += pair_specs(D_PLE, p_layer * lead_tiles, p_layer * (n_sample // tm)) + [
            _const_spec((1, D_MODEL)), _const_spec(wg.shape), _const_spec(wp.shape)]
        args += p_parts + [gp, wg, wp]
    return pl.pallas_call(
        functools.partial(_ffn_kernel, ple=ple_args is not None, h_parts=len(h_parts), p_parts=len(p_parts),
                          lead_tiles=lead_tiles),
        out_shape=jax.ShapeDtypeStruct((n, D_MODEL), F32),
        grid=(n // tm,),
        in_specs=in_specs,
        out_specs=row(D_MODEL),
        compiler_params=_cparams("parallel"),
    )(*args)


_C_U = 0
_C_Q = _C_U + D_SSM
_C_KV = _C_Q + Q_PAD
_C_GN = _C_KV + 6 * D_KV
_C_GA = _C_GN + LANE
_C_GB = _C_GA + D_MODEL
_C_END = _C_GB + D_MODEL


def _mix_in_kernel(h_ref, g_ref, w_ref, gq_ref, gks_ref, gkw_ref,
                   u_ref, qp_ref, kvc_ref, kvs_ref, kvw_ref, kvsb_ref, kvwb_ref, gn_ref, ga_ref, gb_ref):
    xn = _rms(h_ref[...], g_ref[...]).astype(BF16)
    u_ref[...] = _dot(xn, w_ref[:, _C_U:_C_Q])
    zq = _dot(xn, w_ref[:, _C_Q:_C_KV])
    for h in range(N_HEADS):
        qh = zq[:, h * LANE:(h + 1) * LANE]
        ms = jnp.sum(qh * qh, axis=-1, keepdims=True) * (1.0 / HEAD_DIM)
        qn = qh * lax.rsqrt(ms + RMS_EPS) * gq_ref[:, h * LANE:(h + 1) * LANE]
        qp_ref[:, h * LANE:(h + 1) * LANE] = qn
    zkv = _dot(xn, w_ref[:, _C_KV:_C_GN])
    kvc_ref[...] = zkv[:, 0:2 * D_KV]
    ks = _halfnorm(zkv[:, 2 * D_KV:3 * D_KV], gks_ref[...])
    vs = zkv[:, 3 * D_KV:4 * D_KV]
    kw = _halfnorm(zkv[:, 4 * D_KV:5 * D_KV], gkw_ref[...])
    vw = zkv[:, 5 * D_KV:6 * D_KV]
    kvs_ref[:, 0:D_KV] = ks
    kvs_ref[:, D_KV:] = vs
    kvw_ref[:, 0:D_KV] = kw
    kvw_ref[:, D_KV:] = vw
    kvsb_ref[:, 0:D_KV] = ks.astype(BF16)
    kvsb_ref[:, D_KV:2 * D_KV] = vs.astype(BF16)
    kvsb_ref[:, 2 * D_KV:] = jnp.ones((ks.shape[0], D_KV), BF16)
    kvwb_ref[:, 0:D_KV] = kw.astype(BF16)
    kvwb_ref[:, D_KV:] = vw.astype(BF16)
    gn_ref[...] = jax.nn.sigmoid(_dot(xn, w_ref[:, _C_GN:_C_GA]))
    ga_ref[...] = jax.nn.sigmoid(_dot(xn, w_ref[:, _C_GA:_C_GB]))
    gb_ref[...] = jax.nn.sigmoid(_dot(xn, w_ref[:, _C_GB:_C_END]))


def _mix_in(h, g, w, gq, gks, gkw):
    n = h.shape[0]
    tm = _pick_tile(n, 256)
    row = lambda width: pl.BlockSpec((tm, width), lambda i: (i, 0))
    widths = [(D_SSM, F32), (Q_PAD, F32), (2 * D_KV, F32), (2 * D_KV, F32), (2 * D_KV, F32),
              (3 * D_KV, BF16), (2 * D_KV, BF16), (LANE, F32), (D_MODEL, F32), (D_MODEL, F32)]
    return pl.pallas_call(
        _mix_in_kernel,
        out_shape=[jax.ShapeDtypeStruct((n, wd), dt) for wd, dt in widths],
        grid=(n // tm,),
        in_specs=[row(D_MODEL), _const_spec((1, D_MODEL)), _const_spec(w.shape),
                  _const_spec((1, Q_PAD)), _const_spec((1, LANE)), _const_spec((1, LANE))],
        out_specs=[row(wd) for wd, _ in widths],
        compiler_params=_cparams("parallel"),
    )(h, g, w, gq, gks, gkw)


def _gelu_tanh(x):
    return 0.5 * x * (1.0 + jnp.tanh(0.7978845608028654 * (x + 0.044715 * (x * x * x))))


def _s5_kernel(*refs, seg, carry, seq_steps=None):
    if carry:
        (u_ref, bm_ref, cr_ref, ci_ref, d_ref, lam_ref, pr_ref, pi_ref,
         y_ref, hr_out, hi_out, car_ref, cai_ref) = refs
    else:
        (u_ref, bm_ref, cr_ref, ci_ref, d_ref, lam_ref, pr_ref, pi_ref, h0r_ref, h0i_ref, y_all_ref,
         y_ref, hr_out, hi_out) = refs
    u = u_ref[...]
    rows = u.shape[0]
    x = _dot(u.astype(BF16), bm_ref[...])
    xr = x[:, :N_STATE]
    xi = x[:, N_STATE:]
    d, k = 1, 0
    while d < seg:
        lr = jnp.tile(lam_ref[2 * k * seg:(2 * k + 1) * seg, :], (rows // seg, 1))
        li = jnp.tile(lam_ref[(2 * k + 1) * seg:(2 * k + 2) * seg, :], (rows // seg, 1))
        sr = pltpu.roll(xr, d, 0)
        si = pltpu.roll(xi, d, 0)
        xr, xi = xr + lr * sr - li * si, xi + lr * si + li * sr
        d *= 2
        k += 1
    p_r = pr_ref[...]
    p_i = pi_ref[...]
    if carry:
        @pl.when(pl.program_id(0) == 0)
        def _():
            car_ref[...] = jnp.zeros_like(car_ref)
            cai_ref[...] = jnp.zeros_like(cai_ref)
        c_r = car_ref[...]
        c_i = cai_ref[...]
        hr_parts, hi_parts = [], []
        for j in range(rows // seg):
            tr = xr[j * seg:(j + 1) * seg]
            ti = xi[j * seg:(j + 1) * seg]
            hr_j = tr + p_r * c_r - p_i * c_i
            hi_j = ti + p_r * c_i + p_i * c_r
            c_r = hr_j[seg - 1:seg, :]
            c_i = hi_j[seg - 1:seg, :]
            hr_parts.append(hr_j)
            hi_parts.append(hi_j)
        hr = jnp.concatenate(hr_parts, axis=0)
        hi = jnp.concatenate(hi_parts, axis=0)
    else:
        c_r = h0r_ref[...]
        c_i = h0i_ref[...]
        hr = xr + p_r * c_r - p_i * c_i
        hi = xi + p_r * c_i + p_i * c_r
    y = _dot(hr.astype(BF16), cr_ref[...]) + _dot(hi.astype(BF16), ci_ref[...]) + d_ref[...] * u
    y_ref[...] = _gelu_tanh(y).astype(BF16)
    if carry:
        car_ref[...] = hr[rows - 1:rows, :]
        cai_ref[...] = hi[rows - 1:rows, :]

        @pl.when(pl.program_id(0) < seq_steps)
        def _():
            hr_out[...] = hr[rows - 1:rows, :]
            hi_out[...] = hi[rows - 1:rows, :]
    else:
        hr_out[...] = hr
        hi_out[...] = hi


def _s5_prompt(u, t, sp):
    rows = Q_TILE
    consts = [sp["bmat"], sp["cr"], sp["ci"], sp["d"], sp["lam_p"], sp["pr_p"], sp["pi_p"]]
    return pl.pallas_call(
        functools.partial(_s5_kernel, seg=SCAN_SEG, carry=True, seq_steps=t // rows),
        out_shape=[jax.ShapeDtypeStruct((u.shape[0], D_SSM), BF16),
                   jax.ShapeDtypeStruct((1, N_STATE), F32), jax.ShapeDtypeStruct((1, N_STATE), F32)],
        grid=(u.shape[0] // rows,),
        in_specs=[pl.BlockSpec((rows, D_SSM), lambda i: (i, 0))] + [_const_spec(c.shape) for c in consts],
        out_specs=[pl.BlockSpec((rows, D_SSM), lambda i: (i, 0)),
                   _const_spec((1, N_STATE)), _const_spec((1, N_STATE))],
        scratch_shapes=[pltpu.VMEM((1, N_STATE), F32), pltpu.VMEM((1, N_STATE), F32)],
        compiler_params=_cparams("arbitrary"),
    )(u, *consts)


def _s5_sample(u, yg, t, ns_rows, nq, h0r, h0i, sp):
    consts = [sp["bmat"], sp["cr"], sp["ci"], sp["d"], sp["lam_s"], sp["pr_s"], sp["pi_s"], h0r, h0i]
    blk = t // ns_rows
    return pl.pallas_call(
        functools.partial(_s5_kernel, seg=nq, carry=False),
        out_shape=[jax.ShapeDtypeStruct(yg.shape, BF16),
                   jax.ShapeDtypeStruct((ns_rows, N_STATE), F32), jax.ShapeDtypeStruct((ns_rows, N_STATE), F32)],
        grid=(1,),
        in_specs=([pl.BlockSpec((ns_rows, D_SSM), lambda i: (blk, 0))] + [_const_spec(c.shape) for c in consts]
                  + [pl.BlockSpec(memory_space=pl.ANY)]),
        out_specs=[pl.BlockSpec((ns_rows, D_SSM), lambda i: (blk, 0)),
                   _const_spec((ns_rows, N_STATE)), _const_spec((ns_rows, N_STATE))],
        input_output_aliases={len(consts) + 1: 0},
        compiler_params=_cparams("arbitrary"),
    )(u, *consts, yg)


N_SPAIR = CMP_STRIDE // 2


def _compress_part(get_x2, w_ref, a0, a1, part, rows):
    base = pl.multiple_of(part * rows, rows)
    for kv in range(2):
        acc = None
        for sp in range(N_SPAIR):
            term = _dot(get_x2(kv, sp), w_ref[kv, sp])
            acc = term if acc is None else acc + term
        a0[pl.ds(base, rows), kv * D_KV:(kv + 1) * D_KV] = acc[:, 0:D_KV]
        a1[pl.ds(base, rows), kv * D_KV:(kv + 1) * D_KV] = acc[:, D_KV:]


def _compress_finish(a0, a1, w_ref, pe_ref, gk_ref, kcc_ref, vcc_ref, nb):
    bias = []
    for j in range(CMP_BLOCK // CMP_STRIDE):
        halves = []
        for kv in range(2):
            b = jnp.zeros((8, 2 * D_KV), F32)
            for sp in range(N_SPAIR):
                r = (j * 2 + kv) * N_SPAIR + sp
                b = b + _dot(jnp.broadcast_to(pe_ref[r:r + 1, :], (8, 2 * D_KV)).astype(BF16), w_ref[kv, sp])
            halves.append(b[0:1, j * D_KV:(j + 1) * D_KV])
        bias.append(jnp.concatenate(halves, axis=1))
    out = (a0[...] + bias[0]) + pltpu.roll(a1[...] + bias[1], nb - 1, 0)
    valid = lax.broadcasted_iota(jnp.int32, (nb, 1), 0) < nb - 1
    out = jnp.where(valid, out, 0.0)
    kcc_ref[0] = _halfnorm(out[:, 0:D_KV], gk_ref[...]).astype(BF16)
    vcc_ref[0] = out[:, D_KV:].astype(BF16)


def _compress_rows_kernel(xk_ref, xv_ref, w_ref, pe_ref, gk_ref, kcc_ref, vcc_ref, a0, a1, *, nsplit, nb):
    part = pl.program_id(0)
    rows = nb // nsplit

    def get_x2(kv, sp):
        x_ref = (xk_ref, xv_ref)[kv]
        pair = [x_ref[pl.ds(2 * sp + si, rows, stride=CMP_STRIDE), :] for si in range(2)]
        return jnp.concatenate(pair, axis=1).astype(BF16)
    _compress_part(get_x2, w_ref, a0, a1, part, rows)

    @pl.when(part == nsplit - 1)
    def _():
        _compress_finish(a0, a1, w_ref, pe_ref, gk_ref, kcc_ref, vcc_ref, nb)


def _compress_rows(kv_cmp, t, cw):
    nb = t // CMP_STRIDE
    nsplit = 2
    return pl.pallas_call(
        functools.partial(_compress_rows_kernel, nsplit=nsplit, nb=nb),
        out_shape=[jax.ShapeDtypeStruct((1, nb, D_KV), BF16), jax.ShapeDtypeStruct((1, nb, D_KV), BF16)],
        grid=(nsplit,),
        in_specs=[pl.BlockSpec((t // nsplit, D_KV), lambda i: (i, 0)),
                  pl.BlockSpec((t // nsplit, D_KV), lambda i: (i, 1)),
                  _const_spec(cw["w"].shape), _const_spec(cw["pe"].shape), _const_spec((1, LANE))],
        out_specs=[_const_spec((1, nb, D_KV)), _const_spec((1, nb, D_KV))],
        scratch_shapes=[pltpu.VMEM((nb, 2 * D_KV), F32), pltpu.VMEM((nb, 2 * D_KV), F32)],
        compiler_params=_cparams("arbitrary"),
    )(kv_cmp, kv_cmp, cw["w"], cw["pe"], cw["gk"])


def _compress_kernel(pt_ref, pages_ref, w_ref, pe_ref, gk_ref, perm_ref,
                     kcc_ref, vcc_ref, buf, sem, x_scr, a0, a1, *, pps, nsplit, nb):
    s = pl.program_id(0)
    nsteps = pl.num_programs(0)

    def page_copy(step, p, slot):
        b = step // nsplit
        part = step % nsplit
        pg = pt_ref[b, part * pps + p]
        col = pl.multiple_of((p % 2) * PAGE_SIZE, PAGE_SIZE)
        return pltpu.make_async_copy(pages_ref.at[pg], buf.at[slot, p // 2, :, pl.ds(col, PAGE_SIZE)],
                                     sem.at[slot])

    def start(step, slot):
        def body(p, c):
            page_copy(step, p, slot).start()
            return c
        lax.fori_loop(0, pps, body, 0)

    def wait(step, slot):
        def body(p, c):
            page_copy(step, p, slot).wait()
            return c
        lax.fori_loop(0, pps, body, 0)

    @pl.when(s == 0)
    def _():
        start(s, 0)

    @pl.when(s + 1 < nsteps)
    def _():
        start(s + 1, (s + 1) % 2)

    slot = s % 2
    wait(s, slot)
    part = s % nsplit

    def to_rows(q, c):
        xt = buf[slot, q].astype(BF16)
        xp = _dot_nt(perm_ref[...], xt).astype(BF16)
        base = pl.multiple_of(q * CMP_STRIDE, CMP_STRIDE)
        for s2 in range(CMP_STRIDE):
            for kv in range(2):
                x_scr[kv, s2 // 2, pl.ds(base, CMP_STRIDE), (s2 % 2) * D_KV:(s2 % 2 + 1) * D_KV] = (
                    xp[s2 * CMP_STRIDE:(s2 + 1) * CMP_STRIDE, kv * D_KV:(kv + 1) * D_KV])
        return c
    lax.fori_loop(0, pps // 2, to_rows, 0, unroll=True)
    _compress_part(lambda kv, sp: x_scr[kv, sp], w_ref, a0, a1, part, nb // nsplit)

    @pl.when(part == nsplit - 1)
    def _():
        _compress_finish(a0, a1, w_ref, pe_ref, gk_ref, kcc_ref, vcc_ref, nb)


def _compress_paged(page_table, pages, cw):
    bsz, n_pages = page_table.shape
    nb = n_pages * (PAGE_SIZE // CMP_STRIDE)
    nsplit = 2
    pps = n_pages // nsplit
    const = lambda shape: pl.BlockSpec(shape, lambda s, pt: (0,) * len(shape), pipeline_mode=pl.Buffered(1))
    grid_spec = pltpu.PrefetchScalarGridSpec(
        num_scalar_prefetch=1,
        grid=(bsz * nsplit,),
        in_specs=[pl.BlockSpec(memory_space=pl.ANY), const(cw["w"].shape), const(cw["pe"].shape),
                  const((1, LANE)), const(cw["perm"].shape)],
        out_specs=[pl.BlockSpec((1, nb, D_KV), lambda s, pt: (s // nsplit, 0, 0)),
                   pl.BlockSpec((1, nb, D_KV), lambda s, pt: (s // nsplit, 0, 0))],
        scratch_shapes=[pltpu.VMEM((2, pps // 2, 2 * D_KV, 2 * PAGE_SIZE), F32), pltpu.SemaphoreType.DMA((2,)),
                        pltpu.VMEM((2, N_SPAIR, nb // nsplit, 2 * D_KV), BF16),
                        pltpu.VMEM((nb, 2 * D_KV), F32), pltpu.VMEM((nb, 2 * D_KV), F32)],
    )
    return pl.pallas_call(
        functools.partial(_compress_kernel, pps=pps, nsplit=nsplit, nb=nb),
        out_shape=[jax.ShapeDtypeStruct((bsz, nb, D_KV), BF16), jax.ShapeDtypeStruct((bsz, nb, D_KV), BF16)],
        grid_spec=grid_spec,
        compiler_params=_cparams("arbitrary"),
    )(page_table, pages, cw["w"], cw["pe"], cw["gk"], cw["perm"])


def _heads_rows(qp):
    return jnp.concatenate([qp[:, h * LANE:(h + 1) * LANE] for h in range(N_HEADS)], axis=0).astype(BF16)


def _slope2(h):
    return LOG2E * 2.0 ** (-8.0 * (h + 1) / N_HEADS)


def _branch(q_all, tq, pieces):
    s_all = [_dot(q_all, k) if fm else _dot_nt(q_all, k) for k, _, _, _, fm in pieces]
    es, invs = [], []
    for h in range(N_HEADS):
        rows = slice(h * tq, (h + 1) * tq)
        sm = [jnp.where(pc[3], s[rows] - _slope2(h) * pc[2], NEG) for s, pc in zip(s_all, pieces)]
        m = functools.reduce(jnp.maximum, [jnp.max(x, axis=-1, keepdims=True) for x in sm])
        m = jnp.maximum(m, HALF_NEG)
        e = [jnp.exp2(x - m) for x in sm]
        l = functools.reduce(lambda a, b: a + b, [jnp.sum(x, axis=-1, keepdims=True) for x in e])
        es.append(e)
        invs.append(1.0 / jnp.maximum(l, 1e-30))
    o = None
    for i, (_, v, _, _, fm) in enumerate(pieces):
        p = jnp.concatenate([es[h][i] for h in range(N_HEADS)], axis=0).astype(BF16)
        t = _dot_nt(p, v) if fm else _dot(p, v)
        o = t if o is None else o + t
    return o * jnp.concatenate(invs, axis=0), es, invs


def _cmp_win_topk(q_all, tq, qpos, gn, kcc, vcc, mband, win_pieces, nb, ns, topn):
    n_io = lax.broadcasted_iota(jnp.int32, (1, kcc.shape[0]), 1)
    kc_end = n_io * CMP_STRIDE + (CMP_BLOCK - 1)
    mask_c = (qpos >= kc_end) & (n_io < nb - 1)
    kdist_c = (qpos[0:1, :] - kc_end).astype(F32)
    o_c, es, invs = _branch(q_all, tq, [(kcc, vcc, kdist_c, mask_c, False)])
    o_w, _, _ = _branch(q_all, tq, win_pieces)
    imps = []
    for g in range(N_KV):
        psum = es[HPG * g][0] * invs[HPG * g]
        for hh in range(1, HPG):
            psum = psum + es[HPG * g + hh][0] * invs[HPG * g + hh]
        hi = psum.astype(BF16)
        r = psum - hi.astype(F32)
        mid = r.astype(BF16)
        lo = (r - mid.astype(F32)).astype(BF16)
        imps.append(_dot(hi, mband) + _dot(mid, mband) + _dot(lo, mband))
    imp = jnp.concatenate(imps, axis=0)
    blk = lax.broadcasted_iota(jnp.int32, (1, ns), 1)
    cur = jnp.concatenate([jnp.right_shift(qpos, 6)] * N_KV, axis=0)
    forced = (blk == 0) | (blk == cur) | (blk == cur - 1)
    score = jnp.where(forced, imp + SEL_BIG, jnp.where(blk <= cur, imp, -SEL_BIG))
    n_rows = score.shape[0]
    pad = (-n_rows) % LANE
    if pad:
        score = jnp.concatenate([score, jnp.zeros((pad, ns), F32)], axis=0)
    score = score.T
    blkf = lax.broadcasted_iota(jnp.int32, (ns, 1), 0).astype(F32)
    for _ in range(topn):
        m = jnp.max(score, axis=0, keepdims=True)
        first = jnp.min(jnp.where(score == m, blkf, float(ns)), axis=0, keepdims=True)
        score = jnp.where(blkf == first, -jnp.inf, score)
    sel = jnp.where(score == -jnp.inf, 1.0, 0.0).T[:n_rows]
    sel = jnp.where(blk <= cur, sel, 0.0)
    outs = []
    for h in range(N_HEADS):
        rows = slice(h * tq, (h + 1) * tq)
        outs.append(gn[:, h:h + 1] * o_c[rows] + gn[:, 2 * N_HEADS + h:2 * N_HEADS + h + 1] * o_w[rows])
    return jnp.concatenate(outs, axis=1), sel


def _e1_prompt_kernel(qp_ref, gn_ref, kcc_ref, vcc_ref, mband_ref, kvw_ref, ocw_ref, sel_ref, flag_ref,
                      *, nb, ns, widths):
    tq = Q_TILE
    s0 = pl.program_id(0) * tq
    qpos = s0 + lax.broadcasted_iota(jnp.int32, (tq, 1), 0)

    def run(width):
        q_all = _heads_rows(qp_ref[...])
        start = pl.multiple_of(jnp.maximum(s0 - WINDOW, 0), Q_TILE)
        kw = kvw_ref[pl.ds(start, WIN_KEYS), 0:D_KV]
        vw = kvw_ref[pl.ds(start, WIN_KEYS), D_KV:]
        kpos = start + lax.broadcasted_iota(jnp.int32, (1, WIN_KEYS), 1)
        dist_w = qpos - kpos
        mask_w = (dist_w >= 0) & (dist_w < WINDOW)
        win_pieces = [(kw, vw, (s0 - kpos).astype(F32), mask_w, False)]
        ocw, sel = _cmp_win_topk(q_all, tq, qpos, gn_ref[...], kcc_ref[0, 0:width, :], vcc_ref[0, 0:width, :],
                                 mband_ref[0:width, :], win_pieces, nb, ns, TOP_N)
        ocw_ref[...] = ocw
        sel_ref[0] = sel[:tq]
        sel_ref[1] = sel[tq:]
        colany = jnp.max(sel, axis=0, keepdims=True)
        j_io = lax.broadcasted_iota(jnp.int32, (ns, LANE), 0)
        c_io = lax.broadcasted_iota(jnp.int32, (ns, LANE), 1)
        grp = jnp.where(jnp.right_shift(j_io, 2) == c_io, 1.0, 0.0).astype(BF16)
        cnt = _dot(jnp.broadcast_to(colany, (8, ns)).astype(BF16), grp)
        flag_ref[0] = (cnt > 0.5).astype(jnp.int32)

    nvar = len(widths)
    variant = jnp.minimum(pl.program_id(0) // (pl.num_programs(0) // nvar), nvar - 1)
    for vi, width in enumerate(widths):
        pl.when(variant == vi)(functools.partial(run, width))


def _e1_prompt(qp, gn, kcc, vcc, mband, kvw_bf, t):
    nb, ns = mband.shape
    nt = t // Q_TILE
    nvar = max(v for v in (1, 2, 4) if nb % (v * LANE) == 0 and nt % v == 0)
    widths = tuple(nb * (v + 1) // nvar for v in range(nvar))
    row = lambda w: pl.BlockSpec((Q_TILE, w), lambda i: (i, 0))
    return pl.pallas_call(
        functools.partial(_e1_prompt_kernel, nb=nb, ns=ns, widths=widths),
        out_shape=[jax.ShapeDtypeStruct((t, Q_PAD), F32), jax.ShapeDtypeStruct((N_KV, t, ns), F32),
                   jax.ShapeDtypeStruct((nt, 8, LANE), jnp.int32)],
        grid=(nt,),
        in_specs=[row(Q_PAD), row(LANE), _const_spec((1, nb, D_KV)), _const_spec((1, nb, D_KV)),
                  _const_spec(mband.shape), _const_spec(kvw_bf.shape)],
        out_specs=[row(Q_PAD), pl.BlockSpec((N_KV, Q_TILE, ns), lambda i: (0, i, 0)),
                   pl.BlockSpec((1, 8, LANE), lambda i: (i, 0, 0))],
        compiler_params=_cparams("parallel"),
    )(qp, gn, kcc, vcc, mband, kvw_bf)


def _pad_rows(x, rows):
    return jnp.concatenate([x, jnp.zeros((rows - x.shape[0], x.shape[1]), x.dtype)], axis=0)


def _e1_sample_kernel(qp_ref, gn_ref, kcc_ref, vcc_ref, mband_ref, cwin_ref, nwin_ref, ocw_ref, sel_ref,
                      *, nb, ns, nq, past):
    qi = lax.broadcasted_iota(jnp.int32, (nq, 1), 0)
    qpos = past + qi
    q_all = _heads_rows(qp_ref[...])
    cw = cwin_ref[0]
    w_buf = cw.shape[1]
    j_c = lax.broadcasted_iota(jnp.int32, (1, w_buf), 1)
    dist_cw = qi + (w_buf - j_c)
    mask_cw = (dist_cw >= 0) & (dist_cw < WINDOW)
    nw = _pad_rows(nwin_ref[...], LANE)
    j_n = lax.broadcasted_iota(jnp.int32, (1, LANE), 1)
    mask_nw = (qi >= j_n) & (j_n < nq)
    pieces = [(cw[0:D_KV, :].astype(BF16), cw[D_KV:, :].astype(BF16), (w_buf - j_c).astype(F32), mask_cw, True),
              (nw[:, 0:D_KV].astype(BF16), nw[:, D_KV:].astype(BF16), (-j_n).astype(F32), mask_nw, False)]
    ocw, sel = _cmp_win_topk(q_all, nq, qpos, gn_ref[...], kcc_ref[0], vcc_ref[0], mband_ref[...],
                             pieces, nb, ns, TOP_N - 1)
    ocw_ref[...] = ocw
    sel_ref[0, 0] = sel[:nq]
    sel_ref[0, 1] = sel[nq:]


def _e1_sample(qp, gn, kcc, vcc, mband, cache_win, win_off, kv_win, t, bsz, nq, past):
    nb, ns = mband.shape
    w_buf = cache_win.shape[2]
    off = t // nq
    row = lambda w: pl.BlockSpec((nq, w), lambda b: (off + b, 0))
    per_b = lambda shape: pl.BlockSpec((1,) + shape, lambda b: (b,) + (0,) * len(shape))
    return pl.pallas_call(
        functools.partial(_e1_sample_kernel, nb=nb, ns=ns, nq=nq, past=past),
        out_shape=[jax.ShapeDtypeStruct((bsz * nq, Q_PAD), F32), jax.ShapeDtypeStruct((bsz, N_KV, nq, ns), F32)],
        grid=(bsz,),
        in_specs=[row(Q_PAD), row(LANE), per_b((nb, D_KV)), per_b((nb, D_KV)), _const_spec(mband.shape),
                  pl.BlockSpec((1, 2 * D_KV, w_buf), lambda b: (win_off + b, 0, 0)), row(2 * D_KV)],
        out_specs=[pl.BlockSpec((nq, Q_PAD), lambda b: (b, 0)), per_b((N_KV, nq, ns))],
        compiler_params=_cparams("parallel"),
    )(qp, gn, kcc, vcc, mband, cache_win, kv_win)


def _online_step(q_all, k, kdist, masks, v_ones, m_ref, acc_ref, tq, feature_major=False, split_groups=False):
    reps = kdist.shape[1] // LANE
    per_chain = 1 if split_groups else N_HEADS
    chains = [(a, a + per_chain) for a in range(0, N_HEADS, per_chain)]
    for h_lo, h_hi in chains:
        grows = slice(h_lo * tq, h_hi * tq)
        s = _dot(q_all[grows], k) if feature_major else _dot_nt(q_all[grows], k)
        ps, alphas = [], []
        for h in range(h_lo, h_hi):
            hh = h - h_lo
            rows = slice(h * tq, (h + 1) * tq)
            sm = jnp.where(masks[h // HPG], s[hh * tq:(hh + 1) * tq] - _slope2(h) * kdist, NEG)
            m_old = m_ref[rows, :]
            m_new = jnp.maximum(m_old, jnp.max(sm, axis=-1, keepdims=True))
            alphas.append(jnp.exp2(m_old - m_new))
            ps.append(jnp.exp2(sm - jnp.tile(m_new, (1, reps))))
            m_ref[rows, :] = m_new
        p_g = jnp.concatenate(ps, axis=0).astype(BF16)
        a_g = jnp.concatenate(alphas, axis=0)
        pv = _dot_nt(p_g, v_ones) if feature_major else _dot(p_g, v_ones)
        acc_ref[grows, :] = jnp.tile(a_g, (1, 2)) * acc_ref[grows, :] + pv


def _block_to_keys(ns, first_blk, n_keys):
    j_io = lax.broadcasted_iota(jnp.int32, (ns, n_keys), 0)
    kb = first_blk + jnp.right_shift(lax.broadcasted_iota(jnp.int32, (ns, n_keys), 1), 6)
    return jnp.where(j_io == kb, 1.0, 0.0).astype(BF16)


def _init_online(m_ref, acc_ref):
    m_ref[...] = jnp.full(m_ref.shape, HALF_NEG, F32)
    acc_ref[...] = jnp.zeros(acc_ref.shape, F32)


def _finish_online(ocw, gn, acc_ref, tq):
    o_s = acc_ref[:, 0:LANE] * (1.0 / jnp.maximum(acc_ref[:, LANE:], 1e-30))
    outs = []
    for h in range(N_HEADS):
        g1 = gn[:, N_HEADS + h:N_HEADS + h + 1]
        outs.append(ocw[:, h * LANE:(h + 1) * LANE] + g1 * o_s[h * tq:(h + 1) * tq])
    return jnp.concatenate(outs, axis=1)


def _e2_prompt_kernel(order_ref, cnt_ref, qp_ref, sel_ref, gn_ref, ocw_ref, kvs_ref, out_ref, m_ref, acc_ref,
                      *, nchunks, ns, nt):
    tq = Q_TILE
    i = pl.program_id(0)
    s0 = i * tq
    qpos = s0 + lax.broadcasted_iota(jnp.int32, (tq, 1), 0)
    q_all = _heads_rows(qp_ref[...])
    _init_online(m_ref, acc_ref)
    n_act = jnp.where(i < nt, cnt_ref[jnp.minimum(i, nt - 1)], 0)
    blocks_per_chunk = SEL_CHUNK // SEL_BLOCK
    sel_bf = [sel_ref[g].astype(BF16) for g in range(N_KV)]

    def body(j, carry):
        c1 = order_ref[i * nchunks + 2 * j]
        paired = 2 * j + 1 < n_act
        c2 = jnp.where(paired, order_ref[i * nchunks + jnp.minimum(2 * j + 1, nchunks - 1)], c1)
        ks, vs, dists, masks = [], [], [], [[] for _ in range(N_KV)]
        for c, first_blk in ((c1, c1 * blocks_per_chunk), (c2, jnp.where(paired, c2 * blocks_per_chunk, ns))):
            base = pl.multiple_of(c * SEL_CHUNK, SEL_CHUNK)
            ks.append(kvs_ref[pl.ds(base, SEL_CHUNK), 0:D_KV])
            vs.append(kvs_ref[pl.ds(base, SEL_CHUNK), D_KV:])
            kpos = base + lax.broadcasted_iota(jnp.int32, (1, SEL_CHUNK), 1)
            dists.append((s0 - kpos).astype(F32))
            expand = _block_to_keys(ns, first_blk, SEL_CHUNK)
            causal = qpos >= kpos
            for g in range(N_KV):
                masks[g].append((_dot(sel_bf[g], expand) > 0.5) & causal)
        _online_step(q_all, jnp.concatenate(ks, axis=0), jnp.concatenate(dists, axis=1),
                     [jnp.concatenate(mg, axis=1) for mg in masks], jnp.concatenate(vs, axis=0), m_ref, acc_ref, tq,
                     split_groups=True)
        return carry

    lax.fori_loop(0, (n_act + 1) // 2, body, 0)
    out_ref[...] = _finish_online(ocw_ref[...], gn_ref[...], acc_ref, tq)


def _e2_prompt(order, cnt, qp, sel, gn, ocw, kvs_bf, t):
    ns = sel.shape[2]
    nt = t // Q_TILE
    nchunks = order.shape[0] // nt
    row = lambda w: pl.BlockSpec((Q_TILE, w), lambda i, o, c: (i, 0))
    prow = lambda w: pl.BlockSpec((Q_TILE, w), lambda i, o, c: (jnp.minimum(i, nt - 1), 0))
    grid_spec = pltpu.PrefetchScalarGridSpec(
        num_scalar_prefetch=2,
        grid=(qp.shape[0] // Q_TILE,),
        in_specs=[row(Q_PAD), pl.BlockSpec((N_KV, Q_TILE, ns), lambda i, o, c: (0, jnp.minimum(i, nt - 1), 0)),
                  row(LANE), prow(Q_PAD), pl.BlockSpec(kvs_bf.shape, lambda i, o, c: (0, 0))],
        out_specs=row(Q_PAD),
        scratch_shapes=[pltpu.VMEM((N_HEADS * Q_TILE, LANE), F32), pltpu.VMEM((N_HEADS * Q_TILE, 2 * LANE), F32)],
    )
    return pl.pallas_call(
        functools.partial(_e2_prompt_kernel, nchunks=nchunks, ns=ns, nt=nt),
        out_shape=jax.ShapeDtypeStruct((qp.shape[0], Q_PAD), F32),
        grid_spec=grid_spec,
        compiler_params=_cparams("arbitrary"),
    )(order, cnt, qp, sel, gn, ocw, kvs_bf)


def _e2_sample_kernel(pt_ref, live_ref, pages_ref, qp_ref, sel_ref, gn_ref, ocw_ref, nsel_ref, expand_ref,
                      oatt_all_ref, out_ref, buf, sem, m_ref, acc_ref, *, pps, nsplit, nq):
    s = pl.program_id(0)
    nsteps = pl.num_programs(0)
    n_keys = pps * PAGE_SIZE

    def page_copy(step, p, slot):
        b = step // nsplit
        part = step % nsplit
        pg = pt_ref[b, part * pps + p]
        col = pl.multiple_of(p * PAGE_SIZE, PAGE_SIZE)
        return pltpu.make_async_copy(pages_ref.at[pg], buf.at[slot, :, pl.ds(col, PAGE_SIZE)], sem.at[slot])

    def start(step, slot):
        def body(p, c):
            page_copy(step, p, slot).start()
            return c
        lax.fori_loop(0, pps, body, 0)

    def wait(step, slot):
        def body(p, c):
            page_copy(step, p, slot).wait()
            return c
        lax.fori_loop(0, pps, body, 0)

    @pl.when((s == 0) & (live_ref[0] > 0))
    def _():
        start(s, 0)

    @pl.when((s + 1 < nsteps) & (live_ref[jnp.minimum(s + 1, nsteps - 1)] > 0))
    def _():
        start(s + 1, (s + 1) % 2)

    slot = s % 2
    part = s % nsplit
    qi = lax.broadcasted_iota(jnp.int32, (nq, 1), 0)
    q_all = _heads_rows(qp_ref[...])

    @pl.when(part == 0)
    def _():
        _init_online(m_ref, acc_ref)

    @pl.when(live_ref[s] > 0)
    def _():
        wait(s, slot)
        k = buf[slot, 0:D_KV, :].astype(BF16)
        v_ones = jnp.concatenate([buf[slot, D_KV:, :].astype(BF16), jnp.ones((D_KV, n_keys), BF16)], axis=0)
        back = (nsplit - part) * n_keys - lax.broadcasted_iota(jnp.int32, (1, n_keys), 1)
        masks = [_dot(sel_ref[0, g, 0].astype(BF16), expand_ref[...]) > 0.5 for g in range(N_KV)]
        _online_step(q_all, k, back.astype(F32), masks, v_ones, m_ref, acc_ref, nq, feature_major=True)

    @pl.when(part == nsplit - 1)
    def _():
        nw = _pad_rows(nsel_ref[...], LANE)
        j_n = lax.broadcasted_iota(jnp.int32, (1, LANE), 1)
        mask_n = (qi >= j_n) & (j_n < nq)
        vn_ones = jnp.concatenate([nw[:, D_KV:].astype(BF16), jnp.ones((LANE, D_KV), BF16)], axis=1)
        _online_step(q_all, nw[:, 0:D_KV].astype(BF16), (-j_n).astype(F32), [mask_n, mask_n], vn_ones,
                     m_ref, acc_ref, nq)
        out_ref[...] = _finish_online(ocw_ref[...], gn_ref[...], acc_ref, nq)


def _e2_sample(page_table, pages, qp, sel, gn, ocw, kv_sel, oatt, t, nq):
    bsz, n_pages = page_table.shape
    ns = sel.shape[3]
    nsplit = 8
    pps = n_pages // nsplit
    n_keys = pps * PAGE_SIZE
    nblk = ns // nsplit
    off = t // nq
    sel_parts = sel.reshape(bsz, N_KV, nq, nsplit, nblk).transpose(0, 1, 3, 2, 4)
    live = (jnp.max(sel_parts, axis=(1, 3, 4)) > 0).astype(jnp.int32).reshape(-1)
    expand = (jnp.arange(nblk)[:, None] == jnp.arange(n_keys)[None, :] // SEL_BLOCK).astype(BF16)
    row = lambda w: pl.BlockSpec((nq, w), lambda s, pt, lv: (off + s // nsplit, 0))
    grid_spec = pltpu.PrefetchScalarGridSpec(
        num_scalar_prefetch=2,
        grid=(bsz * nsplit,),
        in_specs=[pl.BlockSpec(memory_space=pl.ANY), row(Q_PAD),
                  pl.BlockSpec((1, N_KV, 1, nq, nblk), lambda s, pt, lv: (s // nsplit, 0, s % nsplit, 0, 0)),
                  row(LANE), pl.BlockSpec((nq, Q_PAD), lambda s, pt, lv: (s // nsplit, 0)), row(2 * D_KV),
                  pl.BlockSpec((nblk, n_keys), lambda s, pt, lv: (0, 0), pipeline_mode=pl.Buffered(1)),
                  pl.BlockSpec(memory_space=pl.ANY)],
        out_specs=row(Q_PAD),
        scratch_shapes=[pltpu.VMEM((2, 2 * D_KV, n_keys), F32), pltpu.SemaphoreType.DMA((2,)),
                        pltpu.VMEM((N_HEADS * nq, LANE), F32), pltpu.VMEM((N_HEADS * nq, 2 * LANE), F32)],
    )
    return pl.pallas_call(
        functools.partial(_e2_sample_kernel, pps=pps, nsplit=nsplit, nq=nq),
        out_shape=jax.ShapeDtypeStruct(oatt.shape, F32),
        grid_spec=grid_spec,
        input_output_aliases={9: 0},
        compiler_params=_cparams("arbitrary"),
    )(page_table, live, pages, qp, sel_parts, gn, ocw, kv_sel, expand, oatt)


def _post_kernel(h_ref, yg_ref, oatt_ref, ga_ref, gb_ref, wglu_ref, watt_ref, wout_ref, o_ref):
    gl = _dot(yg_ref[...], wglu_ref[...])
    br_a = gl[:, :D_MODEL] * jax.nn.sigmoid(gl[:, D_MODEL:])
    br_b = _dot(oatt_ref[...].astype(BF16), watt_ref[...])
    merged = (ga_ref[...] * br_a + gb_ref[...] * br_b).astype(BF16)
    o_ref[...] = h_ref[...] + _dot(merged, wout_ref[...])


def _post(h, yg, oatt, ga, gb, wglu, watt, wout):
    n = h.shape[0]
    tm = _pick_tile(n, 256)
    row = lambda w: pl.BlockSpec((tm, w), lambda i: (i, 0))
    return pl.pallas_call(
        _post_kernel,
        out_shape=jax.ShapeDtypeStruct((n, D_MODEL), F32),
        grid=(n // tm,),
        in_specs=[row(D_MODEL), row(D_SSM), row(Q_PAD), row(D_MODEL), row(D_MODEL),
                  _const_spec(wglu.shape), _const_spec(watt.shape), _const_spec(wout.shape)],
        out_specs=row(D_MODEL),
        compiler_params=_cparams("parallel"),
    )(h, yg, oatt, ga, gb, wglu, watt, wout)


def _head_pad_index():
    h = jnp.arange(N_HEADS)[:, None]
    d = jnp.arange(HEAD_DIM)[None, :]
    return (LANE * h + HEAD_DIM * (h // HPG) + d).reshape(-1)


def _prep_mix_weights(w_in, qk_norm):
    idx = _head_pad_index()
    wq = jnp.zeros((D_MODEL, Q_PAD), F32).at[:, idx].set(w_in[:, D_SSM:D_SSM + N_HEADS * HEAD_DIM])
    c0 = D_SSM + N_HEADS * HEAD_DIM
    c1 = c0 + 6 * D_KV
    c2 = c1 + 3 * N_HEADS
    wgn = jnp.zeros((D_MODEL, LANE), F32).at[:, :3 * N_HEADS].set(w_in[:, c1:c2])
    w = jnp.concatenate([w_in[:, :D_SSM], wq, w_in[:, c0:c1], wgn, w_in[:, c2:]], axis=1).astype(BF16)
    gq = jnp.zeros((Q_PAD,), F32).at[idx].set(jnp.tile(qk_norm[0] * (HEAD_DIM ** -0.5 * LOG2E), N_HEADS))[None]
    gks = jnp.tile(qk_norm[2], N_KV)[None]
    gkw = jnp.tile(qk_norm[3], N_KV)[None]
    return w, gq, gks, gkw


def _prep_att_out(w_att_out):
    return jnp.zeros((Q_PAD, D_MODEL), F32).at[_head_pad_index()].set(w_att_out).astype(BF16)


def _prep_compress(pe_k, pe_v, w_k, w_v, gain_k):
    nj = CMP_BLOCK // CMP_STRIDE
    eye = jnp.eye(N_KV, dtype=F32)

    def pair_maps(w):
        w5 = w.reshape(nj, N_SPAIR, 2, HEAD_DIM, HEAD_DIM)
        return jnp.einsum("jpsde,gh->psgdjhe", w5, eye).reshape(N_SPAIR, 2 * D_KV, 2 * D_KV)

    def pair_pe(pe):
        p5 = pe.reshape(nj, N_SPAIR, 2, 1, HEAD_DIM)
        return jnp.broadcast_to(p5, (nj, N_SPAIR, 2, N_KV, HEAD_DIM)).reshape(nj, N_SPAIR, 2 * D_KV)

    w = jnp.stack([pair_maps(w_k), pair_maps(w_v)], axis=0)
    pe = jnp.stack([pair_pe(pe_k), pair_pe(pe_v)], axis=1).reshape(-1, 2 * D_KV)
    r = jnp.arange(2 * PAGE_SIZE)
    perm = (r[None, :] == (CMP_STRIDE * (r % CMP_STRIDE) + r // CMP_STRIDE)[:, None]).astype(BF16)
    return {"w": w.astype(BF16), "pe": pe,
            "gk": jnp.tile(gain_k, N_KV)[None], "perm": perm}


def _prep_s5(a_re, a_im, log_dt, b_re, b_im, c_re, c_im, d, nq):
    dt = jnp.exp(log_dt)[:, None]
    mag = jnp.exp(a_re * dt)
    lr = mag * jnp.cos(a_im * dt)
    li = mag * jnp.sin(a_im * dt)
    den = a_re * a_re + a_im * a_im
    fr = ((lr - 1.0) * a_re + li * a_im) / den
    fi = (li * a_re - (lr - 1.0) * a_im) / den
    bbr = fr[..., None] * b_re - fi[..., None] * b_im
    bbi = fr[..., None] * b_im + fi[..., None] * b_re
    eye = jnp.eye(N_SSM_GROUPS, dtype=F32)
    blk_b = lambda m: jnp.einsum("gpc,gh->gchp", m, eye).reshape(D_SSM, N_STATE)
    blk_c = lambda m: jnp.einsum("gcp,gh->gphc", m, eye).reshape(N_STATE, D_SSM)

    def lam_pow(k):
        kk = k.astype(F32)[:, None, None]
        m = jnp.exp(a_re * dt * kk)
        th = a_im * dt * kk
        return (m * jnp.cos(th)).reshape(-1, N_STATE), (m * jnp.sin(th)).reshape(-1, N_STATE)

    def step_table(seg):
        ks = []
        dd = 1
        while dd < seg:
            ks.append(dd)
            dd *= 2
        re, im = lam_pow(jnp.array(ks))
        live = jnp.arange(seg)[None, :, None] >= jnp.array(ks)[:, None, None]
        tab = jnp.stack([jnp.where(live, re[:, None, :], 0.0), jnp.where(live, im[:, None, :], 0.0)], axis=1)
        return tab.reshape(-1, N_STATE)

    sp = {"bmat": jnp.concatenate([blk_b(bbr), blk_b(bbi)], axis=1).astype(BF16),
          "cr": blk_c(c_re).astype(BF16), "ci": (-blk_c(c_im)).astype(BF16), "d": d[None],
          "lam_p": step_table(SCAN_SEG), "lam_s": step_table(nq)}
    sp["pr_p"], sp["pi_p"] = lam_pow(jnp.arange(SCAN_SEG) + 1)
    return sp, lam_pow


def _band_matrix(nb, ns):
    ratio = SEL_BLOCK // CMP_STRIDE
    lo = CMP_BLOCK // CMP_STRIDE - 1
    c = jnp.arange(nb)[:, None]
    j = jnp.arange(ns)[None, :]
    return ((c >= ratio * j - lo) & (c <= ratio * j + ratio - 1)).astype(BF16)


def kernel(x_prompt, x_sample, cache_kv_cmp, cache_kv_sel, cache_kv_win, state_ssm_re, state_ssm_im, page_table, p_prompt, p_sample, norm_ffn1, w_ffn1_in, w_ffn1_out, norm_mix, w_in, qk_norm, ssm_a_re, ssm_a_im, ssm_log_dt, ssm_b_re, ssm_b_im, ssm_c_re, ssm_c_im, ssm_d, w_glu, cmp_pe_k, cmp_pe_v, cmp_w_k, cmp_w_v, w_att_out, w_out, norm_ffn2, w_ffn2_in, w_ffn2_out, norm_ple, w_ple_gate, w_ple_proj):
    bp, t = x_prompt.shape[:2]
    bsz, nq = x_sample.shape[:2]
    n_pages = page_table.shape[1]
    past = n_pages * PAGE_SIZE
    n_pool = cache_kv_cmp.shape[1]
    w_buf = cache_kv_win.shape[2]
    ns_rows = bsz * nq
    assert bp == 1 and t % SEL_CHUNK == 0 and t >= WIN_KEYS and t % ns_rows == 0 and nq < CMP_STRIDE
    assert past == t and w_buf == WINDOW and ns_rows % Q_TILE == 0
    nb = t // CMP_STRIDE
    ns = t // SEL_BLOCK
    nt = t // Q_TILE
    nchunks = t // SEL_CHUNK
    mband = _band_matrix(nb, ns)
    page_table = page_table.astype(jnp.int32)
    feat_major = lambda c: jnp.transpose(c, (0, 1, 3, 4, 5, 2)).reshape(DEPTH * c.shape[1], 2 * D_KV, c.shape[2])
    pages_cmp = feat_major(cache_kv_cmp)
    pages_sel = feat_major(cache_kv_sel)
    pages_win = feat_major(cache_kv_win)

    h = (x_prompt[0], x_sample.reshape(ns_rows, D_MODEL))
    p_all = (p_prompt.reshape(DEPTH * t, D_PLE), p_sample.reshape(DEPTH * ns_rows, D_PLE))
    st_p = [[] for _ in range(5)]
    st_s = [[] for _ in range(5)]
    for i in range(DEPTH):
        row1 = lambda a: a[i][None]
        h = _ffn(h, row1(norm_ffn1), w_ffn1_in[i].astype(BF16), w_ffn1_out[i].astype(BF16), t, ns_rows)
        w_mix, gq, gks, gkw = _prep_mix_weights(w_in[i], qk_norm[i])
        u, qp, kv_cmp, kv_sel, kv_win, kvs_bf, kvw_bf, gn, ga, gb = _mix_in(h, row1(norm_mix), w_mix, gq, gks, gkw)

        sp, lam_pow = _prep_s5(ssm_a_re[i], ssm_a_im[i], ssm_log_dt[i], ssm_b_re[i], ssm_b_im[i],
                               ssm_c_re[i], ssm_c_im[i], ssm_d[i], nq)
        sp["pr_s"], sp["pi_s"] = lam_pow(jnp.arange(ns_rows) % nq + 1)
        yg, hr_p, hi_p = _s5_prompt(u, t, sp)
        h0r = jnp.repeat(state_ssm_re[i].reshape(bsz, N_STATE), nq, axis=0)
        h0i = jnp.repeat(state_ssm_im[i].reshape(bsz, N_STATE), nq, axis=0)
        yg, hr_s, hi_s = _s5_sample(u, yg, t, ns_rows, nq, h0r, h0i, sp)

        cw = _prep_compress(cmp_pe_k[i], cmp_pe_v[i], cmp_w_k[i], cmp_w_v[i], qk_norm[i, 1])
        layer_pages = page_table + i * n_pool
        kcc_p, vcc_p = _compress_rows(kv_cmp, t, cw)
        kcc_s, vcc_s = _compress_paged(layer_pages, pages_cmp, cw)
        ocw_p, sel_p, flags = _e1_prompt(qp, gn, kcc_p, vcc_p, mband, kvw_bf, t)
        ocw_s, sel_s = _e1_sample(qp, gn, kcc_s, vcc_s, mband, pages_win, i * bsz, kv_win, t, bsz, nq, past)
        idle = flags[:, 0, :nchunks] == 0
        order = jnp.argsort(idle, axis=1, stable=True).astype(jnp.int32).reshape(-1)
        cnt = (nchunks - jnp.sum(idle, axis=1)).astype(jnp.int32)
        oatt = _e2_prompt(order, cnt, qp, sel_p, gn, ocw_p, kvs_bf, t)
        oatt = _e2_sample(layer_pages, pages_sel, qp, sel_s, gn, ocw_s, kv_sel, oatt, t, nq)

        h = _post(h, yg, oatt, ga, gb,
                  w_glu[i].astype(BF16), _prep_att_out(w_att_out[i]), w_out[i].astype(BF16))
        h = _ffn(h, row1(norm_ffn2), w_ffn2_in[i].astype(BF16), w_ffn2_out[i].astype(BF16), t, ns_rows,
                 (p_all, row1(norm_ple), w_ple_gate[i].astype(BF16), w_ple_proj[i].astype(BF16)), p_layer=i)

        kv5 = lambda a, lead: a.reshape(lead + (2, N_KV, HEAD_DIM))
        st_p[0].append(kv5(kv_cmp[:t], (1, t)))
        st_p[1].append(kv5(kv_sel[:t], (1, t)))
        st_p[2].append(kv5(kv_win[t - min(WINDOW, t):t], (1, min(WINDOW, t))))
        st_p[3].append(hr_p.reshape(1, N_SSM_GROUPS, P_STATE))
        st_p[4].append(hi_p.reshape(1, N_SSM_GROUPS, P_STATE))
        st_s[0].append(kv5(kv_cmp[t:], (bsz, nq)))
        st_s[1].append(kv5(kv_sel[t:], (bsz, nq)))
        st_s[2].append(jnp.concatenate([cache_kv_win[i, :, nq:], kv5(kv_win[t:], (bsz, nq))], axis=1))
        st_s[3].append(hr_s[nq - 1::nq].reshape(bsz, N_SSM_GROUPS, P_STATE))
        st_s[4].append(hi_s[nq - 1::nq].reshape(bsz, N_SSM_GROUPS, P_STATE))

    outs_p = [jnp.stack(a) for a in st_p]
    outs_s = [jnp.stack(a) for a in st_s]
    y_prompt = h[:t][None]
    y_sample = h[t:].reshape(bsz, nq, D_MODEL)
    return (y_prompt, y_sample, *outs_p, *outs_s)
```
